```python
import jax
import jax.numpy as jnp
from jax import lax
import numpy as np

D_MODEL = 1024
BATCH = 4
SEQ = 4096
DEPTH = 2
DEC_BATCH = 16
DEC_SEQ = 2048
PAST_LEN = 128

D_RNN = D_MODEL
RNN_BLOCKS = 16
RNN_BLOCK_W = D_RNN // RNN_BLOCKS
CONV_WIDTH = 4
CONV_PAD = ((CONV_WIDTH - 1) // 2, CONV_WIDTH // 2)
RG_C = 8.0
DILATED_CONFIGS = ((128, 1), (512, 4), (2048, 16))
N_ATTN_GROUPS = len(DILATED_CONFIGS)
HEADS_PER_GROUP = 4
N_ATTN_HEADS = N_ATTN_GROUPS * HEADS_PER_GROUP
HEAD_DIM = 128
D_ATTN = N_ATTN_HEADS * HEAD_DIM
D_ATTN_OUT = HEADS_PER_GROUP * HEAD_DIM
NUM_BUCKETS = 32
MAX_DISTANCE = max(w for w, _ in DILATED_CONFIGS) // 2
NEG_INF = -1e30
IN_SPLITS = (D_RNN, D_ATTN, D_ATTN, D_ATTN, D_MODEL, D_MODEL)
D_IN = sum(IN_SPLITS)
SPLIT_POINTS = tuple(sum(IN_SPLITS[:i + 1]) for i in range(len(IN_SPLITS) - 1))
N_EXPERT_GROUPS = 4
EXPERTS_PER_GROUP = 8
N_EXPERTS = N_EXPERT_GROUPS * EXPERTS_PER_GROUP
TOP_K = 2
D_EXPERT = D_MODEL // 2
MOE_BLOCK = 128
ALPHA = (2 * DEPTH) ** 0.25
BETA = (8 * DEPTH) ** -0.25
LN_EPS = 1e-5

kernel_name = 'hybrid_rglru_dilated_attn_hmoe_encoder'


def layer_norm(x, g, b):
    xf = x.astype(jnp.float32)
    mu = xf.mean(-1, keepdims=True)
    var = jnp.square(xf - mu).mean(-1, keepdims=True)
    return ((xf - mu) * lax.rsqrt(var + LN_EPS) * g + b).astype(x.dtype)


def centred_depthwise_conv(x, w, b):
    y = lax.conv_general_dilated(x, w[:, None, :].astype(x.dtype), window_strides=(1,),
                                 padding=[CONV_PAD], dimension_numbers=('NWC', 'WIO', 'NWC'),
                                 feature_group_count=x.shape[-1])
    return y + b


def rg_lru(xc, w_a, b_a, w_i, b_i, lam, reverse):
    b, s, c = xc.shape
    xb = xc.reshape(b, s, RNN_BLOCKS, RNN_BLOCK_W)
    gate_r = jax.nn.sigmoid((jnp.einsum('bshi,hij->bshj', xb, w_a).reshape(b, s, c) + b_a).astype(jnp.float32))
    gate_i = jax.nn.sigmoid((jnp.einsum('bshi,hij->bshj', xb, w_i).reshape(b, s, c) + b_i).astype(jnp.float32))
    log_a = -RG_C * gate_r * jax.nn.softplus(-lam.astype(jnp.float32))
    a = jnp.exp(log_a)
    u = jnp.sqrt(-jnp.expm1(2.0 * log_a)) * (gate_i * xc.astype(jnp.float32))

    def combine(lhs, rhs):
        a1, u1 = lhs
        a2, u2 = rhs
        return a1 * a2, a2 * u1 + u2

    _, h = lax.associative_scan(combine, (a, u), axis=1, reverse=reverse)
    return h


def t5_bucket(rel):
    half = NUM_BUCKETS // 2
    max_exact = half // 2
    n = np.abs(rel)
    large = max_exact + (np.log(np.maximum(n, 1) / max_exact) / np.log(MAX_DISTANCE / max_exact)
                         * (half - max_exact)).astype(np.int32)
    large = np.minimum(large, half - 1)
    return np.where(rel > 0, half, 0) + np.where(n < max_exact, n, large)


def band_attention(q, k, v, bias, radius):
    n, length, h, e = q.shape
    blk = radius
    nb = -(-length // blk)
    lp = nb * blk
    qb = jnp.pad(q, ((0, 0), (0, lp - length), (0, 0), (0, 0))).reshape(n, nb, blk, h, e)
    kv_pad = ((0, 0), (blk, lp - length + blk), (0, 0), (0, 0))

    def windows(t):
        t = jnp.pad(t, kv_pad).reshape(n, nb + 2, blk, h, e)
        return jnp.concatenate([t[:, :-2], t[:, 1:-1], t[:, 2:]], axis=2)

    kw, vw = windows(k), windows(v)
    qi = np.arange(blk)[:, None]
    kj = np.arange(3 * blk)[None, :]
    band = np.abs(kj - blk - qi) <= radius
    kpos = np.arange(nb)[:, None] * blk + np.arange(3 * blk)[None, :] - blk
    valid = (kpos >= 0) & (kpos < length)
    mask = band[None] & valid[:, None, :]
    s = jnp.einsum('nbqhe,nbkhe->nbhqk', qb, kw).astype(jnp.float32) * (e ** -0.5)
    s = jnp.where(mask[None, :, None], s + bias.astype(jnp.float32)[None, None], NEG_INF)
    m = s.max(-1, keepdims=True)
    p = jnp.exp(s - m)
    l = p.sum(-1, keepdims=True)
    o = jnp.einsum('nbhqk,nbkhe->nbqhe', (p / l).astype(v.dtype), vw)
    lse = jnp.transpose((m + jnp.log(l))[..., 0], (0, 1, 3, 2))
    return o.reshape(n, lp, h, e)[:, :length], lse.reshape(n, lp, h)[:, :length]


def fold_stride(t, dil):
    b, s, h, e = t.shape
    return t.reshape(b, s // dil, dil, h, e).transpose(0, 2, 1, 3, 4).reshape(b * dil, s // dil, h, e)


def unfold_stride(t, b, dil):
    l = t.shape[1]
    rest = t.shape[2:]
    t = t.reshape((b, dil, l) + rest)
    perm = (0, 2, 1) + tuple(range(3, t.ndim))
    return t.transpose(perm).reshape((b, dil * l) + rest)


def dilated_attention(q, k, v, rel_bias):
    b, s = q.shape[:2]
    outs, lses = [], []
    for g, (window, dil) in enumerate(DILATED_CONFIGS):
        radius = window // (2 * dil)
        hs = slice(g * HEADS_PER_GROUP, (g + 1) * HEADS_PER_GROUP)
        rel = (np.arange(3 * radius)[None, :] - radius - np.arange(radius)[:, None]) * dil
        bias = jnp.transpose(rel_bias[t5_bucket(rel), hs], (2, 0, 1))
        o, lse = band_attention(fold_stride(q[:, :, hs], dil), fold_stride(k[:, :, hs], dil),
                                fold_stride(v[:, :, hs], dil), bias, radius)
        outs.append(unfold_stride(o, b, dil))
        lses.append(unfold_stride(lse, b, dil))
    wts = jax.nn.softmax(jnp.stack(lses), axis=0)
    o = jnp.einsum('gbsh,gbshe->bshe', wts.astype(q.dtype), jnp.stack(outs))
    return o.reshape(b, s, D_ATTN_OUT)


def grouped_experts(xf, eidx, gate, w_gate, w_up, w_down):
    n, d = xf.shape
    m = eidx.size
    n_blocks = -(-m // MOE_BLOCK) + N_EXPERTS
    slots = n_blocks * MOE_BLOCK
    flat_e = eidx.reshape(m)
    order = jnp.argsort(flat_e)
    sorted_e = flat_e[order]
    sorted_tok = (order // TOP_K).astype(jnp.int32)
    sorted_w = gate.reshape(m)[order]
    counts = jnp.bincount(flat_e, length=N_EXPERTS)
    start = jnp.cumsum(counts) - counts
    padded = (counts + MOE_BLOCK - 1) // MOE_BLOCK * MOE_BLOCK
    pend = jnp.cumsum(padded)
    pstart = pend - padded
    dest = pstart[sorted_e] + jnp.arange(m, dtype=jnp.int32) - start[sorted_e]
    slot_tok = jnp.full((slots,), n, jnp.int32).at[dest].set(sorted_tok)
    slot_w = jnp.zeros((slots,), gate.dtype).at[dest].set(sorted_w)
    block_e = jnp.minimum(jnp.searchsorted(pend, jnp.arange(n_blocks, dtype=jnp.int32) * MOE_BLOCK, side='right'),
                          N_EXPERTS - 1)
    xb = jnp.concatenate([xf, jnp.zeros((1, d), xf.dtype)])[slot_tok].reshape(n_blocks, MOE_BLOCK, d)

    def expert_block(args):
        xblk, e = args
        return (jax.nn.silu(xblk @ w_gate[e]) * (xblk @ w_up[e])) @ w_down[e]

    yb = lax.map(expert_block, (xb, block_e)).reshape(slots, d) * slot_w[:, None]
    return jnp.zeros((n + 1, d), xf.dtype).at[slot_tok].add(yb)[:n]


def hier_moe(x, router_w, router_b, expert_router_w, expert_router_b, w_gate, w_up, w_down):
    b, s, d = x.shape
    xf = x.reshape(b * s, d)
    g_logits = (xf @ router_w + router_b).astype(jnp.float32)
    g_sel = jnp.argmax(g_logits, axis=-1).astype(jnp.int32)
    p_group = jnp.take_along_axis(jax.nn.softmax(g_logits, axis=-1), g_sel[:, None], axis=-1)
    e_logits = jnp.einsum('nd,gde->nge', xf, expert_router_w) + expert_router_b
    e_logits = jnp.take_along_axis(e_logits, g_sel[:, None, None], axis=1)[:, 0].astype(jnp.float32)
    top_v, top_i = lax.top_k(e_logits, TOP_K)
    gate = (p_group * jax.nn.softmax(top_v, axis=-1)).astype(x.dtype)
    eidx = g_sel[:, None] * EXPERTS_PER_GROUP + top_i.astype(jnp.int32)
    return grouped_experts(xf, eidx, gate, w_gate, w_up, w_down).reshape(b, s, d)


def trunk(x, rel_bias, w_in, b_in, conv_w, conv_b, rg_wa, rg_ba, rg_wi, rg_bi, rg_lam,
          w_rnn_out, w_attn_out, w_o, ln1_g, ln1_b, router_w, router_b, expert_router_w,
          expert_router_b, w_gate, w_up, w_down, ln2_g, ln2_b):
    b, s, _ = x.shape
    heads = (b, s, N_ATTN_HEADS, HEAD_DIM)
    for l in range(DEPTH):
        proj = x @ w_in[l] + b_in[l]
        x_rnn, q, k, v, g_rnn, g_attn = jnp.split(proj, list(SPLIT_POINTS), axis=-1)
        xc = centred_depthwise_conv(x_rnn, conv_w[l], conv_b[l])
        h_rnn = (rg_lru(xc, rg_wa[l, 0], rg_ba[l, 0], rg_wi[l, 0], rg_bi[l, 0], rg_lam[l, 0], False)
                 + rg_lru(xc, rg_wa[l, 1], rg_ba[l, 1], rg_wi[l, 1], rg_bi[l, 1], rg_lam[l, 1], True)).astype(x.dtype)
        y_attn = dilated_attention(q.reshape(heads), k.reshape(heads), v.reshape(heads), rel_bias)
        mixed = (jax.nn.sigmoid(g_rnn) * (h_rnn @ w_rnn_out[l])
                 + jax.nn.sigmoid(g_attn) * (y_attn @ w_attn_out[l]))
        x = layer_norm(ALPHA * x + mixed @ w_o[l], ln1_g[l], ln1_b[l])
        ffn = hier_moe(x, router_w[l], router_b[l], expert_router_w[l], expert_router_b[l],
                       w_gate[l], w_up[l], w_down[l])
        x = layer_norm(ALPHA * x + ffn, ln2_g[l], ln2_b[l])
    return x


def setup_inputs(seed: int = 0) -> dict:
    key = jax.random.key(seed)
    ks = jax.random.split(key, 26)

    def nrm(i, shape, scale):
        return jax.random.normal(ks[i], shape, jnp.float32) * scale

    L = DEPTH
    a_base = jax.random.uniform(ks[11], (L, 2, D_RNN), jnp.float32, 0.9, 0.999) ** (1.0 / RG_C)
    return {
        'x_prompt': nrm(0, (BATCH, SEQ, D_MODEL), 1.0),
        'x_sample': nrm(1, (DEC_BATCH, DEC_SEQ, D_MODEL), 1.0),
        'rel_bias': nrm(2, (NUM_BUCKETS, N_ATTN_HEADS), 0.2),
        'w_in': nrm(3, (L, D_MODEL, D_IN), D_MODEL ** -0.5),
        'b_in': nrm(4, (L, D_IN), 0.01),
        'conv_w': nrm(5, (L, CONV_WIDTH, D_RNN), CONV_WIDTH ** -0.5),
        'conv_b': nrm(6, (L, D_RNN), 0.01),
        'rg_wa': nrm(7, (L, 2, RNN_BLOCKS, RNN_BLOCK_W, RNN_BLOCK_W), RNN_BLOCK_W ** -0.5),
        'rg_ba': nrm(8, (L, 2, D_RNN), 0.01),
        'rg_wi': nrm(9, (L, 2, RNN_BLOCKS, RNN_BLOCK_W, RNN_BLOCK_W), RNN_BLOCK_W ** -0.5),
        'rg_bi': nrm(10, (L, 2, D_RNN), 0.01),
        'rg_lam': jnp.log(a_base) - jnp.log1p(-a_base),
        'w_rnn_out': nrm(12, (L, D_RNN, D_MODEL), D_RNN ** -0.5),
        'w_attn_out': nrm(13, (L, D_ATTN_OUT, D_MODEL), D_ATTN_OUT ** -0.5),
        'w_o': nrm(14, (L, D_MODEL, D_MODEL), BETA * D_MODEL ** -0.5),
        'ln1_g': 1.0 + nrm(15, (L, D_MODEL), 0.01),
        'ln1_b': nrm(16, (L, D_MODEL), 0.01),
        'router_w': nrm(17, (L, D_MODEL, N_EXPERT_GROUPS), D_MODEL ** -0.5),
        'router_b': nrm(18, (L, N_EXPERT_GROUPS), 0.01),
        'expert_router_w': nrm(19, (L, N_EXPERT_GROUPS, D_MODEL, EXPERTS_PER_GROUP), D_MODEL ** -0.5),
        'expert_router_b': nrm(20, (L, N_EXPERT_GROUPS, EXPERTS_PER_GROUP), 0.01),
        'w_gate': nrm(21, (L, N_EXPERTS, D_MODEL, D_EXPERT), D_MODEL ** -0.5),
        'w_up': nrm(22, (L, N_EXPERTS, D_MODEL, D_EXPERT), D_MODEL ** -0.5),
        'w_down': nrm(23, (L, N_EXPERTS, D_EXPERT, D_MODEL), BETA * D_EXPERT ** -0.5),
        'ln2_g': 1.0 + nrm(24, (L, D_MODEL), 0.01),
        'ln2_b': nrm(25, (L, D_MODEL), 0.01),
    }


def reference(x_prompt, x_sample, rel_bias, w_in, b_in, conv_w, conv_b, rg_wa, rg_ba, rg_wi, rg_bi,
              rg_lam, w_rnn_out, w_attn_out, w_o, ln1_g, ln1_b, router_w, router_b, expert_router_w,
              expert_router_b, w_gate, w_up, w_down, ln2_g, ln2_b):
    params = (rel_bias, w_in, b_in, conv_w, conv_b, rg_wa, rg_ba, rg_wi, rg_bi, rg_lam,
              w_rnn_out, w_attn_out, w_o, ln1_g, ln1_b, router_w, router_b, expert_router_w,
              expert_router_b, w_gate, w_up, w_down, ln2_g, ln2_b)
    y_prompt = trunk(x_prompt, *params)
    y_sample = trunk(x_sample, *params)
    return (y_prompt, y_sample)
```

```python
import functools

import numpy as np
import jax
import jax.numpy as jnp
from jax import lax
from jax.experimental import pallas as pl
from jax.experimental.pallas import tpu as pltpu

D_MODEL = 1024
DEPTH = 2
D_RNN = D_MODEL
RNN_BLOCKS = 16
RNN_BLOCK_W = D_RNN // RNN_BLOCKS
CONV_WIDTH = 4
RG_C = 8.0
DILATED_CONFIGS = ((128, 1), (512, 4), (2048, 16))
N_ATTN_GROUPS = len(DILATED_CONFIGS)
HEADS_PER_GROUP = 4
N_ATTN_HEADS = N_ATTN_GROUPS * HEADS_PER_GROUP
HEAD_DIM = 128
D_ATTN = N_ATTN_HEADS * HEAD_DIM
D_ATTN_OUT = HEADS_PER_GROUP * HEAD_DIM
NUM_BUCKETS = 32
MAX_DISTANCE = max(w for w, _ in DILATED_CONFIGS) // 2
NEG_INF = -1e30
D_IN = D_RNN + 3 * D_ATTN + 2 * D_MODEL
N_EXPERT_GROUPS = 4
EXPERTS_PER_GROUP = 8
N_EXPERTS = N_EXPERT_GROUPS * EXPERTS_PER_GROUP
TOP_K = 2
D_EXPERT = D_MODEL // 2
ALPHA = (2 * DEPTH) ** 0.25
LN_EPS = 1e-5

LANES = 128
SUBLANES = 8
VMEM_LIMIT = 56 * 1024 * 1024

COL_XRNN = 0
COL_GRNN = D_RNN
COL_GATTN = D_RNN + D_MODEL
COL_Q = D_RNN + 2 * D_MODEL
COL_K = COL_Q + D_ATTN
COL_V = COL_K + D_ATTN

RADIUS = 64
assert all(w // (2 * d) == RADIUS for w, d in DILATED_CONFIGS)
ATTN_ROWS = 1024
RNN_CHUNK = 256
RNN_TILE = 256
MOE_BLK = 256
ROUTE_W = LANES


def _cparams(n_axes):
    return pltpu.CompilerParams(dimension_semantics=("arbitrary",) * n_axes,
                                vmem_limit_bytes=VMEM_LIMIT)


def _proj_kernel(x_ref, w_ref, b_ref, o_ref):
    acc = jnp.dot(x_ref[...], w_ref[...], preferred_element_type=jnp.float32)
    o_ref[...] = acc + b_ref[...]


def in_projection(x_bf, w_bf, b, tm=512, tn=1536):
    n, k = x_bf.shape
    nout = w_bf.shape[1]
    return pl.pallas_call(
        _proj_kernel,
        grid=(nout // tn, n // tm),
        in_specs=[pl.BlockSpec((tm, k), lambda j, i: (i, 0)),
                  pl.BlockSpec((k, tn), lambda j, i: (0, j)),
                  pl.BlockSpec((1, tn), lambda j, i: (0, j))],
        out_specs=pl.BlockSpec((tm, tn), lambda j, i: (i, j)),
        out_shape=jax.ShapeDtypeStruct((n, nout), jnp.float32),
        compiler_params=_cparams(2),
        name="in_projection",
    )(x_bf, w_bf, b)


def _rglru_kernel(x_ref, cw_ref, cb_ref, wg_ref, gb_ref, lam_ref, o_ref,
                  hsum, a_s, u_s, h_s, *, seq, tile):
    c = RNN_CHUNK
    n_tiles = seq // tile
    n_grp = tile // SUBLANES
    big_rows = tile + 2 * SUBLANES
    row_in_grp = lax.broadcasted_iota(jnp.int32, (tile, c), 0) & (SUBLANES - 1)

    def conv_tile(t0):
        centre = x_ref[0, pl.ds(t0, tile), :]
        p0 = pl.multiple_of(jnp.maximum(t0 - SUBLANES, 0), SUBLANES)
        n0 = pl.multiple_of(jnp.minimum(t0 + tile, seq - SUBLANES), SUBLANES)
        prev = jnp.where(t0 > 0, x_ref[0, pl.ds(p0, SUBLANES), :], 0.0)
        nxt = jnp.where(t0 + tile < seq, x_ref[0, pl.ds(n0, SUBLANES), :], 0.0)
        big = jnp.concatenate([prev, centre, nxt], axis=0)
        xc = cb_ref[...] + cw_ref[1:2, :] * centre
        for k in (0, 2, 3):
            off = k - 1
            tap = pltpu.roll(big, (-off) % big_rows, 0)[SUBLANES:SUBLANES + tile]
            xc = xc + cw_ref[k:k + 1, :] * tap
        return xc

    def gates(xc, dr):
        g = jnp.dot(xc.astype(jnp.bfloat16), wg_ref[0, dr],
                    preferred_element_type=jnp.float32) + gb_ref[0, dr]
        gate_r = jax.nn.sigmoid(g[:, :c])
        gate_i = jax.nn.sigmoid(g[:, c:])
        neg_lam = -lam_ref[0, dr]
        softplus = jnp.maximum(neg_lam, 0.0) + jnp.log1p(jnp.exp(-jnp.abs(neg_lam)))
        log_a = -RG_C * gate_r * softplus
        a = jnp.exp(log_a)
        th = jnp.tanh(log_a)
        one_minus_a2 = -2.0 * th / (1.0 - th)
        u = jnp.sqrt(one_minus_a2) * (gate_i * xc)
        return a, u

    def tile_prefix(a, u, reverse):
        for s in (1, 2, 4):
            if reverse:
                a_sh = pltpu.roll(a, tile - s, 0)
                u_sh = pltpu.roll(u, tile - s, 0)
                m = row_in_grp < SUBLANES - s
            else:
                a_sh = pltpu.roll(a, s, 0)
                u_sh = pltpu.roll(u, s, 0)
                m = row_in_grp >= s
            u = jnp.where(m, u + a * u_sh, u)
            a = jnp.where(m, a * a_sh, a)
        a_s[...] = a
        u_s[...] = u

    def run_direction(dr, reverse):
        def tile_body(ti, h):
            t_idx = (n_tiles - 1 - ti) if reverse else ti
            t0 = pl.multiple_of(t_idx * tile, tile)
            xc = conv_tile(t0)
            a, u = gates(xc, dr)
            tile_prefix(a, u, reverse)

            def grp_body(gi, hc):
                g_idx = (n_grp - 1 - gi) if reverse else gi
                r0 = pl.multiple_of(g_idx * SUBLANES, SUBLANES)
                hg = u_s[pl.ds(r0, SUBLANES), :] + a_s[pl.ds(r0, SUBLANES), :] * hc
                h_s[pl.ds(r0, SUBLANES), :] = hg
                edge = hg[0:1, :] if reverse else hg[SUBLANES - 1:SUBLANES, :]
                return jnp.broadcast_to(edge, (SUBLANES, c))

            h = lax.fori_loop(0, n_grp, grp_body, h, unroll=4)
            if reverse:
                o_ref[0, pl.ds(t0, tile), :] = (hsum[pl.ds(t0, tile), :] + h_s[...]).astype(o_ref.dtype)
            else:
                hsum[pl.ds(t0, tile), :] = h_s[...]
            return h

        lax.fori_loop(0, n_tiles, tile_body, jnp.zeros((SUBLANES, c), jnp.float32))

    run_direction(0, False)
    run_direction(1, True)


def rglru_branch(proj3, cw, cb, wg, gb, lam):
    b, s, _ = proj3.shape
    c = RNN_CHUNK
    n_chunks = D_RNN // c
    kern = functools.partial(_rglru_kernel, seq=s, tile=RNN_TILE)
    return pl.pallas_call(
        kern,
        grid=(b, n_chunks),
        in_specs=[pl.BlockSpec((1, s, c), lambda bi, ci: (bi, 0, COL_XRNN // c + ci)),
                  pl.BlockSpec((CONV_WIDTH, c), lambda bi, ci: (0, ci)),
                  pl.BlockSpec((1, c), lambda bi, ci: (0, ci)),
                  pl.BlockSpec((1, 2, c, 2 * c), lambda bi, ci: (ci, 0, 0, 0)),
                  pl.BlockSpec((1, 2, 1, 2 * c), lambda bi, ci: (ci, 0, 0, 0)),
                  pl.BlockSpec((1, 2, 1, c), lambda bi, ci: (ci, 0, 0, 0))],
        out_specs=pl.BlockSpec((1, s, c), lambda bi, ci: (bi, 0, ci)),
        out_shape=jax.ShapeDtypeStruct((b, s, D_RNN), jnp.bfloat16),
        scratch_shapes=[pltpu.VMEM((s, c), jnp.float32),
                        pltpu.VMEM((RNN_TILE, c), jnp.float32),
                        pltpu.VMEM((RNN_TILE, c), jnp.float32),
                        pltpu.VMEM((RNN_TILE, c), jnp.float32)],
        compiler_params=_cparams(2),
        name="rglru_branch",
    )(proj3, cw, cb, wg, gb, lam)


def _attn_kernel(q_ref, kp_ref, kc_ref, kn_ref, vp_ref, vc_ref, vn_ref, bias_ref,
                 o_ref, lse_ref, *, dil, n_row_blocks):
    rows = ATTN_ROWS
    per_res = rows // dil
    n_q = per_res // RADIUS
    win = 3 * RADIUS
    i = pl.program_id(1)
    is_first = i == 0
    is_last = i == n_row_blocks - 1
    qi = lax.broadcasted_iota(jnp.int32, (RADIUS, win), 0)
    kj = lax.broadcasted_iota(jnp.int32, (RADIUS, win), 1)
    band = jnp.abs(kj - RADIUS - qi) <= RADIUS
    mask_first = band & (kj >= jnp.where(is_first, RADIUS, 0))
    mask_last = band & (kj < jnp.where(is_last, 2 * RADIUS, win))
    mask_both = mask_first & mask_last
    scale = HEAD_DIM ** -0.5
    bias = bias_ref[0]

    def fold(ref, r, n):
        if dil == 1:
            return ref[0, pl.ds(r, n), :]
        return ref[0, pl.ds(r, n, stride=dil), :]

    for r in range(dil):
        qf = fold(q_ref, r, per_res).astype(jnp.bfloat16)
        kf = jnp.concatenate([fold(kp_ref, r, RADIUS), fold(kc_ref, r, per_res),
                              fold(kn_ref, r, RADIUS)], axis=0).astype(jnp.bfloat16)
        vf = jnp.concatenate([fold(vp_ref, r, RADIUS), fold(vc_ref, r, per_res),
                              fold(vn_ref, r, RADIUS)], axis=0).astype(jnp.bfloat16)
        for jq in range(n_q):
            if n_q == 1:
                mask = mask_both
            elif jq == 0:
                mask = mask_first
            elif jq == n_q - 1:
                mask = mask_last
            else:
                mask = band
            qh = qf[jq * RADIUS:(jq + 1) * RADIUS]
            kh = kf[jq * RADIUS:jq * RADIUS + win]
            vh = vf[jq * RADIUS:jq * RADIUS + win]
            s = lax.dot_general(qh, kh, (((1,), (1,)), ((), ())),
                                preferred_element_type=jnp.float32) * scale
            s = jnp.where(mask, s + bias, NEG_INF)
            m = jnp.max(s, axis=-1, keepdims=True)
            p = jnp.exp(s - m)
            l = jnp.sum(p, axis=-1, keepdims=True)
            o = jnp.dot(p.astype(jnp.bfloat16), vh, preferred_element_type=jnp.float32) / l
            lse = jnp.broadcast_to(m + jnp.log(l), (RADIUS, HEAD_DIM))
            start = jq * RADIUS * dil + r
            if dil == 1:
                o_ref[0, pl.ds(start, RADIUS), :] = o
                lse_ref[0, pl.ds(start, RADIUS), :] = lse
            else:
                o_ref[0, pl.ds(start, RADIUS, stride=dil), :] = o
                lse_ref[0, pl.ds(start, RADIUS, stride=dil), :] = lse


def attention_group(proj3, bias, g, dil):
    b, s, _ = proj3.shape
    rows = ATTN_ROWS
    halo = RADIUS * dil
    nrb = s // rows
    ratio = rows // halo
    n_halo_blocks = s // halo
    w = HEAD_DIM
    nh = HEADS_PER_GROUP
    qc, kc, vc = COL_Q // w + g * nh, COL_K // w + g * nh, COL_V // w + g * nh

    def prev_map(col):
        return lambda bi, i, h: (bi, jnp.maximum(i * ratio - 1, 0), col + h)

    def next_map(col):
        return lambda bi, i, h: (bi, jnp.minimum((i + 1) * ratio, n_halo_blocks - 1), col + h)

    def cur_map(col):
        return lambda bi, i, h: (bi, i, col + h)

    kern = functools.partial(_attn_kernel, dil=dil, n_row_blocks=nrb)
    out_sds = jax.ShapeDtypeStruct((b, s, D_ATTN_OUT), jnp.float32)
    return pl.pallas_call(
        kern,
        grid=(b, nrb, nh),
        in_specs=[pl.BlockSpec((1, rows, w), cur_map(qc)),
                  pl.BlockSpec((1, halo, w), prev_map(kc)),
                  pl.BlockSpec((1, rows, w), cur_map(kc)),
                  pl.BlockSpec((1, halo, w), next_map(kc)),
                  pl.BlockSpec((1, halo, w), prev_map(vc)),
                  pl.BlockSpec((1, rows, w), cur_map(vc)),
                  pl.BlockSpec((1, halo, w), next_map(vc)),
                  pl.BlockSpec((1, RADIUS, 3 * RADIUS), lambda bi, i, h: (h, 0, 0))],
        out_specs=[pl.BlockSpec((1, rows, w), cur_map(0)),
                   pl.BlockSpec((1, rows, w), cur_map(0))],
        out_shape=[out_sds, out_sds],
        compiler_params=_cparams(3),
        name=f"attention_group{g}",
    )(proj3, proj3, proj3, proj3, proj3, proj3, proj3, bias)


def _layer_norm(z, g, b):
    mu = jnp.mean(z, axis=-1, keepdims=True)
    zc = z - mu
    var = jnp.mean(zc * zc, axis=-1, keepdims=True)
    return zc * lax.rsqrt(var + LN_EPS) * g + b


def _mix_kernel(x_ref, grnn_ref, gattn_ref, h_ref, o0_ref, o1_ref, o2_ref, l0_ref, l1_ref, l2_ref,
                wr_ref, wa_ref, wo_ref, lng_ref, lnb_ref, rw_ref, rb_ref,
                xo_ref, xob_ref, eidx_ref, gate_ref):
    l0, l1, l2 = l0_ref[...], l1_ref[...], l2_ref[...]
    m = jnp.maximum(jnp.maximum(l0, l1), l2)
    e0, e1, e2 = jnp.exp(l0 - m), jnp.exp(l1 - m), jnp.exp(l2 - m)
    y_attn = (e0 * o0_ref[...] + e1 * o1_ref[...] + e2 * o2_ref[...]) / (e0 + e1 + e2)
    rnn = jnp.dot(h_ref[...], wr_ref[...], preferred_element_type=jnp.float32)
    att = jnp.dot(y_attn.astype(jnp.bfloat16), wa_ref[...], preferred_element_type=jnp.float32)
    mixed = jax.nn.sigmoid(grnn_ref[...]) * rnn + jax.nn.sigmoid(gattn_ref[...]) * att
    z = ALPHA * x_ref[...] + jnp.dot(mixed.astype(jnp.bfloat16), wo_ref[...],
                                     preferred_element_type=jnp.float32)
    x1 = _layer_norm(z, lng_ref[...], lnb_ref[...])
    xo_ref[...] = x1
    xob_ref[...] = x1.astype(jnp.bfloat16)

    logits = jnp.dot(x1, rw_ref[...], preferred_element_type=jnp.float32,
                     precision=lax.Precision.HIGHEST) + rb_ref[...]
    tm = logits.shape[0]
    lane = lax.broadcasted_iota(jnp.int32, (tm, ROUTE_W), 1)
    lane_f = lane.astype(jnp.float32)
    far = float(ROUTE_W)
    is_group = lane < N_EXPERT_GROUPS
    gl = jnp.where(is_group, logits, -jnp.inf)
    gmax = jnp.max(gl, axis=-1, keepdims=True)
    gsel = jnp.min(jnp.where(gl == gmax, lane_f, far), axis=-1, keepdims=True)
    p_group = 1.0 / jnp.sum(jnp.where(is_group, jnp.exp(logits - gmax), 0.0), axis=-1, keepdims=True)
    lane_group = ((lane - N_EXPERT_GROUPS) >> 3).astype(jnp.float32)
    in_group = (lane >= N_EXPERT_GROUPS) & (lane < N_EXPERT_GROUPS + N_EXPERTS) & (lane_group == gsel)
    el = jnp.where(in_group, logits, -jnp.inf)
    m1 = jnp.max(el, axis=-1, keepdims=True)
    i1 = jnp.min(jnp.where(el == m1, lane_f, far), axis=-1, keepdims=True)
    el2 = jnp.where(lane_f == i1, -jnp.inf, el)
    m2 = jnp.max(el2, axis=-1, keepdims=True)
    i2 = jnp.min(jnp.where(el2 == m2, lane_f, far), axis=-1, keepdims=True)
    e21 = jnp.exp(m2 - m1)
    den = 1.0 + e21
    g1 = p_group * (1.0 / den)
    g2 = p_group * (e21 / den)
    eidx = jnp.where(lane == 0, i1, jnp.where(lane == 1, i2, float(N_EXPERT_GROUPS))) - float(N_EXPERT_GROUPS)
    eidx_ref[...] = eidx.astype(jnp.int32)
    gate_ref[...] = jnp.where(lane == 0, g1, jnp.where(lane == 1, g2, 0.0))


def mix_and_route(x, proj, h_rnn, attn_outs, attn_lses, wr, wa, wo, lng, lnb, rw, rb, tm=256):
    n = x.shape[0]
    d = D_MODEL
    w = D_ATTN_OUT
    row = lambda width, col=0: pl.BlockSpec((tm, width), lambda i, c=col: (i, c))
    full = lambda a: pl.BlockSpec(a.shape, lambda i: (0,) * a.ndim)
    return pl.pallas_call(
        _mix_kernel,
        grid=(n // tm,),
        in_specs=[row(d), row(d, COL_GRNN // d), row(d, COL_GATTN // d), row(d),
                  row(w), row(w), row(w), row(w), row(w), row(w),
                  full(wr), full(wa), full(wo), full(lng), full(lnb), full(rw), full(rb)],
        out_specs=[row(d), row(d), row(ROUTE_W), row(ROUTE_W)],
        out_shape=[jax.ShapeDtypeStruct((n, d), jnp.float32),
                   jax.ShapeDtypeStruct((n, d), jnp.bfloat16),
                   jax.ShapeDtypeStruct((n, ROUTE_W), jnp.int32),
                   jax.ShapeDtypeStruct((n, ROUTE_W), jnp.float32)],
        compiler_params=_cparams(1),
        name="mix_and_route",
    )(x, proj, proj, h_rnn, *attn_outs, *attn_lses, wr, wa, wo, lng, lnb, rw, rb)


def _expert_kernel(be_ref, nused_ref, xb_ref, sw_ref, wg_ref, wu_ref, wd_ref, y_ref):
    i = pl.program_id(0)

    @pl.when(i < nused_ref[0])
    def _():
        xb = xb_ref[...]
        gate = jnp.dot(xb, wg_ref[0], preferred_element_type=jnp.float32)
        up = jnp.dot(xb, wu_ref[0], preferred_element_type=jnp.float32)
        hid = (jax.nn.silu(gate) * up).astype(jnp.bfloat16)
        y = jnp.dot(hid, wd_ref[0], preferred_element_type=jnp.float32)
        y_ref[...] = y * sw_ref[...]

    @pl.when(i >= nused_ref[0])
    def _():
        y_ref[...] = jnp.zeros_like(y_ref)


def expert_blocks(block_e, n_used, xb, slot_w, wg, wu, wd):
    slots, d = xb.shape
    n_blocks = slots // MOE_BLK
    grid_spec = pltpu.PrefetchScalarGridSpec(
        num_scalar_prefetch=2,
        grid=(n_blocks,),
        in_specs=[pl.BlockSpec((MOE_BLK, d), lambda i, be, nu: (i, 0)),
                  pl.BlockSpec((MOE_BLK, 1), lambda i, be, nu: (i, 0)),
                  pl.BlockSpec((1, d, D_EXPERT), lambda i, be, nu: (be[i], 0, 0)),
                  pl.BlockSpec((1, d, D_EXPERT), lambda i, be, nu: (be[i], 0, 0)),
                  pl.BlockSpec((1, D_EXPERT, d), lambda i, be, nu: (be[i], 0, 0))],
        out_specs=pl.BlockSpec((MOE_BLK, d), lambda i, be, nu: (i, 0)),
    )
    return pl.pallas_call(
        _expert_kernel,
        grid_spec=grid_spec,
        out_shape=jax.ShapeDtypeStruct((slots, d), jnp.float32),
        compiler_params=_cparams(1),
        name="expert_blocks",
    )(block_e, n_used, xb, slot_w, wg, wu, wd)


def _ln2_kernel(x_ref, f_ref, g_ref, b_ref, xo_ref, xob_ref):
    ffn = f_ref[:, :D_MODEL] + f_ref[:, D_MODEL:]
    x2 = _layer_norm(ALPHA * x_ref[...] + ffn, g_ref[...], b_ref[...])
    xo_ref[...] = x2
    xob_ref[...] = x2.astype(jnp.bfloat16)


def combine_ln2(x1, ffn2, g, b, tm=256):
    n, d = x1.shape
    return pl.pallas_call(
        _ln2_kernel,
        grid=(n // tm,),
        in_specs=[pl.BlockSpec((tm, d), lambda i: (i, 0)),
                  pl.BlockSpec((tm, TOP_K * d), lambda i: (i, 0)),
                  pl.BlockSpec((1, d), lambda i: (0, 0)),
                  pl.BlockSpec((1, d), lambda i: (0, 0))],
        out_specs=[pl.BlockSpec((tm, d), lambda i: (i, 0)),
                   pl.BlockSpec((tm, d), lambda i: (i, 0))],
        out_shape=[jax.ShapeDtypeStruct((n, d), jnp.float32),
                   jax.ShapeDtypeStruct((n, d), jnp.bfloat16)],
        compiler_params=_cparams(1),
        name="combine_ln2",
    )(x1, ffn2, g, b)


def moe_layer(x1, x1_bf, eidx, gate, wg, wu, wd, ln_g, ln_b):
    n, d = x1.shape
    m = n * TOP_K
    n_blocks = m // MOE_BLK + N_EXPERTS
    slots = n_blocks * MOE_BLK
    flat_e = eidx.reshape(m)
    onehot = (flat_e[:, None] == jnp.arange(N_EXPERTS, dtype=jnp.int32)[None, :]).astype(jnp.int32)
    csum = jnp.cumsum(onehot, axis=0)
    rank = jnp.sum((csum - onehot) * onehot, axis=1)
    counts = csum[-1]
    padded = (counts + MOE_BLK - 1) // MOE_BLK * MOE_BLK
    pend = jnp.cumsum(padded)
    pstart = pend - padded
    dest = (pstart[flat_e] + rank).astype(jnp.int32)
    tok = jnp.arange(m, dtype=jnp.int32) // TOP_K
    slot_tok = jnp.zeros((slots,), jnp.int32).at[dest].set(tok)
    slot_w = jnp.zeros((slots,), jnp.float32).at[dest].set(gate.reshape(m))
    block_e = jnp.minimum(
        jnp.searchsorted(pend, jnp.arange(n_blocks, dtype=jnp.int32) * MOE_BLK, side='right'),
        N_EXPERTS - 1).astype(jnp.int32)
    n_used = (pend[-1:] // MOE_BLK).astype(jnp.int32)
    xb = x1_bf[slot_tok]
    yb = expert_blocks(block_e, n_used, xb, slot_w[:, None], wg, wu, wd)
    ffn2 = yb[dest].reshape(n, TOP_K * d)
    return combine_ln2(x1, ffn2, ln_g, ln_b)


def _t5_bucket(rel):
    half = NUM_BUCKETS // 2
    max_exact = half // 2
    n = np.abs(rel)
    large = max_exact + (np.log(np.maximum(n, 1) / max_exact) / np.log(MAX_DISTANCE / max_exact)
                         * (half - max_exact)).astype(np.int32)
    large = np.minimum(large, half - 1)
    return np.where(rel > 0, half, 0) + np.where(n < max_exact, n, large)


def _bias_table(rel_bias, g, dil):
    rel = (np.arange(3 * RADIUS)[None, :] - RADIUS - np.arange(RADIUS)[:, None]) * dil
    hs = slice(g * HEADS_PER_GROUP, (g + 1) * HEADS_PER_GROUP)
    return jnp.transpose(rel_bias[_t5_bucket(rel), hs], (2, 0, 1)).astype(jnp.float32)


def _chunk_block_diag(w):
    per = RNN_CHUNK // RNN_BLOCK_W
    n_chunks = RNN_BLOCKS // per
    w4 = w.reshape(n_chunks, per, RNN_BLOCK_W, RNN_BLOCK_W)
    dense = jnp.einsum('chij,hk->chikj', w4, jnp.eye(per, dtype=w.dtype))
    return dense.reshape(n_chunks, RNN_CHUNK, RNN_CHUNK)


def _gate_weights(wa, wi):
    per_dir = [jnp.concatenate([_chunk_block_diag(wa[d]), _chunk_block_diag(wi[d])], axis=-1)
               for d in range(2)]
    return jnp.stack(per_dir, axis=1).astype(jnp.bfloat16)


def _per_chunk(v):
    return v.reshape(2, D_RNN // RNN_CHUNK, 1, RNN_CHUNK).transpose(1, 0, 2, 3)


def _permute_in_cols(a):
    x_rnn, q, k, v, g_rnn, g_attn = jnp.split(
        a, [D_RNN, D_RNN + D_ATTN, D_RNN + 2 * D_ATTN, D_RNN + 3 * D_ATTN,
            D_RNN + 3 * D_ATTN + D_MODEL], axis=-1)
    return jnp.concatenate([x_rnn, g_rnn, g_attn, q, k, v], axis=-1)


def _trunk(x, biases, lp):
    b, s, d = x.shape
    n = b * s
    x2 = x.reshape(n, d)
    x2_bf = x2.astype(jnp.bfloat16)
    for l in range(DEPTH):
        p = lp[l]
        proj = in_projection(x2_bf, p['w_in'], p['b_in'])
        proj3 = proj.reshape(b, s, D_IN)
        h_rnn = rglru_branch(proj3, p['conv_w'], p['conv_b'], p['wg'], p['gb'], p['lam'])
        outs, lses = [], []
        for g, (_, dil) in enumerate(DILATED_CONFIGS):
            o, lse = attention_group(proj3, biases[g], g, dil)
            outs.append(o.reshape(n, D_ATTN_OUT))
            lses.append(lse.reshape(n, D_ATTN_OUT))
        x1, x1_bf, eidx, gate = mix_and_route(
            x2, proj, h_rnn.reshape(n, D_RNN), outs, lses, p['w_rnn_out'], p['w_attn_out'], p['w_o'],
            p['ln1_g'], p['ln1_b'], p['router_w'], p['router_b'])
        x2, x2_bf = moe_layer(x1, x1_bf, eidx[:, :TOP_K], gate[:, :TOP_K],
                              p['w_gate'], p['w_up'], p['w_down'], p['ln2_g'], p['ln2_b'])
    return x2.reshape(b, s, d)


def kernel(x_prompt, x_sample, rel_bias, w_in, b_in, conv_w, conv_b, rg_wa, rg_ba, rg_wi, rg_bi, rg_lam, w_rnn_out, w_attn_out, w_o, ln1_g, ln1_b, router_w, router_b, expert_router_w, expert_router_b, w_gate, w_up, w_down, ln2_g, ln2_b):
    bf = jnp.bfloat16
    biases = [_bias_table(rel_bias, g, dil) for g, (_, dil) in enumerate(DILATED_CONFIGS)]
    lp = []
    for l in range(DEPTH):
        rw = jnp.concatenate(
            [router_w[l], jnp.transpose(expert_router_w[l], (1, 0, 2)).reshape(D_MODEL, N_EXPERTS)], axis=1)
        rb = jnp.concatenate([router_b[l], expert_router_b[l].reshape(N_EXPERTS)])
        pad = ROUTE_W - rw.shape[1]
        lp.append(dict(
            w_in=_permute_in_cols(w_in[l]).astype(bf),
            b_in=_permute_in_cols(b_in[l])[None, :],
            conv_w=conv_w[l], conv_b=conv_b[l][None, :],
            wg=_gate_weights(rg_wa[l], rg_wi[l]),
            gb=jnp.concatenate([_per_chunk(rg_ba[l]), _per_chunk(rg_bi[l])], axis=-1),
            lam=_per_chunk(rg_lam[l]),
            w_rnn_out=w_rnn_out[l].astype(bf), w_attn_out=w_attn_out[l].astype(bf), w_o=w_o[l].astype(bf),
            ln1_g=ln1_g[l][None, :], ln1_b=ln1_b[l][None, :],
            router_w=jnp.pad(rw, ((0, 0), (0, pad))), router_b=jnp.pad(rb, (0, pad))[None, :],
            w_gate=w_gate[l].astype(bf), w_up=w_up[l].astype(bf), w_down=w_down[l].astype(bf),
            ln2_g=ln2_g[l][None, :], ln2_b=ln2_b[l][None, :]))
    return (_trunk(x_prompt, biases, lp), _trunk(x_sample, biases, lp))
```

```python
import functools

import numpy as np
import jax
import jax.numpy as jnp
from jax import lax
from jax.experimental import pallas as pl
from jax.experimental.pallas import tpu as pltpu

D_MODEL = 1024
DEPTH = 2
D_RNN = D_MODEL
RNN_BLOCKS = 16
RNN_BLOCK_W = D_RNN // RNN_BLOCKS
CONV_WIDTH = 4
RG_C = 8.0
DILATED_CONFIGS = ((128, 1), (512, 4), (2048, 16))
N_ATTN_GROUPS = len(DILATED_CONFIGS)
HEADS_PER_GROUP = 4
N_ATTN_HEADS = N_ATTN_GROUPS * HEADS_PER_GROUP
HEAD_DIM = 128
D_ATTN = N_ATTN_HEADS * HEAD_DIM
D_ATTN_OUT = HEADS_PER_GROUP * HEAD_DIM
NUM_BUCKETS = 32
MAX_DISTANCE = max(w for w, _ in DILATED_CONFIGS) // 2
NEG_INF = -1e30
D_IN = D_RNN + 3 * D_ATTN + 2 * D_MODEL
N_EXPERT_GROUPS = 4
EXPERTS_PER_GROUP = 8
N_EXPERTS = N_EXPERT_GROUPS * EXPERTS_PER_GROUP
TOP_K = 2
D_EXPERT = D_MODEL // 2
ALPHA = (2 * DEPTH) ** 0.25
LN_EPS = 1e-5

LANES = 128
SUBLANES = 8
VMEM_LIMIT = 56 * 1024 * 1024

COL_XRNN = 0
COL_GRNN = D_RNN
COL_GATTN = D_RNN + D_MODEL
COL_Q = D_RNN + 2 * D_MODEL
COL_K = COL_Q + D_ATTN
COL_V = COL_K + D_ATTN

RADIUS = 64
assert all(w // (2 * d) == RADIUS for w, d in DILATED_CONFIGS)
ATTN_ROWS = 1024
RNN_CHUNK = 256
RNN_TILE = 256
MOE_BLK = 256
ROUTE_W = LANES


def _cparams(n_axes):
    return pltpu.CompilerParams(dimension_semantics=("arbitrary",) * n_axes,
                                vmem_limit_bytes=VMEM_LIMIT)


def _proj_kernel(x_ref, w_ref, b_ref, o_ref):
    acc = jnp.dot(x_ref[...], w_ref[...], preferred_element_type=jnp.float32)
    o_ref[...] = acc + b_ref[...]


def in_projection(x_bf, w_bf, b, tm=512, tn=1536):
    n, k = x_bf.shape
    nout = w_bf.shape[1]
    return pl.pallas_call(
        _proj_kernel,
        grid=(nout // tn, n // tm),
        in_specs=[pl.BlockSpec((tm, k), lambda j, i: (i, 0)),
                  pl.BlockSpec((k, tn), lambda j, i: (0, j)),
                  pl.BlockSpec((1, tn), lambda j, i: (0, j))],
        out_specs=pl.BlockSpec((tm, tn), lambda j, i: (i, j)),
        out_shape=jax.ShapeDtypeStruct((n, nout), jnp.float32),
        compiler_params=_cparams(2),
        name="in_projection",
    )(x_bf, w_bf, b)


def _rglru_kernel(x_ref, cw_ref, cb_ref, wg_ref, gb_ref, lam_ref, o_ref,
                  hsum, a_s, u_s, h_s, *, seq, tile):
    c = RNN_CHUNK
    n_tiles = seq // tile
    n_grp = tile // SUBLANES
    big_rows = tile + 2 * SUBLANES
    row_in_grp = lax.broadcasted_iota(jnp.int32, (tile, c), 0) & (SUBLANES - 1)

    def conv_tile(t0):
        centre = x_ref[0, pl.ds(t0, tile), :]
        p0 = pl.multiple_of(jnp.maximum(t0 - SUBLANES, 0), SUBLANES)
        n0 = pl.multiple_of(jnp.minimum(t0 + tile, seq - SUBLANES), SUBLANES)
        prev = jnp.where(t0 > 0, x_ref[0, pl.ds(p0, SUBLANES), :], 0.0)
        nxt = jnp.where(t0 + tile < seq, x_ref[0, pl.ds(n0, SUBLANES), :], 0.0)
        big = jnp.concatenate([prev, centre, nxt], axis=0)
        xc = cb_ref[...] + cw_ref[1:2, :] * centre
        for k in (0, 2, 3):
            off = k - 1
            tap = pltpu.roll(big, (-off) % big_rows, 0)[SUBLANES:SUBLANES + tile]
            xc = xc + cw_ref[k:k + 1, :] * tap
        return xc

    def gates(xc, dr):
        g = jnp.dot(xc.astype(jnp.bfloat16), wg_ref[0, dr],
                    preferred_element_type=jnp.float32) + gb_ref[0, dr]
        gate_r = jax.nn.sigmoid(g[:, :c])
        gate_i = jax.nn.sigmoid(g[:, c:])
        neg_lam = -lam_ref[0, dr]
        softplus = jnp.maximum(neg_lam, 0.0) + jnp.log1p(jnp.exp(-jnp.abs(neg_lam)))
        log_a = -RG_C * gate_r * softplus
        a = jnp.exp(log_a)
        th = jnp.tanh(log_a)
        one_minus_a2 = -2.0 * th / (1.0 - th)
        u = jnp.sqrt(one_minus_a2) * (gate_i * xc)
        return a, u

    def tile_prefix(a, u, reverse):
        for s in (1, 2, 4):
            if reverse:
                a_sh = pltpu.roll(a, tile - s, 0)
                u_sh = pltpu.roll(u, tile - s, 0)
                m = row_in_grp < SUBLANES - s
            else:
                a_sh = pltpu.roll(a, s, 0)
                u_sh = pltpu.roll(u, s, 0)
                m = row_in_grp >= s
            u = jnp.where(m, u + a * u_sh, u)
            a = jnp.where(m, a * a_sh, a)
        a_s[...] = a
        u_s[...] = u

    def run_direction(dr, reverse):
        def tile_body(ti, h):
            t_idx = (n_tiles - 1 - ti) if reverse else ti
            t0 = pl.multiple_of(t_idx * tile, tile)
            xc = conv_tile(t0)
            a, u = gates(xc, dr)
            tile_prefix(a, u, reverse)

            def grp_body(gi, hc):
                g_idx = (n_grp - 1 - gi) if reverse else gi
                r0 = pl.multiple_of(g_idx * SUBLANES, SUBLANES)
                hg = u_s[pl.ds(r0, SUBLANES), :] + a_s[pl.ds(r0, SUBLANES), :] * hc
                h_s[pl.ds(r0, SUBLANES), :] = hg
                edge = hg[0:1, :] if reverse else hg[SUBLANES - 1:SUBLANES, :]
                return jnp.broadcast_to(edge, (SUBLANES, c))

            h = lax.fori_loop(0, n_grp, grp_body, h, unroll=4)
            if reverse:
                o_ref[0, pl.ds(t0, tile), :] = (hsum[pl.ds(t0, tile), :] + h_s[...]).astype(o_ref.dtype)
            else:
                hsum[pl.ds(t0, tile), :] = h_s[...]
            return h

        lax.fori_loop(0, n_tiles, tile_body, jnp.zeros((SUBLANES, c), jnp.float32))

    run_direction(0, False)
    run_direction(1, True)


def rglru_branch(proj3, cw, cb, wg, gb, lam):
    b, s, _ = proj3.shape
    c = RNN_CHUNK
    n_chunks = D_RNN // c
    kern = functools.partial(_rglru_kernel, seq=s, tile=RNN_TILE)
    return pl.pallas_call(
        kern,
        grid=(b, n_chunks),
        in_specs=[pl.BlockSpec((1, s, c), lambda bi, ci: (bi, 0, COL_XRNN // c + ci)),
                  pl.BlockSpec((CONV_WIDTH, c), lambda bi, ci: (0, ci)),
                  pl.BlockSpec((1, c), lambda bi, ci: (0, ci)),
                  pl.BlockSpec((1, 2, c, 2 * c), lambda bi, ci: (ci, 0, 0, 0)),
                  pl.BlockSpec((1, 2, 1, 2 * c), lambda bi, ci: (ci, 0, 0, 0)),
                  pl.BlockSpec((1, 2, 1, c), lambda bi, ci: (ci, 0, 0, 0))],
        out_specs=pl.BlockSpec((1, s, c), lambda bi, ci: (bi, 0, ci)),
        out_shape=jax.ShapeDtypeStruct((b, s, D_RNN), jnp.bfloat16),
        scratch_shapes=[pltpu.VMEM((s, c), jnp.float32),
                        pltpu.VMEM((RNN_TILE, c), jnp.float32),
                        pltpu.VMEM((RNN_TILE, c), jnp.float32),
                        pltpu.VMEM((RNN_TILE, c), jnp.float32)],
        compiler_params=_cparams(2),
        name="rglru_branch",
    )(proj3, cw, cb, wg, gb, lam)


def _attn_group(refs, bias, o_dst, l_dst, *, dil, is_first, is_last):
    q_ref, kp_ref, kc_ref, kn_ref, vp_ref, vc_ref, vn_ref = refs
    rows = ATTN_ROWS
    per_res = rows // dil
    n_q = per_res // RADIUS
    win = 3 * RADIUS
    nb = dil * n_q

    def fold(ref, r, n):
        if dil == 1:
            return ref[0, pl.ds(r, n), :]
        return ref[0, pl.ds(r, n, stride=dil), :]

    qs, ks, vs = [], [], []
    for r in range(dil):
        qf = fold(q_ref, r, per_res).astype(jnp.bfloat16)
        kf = jnp.concatenate([fold(kp_ref, r, RADIUS), fold(kc_ref, r, per_res),
                              fold(kn_ref, r, RADIUS)], axis=0).astype(jnp.bfloat16)
        vf = jnp.concatenate([fold(vp_ref, r, RADIUS), fold(vc_ref, r, per_res),
                              fold(vn_ref, r, RADIUS)], axis=0).astype(jnp.bfloat16)
        for jq in range(n_q):
            qs.append(qf[jq * RADIUS:(jq + 1) * RADIUS])
            ks.append(kf[jq * RADIUS:jq * RADIUS + win])
            vs.append(vf[jq * RADIUS:jq * RADIUS + win])
    qb, kb, vb = jnp.stack(qs), jnp.stack(ks), jnp.stack(vs)

    shape = (nb, RADIUS, win)
    jq_of = lax.broadcasted_iota(jnp.int32, shape, 0) & (n_q - 1)
    qi = lax.broadcasted_iota(jnp.int32, shape, 1)
    kj = lax.broadcasted_iota(jnp.int32, shape, 2)
    lo = jnp.where(jq_of == 0, jnp.where(is_first, RADIUS, 0), 0)
    hi = jnp.where(jq_of == n_q - 1, jnp.where(is_last, 2 * RADIUS, win), win)
    mask = (jnp.abs(kj - RADIUS - qi) <= RADIUS) & (kj >= lo) & (kj < hi)

    s = jnp.einsum('bqd,bkd->bqk', qb, kb, preferred_element_type=jnp.float32) * (HEAD_DIM ** -0.5)
    s = jnp.where(mask, s + bias[None], NEG_INF)
    m = jnp.max(s, axis=-1, keepdims=True)
    p = jnp.exp(s - m)
    l = jnp.sum(p, axis=-1, keepdims=True)
    o = jnp.einsum('bqk,bkd->bqd', p.astype(jnp.bfloat16), vb, preferred_element_type=jnp.float32) / l
    lse = m + jnp.log(l)
    for bi in range(nb):
        r, jq = divmod(bi, n_q)
        start = jq * RADIUS * dil + r
        idx = pl.ds(start, RADIUS) if dil == 1 else pl.ds(start, RADIUS, stride=dil)
        o_dst[idx, :] = o[bi]
        l_dst[idx, :] = jnp.broadcast_to(lse[bi], (RADIUS, HEAD_DIM))


def _attn_kernel(*refs, n_row_blocks):
    n_g = N_ATTN_GROUPS
    in_refs = refs[:7 * n_g]
    bias_refs = refs[7 * n_g:8 * n_g]
    y_ref = refs[8 * n_g]
    o_s, l_s = refs[8 * n_g + 1:]
    i = pl.program_id(1)
    for g, (_, dil) in enumerate(DILATED_CONFIGS):
        _attn_group(in_refs[7 * g:7 * g + 7], bias_refs[g][0], o_s.at[g], l_s.at[g],
                    dil=dil, is_first=i == 0, is_last=i == n_row_blocks - 1)
    l0, l1, l2 = l_s[0], l_s[1], l_s[2]
    m = jnp.maximum(jnp.maximum(l0, l1), l2)
    e0, e1, e2 = jnp.exp(l0 - m), jnp.exp(l1 - m), jnp.exp(l2 - m)
    y = (e0 * o_s[0] + e1 * o_s[1] + e2 * o_s[2]) / (e0 + e1 + e2)
    y_ref[0] = y.astype(y_ref.dtype)


def dilated_attention(proj3, biases):
    b, s, _ = proj3.shape
    rows = ATTN_ROWS
    nrb = s // rows
    w = HEAD_DIM
    nh = HEADS_PER_GROUP

    def cur_map(col):
        return lambda bi, i, h: (bi, i, col + h)

    in_specs, operands = [], []
    for g, (_, dil) in enumerate(DILATED_CONFIGS):
        halo = RADIUS * dil
        ratio = rows // halo
        n_halo_blocks = s // halo
        qc, kc, vc = COL_Q // w + g * nh, COL_K // w + g * nh, COL_V // w + g * nh

        def prev_map(col, ratio=ratio):
            return lambda bi, i, h: (bi, jnp.maximum(i * ratio - 1, 0), col + h)

        def next_map(col, ratio=ratio, last=n_halo_blocks - 1):
            return lambda bi, i, h: (bi, jnp.minimum((i + 1) * ratio, last), col + h)

        in_specs += [pl.BlockSpec((1, rows, w), cur_map(qc)),
                     pl.BlockSpec((1, halo, w), prev_map(kc)),
                     pl.BlockSpec((1, rows, w), cur_map(kc)),
                     pl.BlockSpec((1, halo, w), next_map(kc)),
                     pl.BlockSpec((1, halo, w), prev_map(vc)),
                     pl.BlockSpec((1, rows, w), cur_map(vc)),
                     pl.BlockSpec((1, halo, w), next_map(vc))]
        operands += [proj3] * 7
    in_specs += [pl.BlockSpec((1, RADIUS, 3 * RADIUS), lambda bi, i, h: (h, 0, 0))] * N_ATTN_GROUPS
    operands += list(biases)
    return pl.pallas_call(
        functools.partial(_attn_kernel, n_row_blocks=nrb),
        grid=(b, nrb, nh),
        in_specs=in_specs,
        out_specs=pl.BlockSpec((1, rows, w), cur_map(0)),
        out_shape=jax.ShapeDtypeStruct((b, s, D_ATTN_OUT), jnp.bfloat16),
        scratch_shapes=[pltpu.VMEM((N_ATTN_GROUPS, rows, w), jnp.float32),
                        pltpu.VMEM((N_ATTN_GROUPS, rows, w), jnp.float32)],
        compiler_params=_cparams(3),
        name="dilated_attention",
    )(*operands)


def _layer_norm(z, g, b):
    mu = jnp.mean(z, axis=-1, keepdims=True)
    zc = z - mu
    var = jnp.mean(zc * zc, axis=-1, keepdims=True)
    return zc * lax.rsqrt(var + LN_EPS) * g + b


def _mix_kernel(x_ref, grnn_ref, gattn_ref, h_ref, y_ref,
                wr_ref, wa_ref, wo_ref, lng_ref, lnb_ref, rw_ref, rb_ref,
                xo_ref, xob_ref, eidx_ref, gate_ref):
    rnn = jnp.dot(h_ref[...], wr_ref[...], preferred_element_type=jnp.float32)
    att = jnp.dot(y_ref[...], wa_ref[...], preferred_element_type=jnp.float32)
    mixed = jax.nn.sigmoid(grnn_ref[...]) * rnn + jax.nn.sigmoid(gattn_ref[...]) * att
    z = ALPHA * x_ref[...] + jnp.dot(mixed.astype(jnp.bfloat16), wo_ref[...],
                                     preferred_element_type=jnp.float32)
    x1 = _layer_norm(z, lng_ref[...], lnb_ref[...])
    xo_ref[...] = x1
    xob_ref[...] = x1.astype(jnp.bfloat16)

    logits = jnp.dot(x1, rw_ref[...], preferred_element_type=jnp.float32,
                     precision=lax.Precision.HIGHEST) + rb_ref[...]
    tm = logits.shape[0]
    lane = lax.broadcasted_iota(jnp.int32, (tm, ROUTE_W), 1)
    lane_f = lane.astype(jnp.float32)
    far = float(ROUTE_W)
    is_group = lane < N_EXPERT_GROUPS
    gl = jnp.where(is_group, logits, -jnp.inf)
    gmax = jnp.max(gl, axis=-1, keepdims=True)
    gsel = jnp.min(jnp.where(gl == gmax, lane_f, far), axis=-1, keepdims=True)
    p_group = 1.0 / jnp.sum(jnp.where(is_group, jnp.exp(logits - gmax), 0.0), axis=-1, keepdims=True)
    lane_group = ((lane - N_EXPERT_GROUPS) >> 3).astype(jnp.float32)
    in_group = (lane >= N_EXPERT_GROUPS) & (lane < N_EXPERT_GROUPS + N_EXPERTS) & (lane_group == gsel)
    el = jnp.where(in_group, logits, -jnp.inf)
    m1 = jnp.max(el, axis=-1, keepdims=True)
    i1 = jnp.min(jnp.where(el == m1, lane_f, far), axis=-1, keepdims=True)
    el2 = jnp.where(lane_f == i1, -jnp.inf, el)
    m2 = jnp.max(el2, axis=-1, keepdims=True)
    i2 = jnp.min(jnp.where(el2 == m2, lane_f, far), axis=-1, keepdims=True)
    e21 = jnp.exp(m2 - m1)
    den = 1.0 + e21
    g1 = p_group * (1.0 / den)
    g2 = p_group * (e21 / den)
    eidx = jnp.where(lane == 0, i1, jnp.where(lane == 1, i2, float(N_EXPERT_GROUPS))) - float(N_EXPERT_GROUPS)
    eidx_ref[...] = eidx.astype(jnp.int32)
    gate_ref[...] = jnp.where(lane == 0, g1, jnp.where(lane == 1, g2, 0.0))


def mix_and_route(x, proj, h_rnn, y_attn, wr, wa, wo, lng, lnb, rw, rb, tm=256):
    n = x.shape[0]
    d = D_MODEL
    w = D_ATTN_OUT
    row = lambda width, col=0: pl.BlockSpec((tm, width), lambda i, c=col: (i, c))
    full = lambda a: pl.BlockSpec(a.shape, lambda i: (0,) * a.ndim)
    return pl.pallas_call(
        _mix_kernel,
        grid=(n // tm,),
        in_specs=[row(d), row(d, COL_GRNN // d), row(d, COL_GATTN // d), row(d), row(w),
                  full(wr), full(wa), full(wo), full(lng), full(lnb), full(rw), full(rb)],
        out_specs=[row(d), row(d), row(ROUTE_W), row(ROUTE_W)],
        out_shape=[jax.ShapeDtypeStruct((n, d), jnp.float32),
                   jax.ShapeDtypeStruct((n, d), jnp.bfloat16),
                   jax.ShapeDtypeStruct((n, ROUTE_W), jnp.int32),
                   jax.ShapeDtypeStruct((n, ROUTE_W), jnp.float32)],
        compiler_params=_cparams(1),
        name="mix_and_route",
    )(x, proj, proj, h_rnn, y_attn, wr, wa, wo, lng, lnb, rw, rb)


def _expert_kernel(be_ref, nused_ref, xb_ref, sw_ref, wg_ref, wu_ref, wd_ref, y_ref):
    i = pl.program_id(0)

    @pl.when(i < nused_ref[0])
    def _():
        xb = xb_ref[...]
        gate = jnp.dot(xb, wg_ref[0], preferred_element_type=jnp.float32)
        up = jnp.dot(xb, wu_ref[0], preferred_element_type=jnp.float32)
        hid = (jax.nn.silu(gate) * up).astype(jnp.bfloat16)
        y = jnp.dot(hid, wd_ref[0], preferred_element_type=jnp.float32)
        y_ref[...] = y * sw_ref[...]

    @pl.when(i >= nused_ref[0])
    def _():
        y_ref[...] = jnp.zeros_like(y_ref)


def expert_blocks(block_e, n_used, xb, slot_w, wg, wu, wd):
    slots, d = xb.shape
    n_blocks = slots // MOE_BLK
    grid_spec = pltpu.PrefetchScalarGridSpec(
        num_scalar_prefetch=2,
        grid=(n_blocks,),
        in_specs=[pl.BlockSpec((MOE_BLK, d), lambda i, be, nu: (i, 0)),
                  pl.BlockSpec((MOE_BLK, 1), lambda i, be, nu: (i, 0)),
                  pl.BlockSpec((1, d, D_EXPERT), lambda i, be, nu: (be[i], 0, 0)),
                  pl.BlockSpec((1, d, D_EXPERT), lambda i, be, nu: (be[i], 0, 0)),
                  pl.BlockSpec((1, D_EXPERT, d), lambda i, be, nu: (be[i], 0, 0))],
        out_specs=pl.BlockSpec((MOE_BLK, d), lambda i, be, nu: (i, 0)),
    )
    return pl.pallas_call(
        _expert_kernel,
        grid_spec=grid_spec,
        out_shape=jax.ShapeDtypeStruct((slots, d), jnp.float32),
        compiler_params=_cparams(1),
        name="expert_blocks",
    )(block_e, n_used, xb, slot_w, wg, wu, wd)


def _ln2_kernel(x_ref, f_ref, g_ref, b_ref, xo_ref, xob_ref):
    ffn = f_ref[:, :D_MODEL] + f_ref[:, D_MODEL:]
    x2 = _layer_norm(ALPHA * x_ref[...] + ffn, g_ref[...], b_ref[...])
    xo_ref[...] = x2
    xob_ref[...] = x2.astype(jnp.bfloat16)


def combine_ln2(x1, ffn2, g, b, tm=256):
    n, d = x1.shape
    return pl.pallas_call(
        _ln2_kernel,
        grid=(n // tm,),
        in_specs=[pl.BlockSpec((tm, d), lambda i: (i, 0)),
                  pl.BlockSpec((tm, TOP_K * d), lambda i: (i, 0)),
                  pl.BlockSpec((1, d), lambda i: (0, 0)),
                  pl.BlockSpec((1, d), lambda i: (0, 0))],
        out_specs=[pl.BlockSpec((tm, d), lambda i: (i, 0)),
                   pl.BlockSpec((tm, d), lambda i: (i, 0))],
        out_shape=[jax.ShapeDtypeStruct((n, d), jnp.float32),
                   jax.ShapeDtypeStruct((n, d), jnp.bfloat16)],
        compiler_params=_cparams(1),
        name="combine_ln2",
    )(x1, ffn2, g, b)


def moe_layer(x1, x1_bf, eidx, gate, wg, wu, wd, ln_g, ln_b):
    n, d = x1.shape
    m = n * TOP_K
    n_blocks = m // MOE_BLK + N_EXPERTS
    slots = n_blocks * MOE_BLK
    flat_e = eidx.reshape(m)
    onehot = (flat_e[:, None] == jnp.arange(N_EXPERTS, dtype=jnp.int32)[None, :]).astype(jnp.int32)
    csum = jnp.cumsum(onehot, axis=0)
    rank = jnp.sum((csum - onehot) * onehot, axis=1)
    counts = csum[-1]
    padded = (counts + MOE_BLK - 1) // MOE_BLK * MOE_BLK
    pend = jnp.cumsum(padded)
    pstart = pend - padded
    dest = (pstart[flat_e] + rank).astype(jnp.int32)
    tok = jnp.arange(m, dtype=jnp.int32) // TOP_K
    slot_tok = jnp.zeros((slots,), jnp.int32).at[dest].set(tok)
    slot_w = jnp.zeros((slots,), jnp.float32).at[dest].set(gate.reshape(m))
    block_e = jnp.minimum(
        jnp.searchsorted(pend, jnp.arange(n_blocks, dtype=jnp.int32) * MOE_BLK, side='right'),
        N_EXPERTS - 1).astype(jnp.int32)
    n_used = (pend[-1:] // MOE_BLK).astype(jnp.int32)
    xb = x1_bf[slot_tok]
    yb = expert_blocks(block_e, n_used, xb, slot_w[:, None], wg, wu, wd)
    ffn2 = yb[dest].reshape(n, TOP_K * d)
    return combine_ln2(x1, ffn2, ln_g, ln_b)


def _t5_bucket(rel):
    half = NUM_BUCKETS // 2
    max_exact = half // 2
    n = np.abs(rel)
    large = max_exact + (np.log(np.maximum(n, 1) / max_exact) / np.log(MAX_DISTANCE / max_exact)
                         * (half - max_exact)).astype(np.int32)
    large = np.minimum(large, half - 1)
    return np.where(rel > 0, half, 0) + np.where(n < max_exact, n, large)


def _bias_table(rel_bias, g, dil):
    rel = (np.arange(3 * RADIUS)[None, :] - RADIUS - np.arange(RADIUS)[:, None]) * dil
    hs = slice(g * HEADS_PER_GROUP, (g + 1) * HEADS_PER_GROUP)
    return jnp.transpose(rel_bias[_t5_bucket(rel), hs], (2, 0, 1)).astype(jnp.float32)


def _chunk_block_diag(w):
    per = RNN_CHUNK // RNN_BLOCK_W
    n_chunks = RNN_BLOCKS // per
    w4 = w.reshape(n_chunks, per, RNN_BLOCK_W, RNN_BLOCK_W)
    dense = jnp.einsum('chij,hk->chikj', w4, jnp.eye(per, dtype=w.dtype))
    return dense.reshape(n_chunks, RNN_CHUNK, RNN_CHUNK)


def _gate_weights(wa, wi):
    per_dir = [jnp.concatenate([_chunk_block_diag(wa[d]), _chunk_block_diag(wi[d])], axis=-1)
               for d in range(2)]
    return jnp.stack(per_dir, axis=1).astype(jnp.bfloat16)


def _per_chunk(v):
    return v.reshape(2, D_RNN // RNN_CHUNK, 1, RNN_CHUNK).transpose(1, 0, 2, 3)


def _permute_in_cols(a):
    x_rnn, q, k, v, g_rnn, g_attn = jnp.split(
        a, [D_RNN, D_RNN + D_ATTN, D_RNN + 2 * D_ATTN, D_RNN + 3 * D_ATTN,
            D_RNN + 3 * D_ATTN + D_MODEL], axis=-1)
    return jnp.concatenate([x_rnn, g_rnn, g_attn, q, k, v], axis=-1)


def _trunk(x, biases, lp):
    b, s, d = x.shape
    n = b * s
    x2 = x.reshape(n, d)
    x2_bf = x2.astype(jnp.bfloat16)
    for l in range(DEPTH):
        p = lp[l]
        proj = in_projection(x2_bf, p['w_in'], p['b_in'])
        proj3 = proj.reshape(b, s, D_IN)
        h_rnn = rglru_branch(proj3, p['conv_w'], p['conv_b'], p['wg'], p['gb'], p['lam'])
        y_attn = dilated_attention(proj3, biases)
        x1, x1_bf, eidx, gate = mix_and_route(
            x2, proj, h_rnn.reshape(n, D_RNN), y_attn.reshape(n, D_ATTN_OUT),
            p['w_rnn_out'], p['w_attn_out'], p['w_o'],
            p['ln1_g'], p['ln1_b'], p['router_w'], p['router_b'])
        x2, x2_bf = moe_layer(x1, x1_bf, eidx[:, :TOP_K], gate[:, :TOP_K],
                              p['w_gate'], p['w_up'], p['w_down'], p['ln2_g'], p['ln2_b'])
    return x2.reshape(b, s, d)


def kernel(x_prompt, x_sample, rel_bias, w_in, b_in, conv_w, conv_b, rg_wa, rg_ba, rg_wi, rg_bi, rg_lam, w_rnn_out, w_attn_out, w_o, ln1_g, ln1_b, router_w, router_b, expert_router_w, expert_router_b, w_gate, w_up, w_down, ln2_g, ln2_b):
    bf = jnp.bfloat16
    biases = [_bias_table(rel_bias, g, dil) for g, (_, dil) in enumerate(DILATED_CONFIGS)]
    lp = []
    for l in range(DEPTH):
        rw = jnp.concatenate(
            [router_w[l], jnp.transpose(expert_router_w[l], (1, 0, 2)).reshape(D_MODEL, N_EXPERTS)], axis=1)
        rb = jnp.concatenate([router_b[l], expert_router_b[l].reshape(N_EXPERTS)])
        pad = ROUTE_W - rw.shape[1]
        lp.append(dict(
            w_in=_permute_in_cols(w_in[l]).astype(bf),
            b_in=_permute_in_cols(b_in[l])[None, :],
            conv_w=conv_w[l], conv_b=conv_b[l][None, :],
            wg=_gate_weights(rg_wa[l], rg_wi[l]),
            gb=jnp.concatenate([_per_chunk(rg_ba[l]), _per_chunk(rg_bi[l])], axis=-1),
            lam=_per_chunk(rg_lam[l]),
            w_rnn_out=w_rnn_out[l].astype(bf), w_attn_out=w_attn_out[l].astype(bf), w_o=w_o[l].astype(bf),
            ln1_g=ln1_g[l][None, :], ln1_b=ln1_b[l][None, :],
            router_w=jnp.pad(rw, ((0, 0), (0, pad))), router_b=jnp.pad(rb, (0, pad))[None, :],
            w_gate=w_gate[l].astype(bf), w_up=w_up[l].astype(bf), w_down=w_down[l].astype(bf),
            ln2_g=ln2_g[l][None, :], ln2_b=ln2_b[l][None, :]))
    return (_trunk(x_prompt, biases, lp), _trunk(x_sample, biases, lp))
```

```python
import functools

import numpy as np
import jax
import jax.numpy as jnp
from jax import lax
from jax.experimental import pallas as pl
from jax.experimental.pallas import tpu as pltpu

D_MODEL = 1024
DEPTH = 2
D_RNN = D_MODEL
RNN_BLOCKS = 16
RNN_BLOCK_W = D_RNN // RNN_BLOCKS
CONV_WIDTH = 4
RG_C = 8.0
DILATED_CONFIGS = ((128, 1), (512, 4), (2048, 16))
N_ATTN_GROUPS = len(DILATED_CONFIGS)
HEADS_PER_GROUP = 4
N_ATTN_HEADS = N_ATTN_GROUPS * HEADS_PER_GROUP
HEAD_DIM = 128
D_ATTN = N_ATTN_HEADS * HEAD_DIM
D_ATTN_OUT = HEADS_PER_GROUP * HEAD_DIM
NUM_BUCKETS = 32
MAX_DISTANCE = max(w for w, _ in DILATED_CONFIGS) // 2
NEG_INF = -1e30
D_IN = D_RNN + 3 * D_ATTN + 2 * D_MODEL
N_EXPERT_GROUPS = 4
EXPERTS_PER_GROUP = 8
N_EXPERTS = N_EXPERT_GROUPS * EXPERTS_PER_GROUP
TOP_K = 2
D_EXPERT = D_MODEL // 2
ALPHA = (2 * DEPTH) ** 0.25
LN_EPS = 1e-5

LANES = 128
SUBLANES = 8
VMEM_LIMIT = 56 * 1024 * 1024

COL_XRNN = 0
COL_GRNN = D_RNN
COL_GATTN = D_RNN + D_MODEL
COL_Q = D_RNN + 2 * D_MODEL
COL_K = COL_Q + D_ATTN
COL_V = COL_K + D_ATTN

RADIUS = 64
assert all(w // (2 * d) == RADIUS for w, d in DILATED_CONFIGS)
ATTN_ROWS = 1024
RNN_CHUNK = 256
RNN_TILE = 256
MOE_BLK = 256
ROUTE_W = LANES


def _cparams(n_axes):
    return pltpu.CompilerParams(dimension_semantics=("arbitrary",) * n_axes,
                                vmem_limit_bytes=VMEM_LIMIT)


def _proj_kernel(x_ref, w_ref, b_ref, o_ref):
    acc = jnp.dot(x_ref[...], w_ref[...], preferred_element_type=jnp.float32)
    o_ref[...] = acc + b_ref[...]


def in_projection(x_bf, w_bf, b, tm=512, tn=1536):
    n, k = x_bf.shape
    nout = w_bf.shape[1]
    return pl.pallas_call(
        _proj_kernel,
        grid=(nout // tn, n // tm),
        in_specs=[pl.BlockSpec((tm, k), lambda j, i: (i, 0)),
                  pl.BlockSpec((k, tn), lambda j, i: (0, j)),
                  pl.BlockSpec((1, tn), lambda j, i: (0, j))],
        out_specs=pl.BlockSpec((tm, tn), lambda j, i: (i, j)),
        out_shape=jax.ShapeDtypeStruct((n, nout), jnp.float32),
        compiler_params=_cparams(2),
        name="in_projection",
    )(x_bf, w_bf, b)


def _rglru_kernel(x_ref, cw_ref, cb_ref, wg_ref, gb_ref, lam_ref, o_ref,
                  hsum, a_s, u_s, h_s, *, seq, tile):
    c = RNN_CHUNK
    n_tiles = seq // tile
    n_grp = tile // SUBLANES
    big_rows = tile + 2 * SUBLANES
    row_in_grp = lax.broadcasted_iota(jnp.int32, (tile, c), 0) & (SUBLANES - 1)

    def conv_tile(t0):
        centre = x_ref[0, pl.ds(t0, tile), :]
        p0 = pl.multiple_of(jnp.maximum(t0 - SUBLANES, 0), SUBLANES)
        n0 = pl.multiple_of(jnp.minimum(t0 + tile, seq - SUBLANES), SUBLANES)
        prev = jnp.where(t0 > 0, x_ref[0, pl.ds(p0, SUBLANES), :], 0.0)
        nxt = jnp.where(t0 + tile < seq, x_ref[0, pl.ds(n0, SUBLANES), :], 0.0)
        big = jnp.concatenate([prev, centre, nxt], axis=0)
        xc = cb_ref[...] + cw_ref[1:2, :] * centre
        for k in (0, 2, 3):
            off = k - 1
            tap = pltpu.roll(big, (-off) % big_rows, 0)[SUBLANES:SUBLANES + tile]
            xc = xc + cw_ref[k:k + 1, :] * tap
        return xc

    def gates(xc, dr):
        g = jnp.dot(xc.astype(jnp.bfloat16), wg_ref[0, dr],
                    preferred_element_type=jnp.float32) + gb_ref[0, dr]
        gate_r = jax.nn.sigmoid(g[:, :c])
        gate_i = jax.nn.sigmoid(g[:, c:])
        neg_lam = -lam_ref[0, dr]
        softplus = jnp.maximum(neg_lam, 0.0) + jnp.log1p(jnp.exp(-jnp.abs(neg_lam)))
        log_a = -RG_C * gate_r * softplus
        a = jnp.exp(log_a)
        th = jnp.tanh(log_a)
        one_minus_a2 = -2.0 * th / (1.0 - th)
        u = jnp.sqrt(one_minus_a2) * (gate_i * xc)
        return a, u

    def tile_prefix(a, u, reverse):
        for s in (1, 2, 4):
            if reverse:
                a_sh = pltpu.roll(a, tile - s, 0)
                u_sh = pltpu.roll(u, tile - s, 0)
                m = row_in_grp < SUBLANES - s
            else:
                a_sh = pltpu.roll(a, s, 0)
                u_sh = pltpu.roll(u, s, 0)
                m = row_in_grp >= s
            u = jnp.where(m, u + a * u_sh, u)
            a = jnp.where(m, a * a_sh, a)
        a_s[...] = a
        u_s[...] = u

    def run_direction(dr, reverse):
        def tile_body(ti, h):
            t_idx = (n_tiles - 1 - ti) if reverse else ti
            t0 = pl.multiple_of(t_idx * tile, tile)
            xc = conv_tile(t0)
            a, u = gates(xc, dr)
            tile_prefix(a, u, reverse)

            def grp_body(gi, hc):
                g_idx = (n_grp - 1 - gi) if reverse else gi
                r0 = pl.multiple_of(g_idx * SUBLANES, SUBLANES)
                hg = u_s[pl.ds(r0, SUBLANES), :] + a_s[pl.ds(r0, SUBLANES), :] * hc
                h_s[pl.ds(r0, SUBLANES), :] = hg
                edge = hg[0:1, :] if reverse else hg[SUBLANES - 1:SUBLANES, :]
                return jnp.broadcast_to(edge, (SUBLANES, c))

            h = lax.fori_loop(0, n_grp, grp_body, h, unroll=4)
            if reverse:
                o_ref[0, pl.ds(t0, tile), :] = (hsum[pl.ds(t0, tile), :] + h_s[...]).astype(o_ref.dtype)
            else:
                hsum[pl.ds(t0, tile), :] = h_s[...]
            return h

        lax.fori_loop(0, n_tiles, tile_body, jnp.zeros((SUBLANES, c), jnp.float32))

    run_direction(0, False)
    run_direction(1, True)


def rglru_branch(proj3, cw, cb, wg, gb, lam):
    b, s, _ = proj3.shape
    c = RNN_CHUNK
    n_chunks = D_RNN // c
    kern = functools.partial(_rglru_kernel, seq=s, tile=RNN_TILE)
    return pl.pallas_call(
        kern,
        grid=(b, n_chunks),
        in_specs=[pl.BlockSpec((1, s, c), lambda bi, ci: (bi, 0, COL_XRNN // c + ci)),
                  pl.BlockSpec((CONV_WIDTH, c), lambda bi, ci: (0, ci)),
                  pl.BlockSpec((1, c), lambda bi, ci: (0, ci)),
                  pl.BlockSpec((1, 2, c, 2 * c), lambda bi, ci: (ci, 0, 0, 0)),
                  pl.BlockSpec((1, 2, 1, 2 * c), lambda bi, ci: (ci, 0, 0, 0)),
                  pl.BlockSpec((1, 2, 1, c), lambda bi, ci: (ci, 0, 0, 0))],
        out_specs=pl.BlockSpec((1, s, c), lambda bi, ci: (bi, 0, ci)),
        out_shape=jax.ShapeDtypeStruct((b, s, D_RNN), jnp.bfloat16),
        scratch_shapes=[pltpu.VMEM((s, c), jnp.float32),
                        pltpu.VMEM((RNN_TILE, c), jnp.float32),
                        pltpu.VMEM((RNN_TILE, c), jnp.float32),
                        pltpu.VMEM((RNN_TILE, c), jnp.float32)],
        compiler_params=_cparams(2),
        name="rglru_branch",
    )(proj3, cw, cb, wg, gb, lam)


def _attn_group(refs, bias, o_dst, l_dst, *, dil, is_first, is_last):
    q_ref, kp_ref, kc_ref, kn_ref, vp_ref, vc_ref, vn_ref = refs
    rows = ATTN_ROWS
    per_res = rows // dil
    n_q = per_res // RADIUS
    win = 3 * RADIUS
    nb = dil * n_q

    def fold(ref, r, n):
        if dil == 1:
            return ref[0, pl.ds(r, n), :]
        return ref[0, pl.ds(r, n, stride=dil), :]

    qs, ks, vs = [], [], []
    for r in range(dil):
        qf = fold(q_ref, r, per_res).astype(jnp.bfloat16)
        kf = jnp.concatenate([fold(kp_ref, r, RADIUS), fold(kc_ref, r, per_res),
                              fold(kn_ref, r, RADIUS)], axis=0).astype(jnp.bfloat16)
        vf = jnp.concatenate([fold(vp_ref, r, RADIUS), fold(vc_ref, r, per_res),
                              fold(vn_ref, r, RADIUS)], axis=0).astype(jnp.bfloat16)
        for jq in range(n_q):
            qs.append(qf[jq * RADIUS:(jq + 1) * RADIUS])
            ks.append(kf[jq * RADIUS:jq * RADIUS + win])
            vs.append(vf[jq * RADIUS:jq * RADIUS + win])
    qb, kb, vb = jnp.stack(qs), jnp.stack(ks), jnp.stack(vs)

    shape = (nb, RADIUS, win)
    jq_of = lax.broadcasted_iota(jnp.int32, shape, 0) & (n_q - 1)
    qi = lax.broadcasted_iota(jnp.int32, shape, 1)
    kj = lax.broadcasted_iota(jnp.int32, shape, 2)
    lo = jnp.where(jq_of == 0, jnp.where(is_first, RADIUS, 0), 0)
    hi = jnp.where(jq_of == n_q - 1, jnp.where(is_last, 2 * RADIUS, win), win)
    mask = (jnp.abs(kj - RADIUS - qi) <= RADIUS) & (kj >= lo) & (kj < hi)

    s = jnp.einsum('bqd,bkd->bqk', qb, kb, preferred_element_type=jnp.float32) * (HEAD_DIM ** -0.5)
    s = jnp.where(mask, s + bias[None], NEG_INF)
    m = jnp.max(s, axis=-1, keepdims=True)
    p = jnp.exp(s - m)
    l = jnp.sum(p, axis=-1, keepdims=True)
    o = jnp.einsum('bqk,bkd->bqd', p.astype(jnp.bfloat16), vb, preferred_element_type=jnp.float32) / l
    lse = m + jnp.log(l)
    for bi in range(nb):
        r, jq = divmod(bi, n_q)
        start = jq * RADIUS * dil + r
        idx = pl.ds(start, RADIUS) if dil == 1 else pl.ds(start, RADIUS, stride=dil)
        o_dst[idx, :] = o[bi]
        l_dst[idx, :] = jnp.broadcast_to(lse[bi], (RADIUS, HEAD_DIM))


def _attn_kernel(*refs, n_row_blocks):
    n_g = N_ATTN_GROUPS
    in_refs = refs[:7 * n_g]
    bias_refs = refs[7 * n_g:8 * n_g]
    y_ref = refs[8 * n_g]
    o_s, l_s = refs[8 * n_g + 1:]
    i = pl.program_id(1)
    for g, (_, dil) in enumerate(DILATED_CONFIGS):
        _attn_group(in_refs[7 * g:7 * g + 7], bias_refs[g][0], o_s.at[g], l_s.at[g],
                    dil=dil, is_first=i == 0, is_last=i == n_row_blocks - 1)
    l0, l1, l2 = l_s[0], l_s[1], l_s[2]
    m = jnp.maximum(jnp.maximum(l0, l1), l2)
    e0, e1, e2 = jnp.exp(l0 - m), jnp.exp(l1 - m), jnp.exp(l2 - m)
    y = (e0 * o_s[0] + e1 * o_s[1] + e2 * o_s[2]) / (e0 + e1 + e2)
    y_ref[0] = y.astype(y_ref.dtype)


def dilated_attention(proj3, biases):
    b, s, _ = proj3.shape
    rows = ATTN_ROWS
    nrb = s // rows
    w = HEAD_DIM
    nh = HEADS_PER_GROUP

    def cur_map(col):
        return lambda bi, i, h: (bi, i, col + h)

    in_specs, operands = [], []
    for g, (_, dil) in enumerate(DILATED_CONFIGS):
        halo = RADIUS * dil
        ratio = rows // halo
        n_halo_blocks = s // halo
        qc, kc, vc = COL_Q // w + g * nh, COL_K // w + g * nh, COL_V // w + g * nh

        def prev_map(col, ratio=ratio):
            return lambda bi, i, h: (bi, jnp.maximum(i * ratio - 1, 0), col + h)

        def next_map(col, ratio=ratio, last=n_halo_blocks - 1):
            return lambda bi, i, h: (bi, jnp.minimum((i + 1) * ratio, last), col + h)

        in_specs += [pl.BlockSpec((1, rows, w), cur_map(qc)),
                     pl.BlockSpec((1, halo, w), prev_map(kc)),
                     pl.BlockSpec((1, rows, w), cur_map(kc)),
                     pl.BlockSpec((1, halo, w), next_map(kc)),
                     pl.BlockSpec((1, halo, w), prev_map(vc)),
                     pl.BlockSpec((1, rows, w), cur_map(vc)),
                     pl.BlockSpec((1, halo, w), next_map(vc))]
        operands += [proj3] * 7
    in_specs += [pl.BlockSpec((1, RADIUS, 3 * RADIUS), lambda bi, i, h: (h, 0, 0))] * N_ATTN_GROUPS
    operands += list(biases)
    return pl.pallas_call(
        functools.partial(_attn_kernel, n_row_blocks=nrb),
        grid=(b, nrb, nh),
        in_specs=in_specs,
        out_specs=pl.BlockSpec((1, rows, w), cur_map(0)),
        out_shape=jax.ShapeDtypeStruct((b, s, D_ATTN_OUT), jnp.bfloat16),
        scratch_shapes=[pltpu.VMEM((N_ATTN_GROUPS, rows, w), jnp.float32),
                        pltpu.VMEM((N_ATTN_GROUPS, rows, w), jnp.float32)],
        compiler_params=_cparams(3),
        name="dilated_attention",
    )(*operands)


def _layer_norm(z, g, b):
    mu = jnp.mean(z, axis=-1, keepdims=True)
    zc = z - mu
    var = jnp.mean(zc * zc, axis=-1, keepdims=True)
    return zc * lax.rsqrt(var + LN_EPS) * g + b


def _mix_kernel(x_ref, grnn_ref, gattn_ref, h_ref, y_ref,
                wr_ref, wa_ref, wo_ref, lng_ref, lnb_ref, rw_ref, rb_ref,
                xo_ref, eidx_ref, gate_ref):
    rnn = jnp.dot(h_ref[...], wr_ref[...], preferred_element_type=jnp.float32)
    att = jnp.dot(y_ref[...], wa_ref[...], preferred_element_type=jnp.float32)
    mixed = jax.nn.sigmoid(grnn_ref[...]) * rnn + jax.nn.sigmoid(gattn_ref[...]) * att
    z = ALPHA * x_ref[...] + jnp.dot(mixed.astype(jnp.bfloat16), wo_ref[...],
                                     preferred_element_type=jnp.float32)
    x1 = _layer_norm(z, lng_ref[...], lnb_ref[...])
    xo_ref[...] = x1

    logits = jnp.dot(x1, rw_ref[...], preferred_element_type=jnp.float32,
                     precision=lax.Precision.HIGHEST) + rb_ref[...]
    tm = logits.shape[0]
    lane = lax.broadcasted_iota(jnp.int32, (tm, ROUTE_W), 1)
    lane_f = lane.astype(jnp.float32)
    far = float(ROUTE_W)
    is_group = lane < N_EXPERT_GROUPS
    gl = jnp.where(is_group, logits, -jnp.inf)
    gmax = jnp.max(gl, axis=-1, keepdims=True)
    gsel = jnp.min(jnp.where(gl == gmax, lane_f, far), axis=-1, keepdims=True)
    p_group = 1.0 / jnp.sum(jnp.where(is_group, jnp.exp(logits - gmax), 0.0), axis=-1, keepdims=True)
    lane_group = ((lane - N_EXPERT_GROUPS) >> 3).astype(jnp.float32)
    in_group = (lane >= N_EXPERT_GROUPS) & (lane < N_EXPERT_GROUPS + N_EXPERTS) & (lane_group == gsel)
    el = jnp.where(in_group, logits, -jnp.inf)
    m1 = jnp.max(el, axis=-1, keepdims=True)
    i1 = jnp.min(jnp.where(el == m1, lane_f, far), axis=-1, keepdims=True)
    el2 = jnp.where(lane_f == i1, -jnp.inf, el)
    m2 = jnp.max(el2, axis=-1, keepdims=True)
    i2 = jnp.min(jnp.where(el2 == m2, lane_f, far), axis=-1, keepdims=True)
    e21 = jnp.exp(m2 - m1)
    den = 1.0 + e21
    g1 = p_group * (1.0 / den)
    g2 = p_group * (e21 / den)
    eidx = jnp.where(lane == 0, i1, jnp.where(lane == 1, i2, float(N_EXPERT_GROUPS))) - float(N_EXPERT_GROUPS)
    eidx_ref[...] = eidx.astype(jnp.int32)
    gate_ref[...] = jnp.where(lane == 0, g1, jnp.where(lane == 1, g2, 0.0))


def mix_and_route(x, proj, h_rnn, y_attn, wr, wa, wo, lng, lnb, rw, rb, tm=256):
    n = x.shape[0]
    d = D_MODEL
    w = D_ATTN_OUT
    row = lambda width, col=0: pl.BlockSpec((tm, width), lambda i, c=col: (i, c))
    full = lambda a: pl.BlockSpec(a.shape, lambda i: (0,) * a.ndim)
    return pl.pallas_call(
        _mix_kernel,
        grid=(n // tm,),
        in_specs=[row(d), row(d, COL_GRNN // d), row(d, COL_GATTN // d), row(d), row(w),
                  full(wr), full(wa), full(wo), full(lng), full(lnb), full(rw), full(rb)],
        out_specs=[row(d), row(ROUTE_W), row(ROUTE_W)],
        out_shape=[jax.ShapeDtypeStruct((n, d), jnp.float32),
                   jax.ShapeDtypeStruct((n, ROUTE_W), jnp.int32),
                   jax.ShapeDtypeStruct((n, ROUTE_W), jnp.float32)],
        compiler_params=_cparams(1),
        name="mix_and_route",
    )(x, proj, proj, h_rnn, y_attn, wr, wa, wo, lng, lnb, rw, rb)


def _rank_kernel(e_ref, dest_ref, cnt_ref, carry, *, tm):
    ph = pl.program_id(0)
    i = pl.program_id(1)
    lanes = ROUTE_W

    @pl.when((ph == 0) & (i == 0))
    def _():
        carry[...] = jnp.zeros_like(carry)

    @pl.when((ph == 1) & (i == 0))
    def _():
        cnt = carry[...]
        cnt_ref[...] = jnp.broadcast_to(cnt, (lanes, lanes)).astype(jnp.int32)
        padded = jnp.floor((cnt + (MOE_BLK - 1)) * (1.0 / MOE_BLK)) * MOE_BLK
        r = lax.broadcasted_iota(jnp.int32, (lanes, lanes), 0)
        c = lax.broadcasted_iota(jnp.int32, (lanes, lanes), 1)
        lower = (c < r).astype(jnp.float32)
        pstart = jnp.dot(lower, jnp.broadcast_to(padded, (lanes, lanes)),
                         preferred_element_type=jnp.float32, precision=lax.Precision.HIGHEST)
        carry[...] = pstart[:, 0:1]

    r8 = lax.broadcasted_iota(jnp.int32, (SUBLANES, lanes), 0)
    c8 = lax.broadcasted_iota(jnp.int32, (SUBLANES, lanes), 1)
    pick = ((r8 == c8) & (r8 < TOP_K)).astype(jnp.bfloat16)
    ef = e_ref[...].astype(jnp.float32).astype(jnp.bfloat16)
    et = lax.dot_general(pick, ef, (((1,), (1,)), ((), ())), preferred_element_type=jnp.float32)
    e0, e1 = et[0:1, :], et[1:2, :]
    sub = lax.broadcasted_iota(jnp.int32, (lanes, tm), 0).astype(jnp.float32)
    is0, is1 = sub == e0, sub == e1
    member = jnp.where(is0 | is1, 1.0, 0.0)
    s_i = lax.broadcasted_iota(jnp.int32, (tm, tm), 0)
    t_i = lax.broadcasted_iota(jnp.int32, (tm, tm), 1)
    earlier = (s_i < t_i).astype(jnp.bfloat16)
    before = jnp.dot(member.astype(jnp.bfloat16), earlier, preferred_element_type=jnp.float32)
    pos = before + carry[...]
    d0 = jnp.sum(jnp.where(is0, pos, 0.0), axis=0, keepdims=True)
    d1 = jnp.sum(jnp.where(is1, pos, 0.0), axis=0, keepdims=True)
    row = lax.broadcasted_iota(jnp.int32, (SUBLANES, tm), 0)
    dest_ref[...] = jnp.where(row == 0, d0, jnp.where(row == 1, d1, 0.0)).astype(jnp.int32)
    carry[...] = carry[...] + jnp.sum(member, axis=1, keepdims=True)


def route_slots(eidx, tm=512):
    n = eidx.shape[0]
    nt = n // tm
    return pl.pallas_call(
        functools.partial(_rank_kernel, tm=tm),
        grid=(2, nt),
        in_specs=[pl.BlockSpec((tm, ROUTE_W), lambda ph, i: (i, 0))],
        out_specs=[pl.BlockSpec((SUBLANES, tm), lambda ph, i: (0, i * ph)),
                   pl.BlockSpec((ROUTE_W, ROUTE_W), lambda ph, i: (0, 0))],
        out_shape=[jax.ShapeDtypeStruct((SUBLANES, n), jnp.int32),
                   jax.ShapeDtypeStruct((ROUTE_W, ROUTE_W), jnp.int32)],
        scratch_shapes=[pltpu.VMEM((ROUTE_W, 1), jnp.float32)],
        compiler_params=_cparams(2),
        name="route_slots",
    )(eidx)


def _slot_map_kernel(dest_ref, slot_ref, *, n, slots):
    def init(j, c):
        slot_ref[j] = 0
        return c

    lax.fori_loop(0, slots, init, 0)

    def body(t, c):
        slot_ref[dest_ref[t]] = t
        slot_ref[dest_ref[n + t]] = t
        return c

    lax.fori_loop(0, n, body, 0)


def slot_map(dest_flat, n, slots):
    return pl.pallas_call(
        functools.partial(_slot_map_kernel, n=n, slots=slots),
        in_specs=[pl.BlockSpec(memory_space=pltpu.SMEM)],
        out_specs=pl.BlockSpec(memory_space=pltpu.SMEM),
        out_shape=jax.ShapeDtypeStruct((slots,), jnp.int32),
        name="slot_map",
    )(dest_flat)


def _expert_kernel(be_ref, nused_ref, st_ref, x_hbm, wg_ref, wu_ref, wd_ref, y_ref, xbuf, sem):
    i = pl.program_id(0)
    n_used = nused_ref[0]

    def row_copy(tok, slot, j):
        return pltpu.make_async_copy(x_hbm.at[pl.ds(tok, 1)], xbuf.at[slot, pl.ds(j, 1)], sem.at[slot])

    def gather_block(blk, slot):
        base = blk * MOE_BLK

        def body(j, c):
            row_copy(st_ref[base + j], slot, j).start()
            return c

        lax.fori_loop(0, MOE_BLK, body, 0, unroll=8)

    @pl.when(i == 0)
    def _():
        gather_block(0, 0)

    @pl.when(i + 1 < n_used)
    def _():
        gather_block(i + 1, (i + 1) % 2)

    @pl.when(i < n_used)
    def _():
        slot = i % 2
        pltpu.make_async_copy(x_hbm.at[pl.ds(0, MOE_BLK)], xbuf.at[slot], sem.at[slot]).wait()
        xb = xbuf[slot].astype(jnp.bfloat16)
        gate = jnp.dot(xb, wg_ref[0], preferred_element_type=jnp.float32)
        up = jnp.dot(xb, wu_ref[0], preferred_element_type=jnp.float32)
        hid = (jax.nn.silu(gate) * up).astype(jnp.bfloat16)
        y_ref[...] = jnp.dot(hid, wd_ref[0], preferred_element_type=jnp.float32)

    @pl.when(i >= n_used)
    def _():
        y_ref[...] = jnp.zeros_like(y_ref)


def expert_blocks(block_e, n_used, slot_tok, x1, wg, wu, wd):
    n, d = x1.shape
    slots = slot_tok.shape[0]
    n_blocks = slots // MOE_BLK
    grid_spec = pltpu.PrefetchScalarGridSpec(
        num_scalar_prefetch=3,
        grid=(n_blocks,),
        in_specs=[pl.BlockSpec(memory_space=pl.ANY),
                  pl.BlockSpec((1, d, D_EXPERT), lambda i, be, nu, st: (be[i], 0, 0)),
                  pl.BlockSpec((1, d, D_EXPERT), lambda i, be, nu, st: (be[i], 0, 0)),
                  pl.BlockSpec((1, D_EXPERT, d), lambda i, be, nu, st: (be[i], 0, 0))],
        out_specs=pl.BlockSpec((MOE_BLK, d), lambda i, be, nu, st: (i, 0)),
        scratch_shapes=[pltpu.VMEM((2, MOE_BLK, d), jnp.float32),
                        pltpu.SemaphoreType.DMA((2,))],
    )
    return pl.pallas_call(
        _expert_kernel,
        grid_spec=grid_spec,
        out_shape=jax.ShapeDtypeStruct((slots, d), jnp.float32),
        compiler_params=_cparams(1),
        name="expert_blocks",
    )(block_e, n_used, slot_tok, x1, wg, wu, wd)


def _combine_kernel(dest_ref, x_ref, gate_ref, y_hbm, g_ref, b_ref, xo_ref, xob_ref, fbuf, sem, *, n, tm):
    i = pl.program_id(0)
    nt = pl.num_programs(0)

    def gather_tile(tile, slot):
        base = tile * tm

        def body(t, c):
            for k in range(TOP_K):
                src = dest_ref[k * n + base + t]
                pltpu.make_async_copy(y_hbm.at[pl.ds(src, 1)], fbuf.at[slot, pl.ds(k * tm + t, 1)],
                                      sem.at[slot]).start()
            return c

        lax.fori_loop(0, tm, body, 0, unroll=4)

    @pl.when(i == 0)
    def _():
        gather_tile(0, 0)

    @pl.when(i + 1 < nt)
    def _():
        gather_tile(i + 1, (i + 1) % 2)

    slot = i % 2
    pltpu.make_async_copy(y_hbm.at[pl.ds(0, TOP_K * tm)], fbuf.at[slot], sem.at[slot]).wait()
    ffn = gate_ref[:, 0:1] * fbuf[slot, 0:tm, :] + gate_ref[:, 1:2] * fbuf[slot, tm:2 * tm, :]
    x2 = _layer_norm(ALPHA * x_ref[...] + ffn, g_ref[...], b_ref[...])
    xo_ref[...] = x2
    xob_ref[...] = x2.astype(jnp.bfloat16)


def combine_ln2(dest_flat, x1, gate, yb, g, b, tm=256):
    n, d = x1.shape
    grid_spec = pltpu.PrefetchScalarGridSpec(
        num_scalar_prefetch=1,
        grid=(n // tm,),
        in_specs=[pl.BlockSpec((tm, d), lambda i, ds: (i, 0)),
                  pl.BlockSpec((tm, ROUTE_W), lambda i, ds: (i, 0)),
                  pl.BlockSpec(memory_space=pl.ANY),
                  pl.BlockSpec((1, d), lambda i, ds: (0, 0)),
                  pl.BlockSpec((1, d), lambda i, ds: (0, 0))],
        out_specs=[pl.BlockSpec((tm, d), lambda i, ds: (i, 0)),
                   pl.BlockSpec((tm, d), lambda i, ds: (i, 0))],
        scratch_shapes=[pltpu.VMEM((2, TOP_K * tm, d), jnp.float32),
                        pltpu.SemaphoreType.DMA((2,))],
    )
    return pl.pallas_call(
        functools.partial(_combine_kernel, n=n, tm=tm),
        grid_spec=grid_spec,
        out_shape=[jax.ShapeDtypeStruct((n, d), jnp.float32),
                   jax.ShapeDtypeStruct((n, d), jnp.bfloat16)],
        compiler_params=_cparams(1),
        name="combine_ln2",
    )(dest_flat, x1, gate, yb, g, b)


def moe_layer(x1, eidx, gate, wg, wu, wd, ln_g, ln_b):
    n, d = x1.shape
    m = n * TOP_K
    n_blocks = m // MOE_BLK + N_EXPERTS
    slots = n_blocks * MOE_BLK
    dest8, cnt = route_slots(eidx)
    dest_flat = dest8[:TOP_K].reshape(m)
    counts = cnt[:N_EXPERTS, 0]
    pend = jnp.cumsum((counts + MOE_BLK - 1) // MOE_BLK * MOE_BLK)
    block_e = jnp.minimum(
        jnp.searchsorted(pend, jnp.arange(n_blocks, dtype=jnp.int32) * MOE_BLK, side='right'),
        N_EXPERTS - 1).astype(jnp.int32)
    n_used = (pend[-1:] // MOE_BLK).astype(jnp.int32)
    slot_tok = slot_map(dest_flat, n, slots)
    yb = expert_blocks(block_e, n_used, slot_tok, x1, wg, wu, wd)
    return combine_ln2(dest_flat, x1, gate, yb, ln_g, ln_b)


def _t5_bucket(rel):
    half = NUM_BUCKETS // 2
    max_exact = half // 2
    n = np.abs(rel)
    large = max_exact + (np.log(np.maximum(n, 1) / max_exact) / np.log(MAX_DISTANCE / max_exact)
                         * (half - max_exact)).astype(np.int32)
    large = np.minimum(large, half - 1)
    return np.where(rel > 0, half, 0) + np.where(n < max_exact, n, large)


def _bias_table(rel_bias, g, dil):
    rel = (np.arange(3 * RADIUS)[None, :] - RADIUS - np.arange(RADIUS)[:, None]) * dil
    hs = slice(g * HEADS_PER_GROUP, (g + 1) * HEADS_PER_GROUP)
    return jnp.transpose(rel_bias[_t5_bucket(rel), hs], (2, 0, 1)).astype(jnp.float32)


def _chunk_block_diag(w):
    per = RNN_CHUNK // RNN_BLOCK_W
    n_chunks = RNN_BLOCKS // per
    w4 = w.reshape(n_chunks, per, RNN_BLOCK_W, RNN_BLOCK_W)
    dense = jnp.einsum('chij,hk->chikj', w4, jnp.eye(per, dtype=w.dtype))
    return dense.reshape(n_chunks, RNN_CHUNK, RNN_CHUNK)


def _gate_weights(wa, wi):
    per_dir = [jnp.concatenate([_chunk_block_diag(wa[d]), _chunk_block_diag(wi[d])], axis=-1)
               for d in range(2)]
    return jnp.stack(per_dir, axis=1).astype(jnp.bfloat16)


def _per_chunk(v):
    return v.reshape(2, D_RNN // RNN_CHUNK, 1, RNN_CHUNK).transpose(1, 0, 2, 3)


def _permute_in_cols(a):
    x_rnn, q, k, v, g_rnn, g_attn = jnp.split(
        a, [D_RNN, D_RNN + D_ATTN, D_RNN + 2 * D_ATTN, D_RNN + 3 * D_ATTN,
            D_RNN + 3 * D_ATTN + D_MODEL], axis=-1)
    return jnp.concatenate([x_rnn, g_rnn, g_attn, q, k, v], axis=-1)


def _trunk(x, biases, lp):
    b, s, d = x.shape
    n = b * s
    x2 = x.reshape(n, d)
    x2_bf = x2.astype(jnp.bfloat16)
    for l in range(DEPTH):
        p = lp[l]
        proj = in_projection(x2_bf, p['w_in'], p['b_in'])
        proj3 = proj.reshape(b, s, D_IN)
        h_rnn = rglru_branch(proj3, p['conv_w'], p['conv_b'], p['wg'], p['gb'], p['lam'])
        y_attn = dilated_attention(proj3, biases)
        x1, eidx, gate = mix_and_route(
            x2, proj, h_rnn.reshape(n, D_RNN), y_attn.reshape(n, D_ATTN_OUT),
            p['w_rnn_out'], p['w_attn_out'], p['w_o'],
            p['ln1_g'], p['ln1_b'], p['router_w'], p['router_b'])
        x2, x2_bf = moe_layer(x1, eidx, gate, p['w_gate'], p['w_up'], p['w_down'], p['ln2_g'], p['ln2_b'])
    return x2.reshape(b, s, d)


def kernel(x_prompt, x_sample, rel_bias, w_in, b_in, conv_w, conv_b, rg_wa, rg_ba, rg_wi, rg_bi, rg_lam, w_rnn_out, w_attn_out, w_o, ln1_g, ln1_b, router_w, router_b, expert_router_w, expert_router_b, w_gate, w_up, w_down, ln2_g, ln2_b):
    bf = jnp.bfloat16
    biases = [_bias_table(rel_bias, g, dil) for g, (_, dil) in enumerate(DILATED_CONFIGS)]
    lp = []
    for l in range(DEPTH):
        rw = jnp.concatenate(
            [router_w[l], jnp.transpose(expert_router_w[l], (1, 0, 2)).reshape(D_MODEL, N_EXPERTS)], axis=1)
        rb = jnp.concatenate([router_b[l], expert_router_b[l].reshape(N_EXPERTS)])
        pad = ROUTE_W - rw.shape[1]
        lp.append(dict(
            w_in=_permute_in_cols(w_in[l]).astype(bf),
            b_in=_permute_in_cols(b_in[l])[None, :],
            conv_w=conv_w[l], conv_b=conv_b[l][None, :],
            wg=_gate_weights(rg_wa[l], rg_wi[l]),
            gb=jnp.concatenate([_per_chunk(rg_ba[l]), _per_chunk(rg_bi[l])], axis=-1),
            lam=_per_chunk(rg_lam[l]),
            w_rnn_out=w_rnn_out[l].astype(bf), w_attn_out=w_attn_out[l].astype(bf), w_o=w_o[l].astype(bf),
            ln1_g=ln1_g[l][None, :], ln1_b=ln1_b[l][None, :],
            router_w=jnp.pad(rw, ((0, 0), (0, pad))), router_b=jnp.pad(rb, (0, pad))[None, :],
            w_gate=w_gate[l].astype(bf), w_up=w_up[l].astype(bf), w_down=w_down[l].astype(bf),
            ln2_g=ln2_g[l][None, :], ln2_b=ln2_b[l][None, :]))
    return (_trunk(x_prompt, biases, lp), _trunk(x_sample, biases, lp))
```

```python
import functools

import numpy as np
import jax
import jax.numpy as jnp
from jax import lax
from jax.experimental import pallas as pl
from jax.experimental.pallas import tpu as pltpu

D_MODEL = 1024
DEPTH = 2
D_RNN = D_MODEL
RNN_BLOCKS = 16
RNN_BLOCK_W = D_RNN // RNN_BLOCKS
CONV_WIDTH = 4
RG_C = 8.0
DILATED_CONFIGS = ((128, 1), (512, 4), (2048, 16))
N_ATTN_GROUPS = len(DILATED_CONFIGS)
HEADS_PER_GROUP = 4
N_ATTN_HEADS = N_ATTN_GROUPS * HEADS_PER_GROUP
HEAD_DIM = 128
D_ATTN = N_ATTN_HEADS * HEAD_DIM
D_ATTN_OUT = HEADS_PER_GROUP * HEAD_DIM
NUM_BUCKETS = 32
MAX_DISTANCE = max(w for w, _ in DILATED_CONFIGS) // 2
NEG_INF = -1e30
D_IN = D_RNN + 3 * D_ATTN + 2 * D_MODEL
N_EXPERT_GROUPS = 4
EXPERTS_PER_GROUP = 8
N_EXPERTS = N_EXPERT_GROUPS * EXPERTS_PER_GROUP
TOP_K = 2
D_EXPERT = D_MODEL // 2
ALPHA = (2 * DEPTH) ** 0.25
LN_EPS = 1e-5

LANES = 128
SUBLANES = 8
VMEM_LIMIT = 56 * 1024 * 1024

COL_XRNN = 0
COL_GRNN = D_RNN
COL_GATTN = D_RNN + D_MODEL
COL_Q = D_RNN + 2 * D_MODEL
COL_K = COL_Q + D_ATTN
COL_V = COL_K + D_ATTN

RADIUS = 64
assert all(w // (2 * d) == RADIUS for w, d in DILATED_CONFIGS)
ATTN_ROWS = 1024
RNN_CHUNK = 256
RNN_TILE = 256
MOE_BLK = 256
ROUTE_W = LANES


def _cparams(n_axes):
    return pltpu.CompilerParams(dimension_semantics=("arbitrary",) * n_axes,
                                vmem_limit_bytes=VMEM_LIMIT)


def _proj_kernel(x_ref, w_ref, b_ref, o_ref):
    acc = jnp.dot(x_ref[...], w_ref[...], preferred_element_type=jnp.float32)
    o_ref[...] = acc + b_ref[...]


def in_projection(x_bf, w_bf, b, tm=512, tn=1536):
    n, k = x_bf.shape
    nout = w_bf.shape[1]
    return pl.pallas_call(
        _proj_kernel,
        grid=(nout // tn, n // tm),
        in_specs=[pl.BlockSpec((tm, k), lambda j, i: (i, 0)),
                  pl.BlockSpec((k, tn), lambda j, i: (0, j)),
                  pl.BlockSpec((1, tn), lambda j, i: (0, j))],
        out_specs=pl.BlockSpec((tm, tn), lambda j, i: (i, j)),
        out_shape=jax.ShapeDtypeStruct((n, nout), jnp.float32),
        compiler_params=_cparams(2),
        name="in_projection",
    )(x_bf, w_bf, b)


def _rglru_kernel(x_ref, cw_ref, cb_ref, wg_ref, gb_ref, lam_ref, o_ref,
                  hsum, a_s, u_s, h_s, *, seq, tile):
    c = RNN_CHUNK
    n_tiles = seq // tile
    n_grp = tile // SUBLANES
    big_rows = tile + 2 * SUBLANES
    row_in_grp = lax.broadcasted_iota(jnp.int32, (tile, c), 0) & (SUBLANES - 1)

    def conv_tile(t0):
        centre = x_ref[0, pl.ds(t0, tile), :]
        p0 = pl.multiple_of(jnp.maximum(t0 - SUBLANES, 0), SUBLANES)
        n0 = pl.multiple_of(jnp.minimum(t0 + tile, seq - SUBLANES), SUBLANES)
        prev = jnp.where(t0 > 0, x_ref[0, pl.ds(p0, SUBLANES), :], 0.0)
        nxt = jnp.where(t0 + tile < seq, x_ref[0, pl.ds(n0, SUBLANES), :], 0.0)
        big = jnp.concatenate([prev, centre, nxt], axis=0)
        xc = cb_ref[...] + cw_ref[1:2, :] * centre
        for k in (0, 2, 3):
            off = k - 1
            tap = pltpu.roll(big, (-off) % big_rows, 0)[SUBLANES:SUBLANES + tile]
            xc = xc + cw_ref[k:k + 1, :] * tap
        return xc

    def gates(xc, dr):
        g = jnp.dot(xc.astype(jnp.bfloat16), wg_ref[0, dr],
                    preferred_element_type=jnp.float32) + gb_ref[0, dr]
        gate_r = jax.nn.sigmoid(g[:, :c])
        gate_i = jax.nn.sigmoid(g[:, c:])
        neg_lam = -lam_ref[0, dr]
        softplus = jnp.maximum(neg_lam, 0.0) + jnp.log1p(jnp.exp(-jnp.abs(neg_lam)))
        log_a = -RG_C * gate_r * softplus
        a = jnp.exp(log_a)
        th = jnp.tanh(log_a)
        one_minus_a2 = -2.0 * th / (1.0 - th)
        u = jnp.sqrt(one_minus_a2) * (gate_i * xc)
        return a, u

    def tile_prefix(a, u, reverse):
        for s in (1, 2, 4):
            if reverse:
                a_sh = pltpu.roll(a, tile - s, 0)
                u_sh = pltpu.roll(u, tile - s, 0)
                m = row_in_grp < SUBLANES - s
            else:
                a_sh = pltpu.roll(a, s, 0)
                u_sh = pltpu.roll(u, s, 0)
                m = row_in_grp >= s
            u = jnp.where(m, u + a * u_sh, u)
            a = jnp.where(m, a * a_sh, a)
        a_s[...] = a
        u_s[...] = u

    def run_direction(dr, reverse):
        def tile_body(ti, h):
            t_idx = (n_tiles - 1 - ti) if reverse else ti
            t0 = pl.multiple_of(t_idx * tile, tile)
            xc = conv_tile(t0)
            a, u = gates(xc, dr)
            tile_prefix(a, u, reverse)

            def grp_body(gi, hc):
                g_idx = (n_grp - 1 - gi) if reverse else gi
                r0 = pl.multiple_of(g_idx * SUBLANES, SUBLANES)
                hg = u_s[pl.ds(r0, SUBLANES), :] + a_s[pl.ds(r0, SUBLANES), :] * hc
                h_s[pl.ds(r0, SUBLANES), :] = hg
                edge = hg[0:1, :] if reverse else hg[SUBLANES - 1:SUBLANES, :]
                return jnp.broadcast_to(edge, (SUBLANES, c))

            h = lax.fori_loop(0, n_grp, grp_body, h, unroll=4)
            if reverse:
                o_ref[0, pl.ds(t0, tile), :] = (hsum[pl.ds(t0, tile), :] + h_s[...]).astype(o_ref.dtype)
            else:
                hsum[pl.ds(t0, tile), :] = h_s[...]
            return h

        lax.fori_loop(0, n_tiles, tile_body, jnp.zeros((SUBLANES, c), jnp.float32))

    run_direction(0, False)
    run_direction(1, True)


def rglru_branch(proj3, cw, cb, wg, gb, lam):
    b, s, _ = proj3.shape
    c = RNN_CHUNK
    n_chunks = D_RNN // c
    kern = functools.partial(_rglru_kernel, seq=s, tile=RNN_TILE)
    return pl.pallas_call(
        kern,
        grid=(b, n_chunks),
        in_specs=[pl.BlockSpec((1, s, c), lambda bi, ci: (bi, 0, COL_XRNN // c + ci)),
                  pl.BlockSpec((CONV_WIDTH, c), lambda bi, ci: (0, ci)),
                  pl.BlockSpec((1, c), lambda bi, ci: (0, ci)),
                  pl.BlockSpec((1, 2, c, 2 * c), lambda bi, ci: (ci, 0, 0, 0)),
                  pl.BlockSpec((1, 2, 1, 2 * c), lambda bi, ci: (ci, 0, 0, 0)),
                  pl.BlockSpec((1, 2, 1, c), lambda bi, ci: (ci, 0, 0, 0))],
        out_specs=pl.BlockSpec((1, s, c), lambda bi, ci: (bi, 0, ci)),
        out_shape=jax.ShapeDtypeStruct((b, s, D_RNN), jnp.bfloat16),
        scratch_shapes=[pltpu.VMEM((s, c), jnp.float32),
                        pltpu.VMEM((RNN_TILE, c), jnp.float32),
                        pltpu.VMEM((RNN_TILE, c), jnp.float32),
                        pltpu.VMEM((RNN_TILE, c), jnp.float32)],
        compiler_params=_cparams(2),
        name="rglru_branch",
    )(proj3, cw, cb, wg, gb, lam)


def _attn_group(refs, bias, o_dst, l_dst, *, dil, is_first, is_last):
    q_ref, kp_ref, kc_ref, kn_ref, vp_ref, vc_ref, vn_ref = refs
    rows = ATTN_ROWS
    per_res = rows // dil
    n_q = per_res // RADIUS
    win = 3 * RADIUS
    nb = dil * n_q

    def fold(ref, r, n):
        if dil == 1:
            return ref[0, pl.ds(r, n), :]
        return ref[0, pl.ds(r, n, stride=dil), :]

    qs, ks, vs = [], [], []
    for r in range(dil):
        qf = fold(q_ref, r, per_res).astype(jnp.bfloat16)
        kf = jnp.concatenate([fold(kp_ref, r, RADIUS), fold(kc_ref, r, per_res),
                              fold(kn_ref, r, RADIUS)], axis=0).astype(jnp.bfloat16)
        vf = jnp.concatenate([fold(vp_ref, r, RADIUS), fold(vc_ref, r, per_res),
                              fold(vn_ref, r, RADIUS)], axis=0).astype(jnp.bfloat16)
        for jq in range(n_q):
            qs.append(qf[jq * RADIUS:(jq + 1) * RADIUS])
            ks.append(kf[jq * RADIUS:jq * RADIUS + win])
            vs.append(vf[jq * RADIUS:jq * RADIUS + win])
    qb, kb, vb = jnp.stack(qs), jnp.stack(ks), jnp.stack(vs)

    shape = (nb, RADIUS, win)
    jq_of = lax.broadcasted_iota(jnp.int32, shape, 0) & (n_q - 1)
    qi = lax.broadcasted_iota(jnp.int32, shape, 1)
    kj = lax.broadcasted_iota(jnp.int32, shape, 2)
    lo = jnp.where(jq_of == 0, jnp.where(is_first, RADIUS, 0), 0)
    hi = jnp.where(jq_of == n_q - 1, jnp.where(is_last, 2 * RADIUS, win), win)
    mask = (jnp.abs(kj - RADIUS - qi) <= RADIUS) & (kj >= lo) & (kj < hi)

    s = jnp.einsum('bqd,bkd->bqk', qb, kb, preferred_element_type=jnp.float32) * (HEAD_DIM ** -0.5)
    s = jnp.where(mask, s + bias[None], NEG_INF)
    m = jnp.max(s, axis=-1, keepdims=True)
    p = jnp.exp(s - m)
    l = jnp.sum(p, axis=-1, keepdims=True)
    o = jnp.einsum('bqk,bkd->bqd', p.astype(jnp.bfloat16), vb, preferred_element_type=jnp.float32) / l
    lse = m + jnp.log(l)
    for bi in range(nb):
        r, jq = divmod(bi, n_q)
        start = jq * RADIUS * dil + r
        idx = pl.ds(start, RADIUS) if dil == 1 else pl.ds(start, RADIUS, stride=dil)
        o_dst[idx, :] = o[bi]
        l_dst[idx, :] = jnp.broadcast_to(lse[bi], (RADIUS, HEAD_DIM))


def _attn_kernel(*refs, n_row_blocks):
    n_g = N_ATTN_GROUPS
    in_refs = refs[:7 * n_g]
    bias_refs = refs[7 * n_g:8 * n_g]
    y_ref = refs[8 * n_g]
    o_s, l_s = refs[8 * n_g + 1:]
    i = pl.program_id(1)
    for g, (_, dil) in enumerate(DILATED_CONFIGS):
        _attn_group(in_refs[7 * g:7 * g + 7], bias_refs[g][0], o_s.at[g], l_s.at[g],
                    dil=dil, is_first=i == 0, is_last=i == n_row_blocks - 1)
    l0, l1, l2 = l_s[0], l_s[1], l_s[2]
    m = jnp.maximum(jnp.maximum(l0, l1), l2)
    e0, e1, e2 = jnp.exp(l0 - m), jnp.exp(l1 - m), jnp.exp(l2 - m)
    y = (e0 * o_s[0] + e1 * o_s[1] + e2 * o_s[2]) / (e0 + e1 + e2)
    y_ref[0] = y.astype(y_ref.dtype)


def dilated_attention(proj3, biases):
    b, s, _ = proj3.shape
    rows = ATTN_ROWS
    nrb = s // rows
    w = HEAD_DIM
    nh = HEADS_PER_GROUP

    def cur_map(col):
        return lambda bi, i, h: (bi, i, col + h)

    in_specs, operands = [], []
    for g, (_, dil) in enumerate(DILATED_CONFIGS):
        halo = RADIUS * dil
        ratio = rows // halo
        n_halo_blocks = s // halo
        qc, kc, vc = COL_Q // w + g * nh, COL_K // w + g * nh, COL_V // w + g * nh

        def prev_map(col, ratio=ratio):
            return lambda bi, i, h: (bi, jnp.maximum(i * ratio - 1, 0), col + h)

        def next_map(col, ratio=ratio, last=n_halo_blocks - 1):
            return lambda bi, i, h: (bi, jnp.minimum((i + 1) * ratio, last), col + h)

        in_specs += [pl.BlockSpec((1, rows, w), cur_map(qc)),
                     pl.BlockSpec((1, halo, w), prev_map(kc)),
                     pl.BlockSpec((1, rows, w), cur_map(kc)),
                     pl.BlockSpec((1, halo, w), next_map(kc)),
                     pl.BlockSpec((1, halo, w), prev_map(vc)),
                     pl.BlockSpec((1, rows, w), cur_map(vc)),
                     pl.BlockSpec((1, halo, w), next_map(vc))]
        operands += [proj3] * 7
    in_specs += [pl.BlockSpec((1, RADIUS, 3 * RADIUS), lambda bi, i, h: (h, 0, 0))] * N_ATTN_GROUPS
    operands += list(biases)
    return pl.pallas_call(
        functools.partial(_attn_kernel, n_row_blocks=nrb),
        grid=(b, nrb, nh),
        in_specs=in_specs,
        out_specs=pl.BlockSpec((1, rows, w), cur_map(0)),
        out_shape=jax.ShapeDtypeStruct((b, s, D_ATTN_OUT), jnp.bfloat16),
        scratch_shapes=[pltpu.VMEM((N_ATTN_GROUPS, rows, w), jnp.float32),
                        pltpu.VMEM((N_ATTN_GROUPS, rows, w), jnp.float32)],
        compiler_params=_cparams(3),
        name="dilated_attention",
    )(*operands)


def _layer_norm(z, g, b):
    mu = jnp.mean(z, axis=-1, keepdims=True)
    zc = z - mu
    var = jnp.mean(zc * zc, axis=-1, keepdims=True)
    return zc * lax.rsqrt(var + LN_EPS) * g + b


def _mix_kernel(x_ref, grnn_ref, gattn_ref, h_ref, y_ref,
                wr_ref, wa_ref, wo_ref, lng_ref, lnb_ref, rw_ref, rb_ref,
                xo_ref, eidx_ref, gate_ref):
    rnn = jnp.dot(h_ref[...], wr_ref[...], preferred_element_type=jnp.float32)
    att = jnp.dot(y_ref[...], wa_ref[...], preferred_element_type=jnp.float32)
    mixed = jax.nn.sigmoid(grnn_ref[...]) * rnn + jax.nn.sigmoid(gattn_ref[...]) * att
    z = ALPHA * x_ref[...] + jnp.dot(mixed.astype(jnp.bfloat16), wo_ref[...],
                                     preferred_element_type=jnp.float32)
    x1 = _layer_norm(z, lng_ref[...], lnb_ref[...])
    xo_ref[...] = x1

    logits = jnp.dot(x1, rw_ref[...], preferred_element_type=jnp.float32,
                     precision=lax.Precision.HIGHEST) + rb_ref[...]
    tm = logits.shape[0]
    lane = lax.broadcasted_iota(jnp.int32, (tm, ROUTE_W), 1)
    lane_f = lane.astype(jnp.float32)
    far = float(ROUTE_W)
    is_group = lane < N_EXPERT_GROUPS
    gl = jnp.where(is_group, logits, -jnp.inf)
    gmax = jnp.max(gl, axis=-1, keepdims=True)
    gsel = jnp.min(jnp.where(gl == gmax, lane_f, far), axis=-1, keepdims=True)
    p_group = 1.0 / jnp.sum(jnp.where(is_group, jnp.exp(logits - gmax), 0.0), axis=-1, keepdims=True)
    lane_group = ((lane - N_EXPERT_GROUPS) >> 3).astype(jnp.float32)
    in_group = (lane >= N_EXPERT_GROUPS) & (lane < N_EXPERT_GROUPS + N_EXPERTS) & (lane_group == gsel)
    el = jnp.where(in_group, logits, -jnp.inf)
    m1 = jnp.max(el, axis=-1, keepdims=True)
    i1 = jnp.min(jnp.where(el == m1, lane_f, far), axis=-1, keepdims=True)
    el2 = jnp.where(lane_f == i1, -jnp.inf, el)
    m2 = jnp.max(el2, axis=-1, keepdims=True)
    i2 = jnp.min(jnp.where(el2 == m2, lane_f, far), axis=-1, keepdims=True)
    e21 = jnp.exp(m2 - m1)
    den = 1.0 + e21
    g1 = p_group * (1.0 / den)
    g2 = p_group * (e21 / den)
    eidx = jnp.where(lane == 0, i1, jnp.where(lane == 1, i2, float(N_EXPERT_GROUPS))) - float(N_EXPERT_GROUPS)
    eidx_ref[...] = eidx.astype(jnp.int32)
    gate_ref[...] = jnp.where(lane == 0, g1, jnp.where(lane == 1, g2, 0.0))


def mix_and_route(x, proj, h_rnn, y_attn, wr, wa, wo, lng, lnb, rw, rb, tm=256):
    n = x.shape[0]
    d = D_MODEL
    w = D_ATTN_OUT
    row = lambda width, col=0: pl.BlockSpec((tm, width), lambda i, c=col: (i, c))
    full = lambda a: pl.BlockSpec(a.shape, lambda i: (0,) * a.ndim)
    return pl.pallas_call(
        _mix_kernel,
        grid=(n // tm,),
        in_specs=[row(d), row(d, COL_GRNN // d), row(d, COL_GATTN // d), row(d), row(w),
                  full(wr), full(wa), full(wo), full(lng), full(lnb), full(rw), full(rb)],
        out_specs=[row(d), row(ROUTE_W), row(ROUTE_W)],
        out_shape=[jax.ShapeDtypeStruct((n, d), jnp.float32),
                   jax.ShapeDtypeStruct((n, ROUTE_W), jnp.int32),
                   jax.ShapeDtypeStruct((n, ROUTE_W), jnp.float32)],
        compiler_params=_cparams(1),
        name="mix_and_route",
    )(x, proj, proj, h_rnn, y_attn, wr, wa, wo, lng, lnb, rw, rb)


def _rank_kernel(e_ref, dest_ref, cnt_ref, carry, *, tm):
    ph = pl.program_id(0)
    i = pl.program_id(1)
    lanes = ROUTE_W

    @pl.when((ph == 0) & (i == 0))
    def _():
        carry[...] = jnp.zeros_like(carry)

    @pl.when((ph == 1) & (i == 0))
    def _():
        cnt = carry[...]
        cnt_ref[...] = jnp.broadcast_to(cnt, (lanes, lanes)).astype(jnp.int32)
        padded = jnp.floor((cnt + (MOE_BLK - 1)) * (1.0 / MOE_BLK)) * MOE_BLK
        r = lax.broadcasted_iota(jnp.int32, (lanes, lanes), 0)
        c = lax.broadcasted_iota(jnp.int32, (lanes, lanes), 1)
        lower = (c < r).astype(jnp.float32)
        pstart = jnp.dot(lower, jnp.broadcast_to(padded, (lanes, lanes)),
                         preferred_element_type=jnp.float32, precision=lax.Precision.HIGHEST)
        carry[...] = pstart[:, 0:1]

    r8 = lax.broadcasted_iota(jnp.int32, (SUBLANES, lanes), 0)
    c8 = lax.broadcasted_iota(jnp.int32, (SUBLANES, lanes), 1)
    pick = ((r8 == c8) & (r8 < TOP_K)).astype(jnp.bfloat16)
    ef = e_ref[...].astype(jnp.float32).astype(jnp.bfloat16)
    et = lax.dot_general(pick, ef, (((1,), (1,)), ((), ())), preferred_element_type=jnp.float32)
    e0, e1 = et[0:1, :], et[1:2, :]
    sub = lax.broadcasted_iota(jnp.int32, (lanes, tm), 0).astype(jnp.float32)
    is0, is1 = sub == e0, sub == e1
    member = jnp.where(is0 | is1, 1.0, 0.0)
    s_i = lax.broadcasted_iota(jnp.int32, (tm, tm), 0)
    t_i = lax.broadcasted_iota(jnp.int32, (tm, tm), 1)
    earlier = (s_i < t_i).astype(jnp.bfloat16)
    before = jnp.dot(member.astype(jnp.bfloat16), earlier, preferred_element_type=jnp.float32)
    pos = before + carry[...]
    d0 = jnp.sum(jnp.where(is0, pos, 0.0), axis=0, keepdims=True)
    d1 = jnp.sum(jnp.where(is1, pos, 0.0), axis=0, keepdims=True)
    row = lax.broadcasted_iota(jnp.int32, (SUBLANES, tm), 0)
    dest_ref[...] = jnp.where(row == 0, d0, jnp.where(row == 1, d1, 0.0)).astype(jnp.int32)
    carry[...] = carry[...] + jnp.sum(member, axis=1, keepdims=True)


def route_slots(eidx, tm=512):
    n = eidx.shape[0]
    nt = n // tm
    return pl.pallas_call(
        functools.partial(_rank_kernel, tm=tm),
        grid=(2, nt),
        in_specs=[pl.BlockSpec((tm, ROUTE_W), lambda ph, i: (i, 0))],
        out_specs=[pl.BlockSpec((SUBLANES, tm), lambda ph, i: (0, i * ph)),
                   pl.BlockSpec((ROUTE_W, ROUTE_W), lambda ph, i: (0, 0))],
        out_shape=[jax.ShapeDtypeStruct((SUBLANES, n), jnp.int32),
                   jax.ShapeDtypeStruct((ROUTE_W, ROUTE_W), jnp.int32)],
        scratch_shapes=[pltpu.VMEM((ROUTE_W, 1), jnp.float32)],
        compiler_params=_cparams(2),
        name="route_slots",
    )(eidx)


def _slot_map_kernel(dest_ref, slot_ref, *, n, slots):
    def init(j, c):
        slot_ref[j] = 0
        return c

    lax.fori_loop(0, slots, init, 0, unroll=32)

    def body(t, c):
        slot_ref[dest_ref[t]] = t
        slot_ref[dest_ref[n + t]] = t
        return c

    lax.fori_loop(0, n, body, 0, unroll=16)


def slot_map(dest_flat, n, slots):
    return pl.pallas_call(
        functools.partial(_slot_map_kernel, n=n, slots=slots),
        in_specs=[pl.BlockSpec(memory_space=pltpu.SMEM)],
        out_specs=pl.BlockSpec(memory_space=pltpu.SMEM),
        out_shape=jax.ShapeDtypeStruct((slots,), jnp.int32),
        name="slot_map",
    )(dest_flat)


def _expert_kernel(be_ref, nused_ref, st_ref, x_hbm, wg_ref, wu_ref, wd_ref, y_ref, xs, land, sem):
    i = pl.program_id(0)
    n_used = nused_ref[0]
    stage_rows = 32

    def gather_block(blk):
        base = blk * MOE_BLK
        for j in range(MOE_BLK):
            pltpu.make_async_copy(x_hbm.at[pl.ds(st_ref[base + j], 1)], land.at[pl.ds(j, 1)], sem.at[0]).start()

    def wait_block():
        pltpu.make_async_copy(x_hbm.at[pl.ds(0, MOE_BLK)], land, sem.at[0]).wait()

    @pl.when(i == 0)
    def _():
        gather_block(0)

    @pl.when(i < n_used)
    def _():
        wait_block()

        def stage(r, c):
            r0 = pl.multiple_of(r * stage_rows, stage_rows)
            xs[pl.ds(r0, stage_rows), :] = land[pl.ds(r0, stage_rows), :].astype(jnp.bfloat16)
            return c

        lax.fori_loop(0, MOE_BLK // stage_rows, stage, 0)
        gather_block(i + 1)
        xb = xs[...]
        gate = jnp.dot(xb, wg_ref[0], preferred_element_type=jnp.float32)
        up = jnp.dot(xb, wu_ref[0], preferred_element_type=jnp.float32)
        hid = (jax.nn.silu(gate) * up).astype(jnp.bfloat16)
        y_ref[...] = jnp.dot(hid, wd_ref[0], preferred_element_type=jnp.float32)

    @pl.when(i == n_used - 1)
    def _():
        wait_block()

    @pl.when(i >= n_used)
    def _():
        y_ref[...] = jnp.zeros_like(y_ref)


def expert_blocks(block_e, n_used, slot_tok, x1, wg, wu, wd):
    n, d = x1.shape
    n_blocks = slot_tok.shape[0] // MOE_BLK - 1
    slots = n_blocks * MOE_BLK
    grid_spec = pltpu.PrefetchScalarGridSpec(
        num_scalar_prefetch=3,
        grid=(n_blocks,),
        in_specs=[pl.BlockSpec(memory_space=pl.ANY),
                  pl.BlockSpec((1, d, D_EXPERT), lambda i, be, nu, st: (be[i], 0, 0)),
                  pl.BlockSpec((1, d, D_EXPERT), lambda i, be, nu, st: (be[i], 0, 0)),
                  pl.BlockSpec((1, D_EXPERT, d), lambda i, be, nu, st: (be[i], 0, 0))],
        out_specs=pl.BlockSpec((MOE_BLK, d), lambda i, be, nu, st: (i, 0)),
        scratch_shapes=[pltpu.VMEM((MOE_BLK, d), jnp.bfloat16),
                        pltpu.VMEM((MOE_BLK, d), jnp.float32),
                        pltpu.SemaphoreType.DMA((1,))],
    )
    return pl.pallas_call(
        _expert_kernel,
        grid_spec=grid_spec,
        out_shape=jax.ShapeDtypeStruct((slots, d), jnp.float32),
        compiler_params=_cparams(1),
        name="expert_blocks",
    )(block_e, n_used, slot_tok, x1, wg, wu, wd)


def _combine_kernel(dest_ref, x_ref, gate_ref, y_hbm, g_ref, b_ref, xo_ref, xob_ref, fbuf, sem, *, n, tm):
    i = pl.program_id(0)
    nt = pl.num_programs(0)

    def gather_tile(tile, slot):
        base = tile * tm
        for k in range(TOP_K):
            for t in range(tm):
                src = dest_ref[k * n + base + t]
                pltpu.make_async_copy(y_hbm.at[pl.ds(src, 1)], fbuf.at[slot, pl.ds(k * tm + t, 1)],
                                      sem.at[slot]).start()

    def wait_tile(slot):
        pltpu.make_async_copy(y_hbm.at[pl.ds(0, TOP_K * tm)], fbuf.at[slot], sem.at[slot]).wait()

    @pl.when(i == 0)
    def _():
        gather_tile(0, 0)

    slot = i % 2
    wait_tile(slot)
    f0 = fbuf[slot, 0:tm, :]
    f1 = fbuf[slot, tm:2 * tm, :]
    gather_tile(jnp.minimum(i + 1, nt - 1), 1 - slot)
    ffn = gate_ref[:, 0:1] * f0 + gate_ref[:, 1:2] * f1
    x2 = _layer_norm(ALPHA * x_ref[...] + ffn, g_ref[...], b_ref[...])
    xo_ref[...] = x2
    xob_ref[...] = x2.astype(jnp.bfloat16)

    @pl.when(i == nt - 1)
    def _():
        wait_tile(1 - slot)


def combine_ln2(dest_flat, x1, gate, yb, g, b, tm=256):
    n, d = x1.shape
    grid_spec = pltpu.PrefetchScalarGridSpec(
        num_scalar_prefetch=1,
        grid=(n // tm,),
        in_specs=[pl.BlockSpec((tm, d), lambda i, ds: (i, 0)),
                  pl.BlockSpec((tm, ROUTE_W), lambda i, ds: (i, 0)),
                  pl.BlockSpec(memory_space=pl.ANY),
                  pl.BlockSpec((1, d), lambda i, ds: (0, 0)),
                  pl.BlockSpec((1, d), lambda i, ds: (0, 0))],
        out_specs=[pl.BlockSpec((tm, d), lambda i, ds: (i, 0)),
                   pl.BlockSpec((tm, d), lambda i, ds: (i, 0))],
        scratch_shapes=[pltpu.VMEM((2, TOP_K * tm, d), jnp.float32),
                        pltpu.SemaphoreType.DMA((2,))],
    )
    return pl.pallas_call(
        functools.partial(_combine_kernel, n=n, tm=tm),
        grid_spec=grid_spec,
        out_shape=[jax.ShapeDtypeStruct((n, d), jnp.float32),
                   jax.ShapeDtypeStruct((n, d), jnp.bfloat16)],
        compiler_params=_cparams(1),
        name="combine_ln2",
    )(dest_flat, x1, gate, yb, g, b)


def moe_layer(x1, eidx, gate, wg, wu, wd, ln_g, ln_b):
    n, d = x1.shape
    m = n * TOP_K
    n_blocks = m // MOE_BLK + N_EXPERTS
    slots = n_blocks * MOE_BLK
    dest8, cnt = route_slots(eidx)
    dest_flat = dest8[:TOP_K].reshape(m)
    counts = cnt[:N_EXPERTS, 0]
    pend = jnp.cumsum((counts + MOE_BLK - 1) // MOE_BLK * MOE_BLK)
    blk_start = jnp.arange(n_blocks, dtype=jnp.int32) * MOE_BLK
    block_e = jnp.minimum(jnp.sum((pend[None, :] <= blk_start[:, None]).astype(jnp.int32), axis=1),
                          N_EXPERTS - 1)
    n_used = (pend[-1:] // MOE_BLK).astype(jnp.int32)
    slot_tok = slot_map(dest_flat, n, slots + MOE_BLK)
    yb = expert_blocks(block_e, n_used, slot_tok, x1, wg, wu, wd)
    return combine_ln2(dest_flat, x1, gate, yb, ln_g, ln_b)


def _t5_bucket(rel):
    half = NUM_BUCKETS // 2
    max_exact = half // 2
    n = np.abs(rel)
    large = max_exact + (np.log(np.maximum(n, 1) / max_exact) / np.log(MAX_DISTANCE / max_exact)
                         * (half - max_exact)).astype(np.int32)
    large = np.minimum(large, half - 1)
    return np.where(rel > 0, half, 0) + np.where(n < max_exact, n, large)


def _bias_table(rel_bias, g, dil):
    rel = (np.arange(3 * RADIUS)[None, :] - RADIUS - np.arange(RADIUS)[:, None]) * dil
    hs = slice(g * HEADS_PER_GROUP, (g + 1) * HEADS_PER_GROUP)
    onehot = jnp.asarray(_t5_bucket(rel)[..., None] == np.arange(NUM_BUCKETS), jnp.float32)
    return jnp.einsum('qkb,bh->hqk', onehot, rel_bias[:, hs].astype(jnp.float32),
                      precision=lax.Precision.HIGHEST)


def _chunk_block_diag(w):
    per = RNN_CHUNK // RNN_BLOCK_W
    n_chunks = RNN_BLOCKS // per
    w4 = w.reshape(n_chunks, per, RNN_BLOCK_W, RNN_BLOCK_W)
    dense = jnp.einsum('chij,hk->chikj', w4, jnp.eye(per, dtype=w.dtype))
    return dense.reshape(n_chunks, RNN_CHUNK, RNN_CHUNK)


def _gate_weights(wa, wi):
    per_dir = [jnp.concatenate([_chunk_block_diag(wa[d]), _chunk_block_diag(wi[d])], axis=-1)
               for d in range(2)]
    return jnp.stack(per_dir, axis=1).astype(jnp.bfloat16)


def _per_chunk(v):
    return v.reshape(2, D_RNN // RNN_CHUNK, 1, RNN_CHUNK).transpose(1, 0, 2, 3)


def _permute_in_cols(a):
    x_rnn, q, k, v, g_rnn, g_attn = jnp.split(
        a, [D_RNN, D_RNN + D_ATTN, D_RNN + 2 * D_ATTN, D_RNN + 3 * D_ATTN,
            D_RNN + 3 * D_ATTN + D_MODEL], axis=-1)
    return jnp.concatenate([x_rnn, g_rnn, g_attn, q, k, v], axis=-1)


def _trunk(x, biases, lp):
    b, s, d = x.shape
    n = b * s
    x2 = x.reshape(n, d)
    x2_bf = x2.astype(jnp.bfloat16)
    for l in range(DEPTH):
        p = lp[l]
        proj = in_projection(x2_bf, p['w_in'], p['b_in'])
        proj3 = proj.reshape(b, s, D_IN)
        h_rnn = rglru_branch(proj3, p['conv_w'], p['conv_b'], p['wg'], p['gb'], p['lam'])
        y_attn = dilated_attention(proj3, biases)
        x1, eidx, gate = mix_and_route(
            x2, proj, h_rnn.reshape(n, D_RNN), y_attn.reshape(n, D_ATTN_OUT),
            p['w_rnn_out'], p['w_attn_out'], p['w_o'],
            p['ln1_g'], p['ln1_b'], p['router_w'], p['router_b'])
        x2, x2_bf = moe_layer(x1, eidx, gate, p['w_gate'], p['w_up'], p['w_down'], p['ln2_g'], p['ln2_b'])
    return x2.reshape(b, s, d)


def kernel(x_prompt, x_sample, rel_bias, w_in, b_in, conv_w, conv_b, rg_wa, rg_ba, rg_wi, rg_bi, rg_lam, w_rnn_out, w_attn_out, w_o, ln1_g, ln1_b, router_w, router_b, expert_router_w, expert_router_b, w_gate, w_up, w_down, ln2_g, ln2_b):
    bf = jnp.bfloat16
    biases = [_bias_table(rel_bias, g, dil) for g, (_, dil) in enumerate(DILATED_CONFIGS)]
    lp = []
    for l in range(DEPTH):
        rw = jnp.concatenate(
            [router_w[l], jnp.transpose(expert_router_w[l], (1, 0, 2)).reshape(D_MODEL, N_EXPERTS)], axis=1)
        rb = jnp.concatenate([router_b[l], expert_router_b[l].reshape(N_EXPERTS)])
        pad = ROUTE_W - rw.shape[1]
        lp.append(dict(
            w_in=_permute_in_cols(w_in[l]).astype(bf),
            b_in=_permute_in_cols(b_in[l])[None, :],
            conv_w=conv_w[l], conv_b=conv_b[l][None, :],
            wg=_gate_weights(rg_wa[l], rg_wi[l]),
            gb=jnp.concatenate([_per_chunk(rg_ba[l]), _per_chunk(rg_bi[l])], axis=-1),
            lam=_per_chunk(rg_lam[l]),
            w_rnn_out=w_rnn_out[l].astype(bf), w_attn_out=w_attn_out[l].astype(bf), w_o=w_o[l].astype(bf),
            ln1_g=ln1_g[l][None, :], ln1_b=ln1_b[l][None, :],
            router_w=jnp.pad(rw, ((0, 0), (0, pad))), router_b=jnp.pad(rb, (0, pad))[None, :],
            w_gate=w_gate[l].astype(bf), w_up=w_up[l].astype(bf), w_down=w_down[l].astype(bf),
            ln2_g=ln2_g[l][None, :], ln2_b=ln2_b[l][None, :]))
    return (_trunk(x_prompt, biases, lp), _trunk(x_sample, biases, lp))
```

```python
import functools

import numpy as np
import jax
import jax.numpy as jnp
from jax import lax
from jax.experimental import pallas as pl
from jax.experimental.pallas import tpu as pltpu

D_MODEL = 1024
DEPTH = 2
D_RNN = D_MODEL
RNN_BLOCKS = 16
RNN_BLOCK_W = D_RNN // RNN_BLOCKS
CONV_WIDTH = 4
RG_C = 8.0
DILATED_CONFIGS = ((128, 1), (512, 4), (2048, 16))
N_ATTN_GROUPS = len(DILATED_CONFIGS)
HEADS_PER_GROUP = 4
N_ATTN_HEADS = N_ATTN_GROUPS * HEADS_PER_GROUP
HEAD_DIM = 128
D_ATTN = N_ATTN_HEADS * HEAD_DIM
D_ATTN_OUT = HEADS_PER_GROUP * HEAD_DIM
NUM_BUCKETS = 32
MAX_DISTANCE = max(w for w, _ in DILATED_CONFIGS) // 2
NEG_INF = -1e30
D_IN = D_RNN + 3 * D_ATTN + 2 * D_MODEL
N_EXPERT_GROUPS = 4
EXPERTS_PER_GROUP = 8
N_EXPERTS = N_EXPERT_GROUPS * EXPERTS_PER_GROUP
TOP_K = 2
D_EXPERT = D_MODEL // 2
ALPHA = (2 * DEPTH) ** 0.25
LN_EPS = 1e-5

LANES = 128
SUBLANES = 8
VMEM_LIMIT = 56 * 1024 * 1024

COL_XRNN = 0
COL_GRNN = D_RNN
COL_GATTN = D_RNN + D_MODEL
COL_Q = D_RNN + 2 * D_MODEL
COL_K = COL_Q + D_ATTN
COL_V = COL_K + D_ATTN

RADIUS = 64
assert all(w // (2 * d) == RADIUS for w, d in DILATED_CONFIGS)
ATTN_ROWS = 1024
RNN_CHUNK = 256
RNN_TILE = 256
MOE_BLK = 256
TOKEN_TILE = D_MODEL // LANES
assert TOKEN_TILE == SUBLANES
ROUTE_W = LANES


def _cparams(n_axes):
    return pltpu.CompilerParams(dimension_semantics=("arbitrary",) * n_axes,
                                vmem_limit_bytes=VMEM_LIMIT)


def _proj_kernel(x_ref, w_ref, b_ref, o_ref):
    acc = jnp.dot(x_ref[...], w_ref[...], preferred_element_type=jnp.float32)
    o_ref[...] = acc + b_ref[...]


def in_projection(x_bf, w_bf, b, tm=512, tn=1536):
    n, k = x_bf.shape
    nout = w_bf.shape[1]
    return pl.pallas_call(
        _proj_kernel,
        grid=(nout // tn, n // tm),
        in_specs=[pl.BlockSpec((tm, k), lambda j, i: (i, 0)),
                  pl.BlockSpec((k, tn), lambda j, i: (0, j)),
                  pl.BlockSpec((1, tn), lambda j, i: (0, j))],
        out_specs=pl.BlockSpec((tm, tn), lambda j, i: (i, j)),
        out_shape=jax.ShapeDtypeStruct((n, nout), jnp.float32),
        compiler_params=_cparams(2),
        name="in_projection",
    )(x_bf, w_bf, b)


def _rglru_kernel(x_ref, cw_ref, cb_ref, wg_ref, gb_ref, lam_ref, o_ref,
                  hsum, a_s, u_s, h_s, *, seq, tile):
    c = RNN_CHUNK
    n_tiles = seq // tile
    n_grp = tile // SUBLANES
    big_rows = tile + 2 * SUBLANES
    row_in_grp = lax.broadcasted_iota(jnp.int32, (n_grp, SUBLANES, c), 1)

    def conv_tile(t0):
        centre = x_ref[0, pl.ds(t0, tile), :]
        p0 = pl.multiple_of(jnp.maximum(t0 - SUBLANES, 0), SUBLANES)
        n0 = pl.multiple_of(jnp.minimum(t0 + tile, seq - SUBLANES), SUBLANES)
        prev = jnp.where(t0 > 0, x_ref[0, pl.ds(p0, SUBLANES), :], 0.0)
        nxt = jnp.where(t0 + tile < seq, x_ref[0, pl.ds(n0, SUBLANES), :], 0.0)
        big = jnp.concatenate([prev, centre, nxt], axis=0)
        xc = cb_ref[...] + cw_ref[1:2, :] * centre
        for k in (0, 2, 3):
            off = k - 1
            tap = pltpu.roll(big, (-off) % big_rows, 0)[SUBLANES:SUBLANES + tile]
            xc = xc + cw_ref[k:k + 1, :] * tap
        return xc

    def gates(xc, dr):
        g = jnp.dot(xc.astype(jnp.bfloat16), wg_ref[0, dr],
                    preferred_element_type=jnp.float32) + gb_ref[0, dr]
        gate_r = jax.nn.sigmoid(g[:, :c])
        gate_i = jax.nn.sigmoid(g[:, c:])
        neg_lam = -lam_ref[0, dr]
        softplus = jnp.maximum(neg_lam, 0.0) + jnp.log1p(jnp.exp(-jnp.abs(neg_lam)))
        log_a = -RG_C * gate_r * softplus
        a = jnp.exp(log_a)
        th = jnp.tanh(log_a)
        one_minus_a2 = -2.0 * th / (1.0 - th)
        u = jnp.sqrt(one_minus_a2) * (gate_i * xc)
        return a, u

    def tile_prefix(a, u, reverse):
        a = a.reshape(n_grp, SUBLANES, c)
        u = u.reshape(n_grp, SUBLANES, c)
        for s in (1, 2, 4):
            if reverse:
                a_sh = pltpu.roll(a, SUBLANES - s, 1)
                u_sh = pltpu.roll(u, SUBLANES - s, 1)
                m = row_in_grp < SUBLANES - s
            else:
                a_sh = pltpu.roll(a, s, 1)
                u_sh = pltpu.roll(u, s, 1)
                m = row_in_grp >= s
            u = jnp.where(m, u + a * u_sh, u)
            a = jnp.where(m, a * a_sh, a)
        a_s[...] = a.reshape(tile, c)
        u_s[...] = u.reshape(tile, c)

    def run_direction(dr, reverse):
        def tile_body(ti, h):
            t_idx = (n_tiles - 1 - ti) if reverse else ti
            t0 = pl.multiple_of(t_idx * tile, tile)
            xc = conv_tile(t0)
            a, u = gates(xc, dr)
            tile_prefix(a, u, reverse)

            def grp_body(gi, hc):
                g_idx = (n_grp - 1 - gi) if reverse else gi
                r0 = pl.multiple_of(g_idx * SUBLANES, SUBLANES)
                hg = u_s[pl.ds(r0, SUBLANES), :] + a_s[pl.ds(r0, SUBLANES), :] * hc
                h_s[pl.ds(r0, SUBLANES), :] = hg
                edge = hg[0:1, :] if reverse else hg[SUBLANES - 1:SUBLANES, :]
                return jnp.broadcast_to(edge, (SUBLANES, c))

            h = lax.fori_loop(0, n_grp, grp_body, h, unroll=4)
            if reverse:
                o_ref[0, pl.ds(t0, tile), :] = (hsum[pl.ds(t0, tile), :] + h_s[...]).astype(o_ref.dtype)
            else:
                hsum[pl.ds(t0, tile), :] = h_s[...]
            return h

        lax.fori_loop(0, n_tiles, tile_body, jnp.zeros((SUBLANES, c), jnp.float32))

    run_direction(0, False)
    run_direction(1, True)


def rglru_branch(proj3, cw, cb, wg, gb, lam):
    b, s, _ = proj3.shape
    c = RNN_CHUNK
    n_chunks = D_RNN // c
    kern = functools.partial(_rglru_kernel, seq=s, tile=RNN_TILE)
    return pl.pallas_call(
        kern,
        grid=(b, n_chunks),
        in_specs=[pl.BlockSpec((1, s, c), lambda bi, ci: (bi, 0, COL_XRNN // c + ci)),
                  pl.BlockSpec((CONV_WIDTH, c), lambda bi, ci: (0, ci)),
                  pl.BlockSpec((1, c), lambda bi, ci: (0, ci)),
                  pl.BlockSpec((1, 2, c, 2 * c), lambda bi, ci: (ci, 0, 0, 0)),
                  pl.BlockSpec((1, 2, 1, 2 * c), lambda bi, ci: (ci, 0, 0, 0)),
                  pl.BlockSpec((1, 2, 1, c), lambda bi, ci: (ci, 0, 0, 0))],
        out_specs=pl.BlockSpec((1, s, c), lambda bi, ci: (bi, 0, ci)),
        out_shape=jax.ShapeDtypeStruct((b, s, D_RNN), jnp.bfloat16),
        scratch_shapes=[pltpu.VMEM((s, c), jnp.float32),
                        pltpu.VMEM((RNN_TILE, c), jnp.float32),
                        pltpu.VMEM((RNN_TILE, c), jnp.float32),
                        pltpu.VMEM((RNN_TILE, c), jnp.float32)],
        compiler_params=_cparams(2),
        name="rglru_branch",
    )(proj3, cw, cb, wg, gb, lam)


def _attn_group(refs, bias, o_dst, l_dst, *, dil, is_first, is_last):
    q_ref, kp_ref, kc_ref, kn_ref, vp_ref, vc_ref, vn_ref = refs
    rows = ATTN_ROWS
    per_res = rows // dil
    n_q = per_res // RADIUS
    win = 3 * RADIUS
    nb = dil * n_q

    def fold(ref, r, n):
        if dil == 1:
            return ref[0, pl.ds(r, n), :]
        return ref[0, pl.ds(r, n, stride=dil), :]

    qs, ks, vs = [], [], []
    for r in range(dil):
        qf = fold(q_ref, r, per_res).astype(jnp.bfloat16)
        kf = jnp.concatenate([fold(kp_ref, r, RADIUS), fold(kc_ref, r, per_res),
                              fold(kn_ref, r, RADIUS)], axis=0).astype(jnp.bfloat16)
        vf = jnp.concatenate([fold(vp_ref, r, RADIUS), fold(vc_ref, r, per_res),
                              fold(vn_ref, r, RADIUS)], axis=0).astype(jnp.bfloat16)
        for jq in range(n_q):
            qs.append(qf[jq * RADIUS:(jq + 1) * RADIUS])
            ks.append(kf[jq * RADIUS:jq * RADIUS + win])
            vs.append(vf[jq * RADIUS:jq * RADIUS + win])
    qb, kb, vb = jnp.stack(qs), jnp.stack(ks), jnp.stack(vs)

    shape = (nb, RADIUS, win)
    jq_of = lax.broadcasted_iota(jnp.int32, shape, 0) & (n_q - 1)
    qi = lax.broadcasted_iota(jnp.int32, shape, 1)
    kj = lax.broadcasted_iota(jnp.int32, shape, 2)
    lo = jnp.where(jq_of == 0, jnp.where(is_first, RADIUS, 0), 0)
    hi = jnp.where(jq_of == n_q - 1, jnp.where(is_last, 2 * RADIUS, win), win)
    mask = (jnp.abs(kj - RADIUS - qi) <= RADIUS) & (kj >= lo) & (kj < hi)

    s = jnp.einsum('bqd,bkd->bqk', qb, kb, preferred_element_type=jnp.float32) * (HEAD_DIM ** -0.5)
    s = jnp.where(mask, s + bias[None], NEG_INF)
    m = jnp.max(s, axis=-1, keepdims=True)
    p = jnp.exp(s - m)
    l = jnp.sum(p, axis=-1, keepdims=True)
    o = jnp.einsum('bqk,bkd->bqd', p.astype(jnp.bfloat16), vb, preferred_element_type=jnp.float32) / l
    lse = m + jnp.log(l)
    for bi in range(nb):
        r, jq = divmod(bi, n_q)
        start = jq * RADIUS * dil + r
        idx = pl.ds(start, RADIUS) if dil == 1 else pl.ds(start, RADIUS, stride=dil)
        o_dst[idx, :] = o[bi]
        l_dst[idx, :] = jnp.broadcast_to(lse[bi], (RADIUS, HEAD_DIM))


def _attn_kernel(*refs, n_row_blocks):
    n_g = N_ATTN_GROUPS
    in_refs = refs[:7 * n_g]
    bias_refs = refs[7 * n_g:8 * n_g]
    y_ref = refs[8 * n_g]
    o_s, l_s = refs[8 * n_g + 1:]
    i = pl.program_id(1)
    for g, (_, dil) in enumerate(DILATED_CONFIGS):
        _attn_group(in_refs[7 * g:7 * g + 7], bias_refs[g][0], o_s.at[g], l_s.at[g],
                    dil=dil, is_first=i == 0, is_last=i == n_row_blocks - 1)
    l0, l1, l2 = l_s[0], l_s[1], l_s[2]
    m = jnp.maximum(jnp.maximum(l0, l1), l2)
    e0, e1, e2 = jnp.exp(l0 - m), jnp.exp(l1 - m), jnp.exp(l2 - m)
    y = (e0 * o_s[0] + e1 * o_s[1] + e2 * o_s[2]) / (e0 + e1 + e2)
    y_ref[0] = y.astype(y_ref.dtype)


def dilated_attention(proj3, biases):
    b, s, _ = proj3.shape
    rows = ATTN_ROWS
    nrb = s // rows
    w = HEAD_DIM
    nh = HEADS_PER_GROUP

    def cur_map(col):
        return lambda bi, i, h: (bi, i, col + h)

    in_specs, operands = [], []
    for g, (_, dil) in enumerate(DILATED_CONFIGS):
        halo = RADIUS * dil
        ratio = rows // halo
        n_halo_blocks = s // halo
        qc, kc, vc = COL_Q // w + g * nh, COL_K // w + g * nh, COL_V // w + g * nh

        def prev_map(col, ratio=ratio):
            return lambda bi, i, h: (bi, jnp.maximum(i * ratio - 1, 0), col + h)

        def next_map(col, ratio=ratio, last=n_halo_blocks - 1):
            return lambda bi, i, h: (bi, jnp.minimum((i + 1) * ratio, last), col + h)

        in_specs += [pl.BlockSpec((1, rows, w), cur_map(qc)),
                     pl.BlockSpec((1, halo, w), prev_map(kc)),
                     pl.BlockSpec((1, rows, w), cur_map(kc)),
                     pl.BlockSpec((1, halo, w), next_map(kc)),
                     pl.BlockSpec((1, halo, w), prev_map(vc)),
                     pl.BlockSpec((1, rows, w), cur_map(vc)),
                     pl.BlockSpec((1, halo, w), next_map(vc))]
        operands += [proj3] * 7
    in_specs += [pl.BlockSpec((1, RADIUS, 3 * RADIUS), lambda bi, i, h: (h, 0, 0))] * N_ATTN_GROUPS
    operands += list(biases)
    return pl.pallas_call(
        functools.partial(_attn_kernel, n_row_blocks=nrb),
        grid=(b, nrb, nh),
        in_specs=in_specs,
        out_specs=pl.BlockSpec((1, rows, w), cur_map(0)),
        out_shape=jax.ShapeDtypeStruct((b, s, D_ATTN_OUT), jnp.bfloat16),
        scratch_shapes=[pltpu.VMEM((N_ATTN_GROUPS, rows, w), jnp.float32),
                        pltpu.VMEM((N_ATTN_GROUPS, rows, w), jnp.float32)],
        compiler_params=_cparams(3),
        name="dilated_attention",
    )(*operands)


def _store_token_tiles(ref, val):
    rows = val.shape[0]
    for c in range(TOKEN_TILE):
        ref[pl.ds(c, rows, stride=TOKEN_TILE), :] = val[:, c * LANES:(c + 1) * LANES]


def _load_token_tiles(ref, start, rows):
    return [ref[pl.ds(start * TOKEN_TILE + c, rows, stride=TOKEN_TILE), :] for c in range(TOKEN_TILE)]


def _layer_norm(z, g, b):
    mu = jnp.mean(z, axis=-1, keepdims=True)
    zc = z - mu
    var = jnp.mean(zc * zc, axis=-1, keepdims=True)
    return zc * lax.rsqrt(var + LN_EPS) * g + b


def _mix_kernel(x_ref, grnn_ref, gattn_ref, h_ref, y_ref,
                wr_ref, wa_ref, wo_ref, lng_ref, lnb_ref, rw_ref, rb_ref,
                xo_ref, eidx_ref, gate_ref):
    rnn = jnp.dot(h_ref[...], wr_ref[...], preferred_element_type=jnp.float32)
    att = jnp.dot(y_ref[...], wa_ref[...], preferred_element_type=jnp.float32)
    mixed = jax.nn.sigmoid(grnn_ref[...]) * rnn + jax.nn.sigmoid(gattn_ref[...]) * att
    z = ALPHA * x_ref[...] + jnp.dot(mixed.astype(jnp.bfloat16), wo_ref[...],
                                     preferred_element_type=jnp.float32)
    x1 = _layer_norm(z, lng_ref[...], lnb_ref[...])
    _store_token_tiles(xo_ref, x1)

    logits = jnp.dot(x1, rw_ref[...], preferred_element_type=jnp.float32,
                     precision=lax.Precision.HIGHEST) + rb_ref[...]
    tm = logits.shape[0]
    lane = lax.broadcasted_iota(jnp.int32, (tm, ROUTE_W), 1)
    lane_f = lane.astype(jnp.float32)
    far = float(ROUTE_W)
    is_group = lane < N_EXPERT_GROUPS
    gl = jnp.where(is_group, logits, -jnp.inf)
    gmax = jnp.max(gl, axis=-1, keepdims=True)
    gsel = jnp.min(jnp.where(gl == gmax, lane_f, far), axis=-1, keepdims=True)
    p_group = 1.0 / jnp.sum(jnp.where(is_group, jnp.exp(logits - gmax), 0.0), axis=-1, keepdims=True)
    lane_group = ((lane - N_EXPERT_GROUPS) >> 3).astype(jnp.float32)
    in_group = (lane >= N_EXPERT_GROUPS) & (lane < N_EXPERT_GROUPS + N_EXPERTS) & (lane_group == gsel)
    el = jnp.where(in_group, logits, -jnp.inf)
    m1 = jnp.max(el, axis=-1, keepdims=True)
    i1 = jnp.min(jnp.where(el == m1, lane_f, far), axis=-1, keepdims=True)
    el2 = jnp.where(lane_f == i1, -jnp.inf, el)
    m2 = jnp.max(el2, axis=-1, keepdims=True)
    i2 = jnp.min(jnp.where(el2 == m2, lane_f, far), axis=-1, keepdims=True)
    e21 = jnp.exp(m2 - m1)
    den = 1.0 + e21
    g1 = p_group * (1.0 / den)
    g2 = p_group * (e21 / den)
    eidx = jnp.where(lane == 0, i1, jnp.where(lane == 1, i2, float(N_EXPERT_GROUPS))) - float(N_EXPERT_GROUPS)
    eidx_ref[...] = eidx.astype(jnp.int32)
    gate_ref[...] = jnp.where(lane == 0, g1, jnp.where(lane == 1, g2, 0.0))


def mix_and_route(x, proj, h_rnn, y_attn, wr, wa, wo, lng, lnb, rw, rb, tm=512):
    n = x.shape[0]
    d = D_MODEL
    w = D_ATTN_OUT
    row = lambda width, col=0: pl.BlockSpec((tm, width), lambda i, c=col: (i, c))
    full = lambda a: pl.BlockSpec(a.shape, lambda i: (0,) * a.ndim)
    return pl.pallas_call(
        _mix_kernel,
        grid=(n // tm,),
        in_specs=[row(d), row(d, COL_GRNN // d), row(d, COL_GATTN // d), row(d), row(w),
                  full(wr), full(wa), full(wo), full(lng), full(lnb), full(rw), full(rb)],
        out_specs=[pl.BlockSpec((tm * TOKEN_TILE, LANES), lambda i: (i, 0)), row(ROUTE_W), row(ROUTE_W)],
        out_shape=[jax.ShapeDtypeStruct((n * TOKEN_TILE, LANES), jnp.float32),
                   jax.ShapeDtypeStruct((n, ROUTE_W), jnp.int32),
                   jax.ShapeDtypeStruct((n, ROUTE_W), jnp.float32)],
        compiler_params=_cparams(1),
        name="mix_and_route",
    )(x, proj, proj, h_rnn, y_attn, wr, wa, wo, lng, lnb, rw, rb)


def _rank_kernel(e_ref, dest_ref, cnt_ref, carry, *, tm):
    ph = pl.program_id(0)
    i = pl.program_id(1)
    lanes = ROUTE_W

    @pl.when((ph == 0) & (i == 0))
    def _():
        carry[...] = jnp.zeros_like(carry)

    @pl.when((ph == 1) & (i == 0))
    def _():
        cnt = carry[...]
        cnt_ref[...] = jnp.broadcast_to(cnt, (lanes, lanes)).astype(jnp.int32)
        padded = jnp.floor((cnt + (MOE_BLK - 1)) * (1.0 / MOE_BLK)) * MOE_BLK
        r = lax.broadcasted_iota(jnp.int32, (lanes, lanes), 0)
        c = lax.broadcasted_iota(jnp.int32, (lanes, lanes), 1)
        lower = (c < r).astype(jnp.float32)
        pstart = jnp.dot(lower, jnp.broadcast_to(padded, (lanes, lanes)),
                         preferred_element_type=jnp.float32, precision=lax.Precision.HIGHEST)
        carry[...] = pstart[:, 0:1]

    r8 = lax.broadcasted_iota(jnp.int32, (SUBLANES, lanes), 0)
    c8 = lax.broadcasted_iota(jnp.int32, (SUBLANES, lanes), 1)
    pick = ((r8 == c8) & (r8 < TOP_K)).astype(jnp.bfloat16)
    ef = e_ref[...].astype(jnp.float32).astype(jnp.bfloat16)
    et = lax.dot_general(pick, ef, (((1,), (1,)), ((), ())), preferred_element_type=jnp.float32)
    e0, e1 = et[0:1, :], et[1:2, :]
    sub = lax.broadcasted_iota(jnp.int32, (lanes, tm), 0).astype(jnp.float32)
    is0, is1 = sub == e0, sub == e1
    member = jnp.where(is0 | is1, 1.0, 0.0)
    s_i = lax.broadcasted_iota(jnp.int32, (tm, tm), 0)
    t_i = lax.broadcasted_iota(jnp.int32, (tm, tm), 1)
    earlier = (s_i < t_i).astype(jnp.bfloat16)
    before = jnp.dot(member.astype(jnp.bfloat16), earlier, preferred_element_type=jnp.float32)
    pos = before + carry[...]
    d0 = jnp.sum(jnp.where(is0, pos, 0.0), axis=0, keepdims=True)
    d1 = jnp.sum(jnp.where(is1, pos, 0.0), axis=0, keepdims=True)
    row = lax.broadcasted_iota(jnp.int32, (SUBLANES, tm), 0)
    dest_ref[...] = jnp.where(row == 0, d0, jnp.where(row == 1, d1, 0.0)).astype(jnp.int32)
    carry[...] = carry[...] + jnp.sum(member, axis=1, keepdims=True)


def route_slots(eidx, tm=512):
    n = eidx.shape[0]
    nt = n // tm
    return pl.pallas_call(
        functools.partial(_rank_kernel, tm=tm),
        grid=(2, nt),
        in_specs=[pl.BlockSpec((tm, ROUTE_W), lambda ph, i: (i, 0))],
        out_specs=[pl.BlockSpec((SUBLANES, tm), lambda ph, i: (0, i * ph)),
                   pl.BlockSpec((ROUTE_W, ROUTE_W), lambda ph, i: (0, 0))],
        out_shape=[jax.ShapeDtypeStruct((SUBLANES, n), jnp.int32),
                   jax.ShapeDtypeStruct((ROUTE_W, ROUTE_W), jnp.int32)],
        scratch_shapes=[pltpu.VMEM((ROUTE_W, 1), jnp.float32)],
        compiler_params=_cparams(2),
        name="route_slots",
    )(eidx)


def _slot_map_kernel(dest_ref, slot_ref, *, n, slots):
    def init(j, c):
        slot_ref[j] = 0
        return c

    lax.fori_loop(0, slots, init, 0, unroll=32)

    def body(t, c):
        slot_ref[dest_ref[t]] = t
        slot_ref[dest_ref[n + t]] = t
        return c

    lax.fori_loop(0, n, body, 0, unroll=16)


def slot_map(dest_flat, n, slots):
    return pl.pallas_call(
        functools.partial(_slot_map_kernel, n=n, slots=slots),
        in_specs=[pl.BlockSpec(memory_space=pltpu.SMEM)],
        out_specs=pl.BlockSpec(memory_space=pltpu.SMEM),
        out_shape=jax.ShapeDtypeStruct((slots,), jnp.int32),
        name="slot_map",
    )(dest_flat)


def _expert_kernel(be_ref, nused_ref, st_ref, x_hbm, wg_ref, wu_ref, wd_ref, y_ref, xs, land, sem):
    i = pl.program_id(0)
    n_used = nused_ref[0]

    def gather_block(blk):
        base = blk * MOE_BLK
        for j in range(MOE_BLK):
            src = pl.multiple_of(st_ref[base + j] * TOKEN_TILE, TOKEN_TILE)
            pltpu.make_async_copy(x_hbm.at[pl.ds(src, TOKEN_TILE)], land.at[pl.ds(j * TOKEN_TILE, TOKEN_TILE)],
                                  sem.at[0]).start()

    def wait_block():
        pltpu.make_async_copy(x_hbm.at[pl.ds(0, MOE_BLK * TOKEN_TILE)], land, sem.at[0]).wait()

    @pl.when(i == 0)
    def _():
        gather_block(0)

    @pl.when(i < n_used)
    def _():
        wait_block()
        for c, chunk in enumerate(_load_token_tiles(land, 0, MOE_BLK)):
            xs[:, c * LANES:(c + 1) * LANES] = chunk.astype(jnp.bfloat16)
        gather_block(i + 1)
        xb = xs[...]
        gate = jnp.dot(xb, wg_ref[0], preferred_element_type=jnp.float32)
        up = jnp.dot(xb, wu_ref[0], preferred_element_type=jnp.float32)
        hid = (jax.nn.silu(gate) * up).astype(jnp.bfloat16)
        _store_token_tiles(y_ref, jnp.dot(hid, wd_ref[0], preferred_element_type=jnp.float32))

    @pl.when(i == n_used - 1)
    def _():
        wait_block()

    @pl.when(i >= n_used)
    def _():
        y_ref[...] = jnp.zeros_like(y_ref)


def expert_blocks(block_e, n_used, slot_tok, x1t, wg, wu, wd):
    d = D_MODEL
    n_blocks = slot_tok.shape[0] // MOE_BLK - 1
    slots = n_blocks * MOE_BLK
    tile_rows = MOE_BLK * TOKEN_TILE
    grid_spec = pltpu.PrefetchScalarGridSpec(
        num_scalar_prefetch=3,
        grid=(n_blocks,),
        in_specs=[pl.BlockSpec(memory_space=pl.ANY),
                  pl.BlockSpec((1, d, D_EXPERT), lambda i, be, nu, st: (be[i], 0, 0)),
                  pl.BlockSpec((1, d, D_EXPERT), lambda i, be, nu, st: (be[i], 0, 0)),
                  pl.BlockSpec((1, D_EXPERT, d), lambda i, be, nu, st: (be[i], 0, 0))],
        out_specs=pl.BlockSpec((tile_rows, LANES), lambda i, be, nu, st: (i, 0)),
        scratch_shapes=[pltpu.VMEM((MOE_BLK, d), jnp.bfloat16),
                        pltpu.VMEM((tile_rows, LANES), jnp.float32),
                        pltpu.SemaphoreType.DMA((1,))],
    )
    return pl.pallas_call(
        _expert_kernel,
        grid_spec=grid_spec,
        out_shape=jax.ShapeDtypeStruct((slots * TOKEN_TILE, LANES), jnp.float32),
        compiler_params=_cparams(1),
        name="expert_blocks",
    )(block_e, n_used, slot_tok, x1t, wg, wu, wd)


def _combine_kernel(dest_ref, x_ref, gate_ref, y_hbm, g_ref, b_ref, xo_ref, xob_ref, fbuf, sem, *, n, tm):
    i = pl.program_id(0)
    nt = pl.num_programs(0)
    half = tm * TOKEN_TILE

    def gather_tile(tile, slot):
        base = tile * tm
        for k in range(TOP_K):
            for t in range(tm):
                src = pl.multiple_of(dest_ref[k * n + base + t] * TOKEN_TILE, TOKEN_TILE)
                pltpu.make_async_copy(y_hbm.at[pl.ds(src, TOKEN_TILE)],
                                      fbuf.at[slot, pl.ds((k * tm + t) * TOKEN_TILE, TOKEN_TILE)],
                                      sem.at[slot]).start()

    def wait_tile(slot):
        pltpu.make_async_copy(y_hbm.at[pl.ds(0, TOP_K * half)], fbuf.at[slot], sem.at[slot]).wait()

    @pl.when(i == 0)
    def _():
        gather_tile(0, 0)

    slot = i % 2
    wait_tile(slot)
    g0, g1 = gate_ref[:, 0:1], gate_ref[:, 1:2]
    f0 = _load_token_tiles(fbuf.at[slot], 0, tm)
    f1 = _load_token_tiles(fbuf.at[slot], tm, tm)
    xs = _load_token_tiles(x_ref, 0, tm)
    gather_tile(jnp.minimum(i + 1, nt - 1), 1 - slot)
    z = [ALPHA * xs[c] + (g0 * f0[c] + g1 * f1[c]) for c in range(TOKEN_TILE)]
    mu = sum(jnp.sum(zc, axis=-1, keepdims=True) for zc in z) * (1.0 / D_MODEL)
    zc = [zz - mu for zz in z]
    var = sum(jnp.sum(v * v, axis=-1, keepdims=True) for v in zc) * (1.0 / D_MODEL)
    inv = lax.rsqrt(var + LN_EPS)
    for c in range(TOKEN_TILE):
        cols = slice(c * LANES, (c + 1) * LANES)
        x2 = zc[c] * inv * g_ref[:, cols] + b_ref[:, cols]
        xo_ref[:, cols] = x2
        xob_ref[:, cols] = x2.astype(jnp.bfloat16)

    @pl.when(i == nt - 1)
    def _():
        wait_tile(1 - slot)


def combine_ln2(dest_flat, x1t, gate, ybt, g, b, tm=256):
    n = gate.shape[0]
    d = D_MODEL
    tile_rows = tm * TOKEN_TILE
    grid_spec = pltpu.PrefetchScalarGridSpec(
        num_scalar_prefetch=1,
        grid=(n // tm,),
        in_specs=[pl.BlockSpec((tile_rows, LANES), lambda i, ds: (i, 0)),
                  pl.BlockSpec((tm, ROUTE_W), lambda i, ds: (i, 0)),
                  pl.BlockSpec(memory_space=pl.ANY),
                  pl.BlockSpec((1, d), lambda i, ds: (0, 0)),
                  pl.BlockSpec((1, d), lambda i, ds: (0, 0))],
        out_specs=[pl.BlockSpec((tm, d), lambda i, ds: (i, 0)),
                   pl.BlockSpec((tm, d), lambda i, ds: (i, 0))],
        scratch_shapes=[pltpu.VMEM((2, TOP_K * tile_rows, LANES), jnp.float32),
                        pltpu.SemaphoreType.DMA((2,))],
    )
    return pl.pallas_call(
        functools.partial(_combine_kernel, n=n, tm=tm),
        grid_spec=grid_spec,
        out_shape=[jax.ShapeDtypeStruct((n, d), jnp.float32),
                   jax.ShapeDtypeStruct((n, d), jnp.bfloat16)],
        compiler_params=_cparams(1),
        name="combine_ln2",
    )(dest_flat, x1t, gate, ybt, g, b)


def moe_layer(x1, eidx, gate, wg, wu, wd, ln_g, ln_b):
    n = eidx.shape[0]
    m = n * TOP_K
    n_blocks = m // MOE_BLK + N_EXPERTS
    slots = n_blocks * MOE_BLK
    dest8, cnt = route_slots(eidx)
    dest_flat = dest8[:TOP_K].reshape(m)
    counts = cnt[:N_EXPERTS, 0]
    pend = jnp.cumsum((counts + MOE_BLK - 1) // MOE_BLK * MOE_BLK)
    blk_start = jnp.arange(n_blocks, dtype=jnp.int32) * MOE_BLK
    block_e = jnp.minimum(jnp.sum((pend[None, :] <= blk_start[:, None]).astype(jnp.int32), axis=1),
                          N_EXPERTS - 1)
    n_used = (pend[-1:] // MOE_BLK).astype(jnp.int32)
    slot_tok = slot_map(dest_flat, n, slots + MOE_BLK)
    yb = expert_blocks(block_e, n_used, slot_tok, x1, wg, wu, wd)
    return combine_ln2(dest_flat, x1, gate, yb, ln_g, ln_b)


def _t5_bucket(rel):
    half = NUM_BUCKETS // 2
    max_exact = half // 2
    n = np.abs(rel)
    large = max_exact + (np.log(np.maximum(n, 1) / max_exact) / np.log(MAX_DISTANCE / max_exact)
                         * (half - max_exact)).astype(np.int32)
    large = np.minimum(large, half - 1)
    return np.where(rel > 0, half, 0) + np.where(n < max_exact, n, large)


def _bias_table(rel_bias, g, dil):
    rel = (np.arange(3 * RADIUS)[None, :] - RADIUS - np.arange(RADIUS)[:, None]) * dil
    hs = slice(g * HEADS_PER_GROUP, (g + 1) * HEADS_PER_GROUP)
    onehot = jnp.asarray(_t5_bucket(rel)[..., None] == np.arange(NUM_BUCKETS), jnp.float32)
    return jnp.einsum('qkb,bh->hqk', onehot, rel_bias[:, hs].astype(jnp.float32),
                      precision=lax.Precision.HIGHEST)


def _chunk_block_diag(w):
    per = RNN_CHUNK // RNN_BLOCK_W
    n_chunks = RNN_BLOCKS // per
    w4 = w.reshape(n_chunks, per, RNN_BLOCK_W, RNN_BLOCK_W)
    dense = jnp.einsum('chij,hk->chikj', w4, jnp.eye(per, dtype=w.dtype))
    return dense.reshape(n_chunks, RNN_CHUNK, RNN_CHUNK)


def _gate_weights(wa, wi):
    per_dir = [jnp.concatenate([_chunk_block_diag(wa[d]), _chunk_block_diag(wi[d])], axis=-1)
               for d in range(2)]
    return jnp.stack(per_dir, axis=1).astype(jnp.bfloat16)


def _per_chunk(v):
    return v.reshape(2, D_RNN // RNN_CHUNK, 1, RNN_CHUNK).transpose(1, 0, 2, 3)


def _permute_in_cols(a):
    x_rnn, q, k, v, g_rnn, g_attn = jnp.split(
        a, [D_RNN, D_RNN + D_ATTN, D_RNN + 2 * D_ATTN, D_RNN + 3 * D_ATTN,
            D_RNN + 3 * D_ATTN + D_MODEL], axis=-1)
    return jnp.concatenate([x_rnn, g_rnn, g_attn, q, k, v], axis=-1)


def _trunk(x, biases, lp):
    b, s, d = x.shape
    n = b * s
    x2 = x.reshape(n, d)
    x2_bf = x2.astype(jnp.bfloat16)
    for l in range(DEPTH):
        p = lp[l]
        proj = in_projection(x2_bf, p['w_in'], p['b_in'])
        proj3 = proj.reshape(b, s, D_IN)
        h_rnn = rglru_branch(proj3, p['conv_w'], p['conv_b'], p['wg'], p['gb'], p['lam'])
        y_attn = dilated_attention(proj3, biases)
        x1, eidx, gate = mix_and_route(
            x2, proj, h_rnn.reshape(n, D_RNN), y_attn.reshape(n, D_ATTN_OUT),
            p['w_rnn_out'], p['w_attn_out'], p['w_o'],
            p['ln1_g'], p['ln1_b'], p['router_w'], p['router_b'])
        x2, x2_bf = moe_layer(x1, eidx, gate, p['w_gate'], p['w_up'], p['w_down'], p['ln2_g'], p['ln2_b'])
    return x2.reshape(b, s, d)


def kernel(x_prompt, x_sample, rel_bias, w_in, b_in, conv_w, conv_b, rg_wa, rg_ba, rg_wi, rg_bi, rg_lam, w_rnn_out, w_attn_out, w_o, ln1_g, ln1_b, router_w, router_b, expert_router_w, expert_router_b, w_gate, w_up, w_down, ln2_g, ln2_b):
    bf = jnp.bfloat16
    biases = [_bias_table(rel_bias, g, dil) for g, (_, dil) in enumerate(DILATED_CONFIGS)]
    lp = []
    for l in range(DEPTH):
        rw = jnp.concatenate(
            [router_w[l], jnp.transpose(expert_router_w[l], (1, 0, 2)).reshape(D_MODEL, N_EXPERTS)], axis=1)
        rb = jnp.concatenate([router_b[l], expert_router_b[l].reshape(N_EXPERTS)])
        pad = ROUTE_W - rw.shape[1]
        lp.append(dict(
            w_in=_permute_in_cols(w_in[l]).astype(bf),
            b_in=_permute_in_cols(b_in[l])[None, :],
            conv_w=conv_w[l], conv_b=conv_b[l][None, :],
            wg=_gate_weights(rg_wa[l], rg_wi[l]),
            gb=jnp.concatenate([_per_chunk(rg_ba[l]), _per_chunk(rg_bi[l])], axis=-1),
            lam=_per_chunk(rg_lam[l]),
            w_rnn_out=w_rnn_out[l].astype(bf), w_attn_out=w_attn_out[l].astype(bf), w_o=w_o[l].astype(bf),
            ln1_g=ln1_g[l][None, :], ln1_b=ln1_b[l][None, :],
            router_w=jnp.pad(rw, ((0, 0), (0, pad))), router_b=jnp.pad(rb, (0, pad))[None, :],
            w_gate=w_gate[l].astype(bf), w_up=w_up[l].astype(bf), w_down=w_down[l].astype(bf),
            ln2_g=ln2_g[l][None, :], ln2_b=ln2_b[l][None, :]))
    return (_trunk(x_prompt, biases, lp), _trunk(x_sample, biases, lp))
```

```python
import functools

import numpy as np
import jax
import jax.numpy as jnp
from jax import lax
from jax.experimental import pallas as pl
from jax.experimental.pallas import tpu as pltpu

D_MODEL = 1024
DEPTH = 2
D_RNN = D_MODEL
RNN_BLOCKS = 16
RNN_BLOCK_W = D_RNN // RNN_BLOCKS
CONV_WIDTH = 4
RG_C = 8.0
DILATED_CONFIGS = ((128, 1), (512, 4), (2048, 16))
N_ATTN_GROUPS = len(DILATED_CONFIGS)
HEADS_PER_GROUP = 4
N_ATTN_HEADS = N_ATTN_GROUPS * HEADS_PER_GROUP
HEAD_DIM = 128
D_ATTN = N_ATTN_HEADS * HEAD_DIM
D_ATTN_OUT = HEADS_PER_GROUP * HEAD_DIM
NUM_BUCKETS = 32
MAX_DISTANCE = max(w for w, _ in DILATED_CONFIGS) // 2
NEG_INF = -1e30
D_IN = D_RNN + 3 * D_ATTN + 2 * D_MODEL
N_EXPERT_GROUPS = 4
EXPERTS_PER_GROUP = 8
N_EXPERTS = N_EXPERT_GROUPS * EXPERTS_PER_GROUP
TOP_K = 2
D_EXPERT = D_MODEL // 2
ALPHA = (2 * DEPTH) ** 0.25
LN_EPS = 1e-5

LANES = 128
SUBLANES = 8
VMEM_LIMIT = 56 * 1024 * 1024

COL_XRNN = 0
COL_GRNN = D_RNN
COL_GATTN = D_RNN + D_MODEL
COL_Q = D_RNN + 2 * D_MODEL
COL_K = COL_Q + D_ATTN
COL_V = COL_K + D_ATTN

RADIUS = 64
assert all(w // (2 * d) == RADIUS for w, d in DILATED_CONFIGS)
ATTN_ROWS = 1024
RNN_CHUNK = 256
RNN_TILE = 256
MOE_BLK = 256
DMA_QUEUES = 2
TOKEN_TILE = D_MODEL // LANES
assert TOKEN_TILE == SUBLANES
ROUTE_W = LANES


def _cparams(n_axes):
    return pltpu.CompilerParams(dimension_semantics=("arbitrary",) * n_axes,
                                vmem_limit_bytes=VMEM_LIMIT)


def _proj_kernel(x_ref, w_ref, b_ref, o_ref):
    acc = jnp.dot(x_ref[...], w_ref[...], preferred_element_type=jnp.float32)
    o_ref[...] = acc + b_ref[...]


def in_projection(x_bf, w_bf, b, tm=512, tn=1536):
    n, k = x_bf.shape
    nout = w_bf.shape[1]
    return pl.pallas_call(
        _proj_kernel,
        grid=(nout // tn, n // tm),
        in_specs=[pl.BlockSpec((tm, k), lambda j, i: (i, 0)),
                  pl.BlockSpec((k, tn), lambda j, i: (0, j)),
                  pl.BlockSpec((1, tn), lambda j, i: (0, j))],
        out_specs=pl.BlockSpec((tm, tn), lambda j, i: (i, j)),
        out_shape=jax.ShapeDtypeStruct((n, nout), jnp.float32),
        compiler_params=_cparams(2),
        name="in_projection",
    )(x_bf, w_bf, b)


def _rglru_kernel(x_ref, cw_ref, cb_ref, wg_ref, gb_ref, lam_ref, o_ref,
                  hsum, a_s, u_s, h_s, *, seq, tile):
    c = RNN_CHUNK
    n_tiles = seq // tile
    n_grp = tile // SUBLANES
    big_rows = tile + 2 * SUBLANES
    row_in_grp = lax.broadcasted_iota(jnp.int32, (n_grp, SUBLANES, c), 1)

    def conv_tile(t0):
        centre = x_ref[0, pl.ds(t0, tile), :]
        p0 = pl.multiple_of(jnp.maximum(t0 - SUBLANES, 0), SUBLANES)
        n0 = pl.multiple_of(jnp.minimum(t0 + tile, seq - SUBLANES), SUBLANES)
        prev = jnp.where(t0 > 0, x_ref[0, pl.ds(p0, SUBLANES), :], 0.0)
        nxt = jnp.where(t0 + tile < seq, x_ref[0, pl.ds(n0, SUBLANES), :], 0.0)
        big = jnp.concatenate([prev, centre, nxt], axis=0)
        xc = cb_ref[...] + cw_ref[1:2, :] * centre
        for k in (0, 2, 3):
            off = k - 1
            tap = pltpu.roll(big, (-off) % big_rows, 0)[SUBLANES:SUBLANES + tile]
            xc = xc + cw_ref[k:k + 1, :] * tap
        return xc

    def gates(xc, dr):
        g = jnp.dot(xc.astype(jnp.bfloat16), wg_ref[0, dr],
                    preferred_element_type=jnp.float32) + gb_ref[0, dr]
        gate_r = jax.nn.sigmoid(g[:, :c])
        gate_i = jax.nn.sigmoid(g[:, c:])
        neg_lam = -lam_ref[0, dr]
        softplus = jnp.maximum(neg_lam, 0.0) + jnp.log1p(jnp.exp(-jnp.abs(neg_lam)))
        log_a = -RG_C * gate_r * softplus
        a = jnp.exp(log_a)
        th = jnp.tanh(log_a)
        one_minus_a2 = -2.0 * th / (1.0 - th)
        u = jnp.sqrt(one_minus_a2) * (gate_i * xc)
        return a, u

    def tile_prefix(a, u, reverse):
        a = a.reshape(n_grp, SUBLANES, c)
        u = u.reshape(n_grp, SUBLANES, c)
        for s in (1, 2, 4):
            if reverse:
                a_sh = pltpu.roll(a, SUBLANES - s, 1)
                u_sh = pltpu.roll(u, SUBLANES - s, 1)
                m = row_in_grp < SUBLANES - s
            else:
                a_sh = pltpu.roll(a, s, 1)
                u_sh = pltpu.roll(u, s, 1)
                m = row_in_grp >= s
            u = jnp.where(m, u + a * u_sh, u)
            a = jnp.where(m, a * a_sh, a)
        a_s[...] = a.reshape(tile, c)
        u_s[...] = u.reshape(tile, c)

    def run_direction(dr, reverse):
        def tile_body(ti, h):
            t_idx = (n_tiles - 1 - ti) if reverse else ti
            t0 = pl.multiple_of(t_idx * tile, tile)
            xc = conv_tile(t0)
            a, u = gates(xc, dr)
            tile_prefix(a, u, reverse)

            def grp_body(gi, hc):
                g_idx = (n_grp - 1 - gi) if reverse else gi
                r0 = pl.multiple_of(g_idx * SUBLANES, SUBLANES)
                hg = u_s[pl.ds(r0, SUBLANES), :] + a_s[pl.ds(r0, SUBLANES), :] * hc
                h_s[pl.ds(r0, SUBLANES), :] = hg
                edge = hg[0:1, :] if reverse else hg[SUBLANES - 1:SUBLANES, :]
                return jnp.broadcast_to(edge, (SUBLANES, c))

            h = lax.fori_loop(0, n_grp, grp_body, h, unroll=4)
            if reverse:
                o_ref[0, pl.ds(t0, tile), :] = (hsum[pl.ds(t0, tile), :] + h_s[...]).astype(o_ref.dtype)
            else:
                hsum[pl.ds(t0, tile), :] = h_s[...]
            return h

        lax.fori_loop(0, n_tiles, tile_body, jnp.zeros((SUBLANES, c), jnp.float32))

    run_direction(0, False)
    run_direction(1, True)


def rglru_branch(proj3, cw, cb, wg, gb, lam):
    b, s, _ = proj3.shape
    c = RNN_CHUNK
    n_chunks = D_RNN // c
    kern = functools.partial(_rglru_kernel, seq=s, tile=RNN_TILE)
    return pl.pallas_call(
        kern,
        grid=(b, n_chunks),
        in_specs=[pl.BlockSpec((1, s, c), lambda bi, ci: (bi, 0, COL_XRNN // c + ci)),
                  pl.BlockSpec((CONV_WIDTH, c), lambda bi, ci: (0, ci)),
                  pl.BlockSpec((1, c), lambda bi, ci: (0, ci)),
                  pl.BlockSpec((1, 2, c, 2 * c), lambda bi, ci: (ci, 0, 0, 0)),
                  pl.BlockSpec((1, 2, 1, 2 * c), lambda bi, ci: (ci, 0, 0, 0)),
                  pl.BlockSpec((1, 2, 1, c), lambda bi, ci: (ci, 0, 0, 0))],
        out_specs=pl.BlockSpec((1, s, c), lambda bi, ci: (bi, 0, ci)),
        out_shape=jax.ShapeDtypeStruct((b, s, D_RNN), jnp.bfloat16),
        scratch_shapes=[pltpu.VMEM((s, c), jnp.float32),
                        pltpu.VMEM((RNN_TILE, c), jnp.float32),
                        pltpu.VMEM((RNN_TILE, c), jnp.float32),
                        pltpu.VMEM((RNN_TILE, c), jnp.float32)],
        compiler_params=_cparams(2),
        name="rglru_branch",
    )(proj3, cw, cb, wg, gb, lam)


def _attn_group(refs, bias, o_dst, l_dst, *, dil, is_first, is_last):
    q_ref, kp_ref, kc_ref, kn_ref, vp_ref, vc_ref, vn_ref = refs
    rows = ATTN_ROWS
    per_res = rows // dil
    n_q = per_res // RADIUS
    win = 3 * RADIUS
    nb = dil * n_q

    def fold(ref, r, n):
        if dil == 1:
            return ref[0, pl.ds(r, n), :]
        return ref[0, pl.ds(r, n, stride=dil), :]

    qs, ks, vs = [], [], []
    for r in range(dil):
        qf = fold(q_ref, r, per_res).astype(jnp.bfloat16)
        kf = jnp.concatenate([fold(kp_ref, r, RADIUS), fold(kc_ref, r, per_res),
                              fold(kn_ref, r, RADIUS)], axis=0).astype(jnp.bfloat16)
        vf = jnp.concatenate([fold(vp_ref, r, RADIUS), fold(vc_ref, r, per_res),
                              fold(vn_ref, r, RADIUS)], axis=0).astype(jnp.bfloat16)
        for jq in range(n_q):
            qs.append(qf[jq * RADIUS:(jq + 1) * RADIUS])
            ks.append(kf[jq * RADIUS:jq * RADIUS + win])
            vs.append(vf[jq * RADIUS:jq * RADIUS + win])
    qb, kb, vb = jnp.stack(qs), jnp.stack(ks), jnp.stack(vs)

    shape = (nb, RADIUS, win)
    jq_of = lax.broadcasted_iota(jnp.int32, shape, 0) & (n_q - 1)
    qi = lax.broadcasted_iota(jnp.int32, shape, 1)
    kj = lax.broadcasted_iota(jnp.int32, shape, 2)
    lo = jnp.where(jq_of == 0, jnp.where(is_first, RADIUS, 0), 0)
    hi = jnp.where(jq_of == n_q - 1, jnp.where(is_last, 2 * RADIUS, win), win)
    mask = (jnp.abs(kj - RADIUS - qi) <= RADIUS) & (kj >= lo) & (kj < hi)

    s = jnp.einsum('bqd,bkd->bqk', qb, kb, preferred_element_type=jnp.float32) * (HEAD_DIM ** -0.5)
    s = jnp.where(mask, s + bias[None], NEG_INF)
    m = jnp.max(s, axis=-1, keepdims=True)
    p = jnp.exp(s - m)
    l = jnp.sum(p, axis=-1, keepdims=True)
    o = jnp.einsum('bqk,bkd->bqd', p.astype(jnp.bfloat16), vb, preferred_element_type=jnp.float32) / l
    lse = m + jnp.log(l)
    for bi in range(nb):
        r, jq = divmod(bi, n_q)
        start = jq * RADIUS * dil + r
        idx = pl.ds(start, RADIUS) if dil == 1 else pl.ds(start, RADIUS, stride=dil)
        o_dst[idx, :] = o[bi]
        l_dst[idx, :] = jnp.broadcast_to(lse[bi], (RADIUS, HEAD_DIM))


def _attn_kernel(*refs, n_row_blocks):
    n_g = N_ATTN_GROUPS
    in_refs = refs[:7 * n_g]
    bias_refs = refs[7 * n_g:8 * n_g]
    y_ref = refs[8 * n_g]
    o_s, l_s = refs[8 * n_g + 1:]
    i = pl.program_id(1)
    for g, (_, dil) in enumerate(DILATED_CONFIGS):
        _attn_group(in_refs[7 * g:7 * g + 7], bias_refs[g][0], o_s.at[g], l_s.at[g],
                    dil=dil, is_first=i == 0, is_last=i == n_row_blocks - 1)
    l0, l1, l2 = l_s[0], l_s[1], l_s[2]
    m = jnp.maximum(jnp.maximum(l0, l1), l2)
    e0, e1, e2 = jnp.exp(l0 - m), jnp.exp(l1 - m), jnp.exp(l2 - m)
    y = (e0 * o_s[0] + e1 * o_s[1] + e2 * o_s[2]) / (e0 + e1 + e2)
    y_ref[0] = y.astype(y_ref.dtype)


def dilated_attention(proj3, biases):
    b, s, _ = proj3.shape
    rows = ATTN_ROWS
    nrb = s // rows
    w = HEAD_DIM
    nh = HEADS_PER_GROUP

    def cur_map(col):
        return lambda bi, i, h: (bi, i, col + h)

    in_specs, operands = [], []
    for g, (_, dil) in enumerate(DILATED_CONFIGS):
        halo = RADIUS * dil
        ratio = rows // halo
        n_halo_blocks = s // halo
        qc, kc, vc = COL_Q // w + g * nh, COL_K // w + g * nh, COL_V // w + g * nh

        def prev_map(col, ratio=ratio):
            return lambda bi, i, h: (bi, jnp.maximum(i * ratio - 1, 0), col + h)

        def next_map(col, ratio=ratio, last=n_halo_blocks - 1):
            return lambda bi, i, h: (bi, jnp.minimum((i + 1) * ratio, last), col + h)

        in_specs += [pl.BlockSpec((1, rows, w), cur_map(qc)),
                     pl.BlockSpec((1, halo, w), prev_map(kc)),
                     pl.BlockSpec((1, rows, w), cur_map(kc)),
                     pl.BlockSpec((1, halo, w), next_map(kc)),
                     pl.BlockSpec((1, halo, w), prev_map(vc)),
                     pl.BlockSpec((1, rows, w), cur_map(vc)),
                     pl.BlockSpec((1, halo, w), next_map(vc))]
        operands += [proj3] * 7
    in_specs += [pl.BlockSpec((1, RADIUS, 3 * RADIUS), lambda bi, i, h: (h, 0, 0))] * N_ATTN_GROUPS
    operands += list(biases)
    return pl.pallas_call(
        functools.partial(_attn_kernel, n_row_blocks=nrb),
        grid=(b, nrb, nh),
        in_specs=in_specs,
        out_specs=pl.BlockSpec((1, rows, w), cur_map(0)),
        out_shape=jax.ShapeDtypeStruct((b, s, D_ATTN_OUT), jnp.bfloat16),
        scratch_shapes=[pltpu.VMEM((N_ATTN_GROUPS, rows, w), jnp.float32),
                        pltpu.VMEM((N_ATTN_GROUPS, rows, w), jnp.float32)],
        compiler_params=_cparams(3),
        name="dilated_attention",
    )(*operands)


def _store_token_tiles(ref, val):
    rows = val.shape[0]
    for c in range(TOKEN_TILE):
        ref[pl.ds(c, rows, stride=TOKEN_TILE), :] = val[:, c * LANES:(c + 1) * LANES]


def _load_token_tiles(ref, start, rows):
    return [ref[pl.ds(start * TOKEN_TILE + c, rows, stride=TOKEN_TILE), :] for c in range(TOKEN_TILE)]


def _layer_norm(z, g, b):
    mu = jnp.mean(z, axis=-1, keepdims=True)
    zc = z - mu
    var = jnp.mean(zc * zc, axis=-1, keepdims=True)
    return zc * lax.rsqrt(var + LN_EPS) * g + b


def _mix_kernel(x_ref, grnn_ref, gattn_ref, h_ref, y_ref,
                wr_ref, wa_ref, wo_ref, lng_ref, lnb_ref, rw_ref, rb_ref,
                xo_ref, eidx_ref, gate_ref):
    rnn = jnp.dot(h_ref[...], wr_ref[...], preferred_element_type=jnp.float32)
    att = jnp.dot(y_ref[...], wa_ref[...], preferred_element_type=jnp.float32)
    mixed = jax.nn.sigmoid(grnn_ref[...]) * rnn + jax.nn.sigmoid(gattn_ref[...]) * att
    z = ALPHA * x_ref[...] + jnp.dot(mixed.astype(jnp.bfloat16), wo_ref[...],
                                     preferred_element_type=jnp.float32)
    x1 = _layer_norm(z, lng_ref[...], lnb_ref[...])
    _store_token_tiles(xo_ref, x1)

    logits = jnp.dot(x1, rw_ref[...], preferred_element_type=jnp.float32,
                     precision=lax.Precision.HIGHEST) + rb_ref[...]
    tm = logits.shape[0]
    lane = lax.broadcasted_iota(jnp.int32, (tm, ROUTE_W), 1)
    lane_f = lane.astype(jnp.float32)
    far = float(ROUTE_W)
    is_group = lane < N_EXPERT_GROUPS
    gl = jnp.where(is_group, logits, -jnp.inf)
    gmax = jnp.max(gl, axis=-1, keepdims=True)
    gsel = jnp.min(jnp.where(gl == gmax, lane_f, far), axis=-1, keepdims=True)
    p_group = 1.0 / jnp.sum(jnp.where(is_group, jnp.exp(logits - gmax), 0.0), axis=-1, keepdims=True)
    lane_group = ((lane - N_EXPERT_GROUPS) >> 3).astype(jnp.float32)
    in_group = (lane >= N_EXPERT_GROUPS) & (lane < N_EXPERT_GROUPS + N_EXPERTS) & (lane_group == gsel)
    el = jnp.where(in_group, logits, -jnp.inf)
    m1 = jnp.max(el, axis=-1, keepdims=True)
    i1 = jnp.min(jnp.where(el == m1, lane_f, far), axis=-1, keepdims=True)
    el2 = jnp.where(lane_f == i1, -jnp.inf, el)
    m2 = jnp.max(el2, axis=-1, keepdims=True)
    i2 = jnp.min(jnp.where(el2 == m2, lane_f, far), axis=-1, keepdims=True)
    e21 = jnp.exp(m2 - m1)
    den = 1.0 + e21
    g1 = p_group * (1.0 / den)
    g2 = p_group * (e21 / den)
    eidx = jnp.where(lane == 0, i1, jnp.where(lane == 1, i2, float(N_EXPERT_GROUPS))) - float(N_EXPERT_GROUPS)
    eidx_ref[...] = eidx.astype(jnp.int32)
    gate_ref[...] = jnp.where(lane == 0, g1, jnp.where(lane == 1, g2, 0.0))


def mix_and_route(x, proj, h_rnn, y_attn, wr, wa, wo, lng, lnb, rw, rb, tm=512):
    n = x.shape[0]
    d = D_MODEL
    w = D_ATTN_OUT
    row = lambda width, col=0: pl.BlockSpec((tm, width), lambda i, c=col: (i, c))
    full = lambda a: pl.BlockSpec(a.shape, lambda i: (0,) * a.ndim)
    return pl.pallas_call(
        _mix_kernel,
        grid=(n // tm,),
        in_specs=[row(d), row(d, COL_GRNN // d), row(d, COL_GATTN // d), row(d), row(w),
                  full(wr), full(wa), full(wo), full(lng), full(lnb), full(rw), full(rb)],
        out_specs=[pl.BlockSpec((tm * TOKEN_TILE, LANES), lambda i: (i, 0)), row(ROUTE_W), row(ROUTE_W)],
        out_shape=[jax.ShapeDtypeStruct((n * TOKEN_TILE, LANES), jnp.float32),
                   jax.ShapeDtypeStruct((n, ROUTE_W), jnp.int32),
                   jax.ShapeDtypeStruct((n, ROUTE_W), jnp.float32)],
        compiler_params=_cparams(1),
        name="mix_and_route",
    )(x, proj, proj, h_rnn, y_attn, wr, wa, wo, lng, lnb, rw, rb)


def _rank_kernel(e_ref, dest_ref, cnt_ref, carry, *, tm):
    ph = pl.program_id(0)
    i = pl.program_id(1)
    lanes = ROUTE_W

    @pl.when((ph == 0) & (i == 0))
    def _():
        carry[...] = jnp.zeros_like(carry)

    @pl.when((ph == 1) & (i == 0))
    def _():
        cnt = carry[...]
        cnt_ref[...] = jnp.broadcast_to(cnt, (lanes, lanes)).astype(jnp.int32)
        padded = jnp.floor((cnt + (MOE_BLK - 1)) * (1.0 / MOE_BLK)) * MOE_BLK
        r = lax.broadcasted_iota(jnp.int32, (lanes, lanes), 0)
        c = lax.broadcasted_iota(jnp.int32, (lanes, lanes), 1)
        lower = (c < r).astype(jnp.float32)
        pstart = jnp.dot(lower, jnp.broadcast_to(padded, (lanes, lanes)),
                         preferred_element_type=jnp.float32, precision=lax.Precision.HIGHEST)
        carry[...] = pstart[:, 0:1]

    r8 = lax.broadcasted_iota(jnp.int32, (SUBLANES, lanes), 0)
    c8 = lax.broadcasted_iota(jnp.int32, (SUBLANES, lanes), 1)
    pick = ((r8 == c8) & (r8 < TOP_K)).astype(jnp.bfloat16)
    ef = e_ref[...].astype(jnp.float32).astype(jnp.bfloat16)
    et = lax.dot_general(pick, ef, (((1,), (1,)), ((), ())), preferred_element_type=jnp.float32)
    e0, e1 = et[0:1, :], et[1:2, :]
    sub = lax.broadcasted_iota(jnp.int32, (lanes, tm), 0).astype(jnp.float32)
    is0, is1 = sub == e0, sub == e1
    member = jnp.where(is0 | is1, 1.0, 0.0)
    s_i = lax.broadcasted_iota(jnp.int32, (tm, tm), 0)
    t_i = lax.broadcasted_iota(jnp.int32, (tm, tm), 1)
    earlier = (s_i < t_i).astype(jnp.bfloat16)
    before = jnp.dot(member.astype(jnp.bfloat16), earlier, preferred_element_type=jnp.float32)
    pos = before + carry[...]
    d0 = jnp.sum(jnp.where(is0, pos, 0.0), axis=0, keepdims=True)
    d1 = jnp.sum(jnp.where(is1, pos, 0.0), axis=0, keepdims=True)
    row = lax.broadcasted_iota(jnp.int32, (SUBLANES, tm), 0)
    dest_ref[...] = jnp.where(row == 0, d0, jnp.where(row == 1, d1, 0.0)).astype(jnp.int32)
    carry[...] = carry[...] + jnp.sum(member, axis=1, keepdims=True)


def route_slots(eidx, tm=512):
    n = eidx.shape[0]
    nt = n // tm
    return pl.pallas_call(
        functools.partial(_rank_kernel, tm=tm),
        grid=(2, nt),
        in_specs=[pl.BlockSpec((tm, ROUTE_W), lambda ph, i: (i, 0))],
        out_specs=[pl.BlockSpec((SUBLANES, tm), lambda ph, i: (0, i * ph)),
                   pl.BlockSpec((ROUTE_W, ROUTE_W), lambda ph, i: (0, 0))],
        out_shape=[jax.ShapeDtypeStruct((SUBLANES, n), jnp.int32),
                   jax.ShapeDtypeStruct((ROUTE_W, ROUTE_W), jnp.int32)],
        scratch_shapes=[pltpu.VMEM((ROUTE_W, 1), jnp.float32)],
        compiler_params=_cparams(2),
        name="route_slots",
    )(eidx)


def _slot_map_kernel(dest_ref, slot_ref, *, n, slots):
    def init(j, c):
        slot_ref[j] = 0
        return c

    lax.fori_loop(0, slots, init, 0, unroll=32)

    def body(t, c):
        slot_ref[dest_ref[t]] = t
        slot_ref[dest_ref[n + t]] = t
        return c

    lax.fori_loop(0, n, body, 0, unroll=16)


def slot_map(dest_flat, n, slots):
    return pl.pallas_call(
        functools.partial(_slot_map_kernel, n=n, slots=slots),
        in_specs=[pl.BlockSpec(memory_space=pltpu.SMEM)],
        out_specs=pl.BlockSpec(memory_space=pltpu.SMEM),
        out_shape=jax.ShapeDtypeStruct((slots,), jnp.int32),
        name="slot_map",
    )(dest_flat)


def _expert_kernel(be_ref, nused_ref, st_ref, x_hbm, wg_ref, wu_ref, wd_ref, y_ref, xs, land, sem):
    i = pl.program_id(0)
    n_used = nused_ref[0]

    def gather_block(blk):
        base = blk * MOE_BLK
        for j in range(MOE_BLK):
            src = pl.multiple_of(st_ref[base + j] * TOKEN_TILE, TOKEN_TILE)
            pltpu.make_async_copy(x_hbm.at[pl.ds(src, TOKEN_TILE)], land.at[pl.ds(j * TOKEN_TILE, TOKEN_TILE)],
                                  sem.at[0]).start(priority=j % DMA_QUEUES)

    def wait_block():
        pltpu.make_async_copy(x_hbm.at[pl.ds(0, MOE_BLK * TOKEN_TILE)], land, sem.at[0]).wait()

    @pl.when(i == 0)
    def _():
        gather_block(0)

    @pl.when(i < n_used)
    def _():
        wait_block()
        for c, chunk in enumerate(_load_token_tiles(land, 0, MOE_BLK)):
            xs[:, c * LANES:(c + 1) * LANES] = chunk.astype(jnp.bfloat16)
        gather_block(i + 1)
        xb = xs[...]
        gate = jnp.dot(xb, wg_ref[0], preferred_element_type=jnp.float32)
        up = jnp.dot(xb, wu_ref[0], preferred_element_type=jnp.float32)
        hid = (jax.nn.silu(gate) * up).astype(jnp.bfloat16)
        _store_token_tiles(y_ref, jnp.dot(hid, wd_ref[0], preferred_element_type=jnp.float32))

    @pl.when(i == n_used - 1)
    def _():
        wait_block()

    @pl.when(i >= n_used)
    def _():
        y_ref[...] = jnp.zeros_like(y_ref)


def expert_blocks(block_e, n_used, slot_tok, x1t, wg, wu, wd):
    d = D_MODEL
    n_blocks = slot_tok.shape[0] // MOE_BLK - 1
    slots = n_blocks * MOE_BLK
    tile_rows = MOE_BLK * TOKEN_TILE
    grid_spec = pltpu.PrefetchScalarGridSpec(
        num_scalar_prefetch=3,
        grid=(n_blocks,),
        in_specs=[pl.BlockSpec(memory_space=pl.ANY),
                  pl.BlockSpec((1, d, D_EXPERT), lambda i, be, nu, st: (be[i], 0, 0)),
                  pl.BlockSpec((1, d, D_EXPERT), lambda i, be, nu, st: (be[i], 0, 0)),
                  pl.BlockSpec((1, D_EXPERT, d), lambda i, be, nu, st: (be[i], 0, 0))],
        out_specs=pl.BlockSpec((tile_rows, LANES), lambda i, be, nu, st: (i, 0)),
        scratch_shapes=[pltpu.VMEM((MOE_BLK, d), jnp.bfloat16),
                        pltpu.VMEM((tile_rows, LANES), jnp.float32),
                        pltpu.SemaphoreType.DMA((1,))],
    )
    return pl.pallas_call(
        _expert_kernel,
        grid_spec=grid_spec,
        out_shape=jax.ShapeDtypeStruct((slots * TOKEN_TILE, LANES), jnp.float32),
        compiler_params=_cparams(1),
        name="expert_blocks",
    )(block_e, n_used, slot_tok, x1t, wg, wu, wd)


def _combine_kernel(dest_ref, x_ref, gate_ref, y_hbm, g_ref, b_ref, xo_ref, xob_ref, fbuf, sem, *, n, tm):
    i = pl.program_id(0)
    nt = pl.num_programs(0)
    half = tm * TOKEN_TILE

    def gather_tile(tile, slot):
        base = tile * tm
        for k in range(TOP_K):
            for t in range(tm):
                src = pl.multiple_of(dest_ref[k * n + base + t] * TOKEN_TILE, TOKEN_TILE)
                pltpu.make_async_copy(y_hbm.at[pl.ds(src, TOKEN_TILE)],
                                      fbuf.at[slot, pl.ds((k * tm + t) * TOKEN_TILE, TOKEN_TILE)],
                                      sem.at[slot]).start(priority=t % DMA_QUEUES)

    def wait_tile(slot):
        pltpu.make_async_copy(y_hbm.at[pl.ds(0, TOP_K * half)], fbuf.at[slot], sem.at[slot]).wait()

    @pl.when(i == 0)
    def _():
        gather_tile(0, 0)

    slot = i % 2
    wait_tile(slot)
    g0, g1 = gate_ref[:, 0:1], gate_ref[:, 1:2]
    f0 = _load_token_tiles(fbuf.at[slot], 0, tm)
    f1 = _load_token_tiles(fbuf.at[slot], tm, tm)
    xs = _load_token_tiles(x_ref, 0, tm)
    gather_tile(jnp.minimum(i + 1, nt - 1), 1 - slot)
    z = [ALPHA * xs[c] + (g0 * f0[c] + g1 * f1[c]) for c in range(TOKEN_TILE)]
    mu = sum(jnp.sum(zc, axis=-1, keepdims=True) for zc in z) * (1.0 / D_MODEL)
    zc = [zz - mu for zz in z]
    var = sum(jnp.sum(v * v, axis=-1, keepdims=True) for v in zc) * (1.0 / D_MODEL)
    inv = lax.rsqrt(var + LN_EPS)
    for c in range(TOKEN_TILE):
        cols = slice(c * LANES, (c + 1) * LANES)
        x2 = zc[c] * inv * g_ref[:, cols] + b_ref[:, cols]
        xo_ref[:, cols] = x2
        xob_ref[:, cols] = x2.astype(jnp.bfloat16)

    @pl.when(i == nt - 1)
    def _():
        wait_tile(1 - slot)


def combine_ln2(dest_flat, x1t, gate, ybt, g, b, tm=256):
    n = gate.shape[0]
    d = D_MODEL
    tile_rows = tm * TOKEN_TILE
    grid_spec = pltpu.PrefetchScalarGridSpec(
        num_scalar_prefetch=1,
        grid=(n // tm,),
        in_specs=[pl.BlockSpec((tile_rows, LANES), lambda i, ds: (i, 0)),
                  pl.BlockSpec((tm, ROUTE_W), lambda i, ds: (i, 0)),
                  pl.BlockSpec(memory_space=pl.ANY),
                  pl.BlockSpec((1, d), lambda i, ds: (0, 0)),
                  pl.BlockSpec((1, d), lambda i, ds: (0, 0))],
        out_specs=[pl.BlockSpec((tm, d), lambda i, ds: (i, 0)),
                   pl.BlockSpec((tm, d), lambda i, ds: (i, 0))],
        scratch_shapes=[pltpu.VMEM((2, TOP_K * tile_rows, LANES), jnp.float32),
                        pltpu.SemaphoreType.DMA((2,))],
    )
    return pl.pallas_call(
        functools.partial(_combine_kernel, n=n, tm=tm),
        grid_spec=grid_spec,
        out_shape=[jax.ShapeDtypeStruct((n, d), jnp.float32),
                   jax.ShapeDtypeStruct((n, d), jnp.bfloat16)],
        compiler_params=_cparams(1),
        name="combine_ln2",
    )(dest_flat, x1t, gate, ybt, g, b)


def moe_layer(x1, eidx, gate, wg, wu, wd, ln_g, ln_b):
    n = eidx.shape[0]
    m = n * TOP_K
    n_blocks = m // MOE_BLK + N_EXPERTS
    slots = n_blocks * MOE_BLK
    dest8, cnt = route_slots(eidx)
    dest_flat = dest8[:TOP_K].reshape(m)
    counts = cnt[:N_EXPERTS, 0]
    pend = jnp.cumsum((counts + MOE_BLK - 1) // MOE_BLK * MOE_BLK)
    blk_start = jnp.arange(n_blocks, dtype=jnp.int32) * MOE_BLK
    block_e = jnp.minimum(jnp.sum((pend[None, :] <= blk_start[:, None]).astype(jnp.int32), axis=1),
                          N_EXPERTS - 1)
    n_used = (pend[-1:] // MOE_BLK).astype(jnp.int32)
    slot_tok = slot_map(dest_flat, n, slots + MOE_BLK)
    yb = expert_blocks(block_e, n_used, slot_tok, x1, wg, wu, wd)
    return combine_ln2(dest_flat, x1, gate, yb, ln_g, ln_b)


def _t5_bucket(rel):
    half = NUM_BUCKETS // 2
    max_exact = half // 2
    n = np.abs(rel)
    large = max_exact + (np.log(np.maximum(n, 1) / max_exact) / np.log(MAX_DISTANCE / max_exact)
                         * (half - max_exact)).astype(np.int32)
    large = np.minimum(large, half - 1)
    return np.where(rel > 0, half, 0) + np.where(n < max_exact, n, large)


def _bias_table(rel_bias, g, dil):
    rel = (np.arange(3 * RADIUS)[None, :] - RADIUS - np.arange(RADIUS)[:, None]) * dil
    hs = slice(g * HEADS_PER_GROUP, (g + 1) * HEADS_PER_GROUP)
    onehot = jnp.asarray(_t5_bucket(rel)[..., None] == np.arange(NUM_BUCKETS), jnp.float32)
    return jnp.einsum('qkb,bh->hqk', onehot, rel_bias[:, hs].astype(jnp.float32),
                      precision=lax.Precision.HIGHEST)


def _chunk_block_diag(w):
    per = RNN_CHUNK // RNN_BLOCK_W
    n_chunks = RNN_BLOCKS // per
    w4 = w.reshape(n_chunks, per, RNN_BLOCK_W, RNN_BLOCK_W)
    dense = jnp.einsum('chij,hk->chikj', w4, jnp.eye(per, dtype=w.dtype))
    return dense.reshape(n_chunks, RNN_CHUNK, RNN_CHUNK)


def _gate_weights(wa, wi):
    per_dir = [jnp.concatenate([_chunk_block_diag(wa[d]), _chunk_block_diag(wi[d])], axis=-1)
               for d in range(2)]
    return jnp.stack(per_dir, axis=1).astype(jnp.bfloat16)


def _per_chunk(v):
    return v.reshape(2, D_RNN // RNN_CHUNK, 1, RNN_CHUNK).transpose(1, 0, 2, 3)


def _permute_in_cols(a):
    x_rnn, q, k, v, g_rnn, g_attn = jnp.split(
        a, [D_RNN, D_RNN + D_ATTN, D_RNN + 2 * D_ATTN, D_RNN + 3 * D_ATTN,
            D_RNN + 3 * D_ATTN + D_MODEL], axis=-1)
    return jnp.concatenate([x_rnn, g_rnn, g_attn, q, k, v], axis=-1)


def _trunk(x, biases, lp):
    b, s, d = x.shape
    n = b * s
    x2 = x.reshape(n, d)
    x2_bf = x2.astype(jnp.bfloat16)
    for l in range(DEPTH):
        p = lp[l]
        proj = in_projection(x2_bf, p['w_in'], p['b_in'])
        proj3 = proj.reshape(b, s, D_IN)
        h_rnn = rglru_branch(proj3, p['conv_w'], p['conv_b'], p['wg'], p['gb'], p['lam'])
        y_attn = dilated_attention(proj3, biases)
        x1, eidx, gate = mix_and_route(
            x2, proj, h_rnn.reshape(n, D_RNN), y_attn.reshape(n, D_ATTN_OUT),
            p['w_rnn_out'], p['w_attn_out'], p['w_o'],
            p['ln1_g'], p['ln1_b'], p['router_w'], p['router_b'])
        x2, x2_bf = moe_layer(x1, eidx, gate, p['w_gate'], p['w_up'], p['w_down'], p['ln2_g'], p['ln2_b'])
    return x2.reshape(b, s, d)


def kernel(x_prompt, x_sample, rel_bias, w_in, b_in, conv_w, conv_b, rg_wa, rg_ba, rg_wi, rg_bi, rg_lam, w_rnn_out, w_attn_out, w_o, ln1_g, ln1_b, router_w, router_b, expert_router_w, expert_router_b, w_gate, w_up, w_down, ln2_g, ln2_b):
    bf = jnp.bfloat16
    biases = [_bias_table(rel_bias, g, dil) for g, (_, dil) in enumerate(DILATED_CONFIGS)]
    lp = []
    for l in range(DEPTH):
        rw = jnp.concatenate(
            [router_w[l], jnp.transpose(expert_router_w[l], (1, 0, 2)).reshape(D_MODEL, N_EXPERTS)], axis=1)
        rb = jnp.concatenate([router_b[l], expert_router_b[l].reshape(N_EXPERTS)])
        pad = ROUTE_W - rw.shape[1]
        lp.append(dict(
            w_in=_permute_in_cols(w_in[l]).astype(bf),
            b_in=_permute_in_cols(b_in[l])[None, :],
            conv_w=conv_w[l], conv_b=conv_b[l][None, :],
            wg=_gate_weights(rg_wa[l], rg_wi[l]),
            gb=jnp.concatenate([_per_chunk(rg_ba[l]), _per_chunk(rg_bi[l])], axis=-1),
            lam=_per_chunk(rg_lam[l]),
            w_rnn_out=w_rnn_out[l].astype(bf), w_attn_out=w_attn_out[l].astype(bf), w_o=w_o[l].astype(bf),
            ln1_g=ln1_g[l][None, :], ln1_b=ln1_b[l][None, :],
            router_w=jnp.pad(rw, ((0, 0), (0, pad))), router_b=jnp.pad(rb, (0, pad))[None, :],
            w_gate=w_gate[l].astype(bf), w_up=w_up[l].astype(bf), w_down=w_down[l].astype(bf),
            ln2_g=ln2_g[l][None, :], ln2_b=ln2_b[l][None, :]))
    return (_trunk(x_prompt, biases, lp), _trunk(x_sample, biases, lp))
```

```python
import functools

import numpy as np
import jax
import jax.numpy as jnp
from jax import lax
from jax.experimental import pallas as pl
from jax.experimental.pallas import tpu as pltpu

D_MODEL = 1024
DEPTH = 2
D_RNN = D_MODEL
RNN_BLOCKS = 16
RNN_BLOCK_W = D_RNN // RNN_BLOCKS
CONV_WIDTH = 4
RG_C = 8.0
DILATED_CONFIGS = ((128, 1), (512, 4), (2048, 16))
N_ATTN_GROUPS = len(DILATED_CONFIGS)
HEADS_PER_GROUP = 4
N_ATTN_HEADS = N_ATTN_GROUPS * HEADS_PER_GROUP
HEAD_DIM = 128
D_ATTN = N_ATTN_HEADS * HEAD_DIM
D_ATTN_OUT = HEADS_PER_GROUP * HEAD_DIM
NUM_BUCKETS = 32
MAX_DISTANCE = max(w for w, _ in DILATED_CONFIGS) // 2
NEG_INF = -1e30
D_IN = D_RNN + 3 * D_ATTN + 2 * D_MODEL
N_EXPERT_GROUPS = 4
EXPERTS_PER_GROUP = 8
N_EXPERTS = N_EXPERT_GROUPS * EXPERTS_PER_GROUP
TOP_K = 2
D_EXPERT = D_MODEL // 2
ALPHA = (2 * DEPTH) ** 0.25
LN_EPS = 1e-5

LANES = 128
SUBLANES = 8
VMEM_LIMIT = 56 * 1024 * 1024

COL_XRNN = 0
COL_GRNN = D_RNN
COL_GATTN = D_RNN + D_MODEL
COL_Q = D_RNN + 2 * D_MODEL
COL_K = COL_Q + D_ATTN
COL_V = COL_K + D_ATTN

RADIUS = 64
assert all(w // (2 * d) == RADIUS for w, d in DILATED_CONFIGS)
ATTN_ROWS = 1024
RNN_CHUNK = 256
RNN_TILE = 256
MOE_BLK = 256
DMA_QUEUES = 2
TOKEN_TILE = D_MODEL // LANES
assert TOKEN_TILE == SUBLANES
ROUTE_W = LANES


def _cparams(n_axes):
    return pltpu.CompilerParams(dimension_semantics=("arbitrary",) * n_axes,
                                vmem_limit_bytes=VMEM_LIMIT)


def _proj_kernel(x_ref, w_ref, b_ref, o_ref):
    acc = jnp.dot(x_ref[...], w_ref[...], preferred_element_type=jnp.float32)
    o_ref[...] = acc + b_ref[...]


def in_projection(x_bf, w_bf, b, tm=512, tn=1536):
    n, k = x_bf.shape
    nout = w_bf.shape[1]
    return pl.pallas_call(
        _proj_kernel,
        grid=(nout // tn, n // tm),
        in_specs=[pl.BlockSpec((tm, k), lambda j, i: (i, 0)),
                  pl.BlockSpec((k, tn), lambda j, i: (0, j)),
                  pl.BlockSpec((1, tn), lambda j, i: (0, j))],
        out_specs=pl.BlockSpec((tm, tn), lambda j, i: (i, j)),
        out_shape=jax.ShapeDtypeStruct((n, nout), jnp.float32),
        compiler_params=_cparams(2),
        name="in_projection",
    )(x_bf, w_bf, b)


def _rglru_kernel(x_ref, cw_ref, cb_ref, wg_ref, gb_ref, lam_ref, o_ref,
                  hsum, xc_s, a_s, u_s, h_s, *, seq, tile):
    c = RNN_CHUNK
    n_tiles = seq // tile
    n_grp = tile // SUBLANES
    big_rows = tile + 2 * SUBLANES
    row_in_grp = lax.broadcasted_iota(jnp.int32, (n_grp, SUBLANES, c), 1)

    def conv_tile(t0):
        centre = x_ref[0, pl.ds(t0, tile), :]
        p0 = pl.multiple_of(jnp.maximum(t0 - SUBLANES, 0), SUBLANES)
        n0 = pl.multiple_of(jnp.minimum(t0 + tile, seq - SUBLANES), SUBLANES)
        prev = jnp.where(t0 > 0, x_ref[0, pl.ds(p0, SUBLANES), :], 0.0)
        nxt = jnp.where(t0 + tile < seq, x_ref[0, pl.ds(n0, SUBLANES), :], 0.0)
        big = jnp.concatenate([prev, centre, nxt], axis=0)
        xc = cb_ref[...] + cw_ref[1:2, :] * centre
        for k in (0, 2, 3):
            off = k - 1
            tap = pltpu.roll(big, (-off) % big_rows, 0)[SUBLANES:SUBLANES + tile]
            xc = xc + cw_ref[k:k + 1, :] * tap
        return xc

    def gates(xc, dr):
        g = jnp.dot(xc.astype(jnp.bfloat16), wg_ref[0, dr],
                    preferred_element_type=jnp.float32) + gb_ref[0, dr]
        gate_r = jax.nn.sigmoid(g[:, :c])
        gate_i = jax.nn.sigmoid(g[:, c:])
        neg_lam = -lam_ref[0, dr]
        softplus = jnp.maximum(neg_lam, 0.0) + jnp.log1p(jnp.exp(-jnp.abs(neg_lam)))
        log_a = -RG_C * gate_r * softplus
        a = jnp.exp(log_a)
        th = jnp.tanh(log_a)
        one_minus_a2 = -2.0 * th / (1.0 - th)
        u = jnp.sqrt(one_minus_a2) * (gate_i * xc)
        return a, u

    def tile_prefix(a, u, reverse):
        a = a.reshape(n_grp, SUBLANES, c)
        u = u.reshape(n_grp, SUBLANES, c)
        for s in (1, 2, 4):
            if reverse:
                a_sh = pltpu.roll(a, SUBLANES - s, 1)
                u_sh = pltpu.roll(u, SUBLANES - s, 1)
                m = row_in_grp < SUBLANES - s
            else:
                a_sh = pltpu.roll(a, s, 1)
                u_sh = pltpu.roll(u, s, 1)
                m = row_in_grp >= s
            u = jnp.where(m, u + a * u_sh, u)
            a = jnp.where(m, a * a_sh, a)
        a_s[...] = a.reshape(tile, c)
        u_s[...] = u.reshape(tile, c)

    def run_direction(dr, reverse):
        def tile_body(ti, h):
            t_idx = (n_tiles - 1 - ti) if reverse else ti
            t0 = pl.multiple_of(t_idx * tile, tile)
            if reverse:
                xc = xc_s[pl.ds(t0, tile), :]
            else:
                xc = conv_tile(t0)
                xc_s[pl.ds(t0, tile), :] = xc
            a, u = gates(xc, dr)
            tile_prefix(a, u, reverse)

            def grp_body(gi, hc):
                g_idx = (n_grp - 1 - gi) if reverse else gi
                r0 = pl.multiple_of(g_idx * SUBLANES, SUBLANES)
                hg = u_s[pl.ds(r0, SUBLANES), :] + a_s[pl.ds(r0, SUBLANES), :] * hc
                h_s[pl.ds(r0, SUBLANES), :] = hg
                edge = hg[0:1, :] if reverse else hg[SUBLANES - 1:SUBLANES, :]
                return jnp.broadcast_to(edge, (SUBLANES, c))

            h = lax.fori_loop(0, n_grp, grp_body, h, unroll=4)
            if reverse:
                o_ref[0, pl.ds(t0, tile), :] = (hsum[pl.ds(t0, tile), :] + h_s[...]).astype(o_ref.dtype)
            else:
                hsum[pl.ds(t0, tile), :] = h_s[...]
            return h

        lax.fori_loop(0, n_tiles, tile_body, jnp.zeros((SUBLANES, c), jnp.float32))

    run_direction(0, False)
    run_direction(1, True)


def rglru_branch(proj3, cw, cb, wg, gb, lam):
    b, s, _ = proj3.shape
    c = RNN_CHUNK
    n_chunks = D_RNN // c
    kern = functools.partial(_rglru_kernel, seq=s, tile=RNN_TILE)
    return pl.pallas_call(
        kern,
        grid=(b, n_chunks),
        in_specs=[pl.BlockSpec((1, s, c), lambda bi, ci: (bi, 0, COL_XRNN // c + ci)),
                  pl.BlockSpec((CONV_WIDTH, c), lambda bi, ci: (0, ci)),
                  pl.BlockSpec((1, c), lambda bi, ci: (0, ci)),
                  pl.BlockSpec((1, 2, c, 2 * c), lambda bi, ci: (ci, 0, 0, 0)),
                  pl.BlockSpec((1, 2, 1, 2 * c), lambda bi, ci: (ci, 0, 0, 0)),
                  pl.BlockSpec((1, 2, 1, c), lambda bi, ci: (ci, 0, 0, 0))],
        out_specs=pl.BlockSpec((1, s, c), lambda bi, ci: (bi, 0, ci)),
        out_shape=jax.ShapeDtypeStruct((b, s, D_RNN), jnp.bfloat16),
        scratch_shapes=[pltpu.VMEM((s, c), jnp.float32),
                        pltpu.VMEM((s, c), jnp.float32),
                        pltpu.VMEM((RNN_TILE, c), jnp.float32),
                        pltpu.VMEM((RNN_TILE, c), jnp.float32),
                        pltpu.VMEM((RNN_TILE, c), jnp.float32)],
        compiler_params=_cparams(2),
        name="rglru_branch",
    )(proj3, cw, cb, wg, gb, lam)


def _attn_group(refs, bias, o_dst, l_dst, *, dil, is_first, is_last):
    q_ref, kp_ref, kc_ref, kn_ref, vp_ref, vc_ref, vn_ref = refs
    rows = ATTN_ROWS
    per_res = rows // dil
    n_q = per_res // RADIUS
    win = 3 * RADIUS
    nb = dil * n_q

    def fold(ref, r, n):
        if dil == 1:
            return ref[0, pl.ds(r, n), :]
        return ref[0, pl.ds(r, n, stride=dil), :]

    qs, ks, vs = [], [], []
    for r in range(dil):
        qf = fold(q_ref, r, per_res).astype(jnp.bfloat16)
        kf = jnp.concatenate([fold(kp_ref, r, RADIUS), fold(kc_ref, r, per_res),
                              fold(kn_ref, r, RADIUS)], axis=0).astype(jnp.bfloat16)
        vf = jnp.concatenate([fold(vp_ref, r, RADIUS), fold(vc_ref, r, per_res),
                              fold(vn_ref, r, RADIUS)], axis=0).astype(jnp.bfloat16)
        for jq in range(n_q):
            qs.append(qf[jq * RADIUS:(jq + 1) * RADIUS])
            ks.append(kf[jq * RADIUS:jq * RADIUS + win])
            vs.append(vf[jq * RADIUS:jq * RADIUS + win])
    qb, kb, vb = jnp.stack(qs), jnp.stack(ks), jnp.stack(vs)

    shape = (nb, RADIUS, win)
    jq_of = lax.broadcasted_iota(jnp.int32, shape, 0) & (n_q - 1)
    qi = lax.broadcasted_iota(jnp.int32, shape, 1)
    kj = lax.broadcasted_iota(jnp.int32, shape, 2)
    lo = jnp.where(jq_of == 0, jnp.where(is_first, RADIUS, 0), 0)
    hi = jnp.where(jq_of == n_q - 1, jnp.where(is_last, 2 * RADIUS, win), win)
    mask = (jnp.abs(kj - RADIUS - qi) <= RADIUS) & (kj >= lo) & (kj < hi)

    s = jnp.einsum('bqd,bkd->bqk', qb, kb, preferred_element_type=jnp.float32) * (HEAD_DIM ** -0.5)
    s = jnp.where(mask, s + bias[None], NEG_INF)
    m = jnp.max(s, axis=-1, keepdims=True)
    p = jnp.exp(s - m)
    l = jnp.sum(p, axis=-1, keepdims=True)
    o = jnp.einsum('bqk,bkd->bqd', p.astype(jnp.bfloat16), vb, preferred_element_type=jnp.float32) / l
    lse = m + jnp.log(l)
    for bi in range(nb):
        r, jq = divmod(bi, n_q)
        start = jq * RADIUS * dil + r
        idx = pl.ds(start, RADIUS) if dil == 1 else pl.ds(start, RADIUS, stride=dil)
        o_dst[idx, :] = o[bi]
        l_dst[idx, :] = jnp.broadcast_to(lse[bi], (RADIUS, HEAD_DIM))


def _attn_kernel(*refs, n_row_blocks):
    n_g = N_ATTN_GROUPS
    in_refs = refs[:7 * n_g]
    bias_refs = refs[7 * n_g:8 * n_g]
    y_ref = refs[8 * n_g]
    o_s, l_s = refs[8 * n_g + 1:]
    i = pl.program_id(1)
    for g, (_, dil) in enumerate(DILATED_CONFIGS):
        _attn_group(in_refs[7 * g:7 * g + 7], bias_refs[g][0], o_s.at[g], l_s.at[g],
                    dil=dil, is_first=i == 0, is_last=i == n_row_blocks - 1)
    l0, l1, l2 = l_s[0], l_s[1], l_s[2]
    m = jnp.maximum(jnp.maximum(l0, l1), l2)
    e0, e1, e2 = jnp.exp(l0 - m), jnp.exp(l1 - m), jnp.exp(l2 - m)
    y = (e0 * o_s[0] + e1 * o_s[1] + e2 * o_s[2]) / (e0 + e1 + e2)
    y_ref[0] = y.astype(y_ref.dtype)


def dilated_attention(proj3, biases):
    b, s, _ = proj3.shape
    rows = ATTN_ROWS
    nrb = s // rows
    w = HEAD_DIM
    nh = HEADS_PER_GROUP

    def cur_map(col):
        return lambda bi, i, h: (bi, i, col + h)

    in_specs, operands = [], []
    for g, (_, dil) in enumerate(DILATED_CONFIGS):
        halo = RADIUS * dil
        ratio = rows // halo
        n_halo_blocks = s // halo
        qc, kc, vc = COL_Q // w + g * nh, COL_K // w + g * nh, COL_V // w + g * nh

        def prev_map(col, ratio=ratio):
            return lambda bi, i, h: (bi, jnp.maximum(i * ratio - 1, 0), col + h)

        def next_map(col, ratio=ratio, last=n_halo_blocks - 1):
            return lambda bi, i, h: (bi, jnp.minimum((i + 1) * ratio, last), col + h)

        in_specs += [pl.BlockSpec((1, rows, w), cur_map(qc)),
                     pl.BlockSpec((1, halo, w), prev_map(kc)),
                     pl.BlockSpec((1, rows, w), cur_map(kc)),
                     pl.BlockSpec((1, halo, w), next_map(kc)),
                     pl.BlockSpec((1, halo, w), prev_map(vc)),
                     pl.BlockSpec((1, rows, w), cur_map(vc)),
                     pl.BlockSpec((1, halo, w), next_map(vc))]
        operands += [proj3] * 7
    in_specs += [pl.BlockSpec((1, RADIUS, 3 * RADIUS), lambda bi, i, h: (h, 0, 0))] * N_ATTN_GROUPS
    operands += list(biases)
    return pl.pallas_call(
        functools.partial(_attn_kernel, n_row_blocks=nrb),
        grid=(b, nrb, nh),
        in_specs=in_specs,
        out_specs=pl.BlockSpec((1, rows, w), cur_map(0)),
        out_shape=jax.ShapeDtypeStruct((b, s, D_ATTN_OUT), jnp.bfloat16),
        scratch_shapes=[pltpu.VMEM((N_ATTN_GROUPS, rows, w), jnp.float32),
                        pltpu.VMEM((N_ATTN_GROUPS, rows, w), jnp.float32)],
        compiler_params=_cparams(3),
        name="dilated_attention",
    )(*operands)


def _store_token_tiles(ref, val):
    rows = val.shape[0]
    for c in range(TOKEN_TILE):
        ref[pl.ds(c, rows, stride=TOKEN_TILE), :] = val[:, c * LANES:(c + 1) * LANES]


def _load_token_tiles(ref, start, rows):
    return [ref[pl.ds(start * TOKEN_TILE + c, rows, stride=TOKEN_TILE), :] for c in range(TOKEN_TILE)]


def _layer_norm(z, g, b):
    mu = jnp.mean(z, axis=-1, keepdims=True)
    zc = z - mu
    var = jnp.mean(zc * zc, axis=-1, keepdims=True)
    return zc * lax.rsqrt(var + LN_EPS) * g + b


def _mix_kernel(x_ref, grnn_ref, gattn_ref, h_ref, y_ref,
                wr_ref, wa_ref, wo_ref, lng_ref, lnb_ref, rw_ref, rb_ref,
                xo_ref, eidx_ref, gate_ref):
    rnn = jnp.dot(h_ref[...], wr_ref[...], preferred_element_type=jnp.float32)
    att = jnp.dot(y_ref[...], wa_ref[...], preferred_element_type=jnp.float32)
    mixed = jax.nn.sigmoid(grnn_ref[...]) * rnn + jax.nn.sigmoid(gattn_ref[...]) * att
    z = ALPHA * x_ref[...] + jnp.dot(mixed.astype(jnp.bfloat16), wo_ref[...],
                                     preferred_element_type=jnp.float32)
    x1 = _layer_norm(z, lng_ref[...], lnb_ref[...])
    _store_token_tiles(xo_ref, x1)

    logits = jnp.dot(x1, rw_ref[...], preferred_element_type=jnp.float32,
                     precision=lax.Precision.HIGHEST) + rb_ref[...]
    tm = logits.shape[0]
    lane = lax.broadcasted_iota(jnp.int32, (tm, ROUTE_W), 1)
    lane_f = lane.astype(jnp.float32)
    far = float(ROUTE_W)
    is_group = lane < N_EXPERT_GROUPS
    gl = jnp.where(is_group, logits, -jnp.inf)
    gmax = jnp.max(gl, axis=-1, keepdims=True)
    gsel = jnp.min(jnp.where(gl == gmax, lane_f, far), axis=-1, keepdims=True)
    p_group = 1.0 / jnp.sum(jnp.where(is_group, jnp.exp(logits - gmax), 0.0), axis=-1, keepdims=True)
    lane_group = ((lane - N_EXPERT_GROUPS) >> 3).astype(jnp.float32)
    in_group = (lane >= N_EXPERT_GROUPS) & (lane < N_EXPERT_GROUPS + N_EXPERTS) & (lane_group == gsel)
    el = jnp.where(in_group, logits, -jnp.inf)
    m1 = jnp.max(el, axis=-1, keepdims=True)
    i1 = jnp.min(jnp.where(el == m1, lane_f, far), axis=-1, keepdims=True)
    el2 = jnp.where(lane_f == i1, -jnp.inf, el)
    m2 = jnp.max(el2, axis=-1, keepdims=True)
    i2 = jnp.min(jnp.where(el2 == m2, lane_f, far), axis=-1, keepdims=True)
    e21 = jnp.exp(m2 - m1)
    den = 1.0 + e21
    g1 = p_group * (1.0 / den)
    g2 = p_group * (e21 / den)
    eidx = jnp.where(lane == 0, i1, jnp.where(lane == 1, i2, float(N_EXPERT_GROUPS))) - float(N_EXPERT_GROUPS)
    eidx_ref[...] = eidx.astype(jnp.int32)
    gate_ref[...] = jnp.where(lane == 0, g1, jnp.where(lane == 1, g2, 0.0))


def mix_and_route(x, proj, h_rnn, y_attn, wr, wa, wo, lng, lnb, rw, rb, tm=512):
    n = x.shape[0]
    d = D_MODEL
    w = D_ATTN_OUT
    row = lambda width, col=0: pl.BlockSpec((tm, width), lambda i, c=col: (i, c))
    full = lambda a: pl.BlockSpec(a.shape, lambda i: (0,) * a.ndim)
    return pl.pallas_call(
        _mix_kernel,
        grid=(n // tm,),
        in_specs=[row(d), row(d, COL_GRNN // d), row(d, COL_GATTN // d), row(d), row(w),
                  full(wr), full(wa), full(wo), full(lng), full(lnb), full(rw), full(rb)],
        out_specs=[pl.BlockSpec((tm * TOKEN_TILE, LANES), lambda i: (i, 0)), row(ROUTE_W), row(ROUTE_W)],
        out_shape=[jax.ShapeDtypeStruct((n * TOKEN_TILE, LANES), jnp.float32),
                   jax.ShapeDtypeStruct((n, ROUTE_W), jnp.int32),
                   jax.ShapeDtypeStruct((n, ROUTE_W), jnp.float32)],
        compiler_params=_cparams(1),
        name="mix_and_route",
    )(x, proj, proj, h_rnn, y_attn, wr, wa, wo, lng, lnb, rw, rb)


def _rank_kernel(e_ref, dest_ref, cnt_ref, carry, *, tm):
    ph = pl.program_id(0)
    i = pl.program_id(1)
    lanes = ROUTE_W

    @pl.when((ph == 0) & (i == 0))
    def _():
        carry[...] = jnp.zeros_like(carry)

    @pl.when((ph == 1) & (i == 0))
    def _():
        cnt = carry[...]
        cnt_ref[...] = jnp.broadcast_to(cnt, (lanes, lanes)).astype(jnp.int32)
        padded = jnp.floor((cnt + (MOE_BLK - 1)) * (1.0 / MOE_BLK)) * MOE_BLK
        r = lax.broadcasted_iota(jnp.int32, (lanes, lanes), 0)
        c = lax.broadcasted_iota(jnp.int32, (lanes, lanes), 1)
        lower = (c < r).astype(jnp.float32)
        pstart = jnp.dot(lower, jnp.broadcast_to(padded, (lanes, lanes)),
                         preferred_element_type=jnp.float32, precision=lax.Precision.HIGHEST)
        carry[...] = pstart[:, 0:1]

    r8 = lax.broadcasted_iota(jnp.int32, (SUBLANES, lanes), 0)
    c8 = lax.broadcasted_iota(jnp.int32, (SUBLANES, lanes), 1)
    pick = ((r8 == c8) & (r8 < TOP_K)).astype(jnp.bfloat16)
    ef = e_ref[...].astype(jnp.float32).astype(jnp.bfloat16)
    et = lax.dot_general(pick, ef, (((1,), (1,)), ((), ())), preferred_element_type=jnp.float32)
    e0, e1 = et[0:1, :], et[1:2, :]
    sub = lax.broadcasted_iota(jnp.int32, (lanes, tm), 0).astype(jnp.float32)
    is0, is1 = sub == e0, sub == e1
    member = jnp.where(is0 | is1, 1.0, 0.0)
    s_i = lax.broadcasted_iota(jnp.int32, (tm, tm), 0)
    t_i = lax.broadcasted_iota(jnp.int32, (tm, tm), 1)
    earlier = (s_i < t_i).astype(jnp.bfloat16)
    before = jnp.dot(member.astype(jnp.bfloat16), earlier, preferred_element_type=jnp.float32)
    pos = before + carry[...]
    d0 = jnp.sum(jnp.where(is0, pos, 0.0), axis=0, keepdims=True)
    d1 = jnp.sum(jnp.where(is1, pos, 0.0), axis=0, keepdims=True)
    row = lax.broadcasted_iota(jnp.int32, (SUBLANES, tm), 0)
    dest_ref[...] = jnp.where(row == 0, d0, jnp.where(row == 1, d1, 0.0)).astype(jnp.int32)
    carry[...] = carry[...] + jnp.sum(member, axis=1, keepdims=True)


def route_slots(eidx, tm=512):
    n = eidx.shape[0]
    nt = n // tm
    return pl.pallas_call(
        functools.partial(_rank_kernel, tm=tm),
        grid=(2, nt),
        in_specs=[pl.BlockSpec((tm, ROUTE_W), lambda ph, i: (i, 0))],
        out_specs=[pl.BlockSpec((SUBLANES, tm), lambda ph, i: (0, i * ph)),
                   pl.BlockSpec((ROUTE_W, ROUTE_W), lambda ph, i: (0, 0))],
        out_shape=[jax.ShapeDtypeStruct((SUBLANES, n), jnp.int32),
                   jax.ShapeDtypeStruct((ROUTE_W, ROUTE_W), jnp.int32)],
        scratch_shapes=[pltpu.VMEM((ROUTE_W, 1), jnp.float32)],
        compiler_params=_cparams(2),
        name="route_slots",
    )(eidx)


def _slot_map_kernel(dest_ref, slot_ref, *, n, slots):
    def init(j, c):
        slot_ref[j] = 0
        return c

    lax.fori_loop(0, slots, init, 0, unroll=32)

    def body(t, c):
        slot_ref[dest_ref[t]] = t
        slot_ref[dest_ref[n + t]] = t
        return c

    lax.fori_loop(0, n, body, 0, unroll=16)


def slot_map(dest_flat, n, slots):
    return pl.pallas_call(
        functools.partial(_slot_map_kernel, n=n, slots=slots),
        in_specs=[pl.BlockSpec(memory_space=pltpu.SMEM)],
        out_specs=pl.BlockSpec(memory_space=pltpu.SMEM),
        out_shape=jax.ShapeDtypeStruct((slots,), jnp.int32),
        name="slot_map",
    )(dest_flat)


def _expert_kernel(be_ref, nused_ref, st_ref, x_hbm, wg_ref, wu_ref, wd_ref, y_ref, xs, land, sem):
    i = pl.program_id(0)
    n_used = nused_ref[0]

    def gather_block(blk):
        base = blk * MOE_BLK
        for j in range(MOE_BLK):
            src = pl.multiple_of(st_ref[base + j] * TOKEN_TILE, TOKEN_TILE)
            pltpu.make_async_copy(x_hbm.at[pl.ds(src, TOKEN_TILE)], land.at[pl.ds(j * TOKEN_TILE, TOKEN_TILE)],
                                  sem.at[0]).start(priority=j % DMA_QUEUES)

    def wait_block():
        pltpu.make_async_copy(x_hbm.at[pl.ds(0, MOE_BLK * TOKEN_TILE)], land, sem.at[0]).wait()

    @pl.when(i == 0)
    def _():
        gather_block(0)

    @pl.when(i < n_used)
    def _():
        wait_block()
        for c, chunk in enumerate(_load_token_tiles(land, 0, MOE_BLK)):
            xs[:, c * LANES:(c + 1) * LANES] = chunk.astype(jnp.bfloat16)

    @pl.when(i + 1 <= n_used)
    def _():
        gather_block(i + 1)

    @pl.when(n_used > i)
    def _():
        xb = xs[...]
        gate = jnp.dot(xb, wg_ref[0], preferred_element_type=jnp.float32)
        up = jnp.dot(xb, wu_ref[0], preferred_element_type=jnp.float32)
        hid = (jax.nn.silu(gate) * up).astype(jnp.bfloat16)
        _store_token_tiles(y_ref, jnp.dot(hid, wd_ref[0], preferred_element_type=jnp.float32))

    @pl.when(i == n_used - 1)
    def _():
        wait_block()

    @pl.when(i >= n_used)
    def _():
        y_ref[...] = jnp.zeros_like(y_ref)


def expert_blocks(block_e, n_used, slot_tok, x1t, wg, wu, wd):
    d = D_MODEL
    n_blocks = slot_tok.shape[0] // MOE_BLK - 1
    slots = n_blocks * MOE_BLK
    tile_rows = MOE_BLK * TOKEN_TILE
    grid_spec = pltpu.PrefetchScalarGridSpec(
        num_scalar_prefetch=3,
        grid=(n_blocks,),
        in_specs=[pl.BlockSpec(memory_space=pl.ANY),
                  pl.BlockSpec((1, d, D_EXPERT), lambda i, be, nu, st: (be[i], 0, 0)),
                  pl.BlockSpec((1, d, D_EXPERT), lambda i, be, nu, st: (be[i], 0, 0)),
                  pl.BlockSpec((1, D_EXPERT, d), lambda i, be, nu, st: (be[i], 0, 0))],
        out_specs=pl.BlockSpec((tile_rows, LANES), lambda i, be, nu, st: (i, 0)),
        scratch_shapes=[pltpu.VMEM((MOE_BLK, d), jnp.bfloat16),
                        pltpu.VMEM((tile_rows, LANES), jnp.float32),
                        pltpu.SemaphoreType.DMA((1,))],
    )
    return pl.pallas_call(
        _expert_kernel,
        grid_spec=grid_spec,
        out_shape=jax.ShapeDtypeStruct((slots * TOKEN_TILE, LANES), jnp.float32),
        compiler_params=_cparams(1),
        name="expert_blocks",
    )(block_e, n_used, slot_tok, x1t, wg, wu, wd)


def _combine_kernel(dest_ref, x_ref, gate_ref, y_hbm, g_ref, b_ref, xo_ref, xob_ref, fbuf, sem, *, n, tm):
    i = pl.program_id(0)
    nt = pl.num_programs(0)
    half = tm * TOKEN_TILE

    def gather_tile(tile, slot):
        base = tile * tm
        for k in range(TOP_K):
            for t in range(tm):
                src = pl.multiple_of(dest_ref[k * n + base + t] * TOKEN_TILE, TOKEN_TILE)
                pltpu.make_async_copy(y_hbm.at[pl.ds(src, TOKEN_TILE)],
                                      fbuf.at[slot, pl.ds((k * tm + t) * TOKEN_TILE, TOKEN_TILE)],
                                      sem.at[slot]).start(priority=t % DMA_QUEUES)

    def wait_tile(slot):
        pltpu.make_async_copy(y_hbm.at[pl.ds(0, TOP_K * half)], fbuf.at[slot], sem.at[slot]).wait()

    @pl.when(i == 0)
    def _():
        gather_tile(0, 0)

    slot = i % 2
    wait_tile(slot)
    g0, g1 = gate_ref[:, 0:1], gate_ref[:, 1:2]
    f0 = _load_token_tiles(fbuf.at[slot], 0, tm)
    f1 = _load_token_tiles(fbuf.at[slot], tm, tm)
    xs = _load_token_tiles(x_ref, 0, tm)
    gather_tile(jnp.minimum(i + 1, nt - 1), 1 - slot)
    z = [ALPHA * xs[c] + (g0 * f0[c] + g1 * f1[c]) for c in range(TOKEN_TILE)]
    mu = sum(jnp.sum(zc, axis=-1, keepdims=True) for zc in z) * (1.0 / D_MODEL)
    zc = [zz - mu for zz in z]
    var = sum(jnp.sum(v * v, axis=-1, keepdims=True) for v in zc) * (1.0 / D_MODEL)
    inv = lax.rsqrt(var + LN_EPS)
    for c in range(TOKEN_TILE):
        cols = slice(c * LANES, (c + 1) * LANES)
        x2 = zc[c] * inv * g_ref[:, cols] + b_ref[:, cols]
        xo_ref[:, cols] = x2
        xob_ref[:, cols] = x2.astype(jnp.bfloat16)

    @pl.when(i == nt - 1)
    def _():
        wait_tile(1 - slot)


def combine_ln2(dest_flat, x1t, gate, ybt, g, b, tm=256):
    n = gate.shape[0]
    d = D_MODEL
    tile_rows = tm * TOKEN_TILE
    grid_spec = pltpu.PrefetchScalarGridSpec(
        num_scalar_prefetch=1,
        grid=(n // tm,),
        in_specs=[pl.BlockSpec((tile_rows, LANES), lambda i, ds: (i, 0)),
                  pl.BlockSpec((tm, ROUTE_W), lambda i, ds: (i, 0)),
                  pl.BlockSpec(memory_space=pl.ANY),
                  pl.BlockSpec((1, d), lambda i, ds: (0, 0)),
                  pl.BlockSpec((1, d), lambda i, ds: (0, 0))],
        out_specs=[pl.BlockSpec((tm, d), lambda i, ds: (i, 0)),
                   pl.BlockSpec((tm, d), lambda i, ds: (i, 0))],
        scratch_shapes=[pltpu.VMEM((2, TOP_K * tile_rows, LANES), jnp.float32),
                        pltpu.SemaphoreType.DMA((2,))],
    )
    return pl.pallas_call(
        functools.partial(_combine_kernel, n=n, tm=tm),
        grid_spec=grid_spec,
        out_shape=[jax.ShapeDtypeStruct((n, d), jnp.float32),
                   jax.ShapeDtypeStruct((n, d), jnp.bfloat16)],
        compiler_params=_cparams(1),
        name="combine_ln2",
    )(dest_flat, x1t, gate, ybt, g, b)


def moe_layer(x1, eidx, gate, wg, wu, wd, ln_g, ln_b):
    n = eidx.shape[0]
    m = n * TOP_K
    n_blocks = m // MOE_BLK + N_EXPERTS
    slots = n_blocks * MOE_BLK
    dest8, cnt = route_slots(eidx)
    dest_flat = dest8[:TOP_K].reshape(m)
    counts = cnt[:N_EXPERTS, 0]
    pend = jnp.cumsum((counts + MOE_BLK - 1) // MOE_BLK * MOE_BLK)
    blk_start = jnp.arange(n_blocks, dtype=jnp.int32) * MOE_BLK
    block_e = jnp.minimum(jnp.sum((pend[None, :] <= blk_start[:, None]).astype(jnp.int32), axis=1),
                          N_EXPERTS - 1)
    n_used = (pend[-1:] // MOE_BLK).astype(jnp.int32)
    slot_tok = slot_map(dest_flat, n, slots + MOE_BLK)
    yb = expert_blocks(block_e, n_used, slot_tok, x1, wg, wu, wd)
    return combine_ln2(dest_flat, x1, gate, yb, ln_g, ln_b)


def _t5_bucket(rel):
    half = NUM_BUCKETS // 2
    max_exact = half // 2
    n = np.abs(rel)
    large = max_exact + (np.log(np.maximum(n, 1) / max_exact) / np.log(MAX_DISTANCE / max_exact)
                         * (half - max_exact)).astype(np.int32)
    large = np.minimum(large, half - 1)
    return np.where(rel > 0, half, 0) + np.where(n < max_exact, n, large)


def _bias_table(rel_bias, g, dil):
    rel = (np.arange(3 * RADIUS)[None, :] - RADIUS - np.arange(RADIUS)[:, None]) * dil
    hs = slice(g * HEADS_PER_GROUP, (g + 1) * HEADS_PER_GROUP)
    onehot = jnp.asarray(_t5_bucket(rel)[..., None] == np.arange(NUM_BUCKETS), jnp.float32)
    return jnp.einsum('qkb,bh->hqk', onehot, rel_bias[:, hs].astype(jnp.float32),
                      precision=lax.Precision.HIGHEST)


def _chunk_block_diag(w):
    per = RNN_CHUNK // RNN_BLOCK_W
    n_chunks = RNN_BLOCKS // per
    w4 = w.reshape(n_chunks, per, RNN_BLOCK_W, RNN_BLOCK_W)
    dense = jnp.einsum('chij,hk->chikj', w4, jnp.eye(per, dtype=w.dtype))
    return dense.reshape(n_chunks, RNN_CHUNK, RNN_CHUNK)


def _gate_weights(wa, wi):
    per_dir = [jnp.concatenate([_chunk_block_diag(wa[d]), _chunk_block_diag(wi[d])], axis=-1)
               for d in range(2)]
    return jnp.stack(per_dir, axis=1).astype(jnp.bfloat16)


def _per_chunk(v):
    return v.reshape(2, D_RNN // RNN_CHUNK, 1, RNN_CHUNK).transpose(1, 0, 2, 3)


def _permute_in_cols(a):
    x_rnn, q, k, v, g_rnn, g_attn = jnp.split(
        a, [D_RNN, D_RNN + D_ATTN, D_RNN + 2 * D_ATTN, D_RNN + 3 * D_ATTN,
            D_RNN + 3 * D_ATTN + D_MODEL], axis=-1)
    return jnp.concatenate([x_rnn, g_rnn, g_attn, q, k, v], axis=-1)


def _trunk(x, biases, lp):
    b, s, d = x.shape
    n = b * s
    x2 = x.reshape(n, d)
    x2_bf = x2.astype(jnp.bfloat16)
    for l in range(DEPTH):
        p = lp[l]
        proj = in_projection(x2_bf, p['w_in'], p['b_in'])
        proj3 = proj.reshape(b, s, D_IN)
        h_rnn = rglru_branch(proj3, p['conv_w'], p['conv_b'], p['wg'], p['gb'], p['lam'])
        y_attn = dilated_attention(proj3, biases)
        x1, eidx, gate = mix_and_route(
            x2, proj, h_rnn.reshape(n, D_RNN), y_attn.reshape(n, D_ATTN_OUT),
            p['w_rnn_out'], p['w_attn_out'], p['w_o'],
            p['ln1_g'], p['ln1_b'], p['router_w'], p['router_b'])
        x2, x2_bf = moe_layer(x1, eidx, gate, p['w_gate'], p['w_up'], p['w_down'], p['ln2_g'], p['ln2_b'])
    return x2.reshape(b, s, d)


def kernel(x_prompt, x_sample, rel_bias, w_in, b_in, conv_w, conv_b, rg_wa, rg_ba, rg_wi, rg_bi, rg_lam, w_rnn_out, w_attn_out, w_o, ln1_g, ln1_b, router_w, router_b, expert_router_w, expert_router_b, w_gate, w_up, w_down, ln2_g, ln2_b):
    bf = jnp.bfloat16
    biases = [_bias_table(rel_bias, g, dil) for g, (_, dil) in enumerate(DILATED_CONFIGS)]
    lp = []
    for l in range(DEPTH):
        rw = jnp.concatenate(
            [router_w[l], jnp.transpose(expert_router_w[l], (1, 0, 2)).reshape(D_MODEL, N_EXPERTS)], axis=1)
        rb = jnp.concatenate([router_b[l], expert_router_b[l].reshape(N_EXPERTS)])
        pad = ROUTE_W - rw.shape[1]
        lp.append(dict(
            w_in=_permute_in_cols(w_in[l]).astype(bf),
            b_in=_permute_in_cols(b_in[l])[None, :],
            conv_w=conv_w[l], conv_b=conv_b[l][None, :],
            wg=_gate_weights(rg_wa[l], rg_wi[l]),
            gb=jnp.concatenate([_per_chunk(rg_ba[l]), _per_chunk(rg_bi[l])], axis=-1),
            lam=_per_chunk(rg_lam[l]),
            w_rnn_out=w_rnn_out[l].astype(bf), w_attn_out=w_attn_out[l].astype(bf), w_o=w_o[l].astype(bf),
            ln1_g=ln1_g[l][None, :], ln1_b=ln1_b[l][None, :],
            router_w=jnp.pad(rw, ((0, 0), (0, pad))), router_b=jnp.pad(rb, (0, pad))[None, :],
            w_gate=w_gate[l].astype(bf), w_up=w_up[l].astype(bf), w_down=w_down[l].astype(bf),
            ln2_g=ln2_g[l][None, :], ln2_b=ln2_b[l][None, :]))
    return (_trunk(x_prompt, biases, lp), _trunk(x_sample, biases, lp))
```

```python
import functools

import numpy as np
import jax
import jax.numpy as jnp
from jax import lax
from jax.experimental import pallas as pl
from jax.experimental.pallas import tpu as pltpu

D_MODEL = 1024
DEPTH = 2
D_RNN = D_MODEL
RNN_BLOCKS = 16
RNN_BLOCK_W = D_RNN // RNN_BLOCKS
CONV_WIDTH = 4
RG_C = 8.0
DILATED_CONFIGS = ((128, 1), (512, 4), (2048, 16))
N_ATTN_GROUPS = len(DILATED_CONFIGS)
HEADS_PER_GROUP = 4
N_ATTN_HEADS = N_ATTN_GROUPS * HEADS_PER_GROUP
HEAD_DIM = 128
D_ATTN = N_ATTN_HEADS * HEAD_DIM
D_ATTN_OUT = HEADS_PER_GROUP * HEAD_DIM
NUM_BUCKETS = 32
MAX_DISTANCE = max(w for w, _ in DILATED_CONFIGS) // 2
NEG_INF = -1e30
D_IN = D_RNN + 3 * D_ATTN + 2 * D_MODEL
N_EXPERT_GROUPS = 4
EXPERTS_PER_GROUP = 8
N_EXPERTS = N_EXPERT_GROUPS * EXPERTS_PER_GROUP
TOP_K = 2
D_EXPERT = D_MODEL // 2
ALPHA = (2 * DEPTH) ** 0.25
LN_EPS = 1e-5

LANES = 128
SUBLANES = 8
VMEM_LIMIT = 56 * 1024 * 1024

COL_XRNN = 0
COL_GRNN = D_RNN
COL_GATTN = D_RNN + D_MODEL
COL_Q = D_RNN + 2 * D_MODEL
COL_K = COL_Q + D_ATTN
COL_V = COL_K + D_ATTN

RADIUS = 64
assert all(w // (2 * d) == RADIUS for w, d in DILATED_CONFIGS)
ATTN_ROWS = 1024
RNN_CHUNK = 256
RNN_TILE = 256
MOE_BLK = 256
DMA_QUEUES = 2
TOKEN_TILE = D_MODEL // LANES
assert TOKEN_TILE == SUBLANES
ROUTE_W = LANES


def _cparams(n_axes):
    return pltpu.CompilerParams(dimension_semantics=("arbitrary",) * n_axes,
                                vmem_limit_bytes=VMEM_LIMIT)


def _proj_kernel(x_ref, w_ref, b_ref, o_ref):
    acc = jnp.dot(x_ref[...], w_ref[...], preferred_element_type=jnp.float32)
    o_ref[...] = acc + b_ref[...]


def in_projection(x_bf, w_bf, b, tm=512, tn=1536):
    n, k = x_bf.shape
    nout = w_bf.shape[1]
    return pl.pallas_call(
        _proj_kernel,
        grid=(nout // tn, n // tm),
        in_specs=[pl.BlockSpec((tm, k), lambda j, i: (i, 0)),
                  pl.BlockSpec((k, tn), lambda j, i: (0, j)),
                  pl.BlockSpec((1, tn), lambda j, i: (0, j))],
        out_specs=pl.BlockSpec((tm, tn), lambda j, i: (i, j)),
        out_shape=jax.ShapeDtypeStruct((n, nout), jnp.float32),
        compiler_params=_cparams(2),
        name="in_projection",
    )(x_bf, w_bf, b)


def _rglru_kernel(x_ref, cw_ref, cb_ref, wg_ref, gb_ref, lam_ref, o_ref,
                  hsum, xc_s, a_s, u_s, h_s, *, seq, tile):
    c = RNN_CHUNK
    n_tiles = seq // tile
    n_grp = tile // SUBLANES
    big_rows = tile + 2 * SUBLANES
    row_in_grp = lax.broadcasted_iota(jnp.int32, (n_grp, SUBLANES, c), 1)

    def conv_tile(t0):
        centre = x_ref[0, pl.ds(t0, tile), :]
        p0 = pl.multiple_of(jnp.maximum(t0 - SUBLANES, 0), SUBLANES)
        n0 = pl.multiple_of(jnp.minimum(t0 + tile, seq - SUBLANES), SUBLANES)
        prev = jnp.where(t0 > 0, x_ref[0, pl.ds(p0, SUBLANES), :], 0.0)
        nxt = jnp.where(t0 + tile < seq, x_ref[0, pl.ds(n0, SUBLANES), :], 0.0)
        big = jnp.concatenate([prev, centre, nxt], axis=0)
        xc = cb_ref[...] + cw_ref[1:2, :] * centre
        for k in (0, 2, 3):
            off = k - 1
            tap = pltpu.roll(big, (-off) % big_rows, 0)[SUBLANES:SUBLANES + tile]
            xc = xc + cw_ref[k:k + 1, :] * tap
        return xc

    def gates(xc, dr):
        g = jnp.dot(xc.astype(jnp.bfloat16), wg_ref[0, dr],
                    preferred_element_type=jnp.float32) + gb_ref[0, dr]
        gate_r = jax.nn.sigmoid(g[:, :c])
        gate_i = jax.nn.sigmoid(g[:, c:])
        neg_lam = -lam_ref[0, dr]
        softplus = jnp.maximum(neg_lam, 0.0) + jnp.log1p(jnp.exp(-jnp.abs(neg_lam)))
        log_a = -RG_C * gate_r * softplus
        a = jnp.exp(log_a)
        th = jnp.tanh(log_a)
        one_minus_a2 = -2.0 * th / (1.0 - th)
        u = jnp.sqrt(one_minus_a2) * (gate_i * xc)
        return a, u

    def tile_prefix(a, u, reverse):
        a = a.reshape(n_grp, SUBLANES, c)
        u = u.reshape(n_grp, SUBLANES, c)
        for s in (1, 2, 4):
            if reverse:
                a_sh = pltpu.roll(a, SUBLANES - s, 1)
                u_sh = pltpu.roll(u, SUBLANES - s, 1)
                m = row_in_grp < SUBLANES - s
            else:
                a_sh = pltpu.roll(a, s, 1)
                u_sh = pltpu.roll(u, s, 1)
                m = row_in_grp >= s
            u = jnp.where(m, u + a * u_sh, u)
            a = jnp.where(m, a * a_sh, a)
        a_s[...] = a.reshape(tile, c)
        u_s[...] = u.reshape(tile, c)

    def run_direction(dr, reverse):
        def tile_body(ti, h):
            t_idx = (n_tiles - 1 - ti) if reverse else ti
            t0 = pl.multiple_of(t_idx * tile, tile)
            if reverse:
                xc = xc_s[pl.ds(t0, tile), :]
            else:
                xc = conv_tile(t0)
                xc_s[pl.ds(t0, tile), :] = xc
            a, u = gates(xc, dr)
            tile_prefix(a, u, reverse)

            def grp_body(gi, hc):
                g_idx = (n_grp - 1 - gi) if reverse else gi
                r0 = pl.multiple_of(g_idx * SUBLANES, SUBLANES)
                hg = u_s[pl.ds(r0, SUBLANES), :] + a_s[pl.ds(r0, SUBLANES), :] * hc
                h_s[pl.ds(r0, SUBLANES), :] = hg
                edge = hg[0:1, :] if reverse else hg[SUBLANES - 1:SUBLANES, :]
                return jnp.broadcast_to(edge, (SUBLANES, c))

            h = lax.fori_loop(0, n_grp, grp_body, h, unroll=4)
            if reverse:
                o_ref[0, pl.ds(t0, tile), :] = (hsum[pl.ds(t0, tile), :] + h_s[...]).astype(o_ref.dtype)
            else:
                hsum[pl.ds(t0, tile), :] = h_s[...]
            return h

        lax.fori_loop(0, n_tiles, tile_body, jnp.zeros((SUBLANES, c), jnp.float32))

    run_direction(0, False)
    run_direction(1, True)


def rglru_branch(proj3, cw, cb, wg, gb, lam):
    b, s, _ = proj3.shape
    c = RNN_CHUNK
    n_chunks = D_RNN // c
    kern = functools.partial(_rglru_kernel, seq=s, tile=RNN_TILE)
    return pl.pallas_call(
        kern,
        grid=(b, n_chunks),
        in_specs=[pl.BlockSpec((1, s, c), lambda bi, ci: (bi, 0, COL_XRNN // c + ci)),
                  pl.BlockSpec((CONV_WIDTH, c), lambda bi, ci: (0, ci)),
                  pl.BlockSpec((1, c), lambda bi, ci: (0, ci)),
                  pl.BlockSpec((1, 2, c, 2 * c), lambda bi, ci: (ci, 0, 0, 0)),
                  pl.BlockSpec((1, 2, 1, 2 * c), lambda bi, ci: (ci, 0, 0, 0)),
                  pl.BlockSpec((1, 2, 1, c), lambda bi, ci: (ci, 0, 0, 0))],
        out_specs=pl.BlockSpec((1, s, c), lambda bi, ci: (bi, 0, ci)),
        out_shape=jax.ShapeDtypeStruct((b, s, D_RNN), jnp.bfloat16),
        scratch_shapes=[pltpu.VMEM((s, c), jnp.float32),
                        pltpu.VMEM((s, c), jnp.float32),
                        pltpu.VMEM((RNN_TILE, c), jnp.float32),
                        pltpu.VMEM((RNN_TILE, c), jnp.float32),
                        pltpu.VMEM((RNN_TILE, c), jnp.float32)],
        compiler_params=_cparams(2),
        name="rglru_branch",
    )(proj3, cw, cb, wg, gb, lam)


def _attn_group(refs, bias, o_dst, l_dst, *, dil, is_first, is_last):
    q_ref, kp_ref, kc_ref, kn_ref, vp_ref, vc_ref, vn_ref = refs
    rows = ATTN_ROWS
    per_res = rows // dil
    n_q = per_res // RADIUS
    win = 3 * RADIUS
    nb = dil * n_q

    def fold(ref, r, n):
        if dil == 1:
            return ref[0, pl.ds(r, n), :]
        return ref[0, pl.ds(r, n, stride=dil), :]

    qs, ks, vs = [], [], []
    for r in range(dil):
        qf = fold(q_ref, r, per_res).astype(jnp.bfloat16)
        kf = jnp.concatenate([fold(kp_ref, r, RADIUS), fold(kc_ref, r, per_res),
                              fold(kn_ref, r, RADIUS)], axis=0).astype(jnp.bfloat16)
        vf = jnp.concatenate([fold(vp_ref, r, RADIUS), fold(vc_ref, r, per_res),
                              fold(vn_ref, r, RADIUS)], axis=0).astype(jnp.bfloat16)
        for jq in range(n_q):
            qs.append(qf[jq * RADIUS:(jq + 1) * RADIUS])
            ks.append(kf[jq * RADIUS:jq * RADIUS + win])
            vs.append(vf[jq * RADIUS:jq * RADIUS + win])
    qb, kb, vb = jnp.stack(qs), jnp.stack(ks), jnp.stack(vs)

    shape = (nb, RADIUS, win)
    jq_of = lax.broadcasted_iota(jnp.int32, shape, 0) & (n_q - 1)
    qi = lax.broadcasted_iota(jnp.int32, shape, 1)
    kj = lax.broadcasted_iota(jnp.int32, shape, 2)
    lo = jnp.where(jq_of == 0, jnp.where(is_first, RADIUS, 0), 0)
    hi = jnp.where(jq_of == n_q - 1, jnp.where(is_last, 2 * RADIUS, win), win)
    mask = (jnp.abs(kj - RADIUS - qi) <= RADIUS) & (kj >= lo) & (kj < hi)

    s = jnp.einsum('bqd,bkd->bqk', qb, kb, preferred_element_type=jnp.float32) * (HEAD_DIM ** -0.5)
    s = jnp.where(mask, s + bias[None], NEG_INF)
    m = jnp.max(s, axis=-1, keepdims=True)
    p = jnp.exp(s - m)
    l = jnp.sum(p, axis=-1, keepdims=True)
    o = jnp.einsum('bqk,bkd->bqd', p.astype(jnp.bfloat16), vb, preferred_element_type=jnp.float32) / l
    lse = m + jnp.log(l)
    for bi in range(nb):
        r, jq = divmod(bi, n_q)
        start = jq * RADIUS * dil + r
        idx = pl.ds(start, RADIUS) if dil == 1 else pl.ds(start, RADIUS, stride=dil)
        o_dst[idx, :] = o[bi]
        l_dst[idx, :] = jnp.broadcast_to(lse[bi], (RADIUS, HEAD_DIM))


def _attn_kernel(*refs, n_row_blocks):
    n_g = N_ATTN_GROUPS
    in_refs = refs[:7 * n_g]
    bias_refs = refs[7 * n_g:8 * n_g]
    y_ref = refs[8 * n_g]
    o_s, l_s = refs[8 * n_g + 1:]
    i = pl.program_id(1)
    for g, (_, dil) in enumerate(DILATED_CONFIGS):
        _attn_group(in_refs[7 * g:7 * g + 7], bias_refs[g][0], o_s.at[g], l_s.at[g],
                    dil=dil, is_first=i == 0, is_last=i == n_row_blocks - 1)
    l0, l1, l2 = l_s[0], l_s[1], l_s[2]
    m = jnp.maximum(jnp.maximum(l0, l1), l2)
    e0, e1, e2 = jnp.exp(l0 - m), jnp.exp(l1 - m), jnp.exp(l2 - m)
    y = (e0 * o_s[0] + e1 * o_s[1] + e2 * o_s[2]) / (e0 + e1 + e2)
    y_ref[0] = y.astype(y_ref.dtype)


def dilated_attention(proj3, biases):
    b, s, _ = proj3.shape
    rows = ATTN_ROWS
    nrb = s // rows
    w = HEAD_DIM
    nh = HEADS_PER_GROUP

    def cur_map(col):
        return lambda bi, i, h: (bi, i, col + h)

    in_specs, operands = [], []
    for g, (_, dil) in enumerate(DILATED_CONFIGS):
        halo = RADIUS * dil
        ratio = rows // halo
        n_halo_blocks = s // halo
        qc, kc, vc = COL_Q // w + g * nh, COL_K // w + g * nh, COL_V // w + g * nh

        def prev_map(col, ratio=ratio):
            return lambda bi, i, h: (bi, jnp.maximum(i * ratio - 1, 0), col + h)

        def next_map(col, ratio=ratio, last=n_halo_blocks - 1):
            return lambda bi, i, h: (bi, jnp.minimum((i + 1) * ratio, last), col + h)

        in_specs += [pl.BlockSpec((1, rows, w), cur_map(qc)),
                     pl.BlockSpec((1, halo, w), prev_map(kc)),
                     pl.BlockSpec((1, rows, w), cur_map(kc)),
                     pl.BlockSpec((1, halo, w), next_map(kc)),
                     pl.BlockSpec((1, halo, w), prev_map(vc)),
                     pl.BlockSpec((1, rows, w), cur_map(vc)),
                     pl.BlockSpec((1, halo, w), next_map(vc))]
        operands += [proj3] * 7
    in_specs += [pl.BlockSpec((1, RADIUS, 3 * RADIUS), lambda bi, i, h: (h, 0, 0))] * N_ATTN_GROUPS
    operands += list(biases)
    return pl.pallas_call(
        functools.partial(_attn_kernel, n_row_blocks=nrb),
        grid=(b, nrb, nh),
        in_specs=in_specs,
        out_specs=pl.BlockSpec((1, rows, w), cur_map(0)),
        out_shape=jax.ShapeDtypeStruct((b, s, D_ATTN_OUT), jnp.bfloat16),
        scratch_shapes=[pltpu.VMEM((N_ATTN_GROUPS, rows, w), jnp.float32),
                        pltpu.VMEM((N_ATTN_GROUPS, rows, w), jnp.float32)],
        compiler_params=_cparams(3),
        name="dilated_attention",
    )(*operands)


def _store_token_tiles(ref, val, first_chunk=0):
    rows = val.shape[0]
    for c in range(val.shape[1] // LANES):
        ref[pl.ds(first_chunk + c, rows, stride=TOKEN_TILE), :] = val[:, c * LANES:(c + 1) * LANES]


def _load_token_tiles(ref, start, rows):
    return [ref[pl.ds(start * TOKEN_TILE + c, rows, stride=TOKEN_TILE), :] for c in range(TOKEN_TILE)]


def _layer_norm(z, g, b):
    mu = jnp.mean(z, axis=-1, keepdims=True)
    zc = z - mu
    var = jnp.mean(zc * zc, axis=-1, keepdims=True)
    return zc * lax.rsqrt(var + LN_EPS) * g + b


def _mix_kernel(x_ref, grnn_ref, gattn_ref, h_ref, y_ref,
                wr_ref, wa_ref, wo_ref, lng_ref, lnb_ref, rw_ref, rb_ref,
                xo_ref, eidx_ref, gate_ref):
    rnn = jnp.dot(h_ref[...], wr_ref[...], preferred_element_type=jnp.float32)
    att = jnp.dot(y_ref[...], wa_ref[...], preferred_element_type=jnp.float32)
    mixed = jax.nn.sigmoid(grnn_ref[...]) * rnn + jax.nn.sigmoid(gattn_ref[...]) * att
    z = ALPHA * x_ref[...] + jnp.dot(mixed.astype(jnp.bfloat16), wo_ref[...],
                                     preferred_element_type=jnp.float32)
    x1 = _layer_norm(z, lng_ref[...], lnb_ref[...])
    _store_token_tiles(xo_ref, x1)

    logits = jnp.dot(x1, rw_ref[...], preferred_element_type=jnp.float32,
                     precision=lax.Precision.HIGHEST) + rb_ref[...]
    tm = logits.shape[0]
    lane = lax.broadcasted_iota(jnp.int32, (tm, ROUTE_W), 1)
    lane_f = lane.astype(jnp.float32)
    far = float(ROUTE_W)
    is_group = lane < N_EXPERT_GROUPS
    gl = jnp.where(is_group, logits, -jnp.inf)
    gmax = jnp.max(gl, axis=-1, keepdims=True)
    gsel = jnp.min(jnp.where(gl == gmax, lane_f, far), axis=-1, keepdims=True)
    p_group = 1.0 / jnp.sum(jnp.where(is_group, jnp.exp(logits - gmax), 0.0), axis=-1, keepdims=True)
    lane_group = ((lane - N_EXPERT_GROUPS) >> 3).astype(jnp.float32)
    in_group = (lane >= N_EXPERT_GROUPS) & (lane < N_EXPERT_GROUPS + N_EXPERTS) & (lane_group == gsel)
    el = jnp.where(in_group, logits, -jnp.inf)
    m1 = jnp.max(el, axis=-1, keepdims=True)
    i1 = jnp.min(jnp.where(el == m1, lane_f, far), axis=-1, keepdims=True)
    el2 = jnp.where(lane_f == i1, -jnp.inf, el)
    m2 = jnp.max(el2, axis=-1, keepdims=True)
    i2 = jnp.min(jnp.where(el2 == m2, lane_f, far), axis=-1, keepdims=True)
    e21 = jnp.exp(m2 - m1)
    den = 1.0 + e21
    g1 = p_group * (1.0 / den)
    g2 = p_group * (e21 / den)
    eidx = jnp.where(lane == 0, i1, jnp.where(lane == 1, i2, float(N_EXPERT_GROUPS))) - float(N_EXPERT_GROUPS)
    eidx_ref[...] = eidx.astype(jnp.int32)
    gate_ref[...] = jnp.where(lane == 0, g1, jnp.where(lane == 1, g2, 0.0))


def mix_and_route(x, proj, h_rnn, y_attn, wr, wa, wo, lng, lnb, rw, rb, tm=512):
    n = x.shape[0]
    d = D_MODEL
    w = D_ATTN_OUT
    row = lambda width, col=0: pl.BlockSpec((tm, width), lambda i, c=col: (i, c))
    full = lambda a: pl.BlockSpec(a.shape, lambda i: (0,) * a.ndim)
    return pl.pallas_call(
        _mix_kernel,
        grid=(n // tm,),
        in_specs=[row(d), row(d, COL_GRNN // d), row(d, COL_GATTN // d), row(d), row(w),
                  full(wr), full(wa), full(wo), full(lng), full(lnb), full(rw), full(rb)],
        out_specs=[pl.BlockSpec((tm * TOKEN_TILE, LANES), lambda i: (i, 0)), row(ROUTE_W), row(ROUTE_W)],
        out_shape=[jax.ShapeDtypeStruct((n * TOKEN_TILE, LANES), jnp.float32),
                   jax.ShapeDtypeStruct((n, ROUTE_W), jnp.int32),
                   jax.ShapeDtypeStruct((n, ROUTE_W), jnp.float32)],
        compiler_params=_cparams(1),
        name="mix_and_route",
    )(x, proj, proj, h_rnn, y_attn, wr, wa, wo, lng, lnb, rw, rb)


def _rank_kernel(e_ref, dest_ref, cnt_ref, carry, *, tm):
    ph = pl.program_id(0)
    i = pl.program_id(1)
    lanes = ROUTE_W

    @pl.when((ph == 0) & (i == 0))
    def _():
        carry[...] = jnp.zeros_like(carry)

    @pl.when((ph == 1) & (i == 0))
    def _():
        cnt = carry[...]
        cnt_ref[...] = jnp.broadcast_to(cnt, (lanes, lanes)).astype(jnp.int32)
        padded = jnp.floor((cnt + (MOE_BLK - 1)) * (1.0 / MOE_BLK)) * MOE_BLK
        r = lax.broadcasted_iota(jnp.int32, (lanes, lanes), 0)
        c = lax.broadcasted_iota(jnp.int32, (lanes, lanes), 1)
        lower = (c < r).astype(jnp.float32)
        pstart = jnp.dot(lower, jnp.broadcast_to(padded, (lanes, lanes)),
                         preferred_element_type=jnp.float32, precision=lax.Precision.HIGHEST)
        carry[...] = pstart[:, 0:1]

    r8 = lax.broadcasted_iota(jnp.int32, (SUBLANES, lanes), 0)
    c8 = lax.broadcasted_iota(jnp.int32, (SUBLANES, lanes), 1)
    pick = ((r8 == c8) & (r8 < TOP_K)).astype(jnp.bfloat16)
    ef = e_ref[...].astype(jnp.float32).astype(jnp.bfloat16)
    et = lax.dot_general(pick, ef, (((1,), (1,)), ((), ())), preferred_element_type=jnp.float32)
    e0, e1 = et[0:1, :], et[1:2, :]
    sub = lax.broadcasted_iota(jnp.int32, (lanes, tm), 0).astype(jnp.float32)
    is0, is1 = sub == e0, sub == e1
    member = jnp.where(is0 | is1, 1.0, 0.0)
    s_i = lax.broadcasted_iota(jnp.int32, (tm, tm), 0)
    t_i = lax.broadcasted_iota(jnp.int32, (tm, tm), 1)
    earlier = (s_i < t_i).astype(jnp.bfloat16)
    before = jnp.dot(member.astype(jnp.bfloat16), earlier, preferred_element_type=jnp.float32)
    pos = before + carry[...]
    d0 = jnp.sum(jnp.where(is0, pos, 0.0), axis=0, keepdims=True)
    d1 = jnp.sum(jnp.where(is1, pos, 0.0), axis=0, keepdims=True)
    row = lax.broadcasted_iota(jnp.int32, (SUBLANES, tm), 0)
    dest_ref[...] = jnp.where(row == 0, d0, jnp.where(row == 1, d1, 0.0)).astype(jnp.int32)
    carry[...] = carry[...] + jnp.sum(member, axis=1, keepdims=True)


def route_slots(eidx, tm=512):
    n = eidx.shape[0]
    nt = n // tm
    return pl.pallas_call(
        functools.partial(_rank_kernel, tm=tm),
        grid=(2, nt),
        in_specs=[pl.BlockSpec((tm, ROUTE_W), lambda ph, i: (i, 0))],
        out_specs=[pl.BlockSpec((SUBLANES, tm), lambda ph, i: (0, i * ph)),
                   pl.BlockSpec((ROUTE_W, ROUTE_W), lambda ph, i: (0, 0))],
        out_shape=[jax.ShapeDtypeStruct((SUBLANES, n), jnp.int32),
                   jax.ShapeDtypeStruct((ROUTE_W, ROUTE_W), jnp.int32)],
        scratch_shapes=[pltpu.VMEM((ROUTE_W, 1), jnp.float32)],
        compiler_params=_cparams(2),
        name="route_slots",
    )(eidx)


def _slot_map_kernel(dest_ref, slot_ref, *, n, slots):
    def init(j, c):
        slot_ref[j] = 0
        return c

    lax.fori_loop(0, slots, init, 0, unroll=32)

    def body(t, c):
        slot_ref[dest_ref[t]] = t
        slot_ref[dest_ref[n + t]] = t
        return c

    lax.fori_loop(0, n, body, 0, unroll=16)


def slot_map(dest_flat, n, slots):
    return pl.pallas_call(
        functools.partial(_slot_map_kernel, n=n, slots=slots),
        in_specs=[pl.BlockSpec(memory_space=pltpu.SMEM)],
        out_specs=pl.BlockSpec(memory_space=pltpu.SMEM),
        out_shape=jax.ShapeDtypeStruct((slots,), jnp.int32),
        name="slot_map",
    )(dest_flat)


def _expert_kernel(be_ref, nused_ref, st_ref, x_hbm, wg_ref, wu_ref, wd_ref, y_ref,
                   xs, gate_s, hid_s, land, sem):
    i = pl.program_id(0)
    n_used = nused_ref[0]
    n_parts = 4
    part_rows = MOE_BLK // n_parts
    half_out = D_MODEL // 2

    def active(k):
        return (i + k) < (n_used + k)

    def gather_part(blk, part):
        base = blk * MOE_BLK
        for j in range(part * part_rows, (part + 1) * part_rows):
            src = pl.multiple_of(st_ref[base + j] * TOKEN_TILE, TOKEN_TILE)
            pltpu.make_async_copy(x_hbm.at[pl.ds(src, TOKEN_TILE)], land.at[pl.ds(j * TOKEN_TILE, TOKEN_TILE)],
                                  sem.at[0]).start(priority=j % DMA_QUEUES)

    def wait_block():
        pltpu.make_async_copy(x_hbm.at[pl.ds(0, MOE_BLK * TOKEN_TILE)], land, sem.at[0]).wait()

    @pl.when(i == 0)
    def _():
        for part in range(n_parts):
            gather_part(0, part)

    @pl.when(active(0))
    def _():
        wait_block()
        for c, chunk in enumerate(_load_token_tiles(land, 0, MOE_BLK)):
            xs[:, c * LANES:(c + 1) * LANES] = chunk.astype(jnp.bfloat16)

    def compute_piece(piece):
        if piece == 0:
            gate_s[...] = jnp.dot(xs[...], wg_ref[0], preferred_element_type=jnp.float32)
        elif piece == 1:
            up = jnp.dot(xs[...], wu_ref[0], preferred_element_type=jnp.float32)
            hid_s[...] = (jax.nn.silu(gate_s[...]) * up).astype(jnp.bfloat16)
        else:
            cols = slice((piece - 2) * half_out, (piece - 1) * half_out)
            y = jnp.dot(hid_s[...], wd_ref[0, :, cols], preferred_element_type=jnp.float32)
            _store_token_tiles(y_ref, y, first_chunk=(piece - 2) * half_out // LANES)

    for piece in range(n_parts):
        @pl.when(active(2 * piece + 1))
        def _(piece=piece):
            gather_part(i + 1, piece)

        @pl.when(active(2 * piece + 2))
        def _(piece=piece):
            compute_piece(piece)

    @pl.when(i == n_used - 1)
    def _():
        wait_block()

    @pl.when(i >= n_used)
    def _():
        y_ref[...] = jnp.zeros_like(y_ref)


def expert_blocks(block_e, n_used, slot_tok, x1t, wg, wu, wd):
    d = D_MODEL
    n_blocks = slot_tok.shape[0] // MOE_BLK - 1
    slots = n_blocks * MOE_BLK
    tile_rows = MOE_BLK * TOKEN_TILE
    grid_spec = pltpu.PrefetchScalarGridSpec(
        num_scalar_prefetch=3,
        grid=(n_blocks,),
        in_specs=[pl.BlockSpec(memory_space=pl.ANY),
                  pl.BlockSpec((1, d, D_EXPERT), lambda i, be, nu, st: (be[i], 0, 0)),
                  pl.BlockSpec((1, d, D_EXPERT), lambda i, be, nu, st: (be[i], 0, 0)),
                  pl.BlockSpec((1, D_EXPERT, d), lambda i, be, nu, st: (be[i], 0, 0))],
        out_specs=pl.BlockSpec((tile_rows, LANES), lambda i, be, nu, st: (i, 0)),
        scratch_shapes=[pltpu.VMEM((MOE_BLK, d), jnp.bfloat16),
                        pltpu.VMEM((MOE_BLK, D_EXPERT), jnp.float32),
                        pltpu.VMEM((MOE_BLK, D_EXPERT), jnp.bfloat16),
                        pltpu.VMEM((tile_rows, LANES), jnp.float32),
                        pltpu.SemaphoreType.DMA((1,))],
    )
    return pl.pallas_call(
        _expert_kernel,
        grid_spec=grid_spec,
        out_shape=jax.ShapeDtypeStruct((slots * TOKEN_TILE, LANES), jnp.float32),
        compiler_params=_cparams(1),
        name="expert_blocks",
    )(block_e, n_used, slot_tok, x1t, wg, wu, wd)


def _combine_kernel(dest_ref, x_ref, gate_ref, y_hbm, g_ref, b_ref, xo_ref, xob_ref, fbuf, sem, *, n, tm):
    i = pl.program_id(0)
    nt = pl.num_programs(0)
    half = tm * TOKEN_TILE

    def gather_tile(tile, slot):
        base = tile * tm
        for k in range(TOP_K):
            for t in range(tm):
                src = pl.multiple_of(dest_ref[k * n + base + t] * TOKEN_TILE, TOKEN_TILE)
                pltpu.make_async_copy(y_hbm.at[pl.ds(src, TOKEN_TILE)],
                                      fbuf.at[slot, pl.ds((k * tm + t) * TOKEN_TILE, TOKEN_TILE)],
                                      sem.at[slot]).start(priority=t % DMA_QUEUES)

    def wait_tile(slot):
        pltpu.make_async_copy(y_hbm.at[pl.ds(0, TOP_K * half)], fbuf.at[slot], sem.at[slot]).wait()

    @pl.when(i == 0)
    def _():
        gather_tile(0, 0)

    slot = i % 2
    wait_tile(slot)
    g0, g1 = gate_ref[:, 0:1], gate_ref[:, 1:2]
    f0 = _load_token_tiles(fbuf.at[slot], 0, tm)
    f1 = _load_token_tiles(fbuf.at[slot], tm, tm)
    xs = _load_token_tiles(x_ref, 0, tm)
    gather_tile(jnp.minimum(i + 1, nt - 1), 1 - slot)
    z = [ALPHA * xs[c] + (g0 * f0[c] + g1 * f1[c]) for c in range(TOKEN_TILE)]
    mu = sum(jnp.sum(zc, axis=-1, keepdims=True) for zc in z) * (1.0 / D_MODEL)
    zc = [zz - mu for zz in z]
    var = sum(jnp.sum(v * v, axis=-1, keepdims=True) for v in zc) * (1.0 / D_MODEL)
    inv = lax.rsqrt(var + LN_EPS)
    for c in range(TOKEN_TILE):
        cols = slice(c * LANES, (c + 1) * LANES)
        x2 = zc[c] * inv * g_ref[:, cols] + b_ref[:, cols]
        xo_ref[:, cols] = x2
        xob_ref[:, cols] = x2.astype(jnp.bfloat16)

    @pl.when(i == nt - 1)
    def _():
        wait_tile(1 - slot)


def combine_ln2(dest_flat, x1t, gate, ybt, g, b, tm=256):
    n = gate.shape[0]
    d = D_MODEL
    tile_rows = tm * TOKEN_TILE
    grid_spec = pltpu.PrefetchScalarGridSpec(
        num_scalar_prefetch=1,
        grid=(n // tm,),
        in_specs=[pl.BlockSpec((tile_rows, LANES), lambda i, ds: (i, 0)),
                  pl.BlockSpec((tm, ROUTE_W), lambda i, ds: (i, 0)),
                  pl.BlockSpec(memory_space=pl.ANY),
                  pl.BlockSpec((1, d), lambda i, ds: (0, 0)),
                  pl.BlockSpec((1, d), lambda i, ds: (0, 0))],
        out_specs=[pl.BlockSpec((tm, d), lambda i, ds: (i, 0)),
                   pl.BlockSpec((tm, d), lambda i, ds: (i, 0))],
        scratch_shapes=[pltpu.VMEM((2, TOP_K * tile_rows, LANES), jnp.float32),
                        pltpu.SemaphoreType.DMA((2,))],
    )
    return pl.pallas_call(
        functools.partial(_combine_kernel, n=n, tm=tm),
        grid_spec=grid_spec,
        out_shape=[jax.ShapeDtypeStruct((n, d), jnp.float32),
                   jax.ShapeDtypeStruct((n, d), jnp.bfloat16)],
        compiler_params=_cparams(1),
        name="combine_ln2",
    )(dest_flat, x1t, gate, ybt, g, b)


def moe_layer(x1, eidx, gate, wg, wu, wd, ln_g, ln_b):
    n = eidx.shape[0]
    m = n * TOP_K
    n_blocks = m // MOE_BLK + N_EXPERTS
    slots = n_blocks * MOE_BLK
    dest8, cnt = route_slots(eidx)
    dest_flat = dest8[:TOP_K].reshape(m)
    counts = cnt[:N_EXPERTS, 0]
    pend = jnp.cumsum((counts + MOE_BLK - 1) // MOE_BLK * MOE_BLK)
    blk_start = jnp.arange(n_blocks, dtype=jnp.int32) * MOE_BLK
    block_e = jnp.minimum(jnp.sum((pend[None, :] <= blk_start[:, None]).astype(jnp.int32), axis=1),
                          N_EXPERTS - 1)
    n_used = (pend[-1:] // MOE_BLK).astype(jnp.int32)
    slot_tok = slot_map(dest_flat, n, slots + MOE_BLK)
    yb = expert_blocks(block_e, n_used, slot_tok, x1, wg, wu, wd)
    return combine_ln2(dest_flat, x1, gate, yb, ln_g, ln_b)


def _t5_bucket(rel):
    half = NUM_BUCKETS // 2
    max_exact = half // 2
    n = np.abs(rel)
    large = max_exact + (np.log(np.maximum(n, 1) / max_exact) / np.log(MAX_DISTANCE / max_exact)
                         * (half - max_exact)).astype(np.int32)
    large = np.minimum(large, half - 1)
    return np.where(rel > 0, half, 0) + np.where(n < max_exact, n, large)


def _bias_table(rel_bias, g, dil):
    rel = (np.arange(3 * RADIUS)[None, :] - RADIUS - np.arange(RADIUS)[:, None]) * dil
    hs = slice(g * HEADS_PER_GROUP, (g + 1) * HEADS_PER_GROUP)
    onehot = jnp.asarray(_t5_bucket(rel)[..., None] == np.arange(NUM_BUCKETS), jnp.float32)
    return jnp.einsum('qkb,bh->hqk', onehot, rel_bias[:, hs].astype(jnp.float32),
                      precision=lax.Precision.HIGHEST)


def _chunk_block_diag(w):
    per = RNN_CHUNK // RNN_BLOCK_W
    n_chunks = RNN_BLOCKS // per
    w4 = w.reshape(n_chunks, per, RNN_BLOCK_W, RNN_BLOCK_W)
    dense = jnp.einsum('chij,hk->chikj', w4, jnp.eye(per, dtype=w.dtype))
    return dense.reshape(n_chunks, RNN_CHUNK, RNN_CHUNK)


def _gate_weights(wa, wi):
    per_dir = [jnp.concatenate([_chunk_block_diag(wa[d]), _chunk_block_diag(wi[d])], axis=-1)
               for d in range(2)]
    return jnp.stack(per_dir, axis=1).astype(jnp.bfloat16)


def _per_chunk(v):
    return v.reshape(2, D_RNN // RNN_CHUNK, 1, RNN_CHUNK).transpose(1, 0, 2, 3)


def _permute_in_cols(a):
    x_rnn, q, k, v, g_rnn, g_attn = jnp.split(
        a, [D_RNN, D_RNN + D_ATTN, D_RNN + 2 * D_ATTN, D_RNN + 3 * D_ATTN,
            D_RNN + 3 * D_ATTN + D_MODEL], axis=-1)
    return jnp.concatenate([x_rnn, g_rnn, g_attn, q, k, v], axis=-1)


def _trunk(x, biases, lp):
    b, s, d = x.shape
    n = b * s
    x2 = x.reshape(n, d)
    x2_bf = x2.astype(jnp.bfloat16)
    for l in range(DEPTH):
        p = lp[l]
        proj = in_projection(x2_bf, p['w_in'], p['b_in'])
        proj3 = proj.reshape(b, s, D_IN)
        h_rnn = rglru_branch(proj3, p['conv_w'], p['conv_b'], p['wg'], p['gb'], p['lam'])
        y_attn = dilated_attention(proj3, biases)
        x1, eidx, gate = mix_and_route(
            x2, proj, h_rnn.reshape(n, D_RNN), y_attn.reshape(n, D_ATTN_OUT),
            p['w_rnn_out'], p['w_attn_out'], p['w_o'],
            p['ln1_g'], p['ln1_b'], p['router_w'], p['router_b'])
        x2, x2_bf = moe_layer(x1, eidx, gate, p['w_gate'], p['w_up'], p['w_down'], p['ln2_g'], p['ln2_b'])
    return x2.reshape(b, s, d)


def kernel(x_prompt, x_sample, rel_bias, w_in, b_in, conv_w, conv_b, rg_wa, rg_ba, rg_wi, rg_bi, rg_lam, w_rnn_out, w_attn_out, w_o, ln1_g, ln1_b, router_w, router_b, expert_router_w, expert_router_b, w_gate, w_up, w_down, ln2_g, ln2_b):
    bf = jnp.bfloat16
    biases = [_bias_table(rel_bias, g, dil) for g, (_, dil) in enumerate(DILATED_CONFIGS)]
    lp = []
    for l in range(DEPTH):
        rw = jnp.concatenate(
            [router_w[l], jnp.transpose(expert_router_w[l], (1, 0, 2)).reshape(D_MODEL, N_EXPERTS)], axis=1)
        rb = jnp.concatenate([router_b[l], expert_router_b[l].reshape(N_EXPERTS)])
        pad = ROUTE_W - rw.shape[1]
        lp.append(dict(
            w_in=_permute_in_cols(w_in[l]).astype(bf),
            b_in=_permute_in_cols(b_in[l])[None, :],
            conv_w=conv_w[l], conv_b=conv_b[l][None, :],
            wg=_gate_weights(rg_wa[l], rg_wi[l]),
            gb=jnp.concatenate([_per_chunk(rg_ba[l]), _per_chunk(rg_bi[l])], axis=-1),
            lam=_per_chunk(rg_lam[l]),
            w_rnn_out=w_rnn_out[l].astype(bf), w_attn_out=w_attn_out[l].astype(bf), w_o=w_o[l].astype(bf),
            ln1_g=ln1_g[l][None, :], ln1_b=ln1_b[l][None, :],
            router_w=jnp.pad(rw, ((0, 0), (0, pad))), router_b=jnp.pad(rb, (0, pad))[None, :],
            w_gate=w_gate[l].astype(bf), w_up=w_up[l].astype(bf), w_down=w_down[l].astype(bf),
            ln2_g=ln2_g[l][None, :], ln2_b=ln2_b[l][None, :]))
    return (_trunk(x_prompt, biases, lp), _trunk(x_sample, biases, lp))
```

```python
import functools

import numpy as np
import jax
import jax.numpy as jnp
from jax import lax
from jax.experimental import pallas as pl
from jax.experimental.pallas import tpu as pltpu

D_MODEL = 1024
DEPTH = 2
D_RNN = D_MODEL
RNN_BLOCKS = 16
RNN_BLOCK_W = D_RNN // RNN_BLOCKS
CONV_WIDTH = 4
RG_C = 8.0
DILATED_CONFIGS = ((128, 1), (512, 4), (2048, 16))
N_ATTN_GROUPS = len(DILATED_CONFIGS)
HEADS_PER_GROUP = 4
N_ATTN_HEADS = N_ATTN_GROUPS * HEADS_PER_GROUP
HEAD_DIM = 128
D_ATTN = N_ATTN_HEADS * HEAD_DIM
D_ATTN_OUT = HEADS_PER_GROUP * HEAD_DIM
NUM_BUCKETS = 32
MAX_DISTANCE = max(w for w, _ in DILATED_CONFIGS) // 2
NEG_INF = -1e30
D_IN = D_RNN + 3 * D_ATTN + 2 * D_MODEL
N_EXPERT_GROUPS = 4
EXPERTS_PER_GROUP = 8
N_EXPERTS = N_EXPERT_GROUPS * EXPERTS_PER_GROUP
TOP_K = 2
D_EXPERT = D_MODEL // 2
ALPHA = (2 * DEPTH) ** 0.25
LN_EPS = 1e-5

LANES = 128
SUBLANES = 8
VMEM_LIMIT = 56 * 1024 * 1024

COL_XRNN = 0
COL_GRNN = D_RNN
COL_GATTN = D_RNN + D_MODEL
COL_Q = D_RNN + 2 * D_MODEL
COL_K = COL_Q + D_ATTN
COL_V = COL_K + D_ATTN

RADIUS = 64
assert all(w // (2 * d) == RADIUS for w, d in DILATED_CONFIGS)
ATTN_ROWS = 1024
RNN_CHUNK = 256
RNN_TILE = 256
MOE_BLK = 256
DMA_QUEUES = 2
TOKEN_TILE = D_MODEL // LANES
assert TOKEN_TILE == SUBLANES
ROUTE_W = LANES


def _cparams(n_axes):
    return pltpu.CompilerParams(dimension_semantics=("arbitrary",) * n_axes,
                                vmem_limit_bytes=VMEM_LIMIT)


def _proj_kernel(x_ref, w_ref, b_ref, o_ref):
    acc = jnp.dot(x_ref[...], w_ref[...], preferred_element_type=jnp.float32)
    o_ref[...] = acc + b_ref[...]


def in_projection(x_bf, w_bf, b, tm=512, tn=1536):
    n, k = x_bf.shape
    nout = w_bf.shape[1]
    return pl.pallas_call(
        _proj_kernel,
        grid=(nout // tn, n // tm),
        in_specs=[pl.BlockSpec((tm, k), lambda j, i: (i, 0)),
                  pl.BlockSpec((k, tn), lambda j, i: (0, j)),
                  pl.BlockSpec((1, tn), lambda j, i: (0, j))],
        out_specs=pl.BlockSpec((tm, tn), lambda j, i: (i, j)),
        out_shape=jax.ShapeDtypeStruct((n, nout), jnp.float32),
        compiler_params=_cparams(2),
        name="in_projection",
    )(x_bf, w_bf, b)


def _rglru_kernel(x_ref, cw_ref, cb_ref, wg_ref, gb_ref, lam_ref, o_ref,
                  hsum, xc_s, a_s, u_s, h_s, *, seq, tile):
    c = RNN_CHUNK
    n_tiles = seq // tile
    n_grp = tile // SUBLANES
    big_rows = tile + 2 * SUBLANES
    row_in_grp = lax.broadcasted_iota(jnp.int32, (n_grp, SUBLANES, c), 1)

    def conv_tile(t0):
        centre = x_ref[0, pl.ds(t0, tile), :]
        p0 = pl.multiple_of(jnp.maximum(t0 - SUBLANES, 0), SUBLANES)
        n0 = pl.multiple_of(jnp.minimum(t0 + tile, seq - SUBLANES), SUBLANES)
        prev = jnp.where(t0 > 0, x_ref[0, pl.ds(p0, SUBLANES), :], 0.0)
        nxt = jnp.where(t0 + tile < seq, x_ref[0, pl.ds(n0, SUBLANES), :], 0.0)
        big = jnp.concatenate([prev, centre, nxt], axis=0)
        xc = cb_ref[...] + cw_ref[1:2, :] * centre
        for k in (0, 2, 3):
            off = k - 1
            tap = pltpu.roll(big, (-off) % big_rows, 0)[SUBLANES:SUBLANES + tile]
            xc = xc + cw_ref[k:k + 1, :] * tap
        return xc

    def gates(xc, dr):
        g = jnp.dot(xc.astype(jnp.bfloat16), wg_ref[0, dr],
                    preferred_element_type=jnp.float32) + gb_ref[0, dr]
        gate_r = jax.nn.sigmoid(g[:, :c])
        gate_i = jax.nn.sigmoid(g[:, c:])
        neg_lam = -lam_ref[0, dr]
        softplus = jnp.maximum(neg_lam, 0.0) + jnp.log1p(jnp.exp(-jnp.abs(neg_lam)))
        log_a = -RG_C * gate_r * softplus
        a = jnp.exp(log_a)
        th = jnp.tanh(log_a)
        one_minus_a2 = -2.0 * th / (1.0 - th)
        u = jnp.sqrt(one_minus_a2) * (gate_i * xc)
        return a, u

    def tile_prefix(a, u, reverse):
        a = a.reshape(n_grp, SUBLANES, c)
        u = u.reshape(n_grp, SUBLANES, c)
        for s in (1, 2, 4):
            if reverse:
                a_sh = pltpu.roll(a, SUBLANES - s, 1)
                u_sh = pltpu.roll(u, SUBLANES - s, 1)
                m = row_in_grp < SUBLANES - s
            else:
                a_sh = pltpu.roll(a, s, 1)
                u_sh = pltpu.roll(u, s, 1)
                m = row_in_grp >= s
            u = jnp.where(m, u + a * u_sh, u)
            a = jnp.where(m, a * a_sh, a)
        a_s[...] = a.reshape(tile, c)
        u_s[...] = u.reshape(tile, c)

    def run_direction(dr, reverse):
        def tile_body(ti, h):
            t_idx = (n_tiles - 1 - ti) if reverse else ti
            t0 = pl.multiple_of(t_idx * tile, tile)
            if reverse:
                xc = xc_s[pl.ds(t0, tile), :]
            else:
                xc = conv_tile(t0)
                xc_s[pl.ds(t0, tile), :] = xc
            a, u = gates(xc, dr)
            tile_prefix(a, u, reverse)

            def grp_body(gi, hc):
                g_idx = (n_grp - 1 - gi) if reverse else gi
                r0 = pl.multiple_of(g_idx * SUBLANES, SUBLANES)
                hg = u_s[pl.ds(r0, SUBLANES), :] + a_s[pl.ds(r0, SUBLANES), :] * hc
                h_s[pl.ds(r0, SUBLANES), :] = hg
                edge = hg[0:1, :] if reverse else hg[SUBLANES - 1:SUBLANES, :]
                return jnp.broadcast_to(edge, (SUBLANES, c))

            h = lax.fori_loop(0, n_grp, grp_body, h, unroll=4)
            if reverse:
                o_ref[0, pl.ds(t0, tile), :] = (hsum[pl.ds(t0, tile), :] + h_s[...]).astype(o_ref.dtype)
            else:
                hsum[pl.ds(t0, tile), :] = h_s[...]
            return h

        lax.fori_loop(0, n_tiles, tile_body, jnp.zeros((SUBLANES, c), jnp.float32))

    run_direction(0, False)
    run_direction(1, True)


def rglru_branch(proj3, cw, cb, wg, gb, lam):
    b, s, _ = proj3.shape
    c = RNN_CHUNK
    n_chunks = D_RNN // c
    kern = functools.partial(_rglru_kernel, seq=s, tile=RNN_TILE)
    return pl.pallas_call(
        kern,
        grid=(b, n_chunks),
        in_specs=[pl.BlockSpec((1, s, c), lambda bi, ci: (bi, 0, COL_XRNN // c + ci)),
                  pl.BlockSpec((CONV_WIDTH, c), lambda bi, ci: (0, ci)),
                  pl.BlockSpec((1, c), lambda bi, ci: (0, ci)),
                  pl.BlockSpec((1, 2, c, 2 * c), lambda bi, ci: (ci, 0, 0, 0)),
                  pl.BlockSpec((1, 2, 1, 2 * c), lambda bi, ci: (ci, 0, 0, 0)),
                  pl.BlockSpec((1, 2, 1, c), lambda bi, ci: (ci, 0, 0, 0))],
        out_specs=pl.BlockSpec((1, s, c), lambda bi, ci: (bi, 0, ci)),
        out_shape=jax.ShapeDtypeStruct((b, s, D_RNN), jnp.bfloat16),
        scratch_shapes=[pltpu.VMEM((s, c), jnp.float32),
                        pltpu.VMEM((s, c), jnp.float32),
                        pltpu.VMEM((RNN_TILE, c), jnp.float32),
                        pltpu.VMEM((RNN_TILE, c), jnp.float32),
                        pltpu.VMEM((RNN_TILE, c), jnp.float32)],
        compiler_params=_cparams(2),
        name="rglru_branch",
    )(proj3, cw, cb, wg, gb, lam)


def _attn_group(refs, bias, o_dst, l_dst, *, dil, is_first, is_last):
    q_ref, kp_ref, kc_ref, kn_ref, vp_ref, vc_ref, vn_ref = refs
    rows = ATTN_ROWS
    per_res = rows // dil
    n_q = per_res // RADIUS
    win = 3 * RADIUS
    nb = dil * n_q

    def fold(ref, r, n):
        if dil == 1:
            return ref[0, pl.ds(r, n), :]
        return ref[0, pl.ds(r, n, stride=dil), :]

    qs, ks, vs = [], [], []
    for r in range(dil):
        qf = fold(q_ref, r, per_res).astype(jnp.bfloat16)
        kf = jnp.concatenate([fold(kp_ref, r, RADIUS), fold(kc_ref, r, per_res),
                              fold(kn_ref, r, RADIUS)], axis=0).astype(jnp.bfloat16)
        vf = jnp.concatenate([fold(vp_ref, r, RADIUS), fold(vc_ref, r, per_res),
                              fold(vn_ref, r, RADIUS)], axis=0).astype(jnp.bfloat16)
        for jq in range(n_q):
            qs.append(qf[jq * RADIUS:(jq + 1) * RADIUS])
            ks.append(kf[jq * RADIUS:jq * RADIUS + win])
            vs.append(vf[jq * RADIUS:jq * RADIUS + win])
    qb, kb, vb = jnp.stack(qs), jnp.stack(ks), jnp.stack(vs)

    shape = (nb, RADIUS, win)
    jq_of = lax.broadcasted_iota(jnp.int32, shape, 0) & (n_q - 1)
    qi = lax.broadcasted_iota(jnp.int32, shape, 1)
    kj = lax.broadcasted_iota(jnp.int32, shape, 2)
    lo = jnp.where(jq_of == 0, jnp.where(is_first, RADIUS, 0), 0)
    hi = jnp.where(jq_of == n_q - 1, jnp.where(is_last, 2 * RADIUS, win), win)
    mask = (jnp.abs(kj - RADIUS - qi) <= RADIUS) & (kj >= lo) & (kj < hi)

    s = jnp.einsum('bqd,bkd->bqk', qb, kb, preferred_element_type=jnp.float32) * (HEAD_DIM ** -0.5)
    s = jnp.where(mask, s + bias[None], NEG_INF)
    m = jnp.max(s, axis=-1, keepdims=True)
    p = jnp.exp(s - m)
    l = jnp.sum(p, axis=-1, keepdims=True)
    o = jnp.einsum('bqk,bkd->bqd', p.astype(jnp.bfloat16), vb, preferred_element_type=jnp.float32) / l
    lse = m + jnp.log(l)
    for bi in range(nb):
        r, jq = divmod(bi, n_q)
        start = jq * RADIUS * dil + r
        idx = pl.ds(start, RADIUS) if dil == 1 else pl.ds(start, RADIUS, stride=dil)
        o_dst[idx, :] = o[bi]
        l_dst[idx, :] = jnp.broadcast_to(lse[bi], (RADIUS, HEAD_DIM))


def _attn_kernel(*refs, n_row_blocks):
    n_g = N_ATTN_GROUPS
    in_refs = refs[:7 * n_g]
    bias_refs = refs[7 * n_g:8 * n_g]
    y_ref = refs[8 * n_g]
    o_s, l_s = refs[8 * n_g + 1:]
    i = pl.program_id(1)
    for g, (_, dil) in enumerate(DILATED_CONFIGS):
        _attn_group(in_refs[7 * g:7 * g + 7], bias_refs[g][0], o_s.at[g], l_s.at[g],
                    dil=dil, is_first=i == 0, is_last=i == n_row_blocks - 1)
    l0, l1, l2 = l_s[0], l_s[1], l_s[2]
    m = jnp.maximum(jnp.maximum(l0, l1), l2)
    e0, e1, e2 = jnp.exp(l0 - m), jnp.exp(l1 - m), jnp.exp(l2 - m)
    y = (e0 * o_s[0] + e1 * o_s[1] + e2 * o_s[2]) / (e0 + e1 + e2)
    y_ref[0] = y.astype(y_ref.dtype)


def dilated_attention(proj3, biases):
    b, s, _ = proj3.shape
    rows = ATTN_ROWS
    nrb = s // rows
    w = HEAD_DIM
    nh = HEADS_PER_GROUP

    def cur_map(col):
        return lambda bi, i, h: (bi, i, col + h)

    in_specs, operands = [], []
    for g, (_, dil) in enumerate(DILATED_CONFIGS):
        halo = RADIUS * dil
        ratio = rows // halo
        n_halo_blocks = s // halo
        qc, kc, vc = COL_Q // w + g * nh, COL_K // w + g * nh, COL_V // w + g * nh

        def prev_map(col, ratio=ratio):
            return lambda bi, i, h: (bi, jnp.maximum(i * ratio - 1, 0), col + h)

        def next_map(col, ratio=ratio, last=n_halo_blocks - 1):
            return lambda bi, i, h: (bi, jnp.minimum((i + 1) * ratio, last), col + h)

        in_specs += [pl.BlockSpec((1, rows, w), cur_map(qc)),
                     pl.BlockSpec((1, halo, w), prev_map(kc)),
                     pl.BlockSpec((1, rows, w), cur_map(kc)),
                     pl.BlockSpec((1, halo, w), next_map(kc)),
                     pl.BlockSpec((1, halo, w), prev_map(vc)),
                     pl.BlockSpec((1, rows, w), cur_map(vc)),
                     pl.BlockSpec((1, halo, w), next_map(vc))]
        operands += [proj3] * 7
    in_specs += [pl.BlockSpec((1, RADIUS, 3 * RADIUS), lambda bi, i, h: (h, 0, 0))] * N_ATTN_GROUPS
    operands += list(biases)
    return pl.pallas_call(
        functools.partial(_attn_kernel, n_row_blocks=nrb),
        grid=(b, nrb, nh),
        in_specs=in_specs,
        out_specs=pl.BlockSpec((1, rows, w), cur_map(0)),
        out_shape=jax.ShapeDtypeStruct((b, s, D_ATTN_OUT), jnp.bfloat16),
        scratch_shapes=[pltpu.VMEM((N_ATTN_GROUPS, rows, w), jnp.float32),
                        pltpu.VMEM((N_ATTN_GROUPS, rows, w), jnp.float32)],
        compiler_params=_cparams(3),
        name="dilated_attention",
    )(*operands)


def _store_token_tiles(ref, val, first_chunk=0):
    rows = val.shape[0]
    for c in range(val.shape[1] // LANES):
        ref[pl.ds(first_chunk + c, rows, stride=TOKEN_TILE), :] = val[:, c * LANES:(c + 1) * LANES]


def _load_token_tiles(ref, start, rows):
    return [ref[pl.ds(start * TOKEN_TILE + c, rows, stride=TOKEN_TILE), :] for c in range(TOKEN_TILE)]


def _layer_norm(z, g, b):
    mu = jnp.mean(z, axis=-1, keepdims=True)
    zc = z - mu
    var = jnp.mean(zc * zc, axis=-1, keepdims=True)
    return zc * lax.rsqrt(var + LN_EPS) * g + b


def _mix_kernel(x_ref, grnn_ref, gattn_ref, h_ref, y_ref,
                wr_ref, wa_ref, wo_ref, lng_ref, lnb_ref, rw_ref, rb_ref,
                xo_ref, eidx_ref, gate_ref):
    rnn = jnp.dot(h_ref[...], wr_ref[...], preferred_element_type=jnp.float32)
    att = jnp.dot(y_ref[...], wa_ref[...], preferred_element_type=jnp.float32)
    mixed = jax.nn.sigmoid(grnn_ref[...]) * rnn + jax.nn.sigmoid(gattn_ref[...]) * att
    z = ALPHA * x_ref[...] + jnp.dot(mixed.astype(jnp.bfloat16), wo_ref[...],
                                     preferred_element_type=jnp.float32)
    x1 = _layer_norm(z, lng_ref[...], lnb_ref[...])
    _store_token_tiles(xo_ref, x1)

    x_hi = x1.astype(jnp.bfloat16)
    x_lo = (x1 - x_hi.astype(jnp.float32)).astype(jnp.bfloat16)
    t = jnp.dot(x_hi, rw_ref[...], preferred_element_type=jnp.float32)
    logits = (t[:, :ROUTE_W] + t[:, ROUTE_W:]
              + jnp.dot(x_lo, rw_ref[:, :ROUTE_W], preferred_element_type=jnp.float32)) + rb_ref[...]
    tm = logits.shape[0]
    lane = lax.broadcasted_iota(jnp.int32, (tm, ROUTE_W), 1)
    lane_f = lane.astype(jnp.float32)
    far = float(ROUTE_W)
    is_group = lane < N_EXPERT_GROUPS
    gl = jnp.where(is_group, logits, -jnp.inf)
    gmax = jnp.max(gl, axis=-1, keepdims=True)
    gsel = jnp.min(jnp.where(gl == gmax, lane_f, far), axis=-1, keepdims=True)
    p_group = 1.0 / jnp.sum(jnp.where(is_group, jnp.exp(logits - gmax), 0.0), axis=-1, keepdims=True)
    lane_group = ((lane - N_EXPERT_GROUPS) >> 3).astype(jnp.float32)
    in_group = (lane >= N_EXPERT_GROUPS) & (lane < N_EXPERT_GROUPS + N_EXPERTS) & (lane_group == gsel)
    el = jnp.where(in_group, logits, -jnp.inf)
    m1 = jnp.max(el, axis=-1, keepdims=True)
    i1 = jnp.min(jnp.where(el == m1, lane_f, far), axis=-1, keepdims=True)
    el2 = jnp.where(lane_f == i1, -jnp.inf, el)
    m2 = jnp.max(el2, axis=-1, keepdims=True)
    i2 = jnp.min(jnp.where(el2 == m2, lane_f, far), axis=-1, keepdims=True)
    e21 = jnp.exp(m2 - m1)
    den = 1.0 + e21
    g1 = p_group * (1.0 / den)
    g2 = p_group * (e21 / den)
    eidx = jnp.where(lane == 0, i1, jnp.where(lane == 1, i2, float(N_EXPERT_GROUPS))) - float(N_EXPERT_GROUPS)
    eidx_ref[...] = eidx.astype(jnp.int32)
    gate_ref[...] = jnp.where(lane == 0, g1, jnp.where(lane == 1, g2, 0.0))


def mix_and_route(x, proj, h_rnn, y_attn, wr, wa, wo, lng, lnb, rw, rb, tm=512):
    n = x.shape[0]
    d = D_MODEL
    w = D_ATTN_OUT
    row = lambda width, col=0: pl.BlockSpec((tm, width), lambda i, c=col: (i, c))
    full = lambda a: pl.BlockSpec(a.shape, lambda i: (0,) * a.ndim)
    return pl.pallas_call(
        _mix_kernel,
        grid=(n // tm,),
        in_specs=[row(d), row(d, COL_GRNN // d), row(d, COL_GATTN // d), row(d), row(w),
                  full(wr), full(wa), full(wo), full(lng), full(lnb), full(rw), full(rb)],
        out_specs=[pl.BlockSpec((tm * TOKEN_TILE, LANES), lambda i: (i, 0)), row(ROUTE_W), row(ROUTE_W)],
        out_shape=[jax.ShapeDtypeStruct((n * TOKEN_TILE, LANES), jnp.float32),
                   jax.ShapeDtypeStruct((n, ROUTE_W), jnp.int32),
                   jax.ShapeDtypeStruct((n, ROUTE_W), jnp.float32)],
        compiler_params=_cparams(1),
        name="mix_and_route",
    )(x, proj, proj, h_rnn, y_attn, wr, wa, wo, lng, lnb, rw, rb)


def _rank_kernel(e_ref, dest_ref, cnt_ref, carry, *, tm):
    ph = pl.program_id(0)
    i = pl.program_id(1)
    lanes = ROUTE_W

    @pl.when((ph == 0) & (i == 0))
    def _():
        carry[...] = jnp.zeros_like(carry)

    @pl.when((ph == 1) & (i == 0))
    def _():
        cnt = carry[...]
        cnt_ref[...] = jnp.broadcast_to(cnt, (lanes, lanes)).astype(jnp.int32)
        padded = jnp.floor((cnt + (MOE_BLK - 1)) * (1.0 / MOE_BLK)) * MOE_BLK
        r = lax.broadcasted_iota(jnp.int32, (lanes, lanes), 0)
        c = lax.broadcasted_iota(jnp.int32, (lanes, lanes), 1)
        lower = (c < r).astype(jnp.float32)
        pstart = jnp.dot(lower, jnp.broadcast_to(padded, (lanes, lanes)),
                         preferred_element_type=jnp.float32, precision=lax.Precision.HIGHEST)
        carry[...] = pstart[:, 0:1]

    r8 = lax.broadcasted_iota(jnp.int32, (SUBLANES, lanes), 0)
    c8 = lax.broadcasted_iota(jnp.int32, (SUBLANES, lanes), 1)
    pick = ((r8 == c8) & (r8 < TOP_K)).astype(jnp.bfloat16)
    ef = e_ref[...].astype(jnp.float32).astype(jnp.bfloat16)
    et = lax.dot_general(pick, ef, (((1,), (1,)), ((), ())), preferred_element_type=jnp.float32)
    e0, e1 = et[0:1, :], et[1:2, :]
    sub = lax.broadcasted_iota(jnp.int32, (lanes, tm), 0).astype(jnp.float32)
    is0, is1 = sub == e0, sub == e1
    member = jnp.where(is0 | is1, 1.0, 0.0)

    @pl.when(ph == 1)
    def _():
        s_i = lax.broadcasted_iota(jnp.int32, (tm, tm), 0)
        t_i = lax.broadcasted_iota(jnp.int32, (tm, tm), 1)
        earlier = (s_i < t_i).astype(jnp.bfloat16)
        before = jnp.dot(member.astype(jnp.bfloat16), earlier, preferred_element_type=jnp.float32)
        pos = before + carry[...]
        d0 = jnp.sum(jnp.where(is0, pos, 0.0), axis=0, keepdims=True)
        d1 = jnp.sum(jnp.where(is1, pos, 0.0), axis=0, keepdims=True)
        row = lax.broadcasted_iota(jnp.int32, (SUBLANES, tm), 0)
        dest_ref[...] = jnp.where(row == 0, d0, jnp.where(row == 1, d1, 0.0)).astype(jnp.int32)

    carry[...] = carry[...] + jnp.sum(member, axis=1, keepdims=True)


def route_slots(eidx, tm=512):
    n = eidx.shape[0]
    nt = n // tm
    return pl.pallas_call(
        functools.partial(_rank_kernel, tm=tm),
        grid=(2, nt),
        in_specs=[pl.BlockSpec((tm, ROUTE_W), lambda ph, i: (i, 0))],
        out_specs=[pl.BlockSpec((SUBLANES, tm), lambda ph, i: (0, i * ph)),
                   pl.BlockSpec((ROUTE_W, ROUTE_W), lambda ph, i: (0, 0))],
        out_shape=[jax.ShapeDtypeStruct((SUBLANES, n), jnp.int32),
                   jax.ShapeDtypeStruct((ROUTE_W, ROUTE_W), jnp.int32)],
        scratch_shapes=[pltpu.VMEM((ROUTE_W, 1), jnp.float32)],
        compiler_params=_cparams(2),
        name="route_slots",
    )(eidx)


def _slot_map_kernel(dest_ref, slot_ref, *, n, slots):
    def init(j, c):
        slot_ref[j] = 0
        return c

    lax.fori_loop(0, slots, init, 0, unroll=32)

    def body(t, c):
        slot_ref[dest_ref[t]] = t
        slot_ref[dest_ref[n + t]] = t
        return c

    lax.fori_loop(0, n, body, 0, unroll=16)


def slot_map(dest_flat, n, slots):
    return pl.pallas_call(
        functools.partial(_slot_map_kernel, n=n, slots=slots),
        in_specs=[pl.BlockSpec(memory_space=pltpu.SMEM)],
        out_specs=pl.BlockSpec(memory_space=pltpu.SMEM),
        out_shape=jax.ShapeDtypeStruct((slots,), jnp.int32),
        name="slot_map",
    )(dest_flat)


def _expert_kernel(be_ref, nused_ref, st_ref, x_hbm, wg_ref, wu_ref, wd_ref, y_ref, xs, land, sem):
    i = pl.program_id(0)
    n_used = nused_ref[0]

    def gather_block(blk):
        base = blk * MOE_BLK
        for j in range(MOE_BLK):
            src = pl.multiple_of(st_ref[base + j] * TOKEN_TILE, TOKEN_TILE)
            pltpu.make_async_copy(x_hbm.at[pl.ds(src, TOKEN_TILE)], land.at[pl.ds(j * TOKEN_TILE, TOKEN_TILE)],
                                  sem.at[0]).start(priority=j % DMA_QUEUES)

    def wait_block():
        pltpu.make_async_copy(x_hbm.at[pl.ds(0, MOE_BLK * TOKEN_TILE)], land, sem.at[0]).wait()

    @pl.when(i == 0)
    def _():
        gather_block(0)

    @pl.when(i < n_used)
    def _():
        wait_block()
        for c, chunk in enumerate(_load_token_tiles(land, 0, MOE_BLK)):
            xs[:, c * LANES:(c + 1) * LANES] = chunk.astype(jnp.bfloat16)

    @pl.when(i + 1 <= n_used)
    def _():
        gather_block(i + 1)

    @pl.when(n_used > i)
    def _():
        xb = xs[...]
        gate = jnp.dot(xb, wg_ref[0], preferred_element_type=jnp.float32)
        up = jnp.dot(xb, wu_ref[0], preferred_element_type=jnp.float32)
        hid = (jax.nn.silu(gate) * up).astype(jnp.bfloat16)
        _store_token_tiles(y_ref, jnp.dot(hid, wd_ref[0], preferred_element_type=jnp.float32))

    @pl.when(i == n_used - 1)
    def _():
        wait_block()

    @pl.when(i >= n_used)
    def _():
        y_ref[...] = jnp.zeros_like(y_ref)


def expert_blocks(block_e, n_used, slot_tok, x1t, wg, wu, wd):
    d = D_MODEL
    n_blocks = slot_tok.shape[0] // MOE_BLK - 1
    slots = n_blocks * MOE_BLK
    tile_rows = MOE_BLK * TOKEN_TILE
    grid_spec = pltpu.PrefetchScalarGridSpec(
        num_scalar_prefetch=3,
        grid=(n_blocks,),
        in_specs=[pl.BlockSpec(memory_space=pl.ANY),
                  pl.BlockSpec((1, d, D_EXPERT), lambda i, be, nu, st: (be[i], 0, 0)),
                  pl.BlockSpec((1, d, D_EXPERT), lambda i, be, nu, st: (be[i], 0, 0)),
                  pl.BlockSpec((1, D_EXPERT, d), lambda i, be, nu, st: (be[i], 0, 0))],
        out_specs=pl.BlockSpec((tile_rows, LANES), lambda i, be, nu, st: (i, 0)),
        scratch_shapes=[pltpu.VMEM((MOE_BLK, d), jnp.bfloat16),
                        pltpu.VMEM((tile_rows, LANES), jnp.float32),
                        pltpu.SemaphoreType.DMA((1,))],
    )
    return pl.pallas_call(
        _expert_kernel,
        grid_spec=grid_spec,
        out_shape=jax.ShapeDtypeStruct((slots * TOKEN_TILE, LANES), jnp.float32),
        compiler_params=_cparams(1),
        name="expert_blocks",
    )(block_e, n_used, slot_tok, x1t, wg, wu, wd)


def _combine_kernel(dest_ref, x_ref, gate_ref, y_hbm, g_ref, b_ref, xo_ref, xob_ref, fbuf, sem, *, n, tm):
    i = pl.program_id(0)
    nt = pl.num_programs(0)
    half = tm * TOKEN_TILE

    def gather_tile(tile, slot):
        base = tile * tm
        for k in range(TOP_K):
            for t in range(tm):
                src = pl.multiple_of(dest_ref[k * n + base + t] * TOKEN_TILE, TOKEN_TILE)
                pltpu.make_async_copy(y_hbm.at[pl.ds(src, TOKEN_TILE)],
                                      fbuf.at[slot, pl.ds((k * tm + t) * TOKEN_TILE, TOKEN_TILE)],
                                      sem.at[slot]).start(priority=t % DMA_QUEUES)

    def wait_tile(slot):
        pltpu.make_async_copy(y_hbm.at[pl.ds(0, TOP_K * half)], fbuf.at[slot], sem.at[slot]).wait()

    @pl.when(i == 0)
    def _():
        gather_tile(0, 0)

    slot = i % 2
    wait_tile(slot)
    g0, g1 = gate_ref[:, 0:1], gate_ref[:, 1:2]
    f0 = _load_token_tiles(fbuf.at[slot], 0, tm)
    f1 = _load_token_tiles(fbuf.at[slot], tm, tm)
    xs = _load_token_tiles(x_ref, 0, tm)
    gather_tile(jnp.minimum(i + 1, nt - 1), 1 - slot)
    z = [ALPHA * xs[c] + (g0 * f0[c] + g1 * f1[c]) for c in range(TOKEN_TILE)]
    mu = sum(jnp.sum(zc, axis=-1, keepdims=True) for zc in z) * (1.0 / D_MODEL)
    zc = [zz - mu for zz in z]
    var = sum(jnp.sum(v * v, axis=-1, keepdims=True) for v in zc) * (1.0 / D_MODEL)
    inv = lax.rsqrt(var + LN_EPS)
    for c in range(TOKEN_TILE):
        cols = slice(c * LANES, (c + 1) * LANES)
        x2 = zc[c] * inv * g_ref[:, cols] + b_ref[:, cols]
        xo_ref[:, cols] = x2
        xob_ref[:, cols] = x2.astype(jnp.bfloat16)

    @pl.when(i == nt - 1)
    def _():
        wait_tile(1 - slot)


def combine_ln2(dest_flat, x1t, gate, ybt, g, b, tm=256):
    n = gate.shape[0]
    d = D_MODEL
    tile_rows = tm * TOKEN_TILE
    grid_spec = pltpu.PrefetchScalarGridSpec(
        num_scalar_prefetch=1,
        grid=(n // tm,),
        in_specs=[pl.BlockSpec((tile_rows, LANES), lambda i, ds: (i, 0)),
                  pl.BlockSpec((tm, ROUTE_W), lambda i, ds: (i, 0)),
                  pl.BlockSpec(memory_space=pl.ANY),
                  pl.BlockSpec((1, d), lambda i, ds: (0, 0)),
                  pl.BlockSpec((1, d), lambda i, ds: (0, 0))],
        out_specs=[pl.BlockSpec((tm, d), lambda i, ds: (i, 0)),
                   pl.BlockSpec((tm, d), lambda i, ds: (i, 0))],
        scratch_shapes=[pltpu.VMEM((2, TOP_K * tile_rows, LANES), jnp.float32),
                        pltpu.SemaphoreType.DMA((2,))],
    )
    return pl.pallas_call(
        functools.partial(_combine_kernel, n=n, tm=tm),
        grid_spec=grid_spec,
        out_shape=[jax.ShapeDtypeStruct((n, d), jnp.float32),
                   jax.ShapeDtypeStruct((n, d), jnp.bfloat16)],
        compiler_params=_cparams(1),
        name="combine_ln2",
    )(dest_flat, x1t, gate, ybt, g, b)


def moe_layer(x1, eidx, gate, wg, wu, wd, ln_g, ln_b):
    n = eidx.shape[0]
    m = n * TOP_K
    n_blocks = m // MOE_BLK + N_EXPERTS
    slots = n_blocks * MOE_BLK
    dest8, cnt = route_slots(eidx)
    dest_flat = dest8[:TOP_K].reshape(m)
    counts = cnt[:N_EXPERTS, 0]
    pend = jnp.cumsum((counts + MOE_BLK - 1) // MOE_BLK * MOE_BLK)
    blk_start = jnp.arange(n_blocks, dtype=jnp.int32) * MOE_BLK
    block_e = jnp.minimum(jnp.sum((pend[None, :] <= blk_start[:, None]).astype(jnp.int32), axis=1),
                          N_EXPERTS - 1)
    n_used = (pend[-1:] // MOE_BLK).astype(jnp.int32)
    slot_tok = slot_map(dest_flat, n, slots + MOE_BLK)
    yb = expert_blocks(block_e, n_used, slot_tok, x1, wg, wu, wd)
    return combine_ln2(dest_flat, x1, gate, yb, ln_g, ln_b)


def _t5_bucket(rel):
    half = NUM_BUCKETS // 2
    max_exact = half // 2
    n = np.abs(rel)
    large = max_exact + (np.log(np.maximum(n, 1) / max_exact) / np.log(MAX_DISTANCE / max_exact)
                         * (half - max_exact)).astype(np.int32)
    large = np.minimum(large, half - 1)
    return np.where(rel > 0, half, 0) + np.where(n < max_exact, n, large)


def _bias_table(rel_bias, g, dil):
    rel = (np.arange(3 * RADIUS)[None, :] - RADIUS - np.arange(RADIUS)[:, None]) * dil
    hs = slice(g * HEADS_PER_GROUP, (g + 1) * HEADS_PER_GROUP)
    onehot = jnp.asarray(_t5_bucket(rel)[..., None] == np.arange(NUM_BUCKETS), jnp.float32)
    return jnp.einsum('qkb,bh->hqk', onehot, rel_bias[:, hs].astype(jnp.float32),
                      precision=lax.Precision.HIGHEST)


def _chunk_block_diag(w):
    per = RNN_CHUNK // RNN_BLOCK_W
    n_chunks = RNN_BLOCKS // per
    w4 = w.reshape(n_chunks, per, RNN_BLOCK_W, RNN_BLOCK_W)
    dense = jnp.einsum('chij,hk->chikj', w4, jnp.eye(per, dtype=w.dtype))
    return dense.reshape(n_chunks, RNN_CHUNK, RNN_CHUNK)


def _gate_weights(wa, wi):
    per_dir = [jnp.concatenate([_chunk_block_diag(wa[d]), _chunk_block_diag(wi[d])], axis=-1)
               for d in range(2)]
    return jnp.stack(per_dir, axis=1).astype(jnp.bfloat16)


def _per_chunk(v):
    return v.reshape(2, D_RNN // RNN_CHUNK, 1, RNN_CHUNK).transpose(1, 0, 2, 3)


def _split_hi_lo(w):
    hi = w.astype(jnp.bfloat16)
    lo = (w - hi.astype(jnp.float32)).astype(jnp.bfloat16)
    return jnp.concatenate([hi, lo], axis=1)


def _permute_in_cols(a):
    x_rnn, q, k, v, g_rnn, g_attn = jnp.split(
        a, [D_RNN, D_RNN + D_ATTN, D_RNN + 2 * D_ATTN, D_RNN + 3 * D_ATTN,
            D_RNN + 3 * D_ATTN + D_MODEL], axis=-1)
    return jnp.concatenate([x_rnn, g_rnn, g_attn, q, k, v], axis=-1)


def _trunk(x, biases, lp):
    b, s, d = x.shape
    n = b * s
    x2 = x.reshape(n, d)
    x2_bf = x2.astype(jnp.bfloat16)
    for l in range(DEPTH):
        p = lp[l]
        proj = in_projection(x2_bf, p['w_in'], p['b_in'])
        proj3 = proj.reshape(b, s, D_IN)
        h_rnn = rglru_branch(proj3, p['conv_w'], p['conv_b'], p['wg'], p['gb'], p['lam'])
        y_attn = dilated_attention(proj3, biases)
        x1, eidx, gate = mix_and_route(
            x2, proj, h_rnn.reshape(n, D_RNN), y_attn.reshape(n, D_ATTN_OUT),
            p['w_rnn_out'], p['w_attn_out'], p['w_o'],
            p['ln1_g'], p['ln1_b'], p['router_w'], p['router_b'])
        x2, x2_bf = moe_layer(x1, eidx, gate, p['w_gate'], p['w_up'], p['w_down'], p['ln2_g'], p['ln2_b'])
    return x2.reshape(b, s, d)


def kernel(x_prompt, x_sample, rel_bias, w_in, b_in, conv_w, conv_b, rg_wa, rg_ba, rg_wi, rg_bi, rg_lam, w_rnn_out, w_attn_out, w_o, ln1_g, ln1_b, router_w, router_b, expert_router_w, expert_router_b, w_gate, w_up, w_down, ln2_g, ln2_b):
    bf = jnp.bfloat16
    biases = [_bias_table(rel_bias, g, dil) for g, (_, dil) in enumerate(DILATED_CONFIGS)]
    lp = []
    for l in range(DEPTH):
        rw = jnp.concatenate(
            [router_w[l], jnp.transpose(expert_router_w[l], (1, 0, 2)).reshape(D_MODEL, N_EXPERTS)], axis=1)
        rb = jnp.concatenate([router_b[l], expert_router_b[l].reshape(N_EXPERTS)])
        pad = ROUTE_W - rw.shape[1]
        lp.append(dict(
            w_in=_permute_in_cols(w_in[l]).astype(bf),
            b_in=_permute_in_cols(b_in[l])[None, :],
            conv_w=conv_w[l], conv_b=conv_b[l][None, :],
            wg=_gate_weights(rg_wa[l], rg_wi[l]),
            gb=jnp.concatenate([_per_chunk(rg_ba[l]), _per_chunk(rg_bi[l])], axis=-1),
            lam=_per_chunk(rg_lam[l]),
            w_rnn_out=w_rnn_out[l].astype(bf), w_attn_out=w_attn_out[l].astype(bf), w_o=w_o[l].astype(bf),
            ln1_g=ln1_g[l][None, :], ln1_b=ln1_b[l][None, :],
            router_w=_split_hi_lo(jnp.pad(rw, ((0, 0), (0, pad)))), router_b=jnp.pad(rb, (0, pad))[None, :],
            w_gate=w_gate[l].astype(bf), w_up=w_up[l].astype(bf), w_down=w_down[l].astype(bf),
            ln2_g=ln2_g[l][None, :], ln2_b=ln2_b[l][None, :]))
    return (_trunk(x_prompt, biases, lp), _trunk(x_sample, biases, lp))
```

```python
import functools

import numpy as np
import jax
import jax.numpy as jnp
from jax import lax
from jax.experimental import pallas as pl
from jax.experimental.pallas import tpu as pltpu

D_MODEL = 1024
DEPTH = 2
D_RNN = D_MODEL
RNN_BLOCKS = 16
RNN_BLOCK_W = D_RNN // RNN_BLOCKS
CONV_WIDTH = 4
RG_C = 8.0
DILATED_CONFIGS = ((128, 1), (512, 4), (2048, 16))
N_ATTN_GROUPS = len(DILATED_CONFIGS)
HEADS_PER_GROUP = 4
N_ATTN_HEADS = N_ATTN_GROUPS * HEADS_PER_GROUP
HEAD_DIM = 128
D_ATTN = N_ATTN_HEADS * HEAD_DIM
D_ATTN_OUT = HEADS_PER_GROUP * HEAD_DIM
NUM_BUCKETS = 32
MAX_DISTANCE = max(w for w, _ in DILATED_CONFIGS) // 2
NEG_INF = -1e30
D_IN = D_RNN + 3 * D_ATTN + 2 * D_MODEL
N_EXPERT_GROUPS = 4
EXPERTS_PER_GROUP = 8
N_EXPERTS = N_EXPERT_GROUPS * EXPERTS_PER_GROUP
TOP_K = 2
D_EXPERT = D_MODEL // 2
ALPHA = (2 * DEPTH) ** 0.25
LN_EPS = 1e-5

LANES = 128
SUBLANES = 8
VMEM_LIMIT = 56 * 1024 * 1024

COL_XRNN = 0
COL_Q = D_RNN
COL_K = COL_Q + D_ATTN
COL_V = COL_K + D_ATTN
D_MAIN = D_RNN + 3 * D_ATTN
COL_GRNN = 0
COL_GATTN = D_MODEL
D_GATES = 2 * D_MODEL

RADIUS = 64
assert all(w // (2 * d) == RADIUS for w, d in DILATED_CONFIGS)
ATTN_ROWS = 1024
RNN_CHUNK = 256
RNN_TILE = 256
MOE_BLK = 256
DMA_QUEUES = 2
TOKEN_TILE = D_MODEL // LANES
assert TOKEN_TILE == SUBLANES
ROUTE_W = LANES


def _cparams(n_axes):
    return pltpu.CompilerParams(dimension_semantics=("arbitrary",) * n_axes,
                                vmem_limit_bytes=VMEM_LIMIT)


def _proj_kernel(x_ref, w_ref, b_ref, o_ref):
    acc = jnp.dot(x_ref[...], w_ref[...], preferred_element_type=jnp.float32)
    o_ref[...] = (acc + b_ref[...]).astype(o_ref.dtype)


def in_projection(x_bf, w_bf, b, out_dtype, tn, tm=512):
    n, k = x_bf.shape
    nout = w_bf.shape[1]
    assert nout % tn == 0 and tn % LANES == 0
    return pl.pallas_call(
        _proj_kernel,
        grid=(nout // tn, n // tm),
        in_specs=[pl.BlockSpec((tm, k), lambda j, i: (i, 0)),
                  pl.BlockSpec((k, tn), lambda j, i: (0, j)),
                  pl.BlockSpec((1, tn), lambda j, i: (0, j))],
        out_specs=pl.BlockSpec((tm, tn), lambda j, i: (i, j)),
        out_shape=jax.ShapeDtypeStruct((n, nout), out_dtype),
        compiler_params=_cparams(2),
        name="in_projection",
    )(x_bf, w_bf, b)


def _rglru_kernel(x_ref, cw_ref, cb_ref, wg_ref, gb_ref, lam_ref, o_ref,
                  hsum, xc_s, a_s, u_s, h_s, *, seq, tile):
    c = RNN_CHUNK
    n_tiles = seq // tile
    n_grp = tile // SUBLANES
    big_rows = tile + 2 * SUBLANES
    row_in_grp = lax.broadcasted_iota(jnp.int32, (n_grp, SUBLANES, c), 1)

    def conv_tile(t0):
        centre = x_ref[0, pl.ds(t0, tile), :]
        p0 = pl.multiple_of(jnp.maximum(t0 - SUBLANES, 0), SUBLANES)
        n0 = pl.multiple_of(jnp.minimum(t0 + tile, seq - SUBLANES), SUBLANES)
        prev = jnp.where(t0 > 0, x_ref[0, pl.ds(p0, SUBLANES), :], 0.0)
        nxt = jnp.where(t0 + tile < seq, x_ref[0, pl.ds(n0, SUBLANES), :], 0.0)
        big = jnp.concatenate([prev, centre, nxt], axis=0)
        xc = cb_ref[...] + cw_ref[1:2, :] * centre
        for k in (0, 2, 3):
            off = k - 1
            tap = pltpu.roll(big, (-off) % big_rows, 0)[SUBLANES:SUBLANES + tile]
            xc = xc + cw_ref[k:k + 1, :] * tap
        return xc

    def gates(xc, dr):
        g = jnp.dot(xc.astype(jnp.bfloat16), wg_ref[0, dr],
                    preferred_element_type=jnp.float32) + gb_ref[0, dr]
        gate_r = jax.nn.sigmoid(g[:, :c])
        gate_i = jax.nn.sigmoid(g[:, c:])
        neg_lam = -lam_ref[0, dr]
        softplus = jnp.maximum(neg_lam, 0.0) + jnp.log1p(jnp.exp(-jnp.abs(neg_lam)))
        log_a = -RG_C * gate_r * softplus
        a = jnp.exp(log_a)
        th = jnp.tanh(log_a)
        one_minus_a2 = -2.0 * th / (1.0 - th)
        u = jnp.sqrt(one_minus_a2) * (gate_i * xc)
        return a, u

    def tile_prefix(a, u, reverse):
        a = a.reshape(n_grp, SUBLANES, c)
        u = u.reshape(n_grp, SUBLANES, c)
        for s in (1, 2, 4):
            if reverse:
                a_sh = pltpu.roll(a, SUBLANES - s, 1)
                u_sh = pltpu.roll(u, SUBLANES - s, 1)
                m = row_in_grp < SUBLANES - s
            else:
                a_sh = pltpu.roll(a, s, 1)
                u_sh = pltpu.roll(u, s, 1)
                m = row_in_grp >= s
            u = jnp.where(m, u + a * u_sh, u)
            a = jnp.where(m, a * a_sh, a)
        a_s[...] = a.reshape(tile, c)
        u_s[...] = u.reshape(tile, c)

    def run_direction(dr, reverse):
        def tile_body(ti, h):
            t_idx = (n_tiles - 1 - ti) if reverse else ti
            t0 = pl.multiple_of(t_idx * tile, tile)
            if reverse:
                xc = xc_s[pl.ds(t0, tile), :]
            else:
                xc = conv_tile(t0)
                xc_s[pl.ds(t0, tile), :] = xc
            a, u = gates(xc, dr)
            tile_prefix(a, u, reverse)

            def grp_body(gi, hc):
                g_idx = (n_grp - 1 - gi) if reverse else gi
                r0 = pl.multiple_of(g_idx * SUBLANES, SUBLANES)
                hg = u_s[pl.ds(r0, SUBLANES), :] + a_s[pl.ds(r0, SUBLANES), :] * hc
                h_s[pl.ds(r0, SUBLANES), :] = hg
                edge = hg[0:1, :] if reverse else hg[SUBLANES - 1:SUBLANES, :]
                return jnp.broadcast_to(edge, (SUBLANES, c))

            h = lax.fori_loop(0, n_grp, grp_body, h, unroll=4)
            if reverse:
                o_ref[0, pl.ds(t0, tile), :] = (hsum[pl.ds(t0, tile), :] + h_s[...]).astype(o_ref.dtype)
            else:
                hsum[pl.ds(t0, tile), :] = h_s[...]
            return h

        lax.fori_loop(0, n_tiles, tile_body, jnp.zeros((SUBLANES, c), jnp.float32))

    run_direction(0, False)
    run_direction(1, True)


def rglru_branch(proj3, cw, cb, wg, gb, lam):
    b, s, _ = proj3.shape
    c = RNN_CHUNK
    n_chunks = D_RNN // c
    kern = functools.partial(_rglru_kernel, seq=s, tile=RNN_TILE)
    return pl.pallas_call(
        kern,
        grid=(b, n_chunks),
        in_specs=[pl.BlockSpec((1, s, c), lambda bi, ci: (bi, 0, COL_XRNN // c + ci)),
                  pl.BlockSpec((CONV_WIDTH, c), lambda bi, ci: (0, ci)),
                  pl.BlockSpec((1, c), lambda bi, ci: (0, ci)),
                  pl.BlockSpec((1, 2, c, 2 * c), lambda bi, ci: (ci, 0, 0, 0)),
                  pl.BlockSpec((1, 2, 1, 2 * c), lambda bi, ci: (ci, 0, 0, 0)),
                  pl.BlockSpec((1, 2, 1, c), lambda bi, ci: (ci, 0, 0, 0))],
        out_specs=pl.BlockSpec((1, s, c), lambda bi, ci: (bi, 0, ci)),
        out_shape=jax.ShapeDtypeStruct((b, s, D_RNN), jnp.bfloat16),
        scratch_shapes=[pltpu.VMEM((s, c), jnp.float32),
                        pltpu.VMEM((s, c), jnp.float32),
                        pltpu.VMEM((RNN_TILE, c), jnp.float32),
                        pltpu.VMEM((RNN_TILE, c), jnp.float32),
                        pltpu.VMEM((RNN_TILE, c), jnp.float32)],
        compiler_params=_cparams(2),
        name="rglru_branch",
    )(proj3, cw, cb, wg, gb, lam)


def _attn_group(refs, bias, o_dst, l_dst, *, dil, is_first, is_last):
    q_ref, kp_ref, kc_ref, kn_ref, vp_ref, vc_ref, vn_ref = refs
    rows = ATTN_ROWS
    per_res = rows // dil
    n_q = per_res // RADIUS
    win = 3 * RADIUS
    nb = dil * n_q

    def fold(ref, r, n):
        if dil == 1:
            return ref[0, pl.ds(r, n), :]
        return ref[0, pl.ds(r, n, stride=dil), :]

    qs, ks, vs = [], [], []
    for r in range(dil):
        qf = fold(q_ref, r, per_res).astype(jnp.bfloat16)
        kf = jnp.concatenate([fold(kp_ref, r, RADIUS), fold(kc_ref, r, per_res),
                              fold(kn_ref, r, RADIUS)], axis=0).astype(jnp.bfloat16)
        vf = jnp.concatenate([fold(vp_ref, r, RADIUS), fold(vc_ref, r, per_res),
                              fold(vn_ref, r, RADIUS)], axis=0).astype(jnp.bfloat16)
        for jq in range(n_q):
            qs.append(qf[jq * RADIUS:(jq + 1) * RADIUS])
            ks.append(kf[jq * RADIUS:jq * RADIUS + win])
            vs.append(vf[jq * RADIUS:jq * RADIUS + win])
    qb, kb, vb = jnp.stack(qs), jnp.stack(ks), jnp.stack(vs)

    shape = (nb, RADIUS, win)
    jq_of = lax.broadcasted_iota(jnp.int32, shape, 0) & (n_q - 1)
    qi = lax.broadcasted_iota(jnp.int32, shape, 1)
    kj = lax.broadcasted_iota(jnp.int32, shape, 2)
    lo = jnp.where(jq_of == 0, jnp.where(is_first, RADIUS, 0), 0)
    hi = jnp.where(jq_of == n_q - 1, jnp.where(is_last, 2 * RADIUS, win), win)
    mask = (jnp.abs(kj - RADIUS - qi) <= RADIUS) & (kj >= lo) & (kj < hi)

    s = jnp.einsum('bqd,bkd->bqk', qb, kb, preferred_element_type=jnp.float32) * (HEAD_DIM ** -0.5)
    s = jnp.where(mask, s + bias[None], NEG_INF)
    m = jnp.max(s, axis=-1, keepdims=True)
    p = jnp.exp(s - m)
    l = jnp.sum(p, axis=-1, keepdims=True)
    o = jnp.einsum('bqk,bkd->bqd', p.astype(jnp.bfloat16), vb, preferred_element_type=jnp.float32) / l
    lse = m + jnp.log(l)
    for bi in range(nb):
        r, jq = divmod(bi, n_q)
        start = jq * RADIUS * dil + r
        idx = pl.ds(start, RADIUS) if dil == 1 else pl.ds(start, RADIUS, stride=dil)
        o_dst[idx, :] = o[bi]
        l_dst[idx, :] = jnp.broadcast_to(lse[bi], (RADIUS, HEAD_DIM))


def _attn_kernel(*refs, n_row_blocks):
    n_g = N_ATTN_GROUPS
    in_refs = refs[:7 * n_g]
    bias_refs = refs[7 * n_g:8 * n_g]
    y_ref = refs[8 * n_g]
    o_s, l_s = refs[8 * n_g + 1:]
    i = pl.program_id(1)
    for g, (_, dil) in enumerate(DILATED_CONFIGS):
        _attn_group(in_refs[7 * g:7 * g + 7], bias_refs[g][0], o_s.at[g], l_s.at[g],
                    dil=dil, is_first=i == 0, is_last=i == n_row_blocks - 1)
    l0, l1, l2 = l_s[0], l_s[1], l_s[2]
    m = jnp.maximum(jnp.maximum(l0, l1), l2)
    e0, e1, e2 = jnp.exp(l0 - m), jnp.exp(l1 - m), jnp.exp(l2 - m)
    y = (e0 * o_s[0] + e1 * o_s[1] + e2 * o_s[2]) / (e0 + e1 + e2)
    y_ref[0] = y.astype(y_ref.dtype)


def dilated_attention(proj3, biases):
    b, s, _ = proj3.shape
    rows = ATTN_ROWS
    nrb = s // rows
    w = HEAD_DIM
    nh = HEADS_PER_GROUP

    def cur_map(col):
        return lambda bi, i, h: (bi, i, col + h)

    in_specs, operands = [], []
    for g, (_, dil) in enumerate(DILATED_CONFIGS):
        halo = RADIUS * dil
        ratio = rows // halo
        n_halo_blocks = s // halo
        qc, kc, vc = COL_Q // w + g * nh, COL_K // w + g * nh, COL_V // w + g * nh

        def prev_map(col, ratio=ratio):
            return lambda bi, i, h: (bi, jnp.maximum(i * ratio - 1, 0), col + h)

        def next_map(col, ratio=ratio, last=n_halo_blocks - 1):
            return lambda bi, i, h: (bi, jnp.minimum((i + 1) * ratio, last), col + h)

        in_specs += [pl.BlockSpec((1, rows, w), cur_map(qc)),
                     pl.BlockSpec((1, halo, w), prev_map(kc)),
                     pl.BlockSpec((1, rows, w), cur_map(kc)),
                     pl.BlockSpec((1, halo, w), next_map(kc)),
                     pl.BlockSpec((1, halo, w), prev_map(vc)),
                     pl.BlockSpec((1, rows, w), cur_map(vc)),
                     pl.BlockSpec((1, halo, w), next_map(vc))]
        operands += [proj3] * 7
    in_specs += [pl.BlockSpec((1, RADIUS, 3 * RADIUS), lambda bi, i, h: (h, 0, 0))] * N_ATTN_GROUPS
    operands += list(biases)
    return pl.pallas_call(
        functools.partial(_attn_kernel, n_row_blocks=nrb),
        grid=(b, nrb, nh),
        in_specs=in_specs,
        out_specs=pl.BlockSpec((1, rows, w), cur_map(0)),
        out_shape=jax.ShapeDtypeStruct((b, s, D_ATTN_OUT), jnp.bfloat16),
        scratch_shapes=[pltpu.VMEM((N_ATTN_GROUPS, rows, w), jnp.float32),
                        pltpu.VMEM((N_ATTN_GROUPS, rows, w), jnp.float32)],
        compiler_params=_cparams(3),
        name="dilated_attention",
    )(*operands)


def _store_token_tiles(ref, val, first_chunk=0):
    rows = val.shape[0]
    for c in range(val.shape[1] // LANES):
        ref[pl.ds(first_chunk + c, rows, stride=TOKEN_TILE), :] = val[:, c * LANES:(c + 1) * LANES]


def _load_token_tiles(ref, start, rows):
    return [ref[pl.ds(start * TOKEN_TILE + c, rows, stride=TOKEN_TILE), :] for c in range(TOKEN_TILE)]


def _layer_norm(z, g, b):
    mu = jnp.mean(z, axis=-1, keepdims=True)
    zc = z - mu
    var = jnp.mean(zc * zc, axis=-1, keepdims=True)
    return zc * lax.rsqrt(var + LN_EPS) * g + b


def _mix_kernel(x_ref, grnn_ref, gattn_ref, h_ref, y_ref,
                wr_ref, wa_ref, wo_ref, lng_ref, lnb_ref, rw_ref, rb_ref,
                xo_ref, eidx_ref, gate_ref):
    rnn = jnp.dot(h_ref[...], wr_ref[...], preferred_element_type=jnp.float32)
    att = jnp.dot(y_ref[...], wa_ref[...], preferred_element_type=jnp.float32)
    mixed = (jax.nn.sigmoid(grnn_ref[...].astype(jnp.float32)) * rnn
             + jax.nn.sigmoid(gattn_ref[...].astype(jnp.float32)) * att)
    z = ALPHA * x_ref[...] + jnp.dot(mixed.astype(jnp.bfloat16), wo_ref[...],
                                     preferred_element_type=jnp.float32)
    x1 = _layer_norm(z, lng_ref[...], lnb_ref[...])
    _store_token_tiles(xo_ref, x1)

    x_hi = x1.astype(jnp.bfloat16)
    x_lo = (x1 - x_hi.astype(jnp.float32)).astype(jnp.bfloat16)
    t = jnp.dot(x_hi, rw_ref[...], preferred_element_type=jnp.float32)
    logits = (t[:, :ROUTE_W] + t[:, ROUTE_W:]
              + jnp.dot(x_lo, rw_ref[:, :ROUTE_W], preferred_element_type=jnp.float32)) + rb_ref[...]
    tm = logits.shape[0]
    lane = lax.broadcasted_iota(jnp.int32, (tm, ROUTE_W), 1)
    lane_f = lane.astype(jnp.float32)
    far = float(ROUTE_W)
    is_group = lane < N_EXPERT_GROUPS
    gl = jnp.where(is_group, logits, -jnp.inf)
    gmax = jnp.max(gl, axis=-1, keepdims=True)
    gsel = jnp.min(jnp.where(gl == gmax, lane_f, far), axis=-1, keepdims=True)
    p_group = 1.0 / jnp.sum(jnp.where(is_group, jnp.exp(logits - gmax), 0.0), axis=-1, keepdims=True)
    lane_group = ((lane - N_EXPERT_GROUPS) >> 3).astype(jnp.float32)
    in_group = (lane >= N_EXPERT_GROUPS) & (lane < N_EXPERT_GROUPS + N_EXPERTS) & (lane_group == gsel)
    el = jnp.where(in_group, logits, -jnp.inf)
    m1 = jnp.max(el, axis=-1, keepdims=True)
    i1 = jnp.min(jnp.where(el == m1, lane_f, far), axis=-1, keepdims=True)
    el2 = jnp.where(lane_f == i1, -jnp.inf, el)
    m2 = jnp.max(el2, axis=-1, keepdims=True)
    i2 = jnp.min(jnp.where(el2 == m2, lane_f, far), axis=-1, keepdims=True)
    e21 = jnp.exp(m2 - m1)
    den = 1.0 + e21
    g1 = p_group * (1.0 / den)
    g2 = p_group * (e21 / den)
    eidx = jnp.where(lane == 0, i1, jnp.where(lane == 1, i2, float(N_EXPERT_GROUPS))) - float(N_EXPERT_GROUPS)
    eidx_ref[...] = eidx.astype(jnp.int32)
    gate_ref[...] = jnp.where(lane == 0, g1, jnp.where(lane == 1, g2, 0.0))


def mix_and_route(x, gates, h_rnn, y_attn, wr, wa, wo, lng, lnb, rw, rb, tm=512):
    n = x.shape[0]
    d = D_MODEL
    w = D_ATTN_OUT
    row = lambda width, col=0: pl.BlockSpec((tm, width), lambda i, c=col: (i, c))
    full = lambda a: pl.BlockSpec(a.shape, lambda i: (0,) * a.ndim)
    return pl.pallas_call(
        _mix_kernel,
        grid=(n // tm,),
        in_specs=[row(d), row(d, COL_GRNN // d), row(d, COL_GATTN // d), row(d), row(w),
                  full(wr), full(wa), full(wo), full(lng), full(lnb), full(rw), full(rb)],
        out_specs=[pl.BlockSpec((tm * TOKEN_TILE, LANES), lambda i: (i, 0)), row(ROUTE_W), row(ROUTE_W)],
        out_shape=[jax.ShapeDtypeStruct((n * TOKEN_TILE, LANES), jnp.float32),
                   jax.ShapeDtypeStruct((n, ROUTE_W), jnp.int32),
                   jax.ShapeDtypeStruct((n, ROUTE_W), jnp.float32)],
        compiler_params=_cparams(1),
        name="mix_and_route",
    )(x, gates, gates, h_rnn, y_attn, wr, wa, wo, lng, lnb, rw, rb)


def _rank_kernel(e_ref, dest_ref, cnt_ref, carry, *, tm):
    ph = pl.program_id(0)
    i = pl.program_id(1)
    lanes = ROUTE_W

    @pl.when((ph == 0) & (i == 0))
    def _():
        carry[...] = jnp.zeros_like(carry)

    @pl.when((ph == 1) & (i == 0))
    def _():
        cnt = carry[...]
        cnt_ref[...] = jnp.broadcast_to(cnt, (lanes, lanes)).astype(jnp.int32)
        padded = jnp.floor((cnt + (MOE_BLK - 1)) * (1.0 / MOE_BLK)) * MOE_BLK
        r = lax.broadcasted_iota(jnp.int32, (lanes, lanes), 0)
        c = lax.broadcasted_iota(jnp.int32, (lanes, lanes), 1)
        lower = (c < r).astype(jnp.float32)
        pstart = jnp.dot(lower, jnp.broadcast_to(padded, (lanes, lanes)),
                         preferred_element_type=jnp.float32, precision=lax.Precision.HIGHEST)
        carry[...] = pstart[:, 0:1]

    r8 = lax.broadcasted_iota(jnp.int32, (SUBLANES, lanes), 0)
    c8 = lax.broadcasted_iota(jnp.int32, (SUBLANES, lanes), 1)
    pick = ((r8 == c8) & (r8 < TOP_K)).astype(jnp.bfloat16)
    ef = e_ref[...].astype(jnp.float32).astype(jnp.bfloat16)
    et = lax.dot_general(pick, ef, (((1,), (1,)), ((), ())), preferred_element_type=jnp.float32)
    e0, e1 = et[0:1, :], et[1:2, :]
    sub = lax.broadcasted_iota(jnp.int32, (lanes, tm), 0).astype(jnp.float32)
    is0, is1 = sub == e0, sub == e1
    member = jnp.where(is0 | is1, 1.0, 0.0)

    @pl.when(ph == 1)
    def _():
        s_i = lax.broadcasted_iota(jnp.int32, (tm, tm), 0)
        t_i = lax.broadcasted_iota(jnp.int32, (tm, tm), 1)
        earlier = (s_i < t_i).astype(jnp.bfloat16)
        before = jnp.dot(member.astype(jnp.bfloat16), earlier, preferred_element_type=jnp.float32)
        pos = before + carry[...]
        d0 = jnp.sum(jnp.where(is0, pos, 0.0), axis=0, keepdims=True)
        d1 = jnp.sum(jnp.where(is1, pos, 0.0), axis=0, keepdims=True)
        row = lax.broadcasted_iota(jnp.int32, (SUBLANES, tm), 0)
        dest_ref[...] = jnp.where(row == 0, d0, jnp.where(row == 1, d1, 0.0)).astype(jnp.int32)

    carry[...] = carry[...] + jnp.sum(member, axis=1, keepdims=True)


def route_slots(eidx, tm=512):
    n = eidx.shape[0]
    nt = n // tm
    return pl.pallas_call(
        functools.partial(_rank_kernel, tm=tm),
        grid=(2, nt),
        in_specs=[pl.BlockSpec((tm, ROUTE_W), lambda ph, i: (i, 0))],
        out_specs=[pl.BlockSpec((SUBLANES, tm), lambda ph, i: (0, i * ph)),
                   pl.BlockSpec((ROUTE_W, ROUTE_W), lambda ph, i: (0, 0))],
        out_shape=[jax.ShapeDtypeStruct((SUBLANES, n), jnp.int32),
                   jax.ShapeDtypeStruct((ROUTE_W, ROUTE_W), jnp.int32)],
        scratch_shapes=[pltpu.VMEM((ROUTE_W, 1), jnp.float32)],
        compiler_params=_cparams(2),
        name="route_slots",
    )(eidx)


def _slot_map_kernel(dest_ref, slot_ref, *, n, slots):
    def init(j, c):
        slot_ref[j] = 0
        return c

    lax.fori_loop(0, slots, init, 0, unroll=32)

    def body(t, c):
        slot_ref[dest_ref[t]] = t
        slot_ref[dest_ref[n + t]] = t
        return c

    lax.fori_loop(0, n, body, 0, unroll=16)


def slot_map(dest_flat, n, slots):
    return pl.pallas_call(
        functools.partial(_slot_map_kernel, n=n, slots=slots),
        in_specs=[pl.BlockSpec(memory_space=pltpu.SMEM)],
        out_specs=pl.BlockSpec(memory_space=pltpu.SMEM),
        out_shape=jax.ShapeDtypeStruct((slots,), jnp.int32),
        name="slot_map",
    )(dest_flat)


def _expert_kernel(be_ref, nused_ref, st_ref, x_hbm, wg_ref, wu_ref, wd_ref, y_ref, xs, land, sem):
    i = pl.program_id(0)
    n_used = nused_ref[0]

    def gather_block(blk):
        base = blk * MOE_BLK
        for j in range(MOE_BLK):
            src = pl.multiple_of(st_ref[base + j] * TOKEN_TILE, TOKEN_TILE)
            pltpu.make_async_copy(x_hbm.at[pl.ds(src, TOKEN_TILE)], land.at[pl.ds(j * TOKEN_TILE, TOKEN_TILE)],
                                  sem.at[0]).start(priority=j % DMA_QUEUES)

    def wait_block():
        pltpu.make_async_copy(x_hbm.at[pl.ds(0, MOE_BLK * TOKEN_TILE)], land, sem.at[0]).wait()

    @pl.when(i == 0)
    def _():
        gather_block(0)

    @pl.when(i < n_used)
    def _():
        wait_block()
        for c, chunk in enumerate(_load_token_tiles(land, 0, MOE_BLK)):
            xs[:, c * LANES:(c + 1) * LANES] = chunk.astype(jnp.bfloat16)

    @pl.when(i + 1 <= n_used)
    def _():
        gather_block(i + 1)

    @pl.when(n_used > i)
    def _():
        xb = xs[...]
        bf = jnp.bfloat16
        gate = jnp.dot(xb, wg_ref[0].astype(bf), preferred_element_type=jnp.float32)
        up = jnp.dot(xb, wu_ref[0].astype(bf), preferred_element_type=jnp.float32)
        hid = (jax.nn.silu(gate) * up).astype(bf)
        _store_token_tiles(y_ref, jnp.dot(hid, wd_ref[0].astype(bf), preferred_element_type=jnp.float32))

    @pl.when(i == n_used - 1)
    def _():
        wait_block()

    @pl.when(i >= n_used)
    def _():
        y_ref[...] = jnp.zeros_like(y_ref)


def expert_blocks(block_e, n_used, slot_tok, x1t, wg, wu, wd):
    d = D_MODEL
    n_blocks = slot_tok.shape[0] // MOE_BLK - 1
    slots = n_blocks * MOE_BLK
    tile_rows = MOE_BLK * TOKEN_TILE
    grid_spec = pltpu.PrefetchScalarGridSpec(
        num_scalar_prefetch=3,
        grid=(n_blocks,),
        in_specs=[pl.BlockSpec(memory_space=pl.ANY),
                  pl.BlockSpec((1, d, D_EXPERT), lambda i, be, nu, st: (be[i], 0, 0)),
                  pl.BlockSpec((1, d, D_EXPERT), lambda i, be, nu, st: (be[i], 0, 0)),
                  pl.BlockSpec((1, D_EXPERT, d), lambda i, be, nu, st: (be[i], 0, 0))],
        out_specs=pl.BlockSpec((tile_rows, LANES), lambda i, be, nu, st: (i, 0)),
        scratch_shapes=[pltpu.VMEM((MOE_BLK, d), jnp.bfloat16),
                        pltpu.VMEM((tile_rows, LANES), jnp.float32),
                        pltpu.SemaphoreType.DMA((1,))],
    )
    return pl.pallas_call(
        _expert_kernel,
        grid_spec=grid_spec,
        out_shape=jax.ShapeDtypeStruct((slots * TOKEN_TILE, LANES), jnp.float32),
        compiler_params=_cparams(1),
        name="expert_blocks",
    )(block_e, n_used, slot_tok, x1t, wg, wu, wd)


def _combine_kernel(dest_ref, x_ref, gate_ref, y_hbm, g_ref, b_ref, xo_ref, xob_ref, fbuf, sem, *, n, tm):
    i = pl.program_id(0)
    nt = pl.num_programs(0)
    half = tm * TOKEN_TILE

    def gather_tile(tile, slot):
        base = tile * tm
        for k in range(TOP_K):
            for t in range(tm):
                src = pl.multiple_of(dest_ref[k * n + base + t] * TOKEN_TILE, TOKEN_TILE)
                pltpu.make_async_copy(y_hbm.at[pl.ds(src, TOKEN_TILE)],
                                      fbuf.at[slot, pl.ds((k * tm + t) * TOKEN_TILE, TOKEN_TILE)],
                                      sem.at[slot]).start(priority=t % DMA_QUEUES)

    def wait_tile(slot):
        pltpu.make_async_copy(y_hbm.at[pl.ds(0, TOP_K * half)], fbuf.at[slot], sem.at[slot]).wait()

    @pl.when(i == 0)
    def _():
        gather_tile(0, 0)

    slot = i % 2
    wait_tile(slot)
    g0, g1 = gate_ref[:, 0:1], gate_ref[:, 1:2]
    f0 = _load_token_tiles(fbuf.at[slot], 0, tm)
    f1 = _load_token_tiles(fbuf.at[slot], tm, tm)
    xs = _load_token_tiles(x_ref, 0, tm)
    gather_tile(jnp.minimum(i + 1, nt - 1), 1 - slot)
    z = [ALPHA * xs[c] + (g0 * f0[c] + g1 * f1[c]) for c in range(TOKEN_TILE)]
    mu = sum(jnp.sum(zc, axis=-1, keepdims=True) for zc in z) * (1.0 / D_MODEL)
    zc = [zz - mu for zz in z]
    var = sum(jnp.sum(v * v, axis=-1, keepdims=True) for v in zc) * (1.0 / D_MODEL)
    inv = lax.rsqrt(var + LN_EPS)
    for c in range(TOKEN_TILE):
        cols = slice(c * LANES, (c + 1) * LANES)
        x2 = zc[c] * inv * g_ref[:, cols] + b_ref[:, cols]
        xo_ref[:, cols] = x2
        xob_ref[:, cols] = x2.astype(jnp.bfloat16)

    @pl.when(i == nt - 1)
    def _():
        wait_tile(1 - slot)


def combine_ln2(dest_flat, x1t, gate, ybt, g, b, tm=256):
    n = gate.shape[0]
    d = D_MODEL
    tile_rows = tm * TOKEN_TILE
    grid_spec = pltpu.PrefetchScalarGridSpec(
        num_scalar_prefetch=1,
        grid=(n // tm,),
        in_specs=[pl.BlockSpec((tile_rows, LANES), lambda i, ds: (i, 0)),
                  pl.BlockSpec((tm, ROUTE_W), lambda i, ds: (i, 0)),
                  pl.BlockSpec(memory_space=pl.ANY),
                  pl.BlockSpec((1, d), lambda i, ds: (0, 0)),
                  pl.BlockSpec((1, d), lambda i, ds: (0, 0))],
        out_specs=[pl.BlockSpec((tm, d), lambda i, ds: (i, 0)),
                   pl.BlockSpec((tm, d), lambda i, ds: (i, 0))],
        scratch_shapes=[pltpu.VMEM((2, TOP_K * tile_rows, LANES), jnp.float32),
                        pltpu.SemaphoreType.DMA((2,))],
    )
    return pl.pallas_call(
        functools.partial(_combine_kernel, n=n, tm=tm),
        grid_spec=grid_spec,
        out_shape=[jax.ShapeDtypeStruct((n, d), jnp.float32),
                   jax.ShapeDtypeStruct((n, d), jnp.bfloat16)],
        compiler_params=_cparams(1),
        name="combine_ln2",
    )(dest_flat, x1t, gate, ybt, g, b)


def moe_layer(x1, eidx, gate, wg, wu, wd, ln_g, ln_b):
    n = eidx.shape[0]
    m = n * TOP_K
    n_blocks = m // MOE_BLK + N_EXPERTS
    slots = n_blocks * MOE_BLK
    dest8, cnt = route_slots(eidx)
    dest_flat = dest8[:TOP_K].reshape(m)
    counts = cnt[:N_EXPERTS, 0]
    pend = jnp.cumsum((counts + MOE_BLK - 1) // MOE_BLK * MOE_BLK)
    blk_start = jnp.arange(n_blocks, dtype=jnp.int32) * MOE_BLK
    block_e = jnp.minimum(jnp.sum((pend[None, :] <= blk_start[:, None]).astype(jnp.int32), axis=1),
                          N_EXPERTS - 1)
    n_used = (pend[-1:] // MOE_BLK).astype(jnp.int32)
    slot_tok = slot_map(dest_flat, n, slots + MOE_BLK)
    yb = expert_blocks(block_e, n_used, slot_tok, x1, wg, wu, wd)
    return combine_ln2(dest_flat, x1, gate, yb, ln_g, ln_b)


def _t5_bucket(rel):
    half = NUM_BUCKETS // 2
    max_exact = half // 2
    n = np.abs(rel)
    large = max_exact + (np.log(np.maximum(n, 1) / max_exact) / np.log(MAX_DISTANCE / max_exact)
                         * (half - max_exact)).astype(np.int32)
    large = np.minimum(large, half - 1)
    return np.where(rel > 0, half, 0) + np.where(n < max_exact, n, large)


def _bias_table(rel_bias, g, dil):
    rel = (np.arange(3 * RADIUS)[None, :] - RADIUS - np.arange(RADIUS)[:, None]) * dil
    hs = slice(g * HEADS_PER_GROUP, (g + 1) * HEADS_PER_GROUP)
    onehot = jnp.asarray(_t5_bucket(rel)[..., None] == np.arange(NUM_BUCKETS), jnp.float32)
    return jnp.einsum('qkb,bh->hqk', onehot, rel_bias[:, hs].astype(jnp.float32),
                      precision=lax.Precision.HIGHEST)


def _chunk_block_diag(w):
    per = RNN_CHUNK // RNN_BLOCK_W
    n_chunks = RNN_BLOCKS // per
    w4 = w.reshape(n_chunks, per, RNN_BLOCK_W, RNN_BLOCK_W)
    dense = jnp.einsum('chij,hk->chikj', w4, jnp.eye(per, dtype=w.dtype))
    return dense.reshape(n_chunks, RNN_CHUNK, RNN_CHUNK)


def _gate_weights(wa, wi):
    per_dir = [jnp.concatenate([_chunk_block_diag(wa[d]), _chunk_block_diag(wi[d])], axis=-1)
               for d in range(2)]
    return jnp.stack(per_dir, axis=1).astype(jnp.bfloat16)


def _per_chunk(v):
    return v.reshape(2, D_RNN // RNN_CHUNK, 1, RNN_CHUNK).transpose(1, 0, 2, 3)


def _split_hi_lo(w):
    hi = w.astype(jnp.bfloat16)
    lo = (w - hi.astype(jnp.float32)).astype(jnp.bfloat16)
    return jnp.concatenate([hi, lo], axis=1)


def _split_in_cols(a):
    return a[..., :D_MAIN], a[..., D_MAIN:]


def _trunk(x, biases, lp):
    b, s, d = x.shape
    n = b * s
    x2 = x.reshape(n, d)
    x2_bf = x2.astype(jnp.bfloat16)
    for l in range(DEPTH):
        p = lp[l]
        proj = in_projection(x2_bf, p['w_main'], p['b_main'], jnp.float32, tn=D_MAIN // 4)
        gates = in_projection(x2_bf, p['w_gates'], p['b_gates'], jnp.bfloat16, tn=D_GATES // 2)
        proj3 = proj.reshape(b, s, D_MAIN)
        h_rnn = rglru_branch(proj3, p['conv_w'], p['conv_b'], p['wg'], p['gb'], p['lam'])
        y_attn = dilated_attention(proj3, biases)
        x1, eidx, gate = mix_and_route(
            x2, gates, h_rnn.reshape(n, D_RNN), y_attn.reshape(n, D_ATTN_OUT),
            p['w_rnn_out'], p['w_attn_out'], p['w_o'],
            p['ln1_g'], p['ln1_b'], p['router_w'], p['router_b'])
        x2, x2_bf = moe_layer(x1, eidx, gate, p['w_gate'], p['w_up'], p['w_down'], p['ln2_g'], p['ln2_b'])
    return x2.reshape(b, s, d)


def kernel(x_prompt, x_sample, rel_bias, w_in, b_in, conv_w, conv_b, rg_wa, rg_ba, rg_wi, rg_bi, rg_lam, w_rnn_out, w_attn_out, w_o, ln1_g, ln1_b, router_w, router_b, expert_router_w, expert_router_b, w_gate, w_up, w_down, ln2_g, ln2_b):
    bf = jnp.bfloat16
    biases = [_bias_table(rel_bias, g, dil) for g, (_, dil) in enumerate(DILATED_CONFIGS)]
    lp = []
    for l in range(DEPTH):
        rw = jnp.concatenate(
            [router_w[l], jnp.transpose(expert_router_w[l], (1, 0, 2)).reshape(D_MODEL, N_EXPERTS)], axis=1)
        rb = jnp.concatenate([router_b[l], expert_router_b[l].reshape(N_EXPERTS)])
        pad = ROUTE_W - rw.shape[1]
        w_main, w_gates = _split_in_cols(w_in[l])
        b_main, b_gates = _split_in_cols(b_in[l])
        lp.append(dict(
            w_main=w_main.astype(bf), w_gates=w_gates.astype(bf),
            b_main=b_main[None, :], b_gates=b_gates[None, :],
            conv_w=conv_w[l], conv_b=conv_b[l][None, :],
            wg=_gate_weights(rg_wa[l], rg_wi[l]),
            gb=jnp.concatenate([_per_chunk(rg_ba[l]), _per_chunk(rg_bi[l])], axis=-1),
            lam=_per_chunk(rg_lam[l]),
            w_rnn_out=w_rnn_out[l].astype(bf), w_attn_out=w_attn_out[l].astype(bf), w_o=w_o[l].astype(bf),
            ln1_g=ln1_g[l][None, :], ln1_b=ln1_b[l][None, :],
            router_w=_split_hi_lo(jnp.pad(rw, ((0, 0), (0, pad)))), router_b=jnp.pad(rb, (0, pad))[None, :],
            w_gate=w_gate[l], w_up=w_up[l], w_down=w_down[l],
            ln2_g=ln2_g[l][None, :], ln2_b=ln2_b[l][None, :]))
    return (_trunk(x_prompt, biases, lp), _trunk(x_sample, biases, lp))
```

```python
import functools

import numpy as np
import jax
import jax.numpy as jnp
from jax import lax
from jax.experimental import pallas as pl
from jax.experimental.pallas import tpu as pltpu

D_MODEL = 1024
DEPTH = 2
D_RNN = D_MODEL
RNN_BLOCKS = 16
RNN_BLOCK_W = D_RNN // RNN_BLOCKS
CONV_WIDTH = 4
RG_C = 8.0
DILATED_CONFIGS = ((128, 1), (512, 4), (2048, 16))
N_ATTN_GROUPS = len(DILATED_CONFIGS)
HEADS_PER_GROUP = 4
N_ATTN_HEADS = N_ATTN_GROUPS * HEADS_PER_GROUP
HEAD_DIM = 128
D_ATTN = N_ATTN_HEADS * HEAD_DIM
D_ATTN_OUT = HEADS_PER_GROUP * HEAD_DIM
NUM_BUCKETS = 32
MAX_DISTANCE = max(w for w, _ in DILATED_CONFIGS) // 2
NEG_INF = -1e30
D_IN = D_RNN + 3 * D_ATTN + 2 * D_MODEL
N_EXPERT_GROUPS = 4
EXPERTS_PER_GROUP = 8
N_EXPERTS = N_EXPERT_GROUPS * EXPERTS_PER_GROUP
TOP_K = 2
D_EXPERT = D_MODEL // 2
ALPHA = (2 * DEPTH) ** 0.25
LN_EPS = 1e-5

LANES = 128
SUBLANES = 8
VMEM_LIMIT = 56 * 1024 * 1024

COL_XRNN = 0
COL_GRNN = D_RNN
COL_GATTN = D_RNN + D_MODEL
COL_Q = D_RNN + 2 * D_MODEL
COL_K = COL_Q + D_ATTN
COL_V = COL_K + D_ATTN

RADIUS = 64
assert all(w // (2 * d) == RADIUS for w, d in DILATED_CONFIGS)
ATTN_ROWS = 1024
RNN_CHUNK = 256
RNN_TILE = 256
MOE_BLK = 256
DMA_QUEUES = 2
TOKEN_TILE = D_MODEL // LANES
assert TOKEN_TILE == SUBLANES
ROUTE_W = LANES


def _cparams(n_axes):
    return pltpu.CompilerParams(dimension_semantics=("arbitrary",) * n_axes,
                                vmem_limit_bytes=VMEM_LIMIT)


def _proj_kernel(x_ref, w_ref, b_ref, o_ref):
    acc = jnp.dot(x_ref[...], w_ref[...], preferred_element_type=jnp.float32)
    o_ref[...] = acc + b_ref[...]


def in_projection(x_bf, w_bf, b, tm=512, tn=3840):
    n, k = x_bf.shape
    nout = w_bf.shape[1]
    return pl.pallas_call(
        _proj_kernel,
        grid=(nout // tn, n // tm),
        in_specs=[pl.BlockSpec((tm, k), lambda j, i: (i, 0)),
                  pl.BlockSpec((k, tn), lambda j, i: (0, j)),
                  pl.BlockSpec((1, tn), lambda j, i: (0, j))],
        out_specs=pl.BlockSpec((tm, tn), lambda j, i: (i, j)),
        out_shape=jax.ShapeDtypeStruct((n, nout), jnp.float32),
        compiler_params=_cparams(2),
        name="in_projection",
    )(x_bf, w_bf, b)


def _rglru_kernel(x_ref, cw_ref, cb_ref, wg_ref, gb_ref, lam_ref, o_ref,
                  hsum, xc_s, a_s, u_s, h_s, *, seq, tile):
    c = RNN_CHUNK
    n_tiles = seq // tile
    n_grp = tile // SUBLANES
    big_rows = tile + 2 * SUBLANES
    row_in_grp = lax.broadcasted_iota(jnp.int32, (n_grp, SUBLANES, c), 1)

    def conv_tile(t0):
        centre = x_ref[0, pl.ds(t0, tile), :]
        p0 = pl.multiple_of(jnp.maximum(t0 - SUBLANES, 0), SUBLANES)
        n0 = pl.multiple_of(jnp.minimum(t0 + tile, seq - SUBLANES), SUBLANES)
        prev = jnp.where(t0 > 0, x_ref[0, pl.ds(p0, SUBLANES), :], 0.0)
        nxt = jnp.where(t0 + tile < seq, x_ref[0, pl.ds(n0, SUBLANES), :], 0.0)
        big = jnp.concatenate([prev, centre, nxt], axis=0)
        xc = cb_ref[...] + cw_ref[1:2, :] * centre
        for k in (0, 2, 3):
            off = k - 1
            tap = pltpu.roll(big, (-off) % big_rows, 0)[SUBLANES:SUBLANES + tile]
            xc = xc + cw_ref[k:k + 1, :] * tap
        return xc

    def gates(xc, dr):
        g = jnp.dot(xc.astype(jnp.bfloat16), wg_ref[0, dr],
                    preferred_element_type=jnp.float32) + gb_ref[0, dr]
        gate_r = jax.nn.sigmoid(g[:, :c])
        gate_i = jax.nn.sigmoid(g[:, c:])
        neg_lam = -lam_ref[0, dr]
        softplus = jnp.maximum(neg_lam, 0.0) + jnp.log1p(jnp.exp(-jnp.abs(neg_lam)))
        log_a = -RG_C * gate_r * softplus
        a = jnp.exp(log_a)
        th = jnp.tanh(log_a)
        one_minus_a2 = -2.0 * th / (1.0 - th)
        u = jnp.sqrt(one_minus_a2) * (gate_i * xc)
        return a, u

    def tile_prefix(a, u, reverse):
        a = a.reshape(n_grp, SUBLANES, c)
        u = u.reshape(n_grp, SUBLANES, c)
        for s in (1, 2, 4):
            if reverse:
                a_sh = pltpu.roll(a, SUBLANES - s, 1)
                u_sh = pltpu.roll(u, SUBLANES - s, 1)
                m = row_in_grp < SUBLANES - s
            else:
                a_sh = pltpu.roll(a, s, 1)
                u_sh = pltpu.roll(u, s, 1)
                m = row_in_grp >= s
            u = jnp.where(m, u + a * u_sh, u)
            a = jnp.where(m, a * a_sh, a)
        a_s[...] = a.reshape(tile, c)
        u_s[...] = u.reshape(tile, c)

    def run_direction(dr, reverse):
        def tile_body(ti, h):
            t_idx = (n_tiles - 1 - ti) if reverse else ti
            t0 = pl.multiple_of(t_idx * tile, tile)
            if reverse:
                xc = xc_s[pl.ds(t0, tile), :]
            else:
                xc = conv_tile(t0)
                xc_s[pl.ds(t0, tile), :] = xc
            a, u = gates(xc, dr)
            tile_prefix(a, u, reverse)

            def grp_body(gi, hc):
                g_idx = (n_grp - 1 - gi) if reverse else gi
                r0 = pl.multiple_of(g_idx * SUBLANES, SUBLANES)
                hg = u_s[pl.ds(r0, SUBLANES), :] + a_s[pl.ds(r0, SUBLANES), :] * hc
                h_s[pl.ds(r0, SUBLANES), :] = hg
                edge = hg[0:1, :] if reverse else hg[SUBLANES - 1:SUBLANES, :]
                return jnp.broadcast_to(edge, (SUBLANES, c))

            h = lax.fori_loop(0, n_grp, grp_body, h, unroll=4)
            if reverse:
                o_ref[0, pl.ds(t0, tile), :] = (hsum[pl.ds(t0, tile), :] + h_s[...]).astype(o_ref.dtype)
            else:
                hsum[pl.ds(t0, tile), :] = h_s[...]
            return h

        lax.fori_loop(0, n_tiles, tile_body, jnp.zeros((SUBLANES, c), jnp.float32))

    run_direction(0, False)
    run_direction(1, True)


def rglru_branch(proj3, cw, cb, wg, gb, lam):
    b, s, _ = proj3.shape
    c = RNN_CHUNK
    n_chunks = D_RNN // c
    kern = functools.partial(_rglru_kernel, seq=s, tile=RNN_TILE)
    return pl.pallas_call(
        kern,
        grid=(b, n_chunks),
        in_specs=[pl.BlockSpec((1, s, c), lambda bi, ci: (bi, 0, COL_XRNN // c + ci)),
                  pl.BlockSpec((CONV_WIDTH, c), lambda bi, ci: (0, ci)),
                  pl.BlockSpec((1, c), lambda bi, ci: (0, ci)),
                  pl.BlockSpec((1, 2, c, 2 * c), lambda bi, ci: (ci, 0, 0, 0)),
                  pl.BlockSpec((1, 2, 1, 2 * c), lambda bi, ci: (ci, 0, 0, 0)),
                  pl.BlockSpec((1, 2, 1, c), lambda bi, ci: (ci, 0, 0, 0))],
        out_specs=pl.BlockSpec((1, s, c), lambda bi, ci: (bi, 0, ci)),
        out_shape=jax.ShapeDtypeStruct((b, s, D_RNN), jnp.bfloat16),
        scratch_shapes=[pltpu.VMEM((s, c), jnp.float32),
                        pltpu.VMEM((s, c), jnp.float32),
                        pltpu.VMEM((RNN_TILE, c), jnp.float32),
                        pltpu.VMEM((RNN_TILE, c), jnp.float32),
                        pltpu.VMEM((RNN_TILE, c), jnp.float32)],
        compiler_params=_cparams(2),
        name="rglru_branch",
    )(proj3, cw, cb, wg, gb, lam)


def _attn_group(refs, bias, o_dst, l_dst, *, dil, is_first, is_last):
    q_ref, kp_ref, kc_ref, kn_ref, vp_ref, vc_ref, vn_ref = refs
    rows = ATTN_ROWS
    per_res = rows // dil
    n_q = per_res // RADIUS
    win = 3 * RADIUS
    nb = dil * n_q

    def fold(ref, r, n):
        if dil == 1:
            return ref[0, pl.ds(r, n), :]
        return ref[0, pl.ds(r, n, stride=dil), :]

    qs, ks, vs = [], [], []
    for r in range(dil):
        qf = fold(q_ref, r, per_res).astype(jnp.bfloat16)
        kf = jnp.concatenate([fold(kp_ref, r, RADIUS), fold(kc_ref, r, per_res),
                              fold(kn_ref, r, RADIUS)], axis=0).astype(jnp.bfloat16)
        vf = jnp.concatenate([fold(vp_ref, r, RADIUS), fold(vc_ref, r, per_res),
                              fold(vn_ref, r, RADIUS)], axis=0).astype(jnp.bfloat16)
        for jq in range(n_q):
            qs.append(qf[jq * RADIUS:(jq + 1) * RADIUS])
            ks.append(kf[jq * RADIUS:jq * RADIUS + win])
            vs.append(vf[jq * RADIUS:jq * RADIUS + win])
    qb, kb, vb = jnp.stack(qs), jnp.stack(ks), jnp.stack(vs)

    shape = (nb, RADIUS, win)
    jq_of = lax.broadcasted_iota(jnp.int32, shape, 0) & (n_q - 1)
    qi = lax.broadcasted_iota(jnp.int32, shape, 1)
    kj = lax.broadcasted_iota(jnp.int32, shape, 2)
    lo = jnp.where(jq_of == 0, jnp.where(is_first, RADIUS, 0), 0)
    hi = jnp.where(jq_of == n_q - 1, jnp.where(is_last, 2 * RADIUS, win), win)
    mask = (jnp.abs(kj - RADIUS - qi) <= RADIUS) & (kj >= lo) & (kj < hi)

    s = jnp.einsum('bqd,bkd->bqk', qb, kb, preferred_element_type=jnp.float32) * (HEAD_DIM ** -0.5)
    s = jnp.where(mask, s + bias[None], NEG_INF)
    m = jnp.max(s, axis=-1, keepdims=True)
    p = jnp.exp(s - m)
    l = jnp.sum(p, axis=-1, keepdims=True)
    o = jnp.einsum('bqk,bkd->bqd', p.astype(jnp.bfloat16), vb, preferred_element_type=jnp.float32) / l
    lse = m + jnp.log(l)
    for bi in range(nb):
        r, jq = divmod(bi, n_q)
        start = jq * RADIUS * dil + r
        idx = pl.ds(start, RADIUS) if dil == 1 else pl.ds(start, RADIUS, stride=dil)
        o_dst[idx, :] = o[bi]
        l_dst[idx, :] = jnp.broadcast_to(lse[bi], (RADIUS, HEAD_DIM))


def _attn_kernel(*refs, n_row_blocks):
    n_g = N_ATTN_GROUPS
    in_refs = refs[:7 * n_g]
    bias_refs = refs[7 * n_g:8 * n_g]
    y_ref = refs[8 * n_g]
    o_s, l_s = refs[8 * n_g + 1:]
    i = pl.program_id(1)
    for g, (_, dil) in enumerate(DILATED_CONFIGS):
        _attn_group(in_refs[7 * g:7 * g + 7], bias_refs[g][0], o_s.at[g], l_s.at[g],
                    dil=dil, is_first=i == 0, is_last=i == n_row_blocks - 1)
    l0, l1, l2 = l_s[0], l_s[1], l_s[2]
    m = jnp.maximum(jnp.maximum(l0, l1), l2)
    e0, e1, e2 = jnp.exp(l0 - m), jnp.exp(l1 - m), jnp.exp(l2 - m)
    y = (e0 * o_s[0] + e1 * o_s[1] + e2 * o_s[2]) / (e0 + e1 + e2)
    y_ref[0] = y.astype(y_ref.dtype)


def dilated_attention(proj3, biases):
    b, s, _ = proj3.shape
    rows = ATTN_ROWS
    nrb = s // rows
    w = HEAD_DIM
    nh = HEADS_PER_GROUP

    def cur_map(col):
        return lambda bi, i, h: (bi, i, col + h)

    in_specs, operands = [], []
    for g, (_, dil) in enumerate(DILATED_CONFIGS):
        halo = RADIUS * dil
        ratio = rows // halo
        n_halo_blocks = s // halo
        qc, kc, vc = COL_Q // w + g * nh, COL_K // w + g * nh, COL_V // w + g * nh

        def prev_map(col, ratio=ratio):
            return lambda bi, i, h: (bi, jnp.maximum(i * ratio - 1, 0), col + h)

        def next_map(col, ratio=ratio, last=n_halo_blocks - 1):
            return lambda bi, i, h: (bi, jnp.minimum((i + 1) * ratio, last), col + h)

        in_specs += [pl.BlockSpec((1, rows, w), cur_map(qc)),
                     pl.BlockSpec((1, halo, w), prev_map(kc)),
                     pl.BlockSpec((1, rows, w), cur_map(kc)),
                     pl.BlockSpec((1, halo, w), next_map(kc)),
                     pl.BlockSpec((1, halo, w), prev_map(vc)),
                     pl.BlockSpec((1, rows, w), cur_map(vc)),
                     pl.BlockSpec((1, halo, w), next_map(vc))]
        operands += [proj3] * 7
    in_specs += [pl.BlockSpec((1, RADIUS, 3 * RADIUS), lambda bi, i, h: (h, 0, 0))] * N_ATTN_GROUPS
    operands += list(biases)
    return pl.pallas_call(
        functools.partial(_attn_kernel, n_row_blocks=nrb),
        grid=(b, nrb, nh),
        in_specs=in_specs,
        out_specs=pl.BlockSpec((1, rows, w), cur_map(0)),
        out_shape=jax.ShapeDtypeStruct((b, s, D_ATTN_OUT), jnp.bfloat16),
        scratch_shapes=[pltpu.VMEM((N_ATTN_GROUPS, rows, w), jnp.float32),
                        pltpu.VMEM((N_ATTN_GROUPS, rows, w), jnp.float32)],
        compiler_params=_cparams(3),
        name="dilated_attention",
    )(*operands)


def _store_token_tiles(ref, val, first_chunk=0):
    rows = val.shape[0]
    for c in range(val.shape[1] // LANES):
        ref[pl.ds(first_chunk + c, rows, stride=TOKEN_TILE), :] = val[:, c * LANES:(c + 1) * LANES]


def _load_token_tiles(ref, start, rows):
    return [ref[pl.ds(start * TOKEN_TILE + c, rows, stride=TOKEN_TILE), :] for c in range(TOKEN_TILE)]


def _layer_norm(z, g, b):
    mu = jnp.mean(z, axis=-1, keepdims=True)
    zc = z - mu
    var = jnp.mean(zc * zc, axis=-1, keepdims=True)
    return zc * lax.rsqrt(var + LN_EPS) * g + b


def _mix_kernel(x_ref, grnn_ref, gattn_ref, h_ref, y_ref,
                wr_ref, wa_ref, wo_ref, lng_ref, lnb_ref, rw_ref, rb_ref,
                xo_ref, eidx_ref, gate_ref):
    rnn = jnp.dot(h_ref[...], wr_ref[...], preferred_element_type=jnp.float32)
    att = jnp.dot(y_ref[...], wa_ref[...], preferred_element_type=jnp.float32)
    mixed = jax.nn.sigmoid(grnn_ref[...]) * rnn + jax.nn.sigmoid(gattn_ref[...]) * att
    z = ALPHA * x_ref[...] + jnp.dot(mixed.astype(jnp.bfloat16), wo_ref[...],
                                     preferred_element_type=jnp.float32)
    x1 = _layer_norm(z, lng_ref[...], lnb_ref[...])
    _store_token_tiles(xo_ref, x1)

    x_hi = x1.astype(jnp.bfloat16)
    x_lo = (x1 - x_hi.astype(jnp.float32)).astype(jnp.bfloat16)
    t = jnp.dot(x_hi, rw_ref[...], preferred_element_type=jnp.float32)
    logits = (t[:, :ROUTE_W] + t[:, ROUTE_W:]
              + jnp.dot(x_lo, rw_ref[:, :ROUTE_W], preferred_element_type=jnp.float32)) + rb_ref[...]
    tm = logits.shape[0]
    lane = lax.broadcasted_iota(jnp.int32, (tm, ROUTE_W), 1)
    lane_f = lane.astype(jnp.float32)
    far = float(ROUTE_W)
    is_group = lane < N_EXPERT_GROUPS
    gl = jnp.where(is_group, logits, -jnp.inf)
    gmax = jnp.max(gl, axis=-1, keepdims=True)
    gsel = jnp.min(jnp.where(gl == gmax, lane_f, far), axis=-1, keepdims=True)
    p_group = 1.0 / jnp.sum(jnp.where(is_group, jnp.exp(logits - gmax), 0.0), axis=-1, keepdims=True)
    lane_group = ((lane - N_EXPERT_GROUPS) >> 3).astype(jnp.float32)
    in_group = (lane >= N_EXPERT_GROUPS) & (lane < N_EXPERT_GROUPS + N_EXPERTS) & (lane_group == gsel)
    el = jnp.where(in_group, logits, -jnp.inf)
    m1 = jnp.max(el, axis=-1, keepdims=True)
    i1 = jnp.min(jnp.where(el == m1, lane_f, far), axis=-1, keepdims=True)
    el2 = jnp.where(lane_f == i1, -jnp.inf, el)
    m2 = jnp.max(el2, axis=-1, keepdims=True)
    i2 = jnp.min(jnp.where(el2 == m2, lane_f, far), axis=-1, keepdims=True)
    e21 = jnp.exp(m2 - m1)
    den = 1.0 + e21
    g1 = p_group * (1.0 / den)
    g2 = p_group * (e21 / den)
    eidx = jnp.where(lane == 0, i1, jnp.where(lane == 1, i2, float(N_EXPERT_GROUPS))) - float(N_EXPERT_GROUPS)
    eidx_ref[...] = eidx.astype(jnp.int32)
    gate_ref[...] = jnp.where(lane == 0, g1, jnp.where(lane == 1, g2, 0.0))


def mix_and_route(x, proj, h_rnn, y_attn, wr, wa, wo, lng, lnb, rw, rb, tm=512):
    n = x.shape[0]
    d = D_MODEL
    w = D_ATTN_OUT
    row = lambda width, col=0: pl.BlockSpec((tm, width), lambda i, c=col: (i, c))
    full = lambda a: pl.BlockSpec(a.shape, lambda i: (0,) * a.ndim)
    return pl.pallas_call(
        _mix_kernel,
        grid=(n // tm,),
        in_specs=[row(d), row(d, COL_GRNN // d), row(d, COL_GATTN // d), row(d), row(w),
                  full(wr), full(wa), full(wo), full(lng), full(lnb), full(rw), full(rb)],
        out_specs=[pl.BlockSpec((tm * TOKEN_TILE, LANES), lambda i: (i, 0)), row(ROUTE_W), row(ROUTE_W)],
        out_shape=[jax.ShapeDtypeStruct((n * TOKEN_TILE, LANES), jnp.float32),
                   jax.ShapeDtypeStruct((n, ROUTE_W), jnp.int32),
                   jax.ShapeDtypeStruct((n, ROUTE_W), jnp.float32)],
        compiler_params=_cparams(1),
        name="mix_and_route",
    )(x, proj, proj, h_rnn, y_attn, wr, wa, wo, lng, lnb, rw, rb)


def _rank_kernel(e_ref, dest_ref, cnt_ref, carry, *, tm):
    ph = pl.program_id(0)
    i = pl.program_id(1)
    lanes = ROUTE_W

    @pl.when((ph == 0) & (i == 0))
    def _():
        carry[...] = jnp.zeros_like(carry)

    @pl.when((ph == 1) & (i == 0))
    def _():
        cnt = carry[...]
        cnt_ref[...] = jnp.broadcast_to(cnt, (lanes, lanes)).astype(jnp.int32)
        padded = jnp.floor((cnt + (MOE_BLK - 1)) * (1.0 / MOE_BLK)) * MOE_BLK
        r = lax.broadcasted_iota(jnp.int32, (lanes, lanes), 0)
        c = lax.broadcasted_iota(jnp.int32, (lanes, lanes), 1)
        lower = (c < r).astype(jnp.float32)
        pstart = jnp.dot(lower, jnp.broadcast_to(padded, (lanes, lanes)),
                         preferred_element_type=jnp.float32, precision=lax.Precision.HIGHEST)
        carry[...] = pstart[:, 0:1]

    r8 = lax.broadcasted_iota(jnp.int32, (SUBLANES, lanes), 0)
    c8 = lax.broadcasted_iota(jnp.int32, (SUBLANES, lanes), 1)
    pick = ((r8 == c8) & (r8 < TOP_K)).astype(jnp.bfloat16)
    ef = e_ref[...].astype(jnp.float32).astype(jnp.bfloat16)
    et = lax.dot_general(pick, ef, (((1,), (1,)), ((), ())), preferred_element_type=jnp.float32)
    e0, e1 = et[0:1, :], et[1:2, :]
    sub = lax.broadcasted_iota(jnp.int32, (lanes, tm), 0).astype(jnp.float32)
    is0, is1 = sub == e0, sub == e1
    member = jnp.where(is0 | is1, 1.0, 0.0)

    @pl.when(ph == 1)
    def _():
        s_i = lax.broadcasted_iota(jnp.int32, (tm, tm), 0)
        t_i = lax.broadcasted_iota(jnp.int32, (tm, tm), 1)
        earlier = (s_i < t_i).astype(jnp.bfloat16)
        before = jnp.dot(member.astype(jnp.bfloat16), earlier, preferred_element_type=jnp.float32)
        pos = before + carry[...]
        d0 = jnp.sum(jnp.where(is0, pos, 0.0), axis=0, keepdims=True)
        d1 = jnp.sum(jnp.where(is1, pos, 0.0), axis=0, keepdims=True)
        row = lax.broadcasted_iota(jnp.int32, (SUBLANES, tm), 0)
        dest_ref[...] = jnp.where(row == 0, d0, jnp.where(row == 1, d1, 0.0)).astype(jnp.int32)

    carry[...] = carry[...] + jnp.sum(member, axis=1, keepdims=True)


def route_slots(eidx, tm=512):
    n = eidx.shape[0]
    nt = n // tm
    return pl.pallas_call(
        functools.partial(_rank_kernel, tm=tm),
        grid=(2, nt),
        in_specs=[pl.BlockSpec((tm, ROUTE_W), lambda ph, i: (i, 0))],
        out_specs=[pl.BlockSpec((SUBLANES, tm), lambda ph, i: (0, i * ph)),
                   pl.BlockSpec((ROUTE_W, ROUTE_W), lambda ph, i: (0, 0))],
        out_shape=[jax.ShapeDtypeStruct((SUBLANES, n), jnp.int32),
                   jax.ShapeDtypeStruct((ROUTE_W, ROUTE_W), jnp.int32)],
        scratch_shapes=[pltpu.VMEM((ROUTE_W, 1), jnp.float32)],
        compiler_params=_cparams(2),
        name="route_slots",
    )(eidx)


def _slot_map_kernel(dest_ref, slot_ref, *, n, slots):
    def init(j, c):
        slot_ref[j] = 0
        return c

    lax.fori_loop(0, slots, init, 0, unroll=32)

    def body(t, c):
        slot_ref[dest_ref[t]] = t
        slot_ref[dest_ref[n + t]] = t
        return c

    lax.fori_loop(0, n, body, 0, unroll=16)


def slot_map(dest_flat, n, slots):
    return pl.pallas_call(
        functools.partial(_slot_map_kernel, n=n, slots=slots),
        in_specs=[pl.BlockSpec(memory_space=pltpu.SMEM)],
        out_specs=pl.BlockSpec(memory_space=pltpu.SMEM),
        out_shape=jax.ShapeDtypeStruct((slots,), jnp.int32),
        name="slot_map",
    )(dest_flat)


def _expert_kernel(be_ref, nused_ref, st_ref, x_hbm, wg_ref, wu_ref, wd_ref, y_ref, xs, land, sem):
    i = pl.program_id(0)
    n_used = nused_ref[0]

    def gather_block(blk):
        base = blk * MOE_BLK
        for j in range(MOE_BLK):
            src = pl.multiple_of(st_ref[base + j] * TOKEN_TILE, TOKEN_TILE)
            pltpu.make_async_copy(x_hbm.at[pl.ds(src, TOKEN_TILE)], land.at[pl.ds(j * TOKEN_TILE, TOKEN_TILE)],
                                  sem.at[0]).start(priority=j % DMA_QUEUES)

    def wait_block():
        pltpu.make_async_copy(x_hbm.at[pl.ds(0, MOE_BLK * TOKEN_TILE)], land, sem.at[0]).wait()

    @pl.when(i == 0)
    def _():
        gather_block(0)

    @pl.when(i < n_used)
    def _():
        wait_block()
        for c, chunk in enumerate(_load_token_tiles(land, 0, MOE_BLK)):
            xs[:, c * LANES:(c + 1) * LANES] = chunk.astype(jnp.bfloat16)

    @pl.when(i + 1 <= n_used)
    def _():
        gather_block(i + 1)

    @pl.when(n_used > i)
    def _():
        xb = xs[...]
        gate = jnp.dot(xb, wg_ref[0], preferred_element_type=jnp.float32)
        up = jnp.dot(xb, wu_ref[0], preferred_element_type=jnp.float32)
        hid = (jax.nn.silu(gate) * up).astype(jnp.bfloat16)
        _store_token_tiles(y_ref, jnp.dot(hid, wd_ref[0], preferred_element_type=jnp.float32))

    @pl.when(i == n_used - 1)
    def _():
        wait_block()

    @pl.when(i >= n_used)
    def _():
        y_ref[...] = jnp.zeros_like(y_ref)


def expert_blocks(block_e, n_used, slot_tok, x1t, wg, wu, wd):
    d = D_MODEL
    n_blocks = slot_tok.shape[0] // MOE_BLK - 1
    slots = n_blocks * MOE_BLK
    tile_rows = MOE_BLK * TOKEN_TILE
    grid_spec = pltpu.PrefetchScalarGridSpec(
        num_scalar_prefetch=3,
        grid=(n_blocks,),
        in_specs=[pl.BlockSpec(memory_space=pl.ANY),
                  pl.BlockSpec((1, d, D_EXPERT), lambda i, be, nu, st: (be[i], 0, 0)),
                  pl.BlockSpec((1, d, D_EXPERT), lambda i, be, nu, st: (be[i], 0, 0)),
                  pl.BlockSpec((1, D_EXPERT, d), lambda i, be, nu, st: (be[i], 0, 0))],
        out_specs=pl.BlockSpec((tile_rows, LANES), lambda i, be, nu, st: (i, 0)),
        scratch_shapes=[pltpu.VMEM((MOE_BLK, d), jnp.bfloat16),
                        pltpu.VMEM((tile_rows, LANES), jnp.float32),
                        pltpu.SemaphoreType.DMA((1,))],
    )
    return pl.pallas_call(
        _expert_kernel,
        grid_spec=grid_spec,
        out_shape=jax.ShapeDtypeStruct((slots * TOKEN_TILE, LANES), jnp.float32),
        compiler_params=_cparams(1),
        name="expert_blocks",
    )(block_e, n_used, slot_tok, x1t, wg, wu, wd)


def _combine_kernel(dest_ref, x_ref, gate_ref, y_hbm, g_ref, b_ref, xo_ref, xob_ref, fbuf, sem, *, n, tm):
    i = pl.program_id(0)
    nt = pl.num_programs(0)
    half = tm * TOKEN_TILE

    def gather_tile(tile, slot):
        base = tile * tm
        for k in range(TOP_K):
            for t in range(tm):
                src = pl.multiple_of(dest_ref[k * n + base + t] * TOKEN_TILE, TOKEN_TILE)
                pltpu.make_async_copy(y_hbm.at[pl.ds(src, TOKEN_TILE)],
                                      fbuf.at[slot, pl.ds((k * tm + t) * TOKEN_TILE, TOKEN_TILE)],
                                      sem.at[slot]).start(priority=t % DMA_QUEUES)

    def wait_tile(slot):
        pltpu.make_async_copy(y_hbm.at[pl.ds(0, TOP_K * half)], fbuf.at[slot], sem.at[slot]).wait()

    @pl.when(i == 0)
    def _():
        gather_tile(0, 0)

    slot = i % 2
    wait_tile(slot)
    g0, g1 = gate_ref[:, 0:1], gate_ref[:, 1:2]
    f0 = _load_token_tiles(fbuf.at[slot], 0, tm)
    f1 = _load_token_tiles(fbuf.at[slot], tm, tm)
    xs = _load_token_tiles(x_ref, 0, tm)
    gather_tile(jnp.minimum(i + 1, nt - 1), 1 - slot)
    z = [ALPHA * xs[c] + (g0 * f0[c] + g1 * f1[c]) for c in range(TOKEN_TILE)]
    mu = sum(jnp.sum(zc, axis=-1, keepdims=True) for zc in z) * (1.0 / D_MODEL)
    zc = [zz - mu for zz in z]
    var = sum(jnp.sum(v * v, axis=-1, keepdims=True) for v in zc) * (1.0 / D_MODEL)
    inv = lax.rsqrt(var + LN_EPS)
    for c in range(TOKEN_TILE):
        cols = slice(c * LANES, (c + 1) * LANES)
        x2 = zc[c] * inv * g_ref[:, cols] + b_ref[:, cols]
        xo_ref[:, cols] = x2
        xob_ref[:, cols] = x2.astype(jnp.bfloat16)

    @pl.when(i == nt - 1)
    def _():
        wait_tile(1 - slot)


def combine_ln2(dest_flat, x1t, gate, ybt, g, b, tm=256):
    n = gate.shape[0]
    d = D_MODEL
    tile_rows = tm * TOKEN_TILE
    grid_spec = pltpu.PrefetchScalarGridSpec(
        num_scalar_prefetch=1,
        grid=(n // tm,),
        in_specs=[pl.BlockSpec((tile_rows, LANES), lambda i, ds: (i, 0)),
                  pl.BlockSpec((tm, ROUTE_W), lambda i, ds: (i, 0)),
                  pl.BlockSpec(memory_space=pl.ANY),
                  pl.BlockSpec((1, d), lambda i, ds: (0, 0)),
                  pl.BlockSpec((1, d), lambda i, ds: (0, 0))],
        out_specs=[pl.BlockSpec((tm, d), lambda i, ds: (i, 0)),
                   pl.BlockSpec((tm, d), lambda i, ds: (i, 0))],
        scratch_shapes=[pltpu.VMEM((2, TOP_K * tile_rows, LANES), jnp.float32),
                        pltpu.SemaphoreType.DMA((2,))],
    )
    return pl.pallas_call(
        functools.partial(_combine_kernel, n=n, tm=tm),
        grid_spec=grid_spec,
        out_shape=[jax.ShapeDtypeStruct((n, d), jnp.float32),
                   jax.ShapeDtypeStruct((n, d), jnp.bfloat16)],
        compiler_params=_cparams(1),
        name="combine_ln2",
    )(dest_flat, x1t, gate, ybt, g, b)


def moe_layer(x1, eidx, gate, wg, wu, wd, ln_g, ln_b):
    n = eidx.shape[0]
    m = n * TOP_K
    n_blocks = m // MOE_BLK + N_EXPERTS
    slots = n_blocks * MOE_BLK
    dest8, cnt = route_slots(eidx)
    dest_flat = dest8[:TOP_K].reshape(m)
    counts = cnt[:N_EXPERTS, 0]
    pend = jnp.cumsum((counts + MOE_BLK - 1) // MOE_BLK * MOE_BLK)
    blk_start = jnp.arange(n_blocks, dtype=jnp.int32) * MOE_BLK
    block_e = jnp.minimum(jnp.sum((pend[None, :] <= blk_start[:, None]).astype(jnp.int32), axis=1),
                          N_EXPERTS - 1)
    n_used = (pend[-1:] // MOE_BLK).astype(jnp.int32)
    slot_tok = slot_map(dest_flat, n, slots + MOE_BLK)
    yb = expert_blocks(block_e, n_used, slot_tok, x1, wg, wu, wd)
    return combine_ln2(dest_flat, x1, gate, yb, ln_g, ln_b)


def _t5_bucket(rel):
    half = NUM_BUCKETS // 2
    max_exact = half // 2
    n = np.abs(rel)
    large = max_exact + (np.log(np.maximum(n, 1) / max_exact) / np.log(MAX_DISTANCE / max_exact)
                         * (half - max_exact)).astype(np.int32)
    large = np.minimum(large, half - 1)
    return np.where(rel > 0, half, 0) + np.where(n < max_exact, n, large)


def _bias_table(rel_bias, g, dil):
    rel = (np.arange(3 * RADIUS)[None, :] - RADIUS - np.arange(RADIUS)[:, None]) * dil
    hs = slice(g * HEADS_PER_GROUP, (g + 1) * HEADS_PER_GROUP)
    onehot = jnp.asarray(_t5_bucket(rel)[..., None] == np.arange(NUM_BUCKETS), jnp.float32)
    return jnp.einsum('qkb,bh->hqk', onehot, rel_bias[:, hs].astype(jnp.float32),
                      precision=lax.Precision.HIGHEST)


def _chunk_block_diag(w):
    per = RNN_CHUNK // RNN_BLOCK_W
    n_chunks = RNN_BLOCKS // per
    w4 = w.reshape(n_chunks, per, RNN_BLOCK_W, RNN_BLOCK_W)
    dense = jnp.einsum('chij,hk->chikj', w4, jnp.eye(per, dtype=w.dtype))
    return dense.reshape(n_chunks, RNN_CHUNK, RNN_CHUNK)


def _gate_weights(wa, wi):
    per_dir = [jnp.concatenate([_chunk_block_diag(wa[d]), _chunk_block_diag(wi[d])], axis=-1)
               for d in range(2)]
    return jnp.stack(per_dir, axis=1).astype(jnp.bfloat16)


def _per_chunk(v):
    return v.reshape(2, D_RNN // RNN_CHUNK, 1, RNN_CHUNK).transpose(1, 0, 2, 3)


def _split_hi_lo(w):
    hi = w.astype(jnp.bfloat16)
    lo = (w - hi.astype(jnp.float32)).astype(jnp.bfloat16)
    return jnp.concatenate([hi, lo], axis=1)


def _permute_in_cols(a):
    x_rnn, q, k, v, g_rnn, g_attn = jnp.split(
        a, [D_RNN, D_RNN + D_ATTN, D_RNN + 2 * D_ATTN, D_RNN + 3 * D_ATTN,
            D_RNN + 3 * D_ATTN + D_MODEL], axis=-1)
    return jnp.concatenate([x_rnn, g_rnn, g_attn, q, k, v], axis=-1)


def _trunk(x, biases, lp):
    b, s, d = x.shape
    n = b * s
    x2 = x.reshape(n, d)
    x2_bf = x2.astype(jnp.bfloat16)
    for l in range(DEPTH):
        p = lp[l]
        proj = in_projection(x2_bf, p['w_in'], p['b_in'])
        proj3 = proj.reshape(b, s, D_IN)
        h_rnn = rglru_branch(proj3, p['conv_w'], p['conv_b'], p['wg'], p['gb'], p['lam'])
        y_attn = dilated_attention(proj3, biases)
        x1, eidx, gate = mix_and_route(
            x2, proj, h_rnn.reshape(n, D_RNN), y_attn.reshape(n, D_ATTN_OUT),
            p['w_rnn_out'], p['w_attn_out'], p['w_o'],
            p['ln1_g'], p['ln1_b'], p['router_w'], p['router_b'])
        x2, x2_bf = moe_layer(x1, eidx, gate, p['w_gate'], p['w_up'], p['w_down'], p['ln2_g'], p['ln2_b'])
    return x2.reshape(b, s, d)


def kernel(x_prompt, x_sample, rel_bias, w_in, b_in, conv_w, conv_b, rg_wa, rg_ba, rg_wi, rg_bi, rg_lam, w_rnn_out, w_attn_out, w_o, ln1_g, ln1_b, router_w, router_b, expert_router_w, expert_router_b, w_gate, w_up, w_down, ln2_g, ln2_b):
    bf = jnp.bfloat16
    biases = [_bias_table(rel_bias, g, dil) for g, (_, dil) in enumerate(DILATED_CONFIGS)]
    lp = []
    for l in range(DEPTH):
        rw = jnp.concatenate(
            [router_w[l], jnp.transpose(expert_router_w[l], (1, 0, 2)).reshape(D_MODEL, N_EXPERTS)], axis=1)
        rb = jnp.concatenate([router_b[l], expert_router_b[l].reshape(N_EXPERTS)])
        pad = ROUTE_W - rw.shape[1]
        lp.append(dict(
            w_in=_permute_in_cols(w_in[l]).astype(bf),
            b_in=_permute_in_cols(b_in[l])[None, :],
            conv_w=conv_w[l], conv_b=conv_b[l][None, :],
            wg=_gate_weights(rg_wa[l], rg_wi[l]),
            gb=jnp.concatenate([_per_chunk(rg_ba[l]), _per_chunk(rg_bi[l])], axis=-1),
            lam=_per_chunk(rg_lam[l]),
            w_rnn_out=w_rnn_out[l].astype(bf), w_attn_out=w_attn_out[l].astype(bf), w_o=w_o[l].astype(bf),
            ln1_g=ln1_g[l][None, :], ln1_b=ln1_b[l][None, :],
            router_w=_split_hi_lo(jnp.pad(rw, ((0, 0), (0, pad)))), router_b=jnp.pad(rb, (0, pad))[None, :],
            w_gate=w_gate[l].astype(bf), w_up=w_up[l].astype(bf), w_down=w_down[l].astype(bf),
            ln2_g=ln2_g[l][None, :], ln2_b=ln2_b[l][None, :]))
    return (_trunk(x_prompt, biases, lp), _trunk(x_sample, biases, lp))
```

```python
import functools

import numpy as np
import jax
import jax.numpy as jnp
from jax import lax
from jax.experimental import pallas as pl
from jax.experimental.pallas import tpu as pltpu

D_MODEL = 1024
DEPTH = 2
D_RNN = D_MODEL
RNN_BLOCKS = 16
RNN_BLOCK_W = D_RNN // RNN_BLOCKS
CONV_WIDTH = 4
RG_C = 8.0
DILATED_CONFIGS = ((128, 1), (512, 4), (2048, 16))
N_ATTN_GROUPS = len(DILATED_CONFIGS)
HEADS_PER_GROUP = 4
N_ATTN_HEADS = N_ATTN_GROUPS * HEADS_PER_GROUP
HEAD_DIM = 128
D_ATTN = N_ATTN_HEADS * HEAD_DIM
D_ATTN_OUT = HEADS_PER_GROUP * HEAD_DIM
NUM_BUCKETS = 32
MAX_DISTANCE = max(w for w, _ in DILATED_CONFIGS) // 2
NEG_INF = -1e30
D_IN = D_RNN + 3 * D_ATTN + 2 * D_MODEL
N_EXPERT_GROUPS = 4
EXPERTS_PER_GROUP = 8
N_EXPERTS = N_EXPERT_GROUPS * EXPERTS_PER_GROUP
TOP_K = 2
D_EXPERT = D_MODEL // 2
ALPHA = (2 * DEPTH) ** 0.25
LN_EPS = 1e-5

LANES = 128
SUBLANES = 8
VMEM_LIMIT = 56 * 1024 * 1024

COL_XRNN = 0
COL_GRNN = D_RNN
COL_GATTN = D_RNN + D_MODEL
COL_Q = D_RNN + 2 * D_MODEL
COL_K = COL_Q + D_ATTN
COL_V = COL_K + D_ATTN

RADIUS = 64
assert all(w // (2 * d) == RADIUS for w, d in DILATED_CONFIGS)
ATTN_ROWS = 1024
RNN_CHUNK = 256
RNN_TILE = 256
MOE_BLK = 256
DMA_QUEUES = 2
ROUTE_W = LANES


def _cparams(n_axes):
    return pltpu.CompilerParams(dimension_semantics=("arbitrary",) * n_axes,
                                vmem_limit_bytes=VMEM_LIMIT)


def _proj_kernel(x_ref, w_ref, b_ref, o_ref):
    acc = jnp.dot(x_ref[...], w_ref[...], preferred_element_type=jnp.float32)
    o_ref[...] = acc + b_ref[...]


def in_projection(x_bf, w_bf, b, tm=512, tn=3840):
    n, k = x_bf.shape
    nout = w_bf.shape[1]
    return pl.pallas_call(
        _proj_kernel,
        grid=(nout // tn, n // tm),
        in_specs=[pl.BlockSpec((tm, k), lambda j, i: (i, 0)),
                  pl.BlockSpec((k, tn), lambda j, i: (0, j)),
                  pl.BlockSpec((1, tn), lambda j, i: (0, j))],
        out_specs=pl.BlockSpec((tm, tn), lambda j, i: (i, j)),
        out_shape=jax.ShapeDtypeStruct((n, nout), jnp.float32),
        compiler_params=_cparams(2),
        name="in_projection",
    )(x_bf, w_bf, b)


def _rglru_kernel(x_ref, cw_ref, cb_ref, wg_ref, gb_ref, lam_ref, o_ref,
                  hsum, xc_s, a_s, u_s, h_s, *, seq, tile):
    c = RNN_CHUNK
    n_tiles = seq // tile
    n_grp = tile // SUBLANES
    big_rows = tile + 2 * SUBLANES
    row_in_grp = lax.broadcasted_iota(jnp.int32, (n_grp, SUBLANES, c), 1)

    def conv_tile(t0):
        centre = x_ref[0, pl.ds(t0, tile), :]
        p0 = pl.multiple_of(jnp.maximum(t0 - SUBLANES, 0), SUBLANES)
        n0 = pl.multiple_of(jnp.minimum(t0 + tile, seq - SUBLANES), SUBLANES)
        prev = jnp.where(t0 > 0, x_ref[0, pl.ds(p0, SUBLANES), :], 0.0)
        nxt = jnp.where(t0 + tile < seq, x_ref[0, pl.ds(n0, SUBLANES), :], 0.0)
        big = jnp.concatenate([prev, centre, nxt], axis=0)
        xc = cb_ref[...] + cw_ref[1:2, :] * centre
        for k in (0, 2, 3):
            off = k - 1
            tap = pltpu.roll(big, (-off) % big_rows, 0)[SUBLANES:SUBLANES + tile]
            xc = xc + cw_ref[k:k + 1, :] * tap
        return xc

    def gates(xc, dr):
        g = jnp.dot(xc.astype(jnp.bfloat16), wg_ref[0, dr],
                    preferred_element_type=jnp.float32) + gb_ref[0, dr]
        gate_r = jax.nn.sigmoid(g[:, :c])
        gate_i = jax.nn.sigmoid(g[:, c:])
        neg_lam = -lam_ref[0, dr]
        softplus = jnp.maximum(neg_lam, 0.0) + jnp.log1p(jnp.exp(-jnp.abs(neg_lam)))
        log_a = -RG_C * gate_r * softplus
        a = jnp.exp(log_a)
        th = jnp.tanh(log_a)
        one_minus_a2 = -2.0 * th / (1.0 - th)
        u = jnp.sqrt(one_minus_a2) * (gate_i * xc)
        return a, u

    def tile_prefix(a, u, reverse):
        a = a.reshape(n_grp, SUBLANES, c)
        u = u.reshape(n_grp, SUBLANES, c)
        for s in (1, 2, 4):
            if reverse:
                a_sh = pltpu.roll(a, SUBLANES - s, 1)
                u_sh = pltpu.roll(u, SUBLANES - s, 1)
                m = row_in_grp < SUBLANES - s
            else:
                a_sh = pltpu.roll(a, s, 1)
                u_sh = pltpu.roll(u, s, 1)
                m = row_in_grp >= s
            u = jnp.where(m, u + a * u_sh, u)
            a = jnp.where(m, a * a_sh, a)
        a_s[...] = a.reshape(tile, c)
        u_s[...] = u.reshape(tile, c)

    def run_direction(dr, reverse):
        def tile_body(ti, h):
            t_idx = (n_tiles - 1 - ti) if reverse else ti
            t0 = pl.multiple_of(t_idx * tile, tile)
            if reverse:
                xc = xc_s[pl.ds(t0, tile), :]
            else:
                xc = conv_tile(t0)
                xc_s[pl.ds(t0, tile), :] = xc
            a, u = gates(xc, dr)
            tile_prefix(a, u, reverse)

            def grp_body(gi, hc):
                g_idx = (n_grp - 1 - gi) if reverse else gi
                r0 = pl.multiple_of(g_idx * SUBLANES, SUBLANES)
                hg = u_s[pl.ds(r0, SUBLANES), :] + a_s[pl.ds(r0, SUBLANES), :] * hc
                h_s[pl.ds(r0, SUBLANES), :] = hg
                edge = hg[0:1, :] if reverse else hg[SUBLANES - 1:SUBLANES, :]
                return jnp.broadcast_to(edge, (SUBLANES, c))

            h = lax.fori_loop(0, n_grp, grp_body, h, unroll=4)
            if reverse:
                o_ref[0, pl.ds(t0, tile), :] = (hsum[pl.ds(t0, tile), :] + h_s[...]).astype(o_ref.dtype)
            else:
                hsum[pl.ds(t0, tile), :] = h_s[...]
            return h

        lax.fori_loop(0, n_tiles, tile_body, jnp.zeros((SUBLANES, c), jnp.float32))

    run_direction(0, False)
    run_direction(1, True)


def rglru_branch(proj3, cw, cb, wg, gb, lam):
    b, s, _ = proj3.shape
    c = RNN_CHUNK
    n_chunks = D_RNN // c
    kern = functools.partial(_rglru_kernel, seq=s, tile=RNN_TILE)
    return pl.pallas_call(
        kern,
        grid=(b, n_chunks),
        in_specs=[pl.BlockSpec((1, s, c), lambda bi, ci: (bi, 0, COL_XRNN // c + ci)),
                  pl.BlockSpec((CONV_WIDTH, c), lambda bi, ci: (0, ci)),
                  pl.BlockSpec((1, c), lambda bi, ci: (0, ci)),
                  pl.BlockSpec((1, 2, c, 2 * c), lambda bi, ci: (ci, 0, 0, 0)),
                  pl.BlockSpec((1, 2, 1, 2 * c), lambda bi, ci: (ci, 0, 0, 0)),
                  pl.BlockSpec((1, 2, 1, c), lambda bi, ci: (ci, 0, 0, 0))],
        out_specs=pl.BlockSpec((1, s, c), lambda bi, ci: (bi, 0, ci)),
        out_shape=jax.ShapeDtypeStruct((b, s, D_RNN), jnp.bfloat16),
        scratch_shapes=[pltpu.VMEM((s, c), jnp.float32),
                        pltpu.VMEM((s, c), jnp.float32),
                        pltpu.VMEM((RNN_TILE, c), jnp.float32),
                        pltpu.VMEM((RNN_TILE, c), jnp.float32),
                        pltpu.VMEM((RNN_TILE, c), jnp.float32)],
        compiler_params=_cparams(2),
        name="rglru_branch",
    )(proj3, cw, cb, wg, gb, lam)


def _attn_group(refs, bias, o_dst, l_dst, *, dil, is_first, is_last):
    q_ref, kp_ref, kc_ref, kn_ref, vp_ref, vc_ref, vn_ref = refs
    rows = ATTN_ROWS
    per_res = rows // dil
    n_q = per_res // RADIUS
    win = 3 * RADIUS
    nb = dil * n_q

    def fold(ref, r, n):
        if dil == 1:
            return ref[0, pl.ds(r, n), :]
        return ref[0, pl.ds(r, n, stride=dil), :]

    qs, ks, vs = [], [], []
    for r in range(dil):
        qf = fold(q_ref, r, per_res).astype(jnp.bfloat16)
        kf = jnp.concatenate([fold(kp_ref, r, RADIUS), fold(kc_ref, r, per_res),
                              fold(kn_ref, r, RADIUS)], axis=0).astype(jnp.bfloat16)
        vf = jnp.concatenate([fold(vp_ref, r, RADIUS), fold(vc_ref, r, per_res),
                              fold(vn_ref, r, RADIUS)], axis=0).astype(jnp.bfloat16)
        for jq in range(n_q):
            qs.append(qf[jq * RADIUS:(jq + 1) * RADIUS])
            ks.append(kf[jq * RADIUS:jq * RADIUS + win])
            vs.append(vf[jq * RADIUS:jq * RADIUS + win])
    qb, kb, vb = jnp.stack(qs), jnp.stack(ks), jnp.stack(vs)

    shape = (nb, RADIUS, win)
    jq_of = lax.broadcasted_iota(jnp.int32, shape, 0) & (n_q - 1)
    qi = lax.broadcasted_iota(jnp.int32, shape, 1)
    kj = lax.broadcasted_iota(jnp.int32, shape, 2)
    lo = jnp.where(jq_of == 0, jnp.where(is_first, RADIUS, 0), 0)
    hi = jnp.where(jq_of == n_q - 1, jnp.where(is_last, 2 * RADIUS, win), win)
    mask = (jnp.abs(kj - RADIUS - qi) <= RADIUS) & (kj >= lo) & (kj < hi)

    s = jnp.einsum('bqd,bkd->bqk', qb, kb, preferred_element_type=jnp.float32) * (HEAD_DIM ** -0.5)
    s = jnp.where(mask, s + bias[None], NEG_INF)
    m = jnp.max(s, axis=-1, keepdims=True)
    p = jnp.exp(s - m)
    l = jnp.sum(p, axis=-1, keepdims=True)
    o = jnp.einsum('bqk,bkd->bqd', p.astype(jnp.bfloat16), vb, preferred_element_type=jnp.float32) / l
    lse = m + jnp.log(l)
    for bi in range(nb):
        r, jq = divmod(bi, n_q)
        start = jq * RADIUS * dil + r
        idx = pl.ds(start, RADIUS) if dil == 1 else pl.ds(start, RADIUS, stride=dil)
        o_dst[idx, :] = o[bi]
        l_dst[idx, :] = jnp.broadcast_to(lse[bi], (RADIUS, HEAD_DIM))


def _attn_kernel(*refs, n_row_blocks):
    n_g = N_ATTN_GROUPS
    in_refs = refs[:7 * n_g]
    bias_refs = refs[7 * n_g:8 * n_g]
    y_ref = refs[8 * n_g]
    o_s, l_s = refs[8 * n_g + 1:]
    i = pl.program_id(1)
    for g, (_, dil) in enumerate(DILATED_CONFIGS):
        _attn_group(in_refs[7 * g:7 * g + 7], bias_refs[g][0], o_s.at[g], l_s.at[g],
                    dil=dil, is_first=i == 0, is_last=i == n_row_blocks - 1)
    l0, l1, l2 = l_s[0], l_s[1], l_s[2]
    m = jnp.maximum(jnp.maximum(l0, l1), l2)
    e0, e1, e2 = jnp.exp(l0 - m), jnp.exp(l1 - m), jnp.exp(l2 - m)
    y = (e0 * o_s[0] + e1 * o_s[1] + e2 * o_s[2]) / (e0 + e1 + e2)
    y_ref[0] = y.astype(y_ref.dtype)


def dilated_attention(proj3, biases):
    b, s, _ = proj3.shape
    rows = ATTN_ROWS
    nrb = s // rows
    w = HEAD_DIM
    nh = HEADS_PER_GROUP

    def cur_map(col):
        return lambda bi, i, h: (bi, i, col + h)

    in_specs, operands = [], []
    for g, (_, dil) in enumerate(DILATED_CONFIGS):
        halo = RADIUS * dil
        ratio = rows // halo
        n_halo_blocks = s // halo
        qc, kc, vc = COL_Q // w + g * nh, COL_K // w + g * nh, COL_V // w + g * nh

        def prev_map(col, ratio=ratio):
            return lambda bi, i, h: (bi, jnp.maximum(i * ratio - 1, 0), col + h)

        def next_map(col, ratio=ratio, last=n_halo_blocks - 1):
            return lambda bi, i, h: (bi, jnp.minimum((i + 1) * ratio, last), col + h)

        in_specs += [pl.BlockSpec((1, rows, w), cur_map(qc)),
                     pl.BlockSpec((1, halo, w), prev_map(kc)),
                     pl.BlockSpec((1, rows, w), cur_map(kc)),
                     pl.BlockSpec((1, halo, w), next_map(kc)),
                     pl.BlockSpec((1, halo, w), prev_map(vc)),
                     pl.BlockSpec((1, rows, w), cur_map(vc)),
                     pl.BlockSpec((1, halo, w), next_map(vc))]
        operands += [proj3] * 7
    in_specs += [pl.BlockSpec((1, RADIUS, 3 * RADIUS), lambda bi, i, h: (h, 0, 0))] * N_ATTN_GROUPS
    operands += list(biases)
    return pl.pallas_call(
        functools.partial(_attn_kernel, n_row_blocks=nrb),
        grid=(b, nrb, nh),
        in_specs=in_specs,
        out_specs=pl.BlockSpec((1, rows, w), cur_map(0)),
        out_shape=jax.ShapeDtypeStruct((b, s, D_ATTN_OUT), jnp.bfloat16),
        scratch_shapes=[pltpu.VMEM((N_ATTN_GROUPS, rows, w), jnp.float32),
                        pltpu.VMEM((N_ATTN_GROUPS, rows, w), jnp.float32)],
        compiler_params=_cparams(3),
        name="dilated_attention",
    )(*operands)


def _layer_norm(z, g, b):
    mu = jnp.mean(z, axis=-1, keepdims=True)
    zc = z - mu
    var = jnp.mean(zc * zc, axis=-1, keepdims=True)
    return zc * lax.rsqrt(var + LN_EPS) * g + b


def _mix_kernel(x_ref, grnn_ref, gattn_ref, h_ref, y_ref,
                wr_ref, wa_ref, wo_ref, lng_ref, lnb_ref, rw_ref, rb_ref,
                xo_ref, eidx_ref, gate_ref):
    rnn = jnp.dot(h_ref[...], wr_ref[...], preferred_element_type=jnp.float32)
    att = jnp.dot(y_ref[...], wa_ref[...], preferred_element_type=jnp.float32)
    mixed = jax.nn.sigmoid(grnn_ref[...]) * rnn + jax.nn.sigmoid(gattn_ref[...]) * att
    z = ALPHA * x_ref[...] + jnp.dot(mixed.astype(jnp.bfloat16), wo_ref[...],
                                     preferred_element_type=jnp.float32)
    x1 = _layer_norm(z, lng_ref[...], lnb_ref[...])
    xo_ref[...] = x1

    x_hi = x1.astype(jnp.bfloat16)
    x_lo = (x1 - x_hi.astype(jnp.float32)).astype(jnp.bfloat16)
    t = jnp.dot(x_hi, rw_ref[...], preferred_element_type=jnp.float32)
    logits = (t[:, :ROUTE_W] + t[:, ROUTE_W:]
              + jnp.dot(x_lo, rw_ref[:, :ROUTE_W], preferred_element_type=jnp.float32)) + rb_ref[...]
    tm = logits.shape[0]
    lane = lax.broadcasted_iota(jnp.int32, (tm, ROUTE_W), 1)
    lane_f = lane.astype(jnp.float32)
    far = float(ROUTE_W)
    is_group = lane < N_EXPERT_GROUPS
    gl = jnp.where(is_group, logits, -jnp.inf)
    gmax = jnp.max(gl, axis=-1, keepdims=True)
    gsel = jnp.min(jnp.where(gl == gmax, lane_f, far), axis=-1, keepdims=True)
    p_group = 1.0 / jnp.sum(jnp.where(is_group, jnp.exp(logits - gmax), 0.0), axis=-1, keepdims=True)
    lane_group = ((lane - N_EXPERT_GROUPS) >> 3).astype(jnp.float32)
    in_group = (lane >= N_EXPERT_GROUPS) & (lane < N_EXPERT_GROUPS + N_EXPERTS) & (lane_group == gsel)
    el = jnp.where(in_group, logits, -jnp.inf)
    m1 = jnp.max(el, axis=-1, keepdims=True)
    i1 = jnp.min(jnp.where(el == m1, lane_f, far), axis=-1, keepdims=True)
    el2 = jnp.where(lane_f == i1, -jnp.inf, el)
    m2 = jnp.max(el2, axis=-1, keepdims=True)
    i2 = jnp.min(jnp.where(el2 == m2, lane_f, far), axis=-1, keepdims=True)
    e21 = jnp.exp(m2 - m1)
    den = 1.0 + e21
    g1 = p_group * (1.0 / den)
    g2 = p_group * (e21 / den)
    eidx = jnp.where(lane == 0, i1, jnp.where(lane == 1, i2, float(N_EXPERT_GROUPS))) - float(N_EXPERT_GROUPS)
    eidx_ref[...] = eidx.astype(jnp.int32)
    gate_ref[...] = jnp.where(lane == 0, g1, jnp.where(lane == 1, g2, 0.0))


def mix_and_route(x, proj, h_rnn, y_attn, wr, wa, wo, lng, lnb, rw, rb, tm=512):
    n = x.shape[0]
    d = D_MODEL
    w = D_ATTN_OUT
    row = lambda width, col=0: pl.BlockSpec((tm, width), lambda i, c=col: (i, c))
    full = lambda a: pl.BlockSpec(a.shape, lambda i: (0,) * a.ndim)
    return pl.pallas_call(
        _mix_kernel,
        grid=(n // tm,),
        in_specs=[row(d), row(d, COL_GRNN // d), row(d, COL_GATTN // d), row(d), row(w),
                  full(wr), full(wa), full(wo), full(lng), full(lnb), full(rw), full(rb)],
        out_specs=[row(d), row(ROUTE_W), row(ROUTE_W)],
        out_shape=[jax.ShapeDtypeStruct((n, d), jnp.float32),
                   jax.ShapeDtypeStruct((n, ROUTE_W), jnp.int32),
                   jax.ShapeDtypeStruct((n, ROUTE_W), jnp.float32)],
        compiler_params=_cparams(1),
        name="mix_and_route",
    )(x, proj, proj, h_rnn, y_attn, wr, wa, wo, lng, lnb, rw, rb)


def _rank_kernel(e_ref, dest_ref, cnt_ref, carry, *, tm):
    ph = pl.program_id(0)
    i = pl.program_id(1)
    lanes = ROUTE_W

    @pl.when((ph == 0) & (i == 0))
    def _():
        carry[...] = jnp.zeros_like(carry)

    @pl.when((ph == 1) & (i == 0))
    def _():
        cnt = carry[...]
        cnt_ref[...] = jnp.broadcast_to(cnt, (lanes, lanes)).astype(jnp.int32)
        padded = jnp.floor((cnt + (MOE_BLK - 1)) * (1.0 / MOE_BLK)) * MOE_BLK
        r = lax.broadcasted_iota(jnp.int32, (lanes, lanes), 0)
        c = lax.broadcasted_iota(jnp.int32, (lanes, lanes), 1)
        lower = (c < r).astype(jnp.float32)
        pstart = jnp.dot(lower, jnp.broadcast_to(padded, (lanes, lanes)),
                         preferred_element_type=jnp.float32, precision=lax.Precision.HIGHEST)
        carry[...] = pstart[:, 0:1]

    r8 = lax.broadcasted_iota(jnp.int32, (SUBLANES, lanes), 0)
    c8 = lax.broadcasted_iota(jnp.int32, (SUBLANES, lanes), 1)
    pick = ((r8 == c8) & (r8 < TOP_K)).astype(jnp.bfloat16)
    ef = e_ref[...].astype(jnp.float32).astype(jnp.bfloat16)
    et = lax.dot_general(pick, ef, (((1,), (1,)), ((), ())), preferred_element_type=jnp.float32)
    e0, e1 = et[0:1, :], et[1:2, :]
    sub = lax.broadcasted_iota(jnp.int32, (lanes, tm), 0).astype(jnp.float32)
    is0, is1 = sub == e0, sub == e1
    member = jnp.where(is0 | is1, 1.0, 0.0)

    @pl.when(ph == 1)
    def _():
        s_i = lax.broadcasted_iota(jnp.int32, (tm, tm), 0)
        t_i = lax.broadcasted_iota(jnp.int32, (tm, tm), 1)
        earlier = (s_i < t_i).astype(jnp.bfloat16)
        before = jnp.dot(member.astype(jnp.bfloat16), earlier, preferred_element_type=jnp.float32)
        pos = before + carry[...]
        d0 = jnp.sum(jnp.where(is0, pos, 0.0), axis=0, keepdims=True)
        d1 = jnp.sum(jnp.where(is1, pos, 0.0), axis=0, keepdims=True)
        row = lax.broadcasted_iota(jnp.int32, (SUBLANES, tm), 0)
        dest_ref[...] = jnp.where(row == 0, d0, jnp.where(row == 1, d1, 0.0)).astype(jnp.int32)

    carry[...] = carry[...] + jnp.sum(member, axis=1, keepdims=True)


def route_slots(eidx, tm=512):
    n = eidx.shape[0]
    nt = n // tm
    return pl.pallas_call(
        functools.partial(_rank_kernel, tm=tm),
        grid=(2, nt),
        in_specs=[pl.BlockSpec((tm, ROUTE_W), lambda ph, i: (i, 0))],
        out_specs=[pl.BlockSpec((SUBLANES, tm), lambda ph, i: (0, i * ph)),
                   pl.BlockSpec((ROUTE_W, ROUTE_W), lambda ph, i: (0, 0))],
        out_shape=[jax.ShapeDtypeStruct((SUBLANES, n), jnp.int32),
                   jax.ShapeDtypeStruct((ROUTE_W, ROUTE_W), jnp.int32)],
        scratch_shapes=[pltpu.VMEM((ROUTE_W, 1), jnp.float32)],
        compiler_params=_cparams(2),
        name="route_slots",
    )(eidx)


def _slot_map_kernel(dest_ref, slot_ref, *, n, slots):
    def init(j, c):
        slot_ref[j] = 0
        return c

    lax.fori_loop(0, slots, init, 0, unroll=32)

    def body(t, c):
        slot_ref[dest_ref[t]] = t
        slot_ref[dest_ref[n + t]] = t
        return c

    lax.fori_loop(0, n, body, 0, unroll=16)


def slot_map(dest_flat, n, slots):
    return pl.pallas_call(
        functools.partial(_slot_map_kernel, n=n, slots=slots),
        in_specs=[pl.BlockSpec(memory_space=pltpu.SMEM)],
        out_specs=pl.BlockSpec(memory_space=pltpu.SMEM),
        out_shape=jax.ShapeDtypeStruct((slots,), jnp.int32),
        name="slot_map",
    )(dest_flat)


def _expert_kernel(be_ref, nused_ref, st_ref, x_hbm, wg_ref, wu_ref, wd_ref, y_ref, xs, land, sem):
    i = pl.program_id(0)
    n_used = nused_ref[0]

    def gather_block(blk):
        base = blk * MOE_BLK
        for j in range(MOE_BLK):
            pltpu.make_async_copy(x_hbm.at[pl.ds(st_ref[base + j], 1)], land.at[pl.ds(j, 1)],
                                  sem.at[0]).start(priority=j % DMA_QUEUES)

    def wait_block():
        pltpu.make_async_copy(x_hbm.at[pl.ds(0, MOE_BLK)], land, sem.at[0]).wait()

    @pl.when(i == 0)
    def _():
        gather_block(0)

    @pl.when(i < n_used)
    def _():
        wait_block()
        xs[...] = land[...].astype(jnp.bfloat16)

    @pl.when(i + 1 <= n_used)
    def _():
        gather_block(i + 1)

    @pl.when(n_used > i)
    def _():
        xb = xs[...]
        gate = jnp.dot(xb, wg_ref[0], preferred_element_type=jnp.float32)
        up = jnp.dot(xb, wu_ref[0], preferred_element_type=jnp.float32)
        hid = (jax.nn.silu(gate) * up).astype(jnp.bfloat16)
        y_ref[...] = jnp.dot(hid, wd_ref[0], preferred_element_type=jnp.float32)

    @pl.when(i == n_used - 1)
    def _():
        wait_block()

    @pl.when(i >= n_used)
    def _():
        y_ref[...] = jnp.zeros_like(y_ref)


def expert_blocks(block_e, n_used, slot_tok, x1, wg, wu, wd):
    n, d = x1.shape
    n_blocks = slot_tok.shape[0] // MOE_BLK - 1
    slots = n_blocks * MOE_BLK
    grid_spec = pltpu.PrefetchScalarGridSpec(
        num_scalar_prefetch=3,
        grid=(n_blocks,),
        in_specs=[pl.BlockSpec(memory_space=pl.ANY),
                  pl.BlockSpec((1, d, D_EXPERT), lambda i, be, nu, st: (be[i], 0, 0)),
                  pl.BlockSpec((1, d, D_EXPERT), lambda i, be, nu, st: (be[i], 0, 0)),
                  pl.BlockSpec((1, D_EXPERT, d), lambda i, be, nu, st: (be[i], 0, 0))],
        out_specs=pl.BlockSpec((MOE_BLK, d), lambda i, be, nu, st: (i, 0)),
        scratch_shapes=[pltpu.VMEM((MOE_BLK, d), jnp.bfloat16),
                        pltpu.VMEM((MOE_BLK, d), jnp.float32),
                        pltpu.SemaphoreType.DMA((1,))],
    )
    return pl.pallas_call(
        _expert_kernel,
        grid_spec=grid_spec,
        out_shape=jax.ShapeDtypeStruct((slots, d), jnp.float32),
        compiler_params=_cparams(1),
        name="expert_blocks",
    )(block_e, n_used, slot_tok, x1, wg, wu, wd)


def _combine_kernel(dest_ref, x_ref, gate_ref, y_hbm, g_ref, b_ref, xo_ref, xob_ref, fbuf, sem, *, n, tm):
    i = pl.program_id(0)
    nt = pl.num_programs(0)

    def gather_tile(tile, slot):
        base = tile * tm
        for k in range(TOP_K):
            for t in range(tm):
                src = dest_ref[k * n + base + t]
                pltpu.make_async_copy(y_hbm.at[pl.ds(src, 1)], fbuf.at[slot, pl.ds(k * tm + t, 1)],
                                      sem.at[slot]).start(priority=t % DMA_QUEUES)

    def wait_tile(slot):
        pltpu.make_async_copy(y_hbm.at[pl.ds(0, TOP_K * tm)], fbuf.at[slot], sem.at[slot]).wait()

    @pl.when(i == 0)
    def _():
        gather_tile(0, 0)

    slot = i % 2
    wait_tile(slot)
    f0 = fbuf[slot, 0:tm, :]
    f1 = fbuf[slot, tm:2 * tm, :]
    gather_tile(jnp.minimum(i + 1, nt - 1), 1 - slot)
    ffn = gate_ref[:, 0:1] * f0 + gate_ref[:, 1:2] * f1
    x2 = _layer_norm(ALPHA * x_ref[...] + ffn, g_ref[...], b_ref[...])
    xo_ref[...] = x2
    xob_ref[...] = x2.astype(jnp.bfloat16)

    @pl.when(i == nt - 1)
    def _():
        wait_tile(1 - slot)


def combine_ln2(dest_flat, x1, gate, yb, g, b, tm=256):
    n, d = x1.shape
    grid_spec = pltpu.PrefetchScalarGridSpec(
        num_scalar_prefetch=1,
        grid=(n // tm,),
        in_specs=[pl.BlockSpec((tm, d), lambda i, ds: (i, 0)),
                  pl.BlockSpec((tm, ROUTE_W), lambda i, ds: (i, 0)),
                  pl.BlockSpec(memory_space=pl.ANY),
                  pl.BlockSpec((1, d), lambda i, ds: (0, 0)),
                  pl.BlockSpec((1, d), lambda i, ds: (0, 0))],
        out_specs=[pl.BlockSpec((tm, d), lambda i, ds: (i, 0)),
                   pl.BlockSpec((tm, d), lambda i, ds: (i, 0))],
        scratch_shapes=[pltpu.VMEM((2, TOP_K * tm, d), jnp.float32),
                        pltpu.SemaphoreType.DMA((2,))],
    )
    return pl.pallas_call(
        functools.partial(_combine_kernel, n=n, tm=tm),
        grid_spec=grid_spec,
        out_shape=[jax.ShapeDtypeStruct((n, d), jnp.float32),
                   jax.ShapeDtypeStruct((n, d), jnp.bfloat16)],
        compiler_params=_cparams(1),
        name="combine_ln2",
    )(dest_flat, x1, gate, yb, g, b)


def moe_layer(x1, eidx, gate, wg, wu, wd, ln_g, ln_b):
    n = eidx.shape[0]
    m = n * TOP_K
    n_blocks = m // MOE_BLK + N_EXPERTS
    slots = n_blocks * MOE_BLK
    dest8, cnt = route_slots(eidx)
    dest_flat = dest8[:TOP_K].reshape(m)
    counts = cnt[:N_EXPERTS, 0]
    pend = jnp.cumsum((counts + MOE_BLK - 1) // MOE_BLK * MOE_BLK)
    blk_start = jnp.arange(n_blocks, dtype=jnp.int32) * MOE_BLK
    block_e = jnp.minimum(jnp.sum((pend[None, :] <= blk_start[:, None]).astype(jnp.int32), axis=1),
                          N_EXPERTS - 1)
    n_used = (pend[-1:] // MOE_BLK).astype(jnp.int32)
    slot_tok = slot_map(dest_flat, n, slots + MOE_BLK)
    yb = expert_blocks(block_e, n_used, slot_tok, x1, wg, wu, wd)
    return combine_ln2(dest_flat, x1, gate, yb, ln_g, ln_b)


def _t5_bucket(rel):
    half = NUM_BUCKETS // 2
    max_exact = half // 2
    n = np.abs(rel)
    large = max_exact + (np.log(np.maximum(n, 1) / max_exact) / np.log(MAX_DISTANCE / max_exact)
                         * (half - max_exact)).astype(np.int32)
    large = np.minimum(large, half - 1)
    return np.where(rel > 0, half, 0) + np.where(n < max_exact, n, large)


def _bias_table(rel_bias, g, dil):
    rel = (np.arange(3 * RADIUS)[None, :] - RADIUS - np.arange(RADIUS)[:, None]) * dil
    hs = slice(g * HEADS_PER_GROUP, (g + 1) * HEADS_PER_GROUP)
    onehot = jnp.asarray(_t5_bucket(rel)[..., None] == np.arange(NUM_BUCKETS), jnp.float32)
    return jnp.einsum('qkb,bh->hqk', onehot, rel_bias[:, hs].astype(jnp.float32),
                      precision=lax.Precision.HIGHEST)


def _chunk_block_diag(w):
    per = RNN_CHUNK // RNN_BLOCK_W
    n_chunks = RNN_BLOCKS // per
    w4 = w.reshape(n_chunks, per, RNN_BLOCK_W, RNN_BLOCK_W)
    dense = jnp.einsum('chij,hk->chikj', w4, jnp.eye(per, dtype=w.dtype))
    return dense.reshape(n_chunks, RNN_CHUNK, RNN_CHUNK)


def _gate_weights(wa, wi):
    per_dir = [jnp.concatenate([_chunk_block_diag(wa[d]), _chunk_block_diag(wi[d])], axis=-1)
               for d in range(2)]
    return jnp.stack(per_dir, axis=1).astype(jnp.bfloat16)


def _per_chunk(v):
    return v.reshape(2, D_RNN // RNN_CHUNK, 1, RNN_CHUNK).transpose(1, 0, 2, 3)


def _split_hi_lo(w):
    hi = w.astype(jnp.bfloat16)
    lo = (w - hi.astype(jnp.float32)).astype(jnp.bfloat16)
    return jnp.concatenate([hi, lo], axis=1)


def _permute_in_cols(a):
    x_rnn, q, k, v, g_rnn, g_attn = jnp.split(
        a, [D_RNN, D_RNN + D_ATTN, D_RNN + 2 * D_ATTN, D_RNN + 3 * D_ATTN,
            D_RNN + 3 * D_ATTN + D_MODEL], axis=-1)
    return jnp.concatenate([x_rnn, g_rnn, g_attn, q, k, v], axis=-1)


def _trunk(x, biases, lp):
    b, s, d = x.shape
    n = b * s
    x2 = x.reshape(n, d)
    x2_bf = x2.astype(jnp.bfloat16)
    for l in range(DEPTH):
        p = lp[l]
        proj = in_projection(x2_bf, p['w_in'], p['b_in'])
        proj3 = proj.reshape(b, s, D_IN)
        h_rnn = rglru_branch(proj3, p['conv_w'], p['conv_b'], p['wg'], p['gb'], p['lam'])
        y_attn = dilated_attention(proj3, biases)
        x1, eidx, gate = mix_and_route(
            x2, proj, h_rnn.reshape(n, D_RNN), y_attn.reshape(n, D_ATTN_OUT),
            p['w_rnn_out'], p['w_attn_out'], p['w_o'],
            p['ln1_g'], p['ln1_b'], p['router_w'], p['router_b'])
        x2, x2_bf = moe_layer(x1, eidx, gate, p['w_gate'], p['w_up'], p['w_down'], p['ln2_g'], p['ln2_b'])
    return x2.reshape(b, s, d)


def kernel(x_prompt, x_sample, rel_bias, w_in, b_in, conv_w, conv_b, rg_wa, rg_ba, rg_wi, rg_bi, rg_lam, w_rnn_out, w_attn_out, w_o, ln1_g, ln1_b, router_w, router_b, expert_router_w, expert_router_b, w_gate, w_up, w_down, ln2_g, ln2_b):
    bf = jnp.bfloat16
    biases = [_bias_table(rel_bias, g, dil) for g, (_, dil) in enumerate(DILATED_CONFIGS)]
    lp = []
    for l in range(DEPTH):
        rw = jnp.concatenate(
            [router_w[l], jnp.transpose(expert_router_w[l], (1, 0, 2)).reshape(D_MODEL, N_EXPERTS)], axis=1)
        rb = jnp.concatenate([router_b[l], expert_router_b[l].reshape(N_EXPERTS)])
        pad = ROUTE_W - rw.shape[1]
        lp.append(dict(
            w_in=_permute_in_cols(w_in[l]).astype(bf),
            b_in=_permute_in_cols(b_in[l])[None, :],
            conv_w=conv_w[l], conv_b=conv_b[l][None, :],
            wg=_gate_weights(rg_wa[l], rg_wi[l]),
            gb=jnp.concatenate([_per_chunk(rg_ba[l]), _per_chunk(rg_bi[l])], axis=-1),
            lam=_per_chunk(rg_lam[l]),
            w_rnn_out=w_rnn_out[l].astype(bf), w_attn_out=w_attn_out[l].astype(bf), w_o=w_o[l].astype(bf),
            ln1_g=ln1_g[l][None, :], ln1_b=ln1_b[l][None, :],
            router_w=_split_hi_lo(jnp.pad(rw, ((0, 0), (0, pad)))), router_b=jnp.pad(rb, (0, pad))[None, :],
            w_gate=w_gate[l].astype(bf), w_up=w_up[l].astype(bf), w_down=w_down[l].astype(bf),
            ln2_g=ln2_g[l][None, :], ln2_b=ln2_b[l][None, :]))
    return (_trunk(x_prompt, biases, lp), _trunk(x_sample, biases, lp))
```

```python
import functools

import numpy as np
import jax
import jax.numpy as jnp
from jax import lax
from jax.experimental import pallas as pl
from jax.experimental.pallas import tpu as pltpu

D_MODEL = 1024
DEPTH = 2
D_RNN = D_MODEL
RNN_BLOCKS = 16
RNN_BLOCK_W = D_RNN // RNN_BLOCKS
CONV_WIDTH = 4
RG_C = 8.0
DILATED_CONFIGS = ((128, 1), (512, 4), (2048, 16))
N_ATTN_GROUPS = len(DILATED_CONFIGS)
HEADS_PER_GROUP = 4
N_ATTN_HEADS = N_ATTN_GROUPS * HEADS_PER_GROUP
HEAD_DIM = 128
D_ATTN = N_ATTN_HEADS * HEAD_DIM
D_ATTN_OUT = HEADS_PER_GROUP * HEAD_DIM
NUM_BUCKETS = 32
MAX_DISTANCE = max(w for w, _ in DILATED_CONFIGS) // 2
NEG_INF = -1e30
D_IN = D_RNN + 3 * D_ATTN + 2 * D_MODEL
N_EXPERT_GROUPS = 4
EXPERTS_PER_GROUP = 8
N_EXPERTS = N_EXPERT_GROUPS * EXPERTS_PER_GROUP
TOP_K = 2
D_EXPERT = D_MODEL // 2
ALPHA = (2 * DEPTH) ** 0.25
LN_EPS = 1e-5

LANES = 128
SUBLANES = 8
VMEM_LIMIT = 56 * 1024 * 1024

COL_XRNN = 0
COL_GRNN = D_RNN
COL_GATTN = D_RNN + D_MODEL
COL_Q = D_RNN + 2 * D_MODEL
COL_K = COL_Q + D_ATTN
COL_V = COL_K + D_ATTN

RADIUS = 64
assert all(w // (2 * d) == RADIUS for w, d in DILATED_CONFIGS)
ATTN_ROWS = 2048
RNN_CHUNK = 256
RNN_TILE = 256
MOE_BLK = 256
DMA_QUEUES = 2
TOKEN_TILE = D_MODEL // LANES
assert TOKEN_TILE == SUBLANES
ROUTE_W = LANES


def _cparams(n_axes):
    return pltpu.CompilerParams(dimension_semantics=("arbitrary",) * n_axes,
                                vmem_limit_bytes=VMEM_LIMIT)


def _proj_kernel(x_ref, w_ref, b_ref, o_ref):
    acc = jnp.dot(x_ref[...], w_ref[...], preferred_element_type=jnp.float32)
    o_ref[...] = acc + b_ref[...]


def in_projection(x_bf, w_bf, b, tm=512, tn=3840):
    n, k = x_bf.shape
    nout = w_bf.shape[1]
    return pl.pallas_call(
        _proj_kernel,
        grid=(nout // tn, n // tm),
        in_specs=[pl.BlockSpec((tm, k), lambda j, i: (i, 0)),
                  pl.BlockSpec((k, tn), lambda j, i: (0, j)),
                  pl.BlockSpec((1, tn), lambda j, i: (0, j))],
        out_specs=pl.BlockSpec((tm, tn), lambda j, i: (i, j)),
        out_shape=jax.ShapeDtypeStruct((n, nout), jnp.float32),
        compiler_params=_cparams(2),
        name="in_projection",
    )(x_bf, w_bf, b)


def _rglru_kernel(x_ref, cw_ref, cb_ref, wg_ref, gb_ref, lam_ref, o_ref,
                  hsum, xc_s, a_s, u_s, h_s, *, seq, tile):
    c = RNN_CHUNK
    n_tiles = seq // tile
    n_grp = tile // SUBLANES
    big_rows = tile + 2 * SUBLANES
    row_in_grp = lax.broadcasted_iota(jnp.int32, (n_grp, SUBLANES, c), 1)

    def conv_tile(t0):
        centre = x_ref[0, pl.ds(t0, tile), :]
        p0 = pl.multiple_of(jnp.maximum(t0 - SUBLANES, 0), SUBLANES)
        n0 = pl.multiple_of(jnp.minimum(t0 + tile, seq - SUBLANES), SUBLANES)
        prev = jnp.where(t0 > 0, x_ref[0, pl.ds(p0, SUBLANES), :], 0.0)
        nxt = jnp.where(t0 + tile < seq, x_ref[0, pl.ds(n0, SUBLANES), :], 0.0)
        big = jnp.concatenate([prev, centre, nxt], axis=0)
        xc = cb_ref[...] + cw_ref[1:2, :] * centre
        for k in (0, 2, 3):
            off = k - 1
            tap = pltpu.roll(big, (-off) % big_rows, 0)[SUBLANES:SUBLANES + tile]
            xc = xc + cw_ref[k:k + 1, :] * tap
        return xc

    def gates(xc, dr):
        g = jnp.dot(xc.astype(jnp.bfloat16), wg_ref[0, dr],
                    preferred_element_type=jnp.float32) + gb_ref[0, dr]
        gate_r = jax.nn.sigmoid(g[:, :c])
        gate_i = jax.nn.sigmoid(g[:, c:])
        neg_lam = -lam_ref[0, dr]
        softplus = jnp.maximum(neg_lam, 0.0) + jnp.log1p(jnp.exp(-jnp.abs(neg_lam)))
        log_a = -RG_C * gate_r * softplus
        a = jnp.exp(log_a)
        th = jnp.tanh(log_a)
        one_minus_a2 = -2.0 * th / (1.0 - th)
        u = jnp.sqrt(one_minus_a2) * (gate_i * xc)
        return a, u

    def tile_prefix(a, u, reverse):
        a = a.reshape(n_grp, SUBLANES, c)
        u = u.reshape(n_grp, SUBLANES, c)
        for s in (1, 2, 4):
            if reverse:
                a_sh = pltpu.roll(a, SUBLANES - s, 1)
                u_sh = pltpu.roll(u, SUBLANES - s, 1)
                m = row_in_grp < SUBLANES - s
            else:
                a_sh = pltpu.roll(a, s, 1)
                u_sh = pltpu.roll(u, s, 1)
                m = row_in_grp >= s
            u = jnp.where(m, u + a * u_sh, u)
            a = jnp.where(m, a * a_sh, a)
        a_s[...] = a.reshape(tile, c)
        u_s[...] = u.reshape(tile, c)

    def run_direction(dr, reverse):
        def tile_body(ti, h):
            t_idx = (n_tiles - 1 - ti) if reverse else ti
            t0 = pl.multiple_of(t_idx * tile, tile)
            if reverse:
                xc = xc_s[pl.ds(t0, tile), :]
            else:
                xc = conv_tile(t0)
                xc_s[pl.ds(t0, tile), :] = xc
            a, u = gates(xc, dr)
            tile_prefix(a, u, reverse)

            def grp_body(gi, hc):
                g_idx = (n_grp - 1 - gi) if reverse else gi
                r0 = pl.multiple_of(g_idx * SUBLANES, SUBLANES)
                hg = u_s[pl.ds(r0, SUBLANES), :] + a_s[pl.ds(r0, SUBLANES), :] * hc
                h_s[pl.ds(r0, SUBLANES), :] = hg
                edge = hg[0:1, :] if reverse else hg[SUBLANES - 1:SUBLANES, :]
                return jnp.broadcast_to(edge, (SUBLANES, c))

            h = lax.fori_loop(0, n_grp, grp_body, h, unroll=4)
            if reverse:
                o_ref[0, pl.ds(t0, tile), :] = (hsum[pl.ds(t0, tile), :] + h_s[...]).astype(o_ref.dtype)
            else:
                hsum[pl.ds(t0, tile), :] = h_s[...]
            return h

        lax.fori_loop(0, n_tiles, tile_body, jnp.zeros((SUBLANES, c), jnp.float32))

    run_direction(0, False)
    run_direction(1, True)


def rglru_branch(proj3, cw, cb, wg, gb, lam):
    b, s, _ = proj3.shape
    c = RNN_CHUNK
    n_chunks = D_RNN // c
    kern = functools.partial(_rglru_kernel, seq=s, tile=RNN_TILE)
    return pl.pallas_call(
        kern,
        grid=(b, n_chunks),
        in_specs=[pl.BlockSpec((1, s, c), lambda bi, ci: (bi, 0, COL_XRNN // c + ci)),
                  pl.BlockSpec((CONV_WIDTH, c), lambda bi, ci: (0, ci)),
                  pl.BlockSpec((1, c), lambda bi, ci: (0, ci)),
                  pl.BlockSpec((1, 2, c, 2 * c), lambda bi, ci: (ci, 0, 0, 0)),
                  pl.BlockSpec((1, 2, 1, 2 * c), lambda bi, ci: (ci, 0, 0, 0)),
                  pl.BlockSpec((1, 2, 1, c), lambda bi, ci: (ci, 0, 0, 0))],
        out_specs=pl.BlockSpec((1, s, c), lambda bi, ci: (bi, 0, ci)),
        out_shape=jax.ShapeDtypeStruct((b, s, D_RNN), jnp.bfloat16),
        scratch_shapes=[pltpu.VMEM((s, c), jnp.float32),
                        pltpu.VMEM((s, c), jnp.float32),
                        pltpu.VMEM((RNN_TILE, c), jnp.float32),
                        pltpu.VMEM((RNN_TILE, c), jnp.float32),
                        pltpu.VMEM((RNN_TILE, c), jnp.float32)],
        compiler_params=_cparams(2),
        name="rglru_branch",
    )(proj3, cw, cb, wg, gb, lam)


def _attn_group(refs, bias, o_dst, l_dst, *, dil, is_first, is_last):
    q_ref, kp_ref, kc_ref, kn_ref, vp_ref, vc_ref, vn_ref = refs
    rows = ATTN_ROWS
    per_res = rows // dil
    n_q = per_res // RADIUS
    win = 3 * RADIUS
    nb = dil * n_q

    def fold(ref, r, n):
        if dil == 1:
            return ref[0, pl.ds(r, n), :]
        return ref[0, pl.ds(r, n, stride=dil), :]

    qs, ks, vs = [], [], []
    for r in range(dil):
        qf = fold(q_ref, r, per_res).astype(jnp.bfloat16)
        kf = jnp.concatenate([fold(kp_ref, r, RADIUS), fold(kc_ref, r, per_res),
                              fold(kn_ref, r, RADIUS)], axis=0).astype(jnp.bfloat16)
        vf = jnp.concatenate([fold(vp_ref, r, RADIUS), fold(vc_ref, r, per_res),
                              fold(vn_ref, r, RADIUS)], axis=0).astype(jnp.bfloat16)
        for jq in range(n_q):
            qs.append(qf[jq * RADIUS:(jq + 1) * RADIUS])
            ks.append(kf[jq * RADIUS:jq * RADIUS + win])
            vs.append(vf[jq * RADIUS:jq * RADIUS + win])
    qb, kb, vb = jnp.stack(qs), jnp.stack(ks), jnp.stack(vs)

    shape = (nb, RADIUS, win)
    jq_of = lax.broadcasted_iota(jnp.int32, shape, 0) & (n_q - 1)
    qi = lax.broadcasted_iota(jnp.int32, shape, 1)
    kj = lax.broadcasted_iota(jnp.int32, shape, 2)
    lo = jnp.where(jq_of == 0, jnp.where(is_first, RADIUS, 0), 0)
    hi = jnp.where(jq_of == n_q - 1, jnp.where(is_last, 2 * RADIUS, win), win)
    mask = (jnp.abs(kj - RADIUS - qi) <= RADIUS) & (kj >= lo) & (kj < hi)

    s = jnp.einsum('bqd,bkd->bqk', qb, kb, preferred_element_type=jnp.float32) * (HEAD_DIM ** -0.5)
    s = jnp.where(mask, s + bias[None], NEG_INF)
    m = jnp.max(s, axis=-1, keepdims=True)
    p = jnp.exp(s - m)
    l = jnp.sum(p, axis=-1, keepdims=True)
    o = jnp.einsum('bqk,bkd->bqd', p.astype(jnp.bfloat16), vb, preferred_element_type=jnp.float32) / l
    lse = m + jnp.log(l)
    for bi in range(nb):
        r, jq = divmod(bi, n_q)
        start = jq * RADIUS * dil + r
        idx = pl.ds(start, RADIUS) if dil == 1 else pl.ds(start, RADIUS, stride=dil)
        o_dst[idx, :] = o[bi]
        l_dst[idx, :] = jnp.broadcast_to(lse[bi], (RADIUS, HEAD_DIM))


def _attn_kernel(*refs, n_row_blocks):
    n_g = N_ATTN_GROUPS
    in_refs = refs[:7 * n_g]
    bias_refs = refs[7 * n_g:8 * n_g]
    y_ref = refs[8 * n_g]
    o_s, l_s = refs[8 * n_g + 1:]
    i = pl.program_id(1)
    for g, (_, dil) in enumerate(DILATED_CONFIGS):
        _attn_group(in_refs[7 * g:7 * g + 7], bias_refs[g][0], o_s.at[g], l_s.at[g],
                    dil=dil, is_first=i == 0, is_last=i == n_row_blocks - 1)
    l0, l1, l2 = l_s[0], l_s[1], l_s[2]
    m = jnp.maximum(jnp.maximum(l0, l1), l2)
    e0, e1, e2 = jnp.exp(l0 - m), jnp.exp(l1 - m), jnp.exp(l2 - m)
    y = (e0 * o_s[0] + e1 * o_s[1] + e2 * o_s[2]) / (e0 + e1 + e2)
    y_ref[0] = y.astype(y_ref.dtype)


def dilated_attention(proj3, biases):
    b, s, _ = proj3.shape
    rows = ATTN_ROWS
    nrb = s // rows
    w = HEAD_DIM
    nh = HEADS_PER_GROUP

    def cur_map(col):
        return lambda bi, i, h: (bi, i, col + h)

    in_specs, operands = [], []
    for g, (_, dil) in enumerate(DILATED_CONFIGS):
        halo = RADIUS * dil
        ratio = rows // halo
        n_halo_blocks = s // halo
        qc, kc, vc = COL_Q // w + g * nh, COL_K // w + g * nh, COL_V // w + g * nh

        def prev_map(col, ratio=ratio):
            return lambda bi, i, h: (bi, jnp.maximum(i * ratio - 1, 0), col + h)

        def next_map(col, ratio=ratio, last=n_halo_blocks - 1):
            return lambda bi, i, h: (bi, jnp.minimum((i + 1) * ratio, last), col + h)

        in_specs += [pl.BlockSpec((1, rows, w), cur_map(qc)),
                     pl.BlockSpec((1, halo, w), prev_map(kc)),
                     pl.BlockSpec((1, rows, w), cur_map(kc)),
                     pl.BlockSpec((1, halo, w), next_map(kc)),
                     pl.BlockSpec((1, halo, w), prev_map(vc)),
                     pl.BlockSpec((1, rows, w), cur_map(vc)),
                     pl.BlockSpec((1, halo, w), next_map(vc))]
        operands += [proj3] * 7
    in_specs += [pl.BlockSpec((1, RADIUS, 3 * RADIUS), lambda bi, i, h: (h, 0, 0))] * N_ATTN_GROUPS
    operands += list(biases)
    return pl.pallas_call(
        functools.partial(_attn_kernel, n_row_blocks=nrb),
        grid=(b, nrb, nh),
        in_specs=in_specs,
        out_specs=pl.BlockSpec((1, rows, w), cur_map(0)),
        out_shape=jax.ShapeDtypeStruct((b, s, D_ATTN_OUT), jnp.bfloat16),
        scratch_shapes=[pltpu.VMEM((N_ATTN_GROUPS, rows, w), jnp.float32),
                        pltpu.VMEM((N_ATTN_GROUPS, rows, w), jnp.float32)],
        compiler_params=_cparams(3),
        name="dilated_attention",
    )(*operands)


def _store_token_tiles(ref, val, first_chunk=0):
    rows = val.shape[0]
    for c in range(val.shape[1] // LANES):
        ref[pl.ds(first_chunk + c, rows, stride=TOKEN_TILE), :] = val[:, c * LANES:(c + 1) * LANES]


def _load_token_tiles(ref, start, rows):
    return [ref[pl.ds(start * TOKEN_TILE + c, rows, stride=TOKEN_TILE), :] for c in range(TOKEN_TILE)]


def _layer_norm(z, g, b):
    mu = jnp.mean(z, axis=-1, keepdims=True)
    zc = z - mu
    var = jnp.mean(zc * zc, axis=-1, keepdims=True)
    return zc * lax.rsqrt(var + LN_EPS) * g + b


def _mix_kernel(x_ref, grnn_ref, gattn_ref, h_ref, y_ref,
                wr_ref, wa_ref, wo_ref, lng_ref, lnb_ref, rw_ref, rb_ref,
                xo_ref, eidx_ref, gate_ref):
    rnn = jnp.dot(h_ref[...], wr_ref[...], preferred_element_type=jnp.float32)
    att = jnp.dot(y_ref[...], wa_ref[...], preferred_element_type=jnp.float32)
    mixed = jax.nn.sigmoid(grnn_ref[...]) * rnn + jax.nn.sigmoid(gattn_ref[...]) * att
    z = ALPHA * x_ref[...] + jnp.dot(mixed.astype(jnp.bfloat16), wo_ref[...],
                                     preferred_element_type=jnp.float32)
    x1 = _layer_norm(z, lng_ref[...], lnb_ref[...])
    _store_token_tiles(xo_ref, x1)

    x_hi = x1.astype(jnp.bfloat16)
    x_lo = (x1 - x_hi.astype(jnp.float32)).astype(jnp.bfloat16)
    t = jnp.dot(x_hi, rw_ref[...], preferred_element_type=jnp.float32)
    logits = (t[:, :ROUTE_W] + t[:, ROUTE_W:]
              + jnp.dot(x_lo, rw_ref[:, :ROUTE_W], preferred_element_type=jnp.float32)) + rb_ref[...]
    tm = logits.shape[0]
    lane = lax.broadcasted_iota(jnp.int32, (tm, ROUTE_W), 1)
    lane_f = lane.astype(jnp.float32)
    far = float(ROUTE_W)
    is_group = lane < N_EXPERT_GROUPS
    gl = jnp.where(is_group, logits, -jnp.inf)
    gmax = jnp.max(gl, axis=-1, keepdims=True)
    gsel = jnp.min(jnp.where(gl == gmax, lane_f, far), axis=-1, keepdims=True)
    p_group = 1.0 / jnp.sum(jnp.where(is_group, jnp.exp(logits - gmax), 0.0), axis=-1, keepdims=True)
    lane_group = ((lane - N_EXPERT_GROUPS) >> 3).astype(jnp.float32)
    in_group = (lane >= N_EXPERT_GROUPS) & (lane < N_EXPERT_GROUPS + N_EXPERTS) & (lane_group == gsel)
    el = jnp.where(in_group, logits, -jnp.inf)
    m1 = jnp.max(el, axis=-1, keepdims=True)
    i1 = jnp.min(jnp.where(el == m1, lane_f, far), axis=-1, keepdims=True)
    el2 = jnp.where(lane_f == i1, -jnp.inf, el)
    m2 = jnp.max(el2, axis=-1, keepdims=True)
    i2 = jnp.min(jnp.where(el2 == m2, lane_f, far), axis=-1, keepdims=True)
    e21 = jnp.exp(m2 - m1)
    den = 1.0 + e21
    g1 = p_group * (1.0 / den)
    g2 = p_group * (e21 / den)
    eidx = jnp.where(lane == 0, i1, jnp.where(lane == 1, i2, float(N_EXPERT_GROUPS))) - float(N_EXPERT_GROUPS)
    eidx_ref[...] = eidx.astype(jnp.int32)
    gate_ref[...] = jnp.where(lane == 0, g1, jnp.where(lane == 1, g2, 0.0))


def mix_and_route(x, proj, h_rnn, y_attn, wr, wa, wo, lng, lnb, rw, rb, tm=512):
    n = x.shape[0]
    d = D_MODEL
    w = D_ATTN_OUT
    row = lambda width, col=0: pl.BlockSpec((tm, width), lambda i, c=col: (i, c))
    full = lambda a: pl.BlockSpec(a.shape, lambda i: (0,) * a.ndim)
    return pl.pallas_call(
        _mix_kernel,
        grid=(n // tm,),
        in_specs=[row(d), row(d, COL_GRNN // d), row(d, COL_GATTN // d), row(d), row(w),
                  full(wr), full(wa), full(wo), full(lng), full(lnb), full(rw), full(rb)],
        out_specs=[pl.BlockSpec((tm * TOKEN_TILE, LANES), lambda i: (i, 0)), row(ROUTE_W), row(ROUTE_W)],
        out_shape=[jax.ShapeDtypeStruct((n * TOKEN_TILE, LANES), jnp.float32),
                   jax.ShapeDtypeStruct((n, ROUTE_W), jnp.int32),
                   jax.ShapeDtypeStruct((n, ROUTE_W), jnp.float32)],
        compiler_params=_cparams(1),
        name="mix_and_route",
    )(x, proj, proj, h_rnn, y_attn, wr, wa, wo, lng, lnb, rw, rb)


def _rank_kernel(e_ref, dest_ref, cnt_ref, carry, *, tm):
    ph = pl.program_id(0)
    i = pl.program_id(1)
    lanes = ROUTE_W

    @pl.when((ph == 0) & (i == 0))
    def _():
        carry[...] = jnp.zeros_like(carry)

    @pl.when((ph == 1) & (i == 0))
    def _():
        cnt = carry[...]
        cnt_ref[...] = jnp.broadcast_to(cnt, (lanes, lanes)).astype(jnp.int32)
        padded = jnp.floor((cnt + (MOE_BLK - 1)) * (1.0 / MOE_BLK)) * MOE_BLK
        r = lax.broadcasted_iota(jnp.int32, (lanes, lanes), 0)
        c = lax.broadcasted_iota(jnp.int32, (lanes, lanes), 1)
        lower = (c < r).astype(jnp.float32)
        pstart = jnp.dot(lower, jnp.broadcast_to(padded, (lanes, lanes)),
                         preferred_element_type=jnp.float32, precision=lax.Precision.HIGHEST)
        carry[...] = pstart[:, 0:1]

    r8 = lax.broadcasted_iota(jnp.int32, (SUBLANES, lanes), 0)
    c8 = lax.broadcasted_iota(jnp.int32, (SUBLANES, lanes), 1)
    pick = ((r8 == c8) & (r8 < TOP_K)).astype(jnp.bfloat16)
    ef = e_ref[...].astype(jnp.float32).astype(jnp.bfloat16)
    et = lax.dot_general(pick, ef, (((1,), (1,)), ((), ())), preferred_element_type=jnp.float32)
    e0, e1 = et[0:1, :], et[1:2, :]
    sub = lax.broadcasted_iota(jnp.int32, (lanes, tm), 0).astype(jnp.float32)
    is0, is1 = sub == e0, sub == e1
    member = jnp.where(is0 | is1, 1.0, 0.0)

    @pl.when(ph == 1)
    def _():
        s_i = lax.broadcasted_iota(jnp.int32, (tm, tm), 0)
        t_i = lax.broadcasted_iota(jnp.int32, (tm, tm), 1)
        earlier = (s_i < t_i).astype(jnp.bfloat16)
        before = jnp.dot(member.astype(jnp.bfloat16), earlier, preferred_element_type=jnp.float32)
        pos = before + carry[...]
        d0 = jnp.sum(jnp.where(is0, pos, 0.0), axis=0, keepdims=True)
        d1 = jnp.sum(jnp.where(is1, pos, 0.0), axis=0, keepdims=True)
        row = lax.broadcasted_iota(jnp.int32, (SUBLANES, tm), 0)
        dest_ref[...] = jnp.where(row == 0, d0, jnp.where(row == 1, d1, 0.0)).astype(jnp.int32)

    carry[...] = carry[...] + jnp.sum(member, axis=1, keepdims=True)


def route_slots(eidx, tm=512):
    n = eidx.shape[0]
    nt = n // tm
    return pl.pallas_call(
        functools.partial(_rank_kernel, tm=tm),
        grid=(2, nt),
        in_specs=[pl.BlockSpec((tm, ROUTE_W), lambda ph, i: (i, 0))],
        out_specs=[pl.BlockSpec((SUBLANES, tm), lambda ph, i: (0, i * ph)),
                   pl.BlockSpec((ROUTE_W, ROUTE_W), lambda ph, i: (0, 0))],
        out_shape=[jax.ShapeDtypeStruct((SUBLANES, n), jnp.int32),
                   jax.ShapeDtypeStruct((ROUTE_W, ROUTE_W), jnp.int32)],
        scratch_shapes=[pltpu.VMEM((ROUTE_W, 1), jnp.float32)],
        compiler_params=_cparams(2),
        name="route_slots",
    )(eidx)


def _slot_map_kernel(dest_ref, slot_ref, *, n, slots):
    def init(j, c):
        slot_ref[j] = 0
        return c

    lax.fori_loop(0, slots, init, 0, unroll=32)

    def body(t, c):
        slot_ref[dest_ref[t]] = t
        slot_ref[dest_ref[n + t]] = t
        return c

    lax.fori_loop(0, n, body, 0, unroll=16)


def slot_map(dest_flat, n, slots):
    return pl.pallas_call(
        functools.partial(_slot_map_kernel, n=n, slots=slots),
        in_specs=[pl.BlockSpec(memory_space=pltpu.SMEM)],
        out_specs=pl.BlockSpec(memory_space=pltpu.SMEM),
        out_shape=jax.ShapeDtypeStruct((slots,), jnp.int32),
        name="slot_map",
    )(dest_flat)


def _expert_kernel(be_ref, nused_ref, st_ref, x_hbm, wg_ref, wu_ref, wd_ref, y_ref, xs, land, sem):
    i = pl.program_id(0)
    n_used = nused_ref[0]

    def gather_block(blk):
        base = blk * MOE_BLK
        for j in range(MOE_BLK):
            src = pl.multiple_of(st_ref[base + j] * TOKEN_TILE, TOKEN_TILE)
            pltpu.make_async_copy(x_hbm.at[pl.ds(src, TOKEN_TILE)], land.at[pl.ds(j * TOKEN_TILE, TOKEN_TILE)],
                                  sem.at[0]).start(priority=j % DMA_QUEUES)

    def wait_block():
        pltpu.make_async_copy(x_hbm.at[pl.ds(0, MOE_BLK * TOKEN_TILE)], land, sem.at[0]).wait()

    @pl.when(i == 0)
    def _():
        gather_block(0)

    @pl.when(i < n_used)
    def _():
        wait_block()
        for c, chunk in enumerate(_load_token_tiles(land, 0, MOE_BLK)):
            xs[:, c * LANES:(c + 1) * LANES] = chunk.astype(jnp.bfloat16)

    @pl.when(i + 1 <= n_used)
    def _():
        gather_block(i + 1)

    @pl.when(n_used > i)
    def _():
        xb = xs[...]
        gate = jnp.dot(xb, wg_ref[0], preferred_element_type=jnp.float32)
        up = jnp.dot(xb, wu_ref[0], preferred_element_type=jnp.float32)
        hid = (jax.nn.silu(gate) * up).astype(jnp.bfloat16)
        _store_token_tiles(y_ref, jnp.dot(hid, wd_ref[0], preferred_element_type=jnp.float32))

    @pl.when(i == n_used - 1)
    def _():
        wait_block()

    @pl.when(i >= n_used)
    def _():
        y_ref[...] = jnp.zeros_like(y_ref)


def expert_blocks(block_e, n_used, slot_tok, x1t, wg, wu, wd):
    d = D_MODEL
    n_blocks = slot_tok.shape[0] // MOE_BLK - 1
    slots = n_blocks * MOE_BLK
    tile_rows = MOE_BLK * TOKEN_TILE
    grid_spec = pltpu.PrefetchScalarGridSpec(
        num_scalar_prefetch=3,
        grid=(n_blocks,),
        in_specs=[pl.BlockSpec(memory_space=pl.ANY),
                  pl.BlockSpec((1, d, D_EXPERT), lambda i, be, nu, st: (be[i], 0, 0)),
                  pl.BlockSpec((1, d, D_EXPERT), lambda i, be, nu, st: (be[i], 0, 0)),
                  pl.BlockSpec((1, D_EXPERT, d), lambda i, be, nu, st: (be[i], 0, 0))],
        out_specs=pl.BlockSpec((tile_rows, LANES), lambda i, be, nu, st: (i, 0)),
        scratch_shapes=[pltpu.VMEM((MOE_BLK, d), jnp.bfloat16),
                        pltpu.VMEM((tile_rows, LANES), jnp.float32),
                        pltpu.SemaphoreType.DMA((1,))],
    )
    return pl.pallas_call(
        _expert_kernel,
        grid_spec=grid_spec,
        out_shape=jax.ShapeDtypeStruct((slots * TOKEN_TILE, LANES), jnp.float32),
        compiler_params=_cparams(1),
        name="expert_blocks",
    )(block_e, n_used, slot_tok, x1t, wg, wu, wd)


def _combine_kernel(dest_ref, x_ref, gate_ref, y_hbm, g_ref, b_ref, xo_ref, xob_ref, fbuf, sem, *, n, tm):
    i = pl.program_id(0)
    nt = pl.num_programs(0)
    half = tm * TOKEN_TILE

    def gather_tile(tile, slot):
        base = tile * tm
        for k in range(TOP_K):
            for t in range(tm):
                src = pl.multiple_of(dest_ref[k * n + base + t] * TOKEN_TILE, TOKEN_TILE)
                pltpu.make_async_copy(y_hbm.at[pl.ds(src, TOKEN_TILE)],
                                      fbuf.at[slot, pl.ds((k * tm + t) * TOKEN_TILE, TOKEN_TILE)],
                                      sem.at[slot]).start(priority=t % DMA_QUEUES)

    def wait_tile(slot):
        pltpu.make_async_copy(y_hbm.at[pl.ds(0, TOP_K * half)], fbuf.at[slot], sem.at[slot]).wait()

    @pl.when(i == 0)
    def _():
        gather_tile(0, 0)

    slot = i % 2
    wait_tile(slot)
    g0, g1 = gate_ref[:, 0:1], gate_ref[:, 1:2]
    f0 = _load_token_tiles(fbuf.at[slot], 0, tm)
    f1 = _load_token_tiles(fbuf.at[slot], tm, tm)
    xs = _load_token_tiles(x_ref, 0, tm)
    gather_tile(jnp.minimum(i + 1, nt - 1), 1 - slot)
    z = [ALPHA * xs[c] + (g0 * f0[c] + g1 * f1[c]) for c in range(TOKEN_TILE)]
    mu = sum(jnp.sum(zc, axis=-1, keepdims=True) for zc in z) * (1.0 / D_MODEL)
    zc = [zz - mu for zz in z]
    var = sum(jnp.sum(v * v, axis=-1, keepdims=True) for v in zc) * (1.0 / D_MODEL)
    inv = lax.rsqrt(var + LN_EPS)
    for c in range(TOKEN_TILE):
        cols = slice(c * LANES, (c + 1) * LANES)
        x2 = zc[c] * inv * g_ref[:, cols] + b_ref[:, cols]
        xo_ref[:, cols] = x2
        xob_ref[:, cols] = x2.astype(jnp.bfloat16)

    @pl.when(i == nt - 1)
    def _():
        wait_tile(1 - slot)


def combine_ln2(dest_flat, x1t, gate, ybt, g, b, tm=256):
    n = gate.shape[0]
    d = D_MODEL
    tile_rows = tm * TOKEN_TILE
    grid_spec = pltpu.PrefetchScalarGridSpec(
        num_scalar_prefetch=1,
        grid=(n // tm,),
        in_specs=[pl.BlockSpec((tile_rows, LANES), lambda i, ds: (i, 0)),
                  pl.BlockSpec((tm, ROUTE_W), lambda i, ds: (i, 0)),
                  pl.BlockSpec(memory_space=pl.ANY),
                  pl.BlockSpec((1, d), lambda i, ds: (0, 0)),
                  pl.BlockSpec((1, d), lambda i, ds: (0, 0))],
        out_specs=[pl.BlockSpec((tm, d), lambda i, ds: (i, 0)),
                   pl.BlockSpec((tm, d), lambda i, ds: (i, 0))],
        scratch_shapes=[pltpu.VMEM((2, TOP_K * tile_rows, LANES), jnp.float32),
                        pltpu.SemaphoreType.DMA((2,))],
    )
    return pl.pallas_call(
        functools.partial(_combine_kernel, n=n, tm=tm),
        grid_spec=grid_spec,
        out_shape=[jax.ShapeDtypeStruct((n, d), jnp.float32),
                   jax.ShapeDtypeStruct((n, d), jnp.bfloat16)],
        compiler_params=_cparams(1),
        name="combine_ln2",
    )(dest_flat, x1t, gate, ybt, g, b)


def moe_layer(x1, eidx, gate, wg, wu, wd, ln_g, ln_b):
    n = eidx.shape[0]
    m = n * TOP_K
    n_blocks = m // MOE_BLK + N_EXPERTS
    slots = n_blocks * MOE_BLK
    dest8, cnt = route_slots(eidx)
    dest_flat = dest8[:TOP_K].reshape(m)
    counts = cnt[:N_EXPERTS, 0]
    pend = jnp.cumsum((counts + MOE_BLK - 1) // MOE_BLK * MOE_BLK)
    blk_start = jnp.arange(n_blocks, dtype=jnp.int32) * MOE_BLK
    block_e = jnp.minimum(jnp.sum((pend[None, :] <= blk_start[:, None]).astype(jnp.int32), axis=1),
                          N_EXPERTS - 1)
    n_used = (pend[-1:] // MOE_BLK).astype(jnp.int32)
    slot_tok = slot_map(dest_flat, n, slots + MOE_BLK)
    yb = expert_blocks(block_e, n_used, slot_tok, x1, wg, wu, wd)
    return combine_ln2(dest_flat, x1, gate, yb, ln_g, ln_b)


def _t5_bucket(rel):
    half = NUM_BUCKETS // 2
    max_exact = half // 2
    n = np.abs(rel)
    large = max_exact + (np.log(np.maximum(n, 1) / max_exact) / np.log(MAX_DISTANCE / max_exact)
                         * (half - max_exact)).astype(np.int32)
    large = np.minimum(large, half - 1)
    return np.where(rel > 0, half, 0) + np.where(n < max_exact, n, large)


def _bias_table(rel_bias, g, dil):
    rel = (np.arange(3 * RADIUS)[None, :] - RADIUS - np.arange(RADIUS)[:, None]) * dil
    hs = slice(g * HEADS_PER_GROUP, (g + 1) * HEADS_PER_GROUP)
    onehot = jnp.asarray(_t5_bucket(rel)[..., None] == np.arange(NUM_BUCKETS), jnp.float32)
    return jnp.einsum('qkb,bh->hqk', onehot, rel_bias[:, hs].astype(jnp.float32),
                      precision=lax.Precision.HIGHEST)


def _chunk_block_diag(w):
    per = RNN_CHUNK // RNN_BLOCK_W
    n_chunks = RNN_BLOCKS // per
    w4 = w.reshape(n_chunks, per, RNN_BLOCK_W, RNN_BLOCK_W)
    dense = jnp.einsum('chij,hk->chikj', w4, jnp.eye(per, dtype=w.dtype))
    return dense.reshape(n_chunks, RNN_CHUNK, RNN_CHUNK)


def _gate_weights(wa, wi):
    per_dir = [jnp.concatenate([_chunk_block_diag(wa[d]), _chunk_block_diag(wi[d])], axis=-1)
               for d in range(2)]
    return jnp.stack(per_dir, axis=1).astype(jnp.bfloat16)


def _per_chunk(v):
    return v.reshape(2, D_RNN // RNN_CHUNK, 1, RNN_CHUNK).transpose(1, 0, 2, 3)


def _split_hi_lo(w):
    hi = w.astype(jnp.bfloat16)
    lo = (w - hi.astype(jnp.float32)).astype(jnp.bfloat16)
    return jnp.concatenate([hi, lo], axis=1)


def _permute_in_cols(a):
    x_rnn, q, k, v, g_rnn, g_attn = jnp.split(
        a, [D_RNN, D_RNN + D_ATTN, D_RNN + 2 * D_ATTN, D_RNN + 3 * D_ATTN,
            D_RNN + 3 * D_ATTN + D_MODEL], axis=-1)
    return jnp.concatenate([x_rnn, g_rnn, g_attn, q, k, v], axis=-1)


def _trunk(x, biases, lp):
    b, s, d = x.shape
    n = b * s
    x2 = x.reshape(n, d)
    x2_bf = x2.astype(jnp.bfloat16)
    for l in range(DEPTH):
        p = lp[l]
        proj = in_projection(x2_bf, p['w_in'], p['b_in'])
        proj3 = proj.reshape(b, s, D_IN)
        h_rnn = rglru_branch(proj3, p['conv_w'], p['conv_b'], p['wg'], p['gb'], p['lam'])
        y_attn = dilated_attention(proj3, biases)
        x1, eidx, gate = mix_and_route(
            x2, proj, h_rnn.reshape(n, D_RNN), y_attn.reshape(n, D_ATTN_OUT),
            p['w_rnn_out'], p['w_attn_out'], p['w_o'],
            p['ln1_g'], p['ln1_b'], p['router_w'], p['router_b'])
        x2, x2_bf = moe_layer(x1, eidx, gate, p['w_gate'], p['w_up'], p['w_down'], p['ln2_g'], p['ln2_b'])
    return x2.reshape(b, s, d)


def kernel(x_prompt, x_sample, rel_bias, w_in, b_in, conv_w, conv_b, rg_wa, rg_ba, rg_wi, rg_bi, rg_lam, w_rnn_out, w_attn_out, w_o, ln1_g, ln1_b, router_w, router_b, expert_router_w, expert_router_b, w_gate, w_up, w_down, ln2_g, ln2_b):
    bf = jnp.bfloat16
    biases = [_bias_table(rel_bias, g, dil) for g, (_, dil) in enumerate(DILATED_CONFIGS)]
    lp = []
    for l in range(DEPTH):
        rw = jnp.concatenate(
            [router_w[l], jnp.transpose(expert_router_w[l], (1, 0, 2)).reshape(D_MODEL, N_EXPERTS)], axis=1)
        rb = jnp.concatenate([router_b[l], expert_router_b[l].reshape(N_EXPERTS)])
        pad = ROUTE_W - rw.shape[1]
        lp.append(dict(
            w_in=_permute_in_cols(w_in[l]).astype(bf),
            b_in=_permute_in_cols(b_in[l])[None, :],
            conv_w=conv_w[l], conv_b=conv_b[l][None, :],
            wg=_gate_weights(rg_wa[l], rg_wi[l]),
            gb=jnp.concatenate([_per_chunk(rg_ba[l]), _per_chunk(rg_bi[l])], axis=-1),
            lam=_per_chunk(rg_lam[l]),
            w_rnn_out=w_rnn_out[l].astype(bf), w_attn_out=w_attn_out[l].astype(bf), w_o=w_o[l].astype(bf),
            ln1_g=ln1_g[l][None, :], ln1_b=ln1_b[l][None, :],
            router_w=_split_hi_lo(jnp.pad(rw, ((0, 0), (0, pad)))), router_b=jnp.pad(rb, (0, pad))[None, :],
            w_gate=w_gate[l].astype(bf), w_up=w_up[l].astype(bf), w_down=w_down[l].astype(bf),
            ln2_g=ln2_g[l][None, :], ln2_b=ln2_b[l][None, :]))
    return (_trunk(x_prompt, biases, lp), _trunk(x_sample, biases, lp))
```

```python
import functools

import numpy as np
import jax
import jax.numpy as jnp
from jax import lax
from jax.experimental import pallas as pl
from jax.experimental.pallas import tpu as pltpu

D_MODEL = 1024
DEPTH = 2
D_RNN = D_MODEL
RNN_BLOCKS = 16
RNN_BLOCK_W = D_RNN // RNN_BLOCKS
CONV_WIDTH = 4
RG_C = 8.0
DILATED_CONFIGS = ((128, 1), (512, 4), (2048, 16))
N_ATTN_GROUPS = len(DILATED_CONFIGS)
HEADS_PER_GROUP = 4
N_ATTN_HEADS = N_ATTN_GROUPS * HEADS_PER_GROUP
HEAD_DIM = 128
D_ATTN = N_ATTN_HEADS * HEAD_DIM
D_ATTN_OUT = HEADS_PER_GROUP * HEAD_DIM
NUM_BUCKETS = 32
MAX_DISTANCE = max(w for w, _ in DILATED_CONFIGS) // 2
NEG_INF = -1e30
D_IN = D_RNN + 3 * D_ATTN + 2 * D_MODEL
N_EXPERT_GROUPS = 4
EXPERTS_PER_GROUP = 8
N_EXPERTS = N_EXPERT_GROUPS * EXPERTS_PER_GROUP
TOP_K = 2
D_EXPERT = D_MODEL // 2
ALPHA = (2 * DEPTH) ** 0.25
LN_EPS = 1e-5

LANES = 128
SUBLANES = 8
VMEM_LIMIT = 56 * 1024 * 1024

COL_XRNN = 0
COL_GRNN = D_RNN
COL_GATTN = D_RNN + D_MODEL
COL_Q = D_RNN + 2 * D_MODEL
COL_K = COL_Q + D_ATTN
COL_V = COL_K + D_ATTN

RADIUS = 64
assert all(w // (2 * d) == RADIUS for w, d in DILATED_CONFIGS)
ATTN_ROWS = 2048
RNN_CHUNK = 256
RNN_TILE = 256
MOE_BLK = 256
DMA_QUEUES = 2
TOKEN_TILE = D_MODEL // LANES
assert TOKEN_TILE == SUBLANES
ROUTE_W = LANES


def _cparams(n_axes):
    return pltpu.CompilerParams(dimension_semantics=("arbitrary",) * n_axes,
                                vmem_limit_bytes=VMEM_LIMIT)


def _proj_kernel(x_ref, w_ref, b_ref, o_ref):
    acc = jnp.dot(x_ref[...].astype(jnp.bfloat16), w_ref[...], preferred_element_type=jnp.float32)
    o_ref[...] = acc + b_ref[...]


def in_projection(x_bf, w_bf, b, tm=512, tn=3840):
    n, k = x_bf.shape
    nout = w_bf.shape[1]
    return pl.pallas_call(
        _proj_kernel,
        grid=(nout // tn, n // tm),
        in_specs=[pl.BlockSpec((tm, k), lambda j, i: (i, 0)),
                  pl.BlockSpec((k, tn), lambda j, i: (0, j)),
                  pl.BlockSpec((1, tn), lambda j, i: (0, j))],
        out_specs=pl.BlockSpec((tm, tn), lambda j, i: (i, j)),
        out_shape=jax.ShapeDtypeStruct((n, nout), jnp.float32),
        compiler_params=_cparams(2),
        name="in_projection",
    )(x_bf, w_bf, b)


def _rglru_kernel(x_ref, cw_ref, cb_ref, wg_ref, gb_ref, lam_ref, o_ref,
                  hsum, xc_s, a_s, u_s, h_s, *, seq, tile):
    c = RNN_CHUNK
    n_tiles = seq // tile
    n_grp = tile // SUBLANES
    big_rows = tile + 2 * SUBLANES
    row_in_grp = lax.broadcasted_iota(jnp.int32, (n_grp, SUBLANES, c), 1)

    def conv_tile(t0):
        centre = x_ref[0, pl.ds(t0, tile), :]
        p0 = pl.multiple_of(jnp.maximum(t0 - SUBLANES, 0), SUBLANES)
        n0 = pl.multiple_of(jnp.minimum(t0 + tile, seq - SUBLANES), SUBLANES)
        prev = jnp.where(t0 > 0, x_ref[0, pl.ds(p0, SUBLANES), :], 0.0)
        nxt = jnp.where(t0 + tile < seq, x_ref[0, pl.ds(n0, SUBLANES), :], 0.0)
        big = jnp.concatenate([prev, centre, nxt], axis=0)
        xc = cb_ref[...] + cw_ref[1:2, :] * centre
        for k in (0, 2, 3):
            off = k - 1
            tap = pltpu.roll(big, (-off) % big_rows, 0)[SUBLANES:SUBLANES + tile]
            xc = xc + cw_ref[k:k + 1, :] * tap
        return xc

    def gates(xc, dr):
        g = jnp.dot(xc.astype(jnp.bfloat16), wg_ref[0, dr],
                    preferred_element_type=jnp.float32) + gb_ref[0, dr]
        gate_r = jax.nn.sigmoid(g[:, :c])
        gate_i = jax.nn.sigmoid(g[:, c:])
        neg_lam = -lam_ref[0, dr]
        softplus = jnp.maximum(neg_lam, 0.0) + jnp.log1p(jnp.exp(-jnp.abs(neg_lam)))
        log_a = -RG_C * gate_r * softplus
        a = jnp.exp(log_a)
        th = jnp.tanh(log_a)
        one_minus_a2 = -2.0 * th / (1.0 - th)
        u = jnp.sqrt(one_minus_a2) * (gate_i * xc)
        return a, u

    def tile_prefix(a, u, reverse):
        a = a.reshape(n_grp, SUBLANES, c)
        u = u.reshape(n_grp, SUBLANES, c)
        for s in (1, 2, 4):
            if reverse:
                a_sh = pltpu.roll(a, SUBLANES - s, 1)
                u_sh = pltpu.roll(u, SUBLANES - s, 1)
                m = row_in_grp < SUBLANES - s
            else:
                a_sh = pltpu.roll(a, s, 1)
                u_sh = pltpu.roll(u, s, 1)
                m = row_in_grp >= s
            u = jnp.where(m, u + a * u_sh, u)
            a = jnp.where(m, a * a_sh, a)
        a_s[...] = a.reshape(tile, c)
        u_s[...] = u.reshape(tile, c)

    def run_direction(dr, reverse):
        def tile_body(ti, h):
            t_idx = (n_tiles - 1 - ti) if reverse else ti
            t0 = pl.multiple_of(t_idx * tile, tile)
            if reverse:
                xc = xc_s[pl.ds(t0, tile), :]
            else:
                xc = conv_tile(t0)
                xc_s[pl.ds(t0, tile), :] = xc
            a, u = gates(xc, dr)
            tile_prefix(a, u, reverse)

            def grp_body(gi, hc):
                g_idx = (n_grp - 1 - gi) if reverse else gi
                r0 = pl.multiple_of(g_idx * SUBLANES, SUBLANES)
                hg = u_s[pl.ds(r0, SUBLANES), :] + a_s[pl.ds(r0, SUBLANES), :] * hc
                h_s[pl.ds(r0, SUBLANES), :] = hg
                edge = hg[0:1, :] if reverse else hg[SUBLANES - 1:SUBLANES, :]
                return jnp.broadcast_to(edge, (SUBLANES, c))

            h = lax.fori_loop(0, n_grp, grp_body, h, unroll=4)
            if reverse:
                o_ref[0, pl.ds(t0, tile), :] = (hsum[pl.ds(t0, tile), :] + h_s[...]).astype(o_ref.dtype)
            else:
                hsum[pl.ds(t0, tile), :] = h_s[...]
            return h

        lax.fori_loop(0, n_tiles, tile_body, jnp.zeros((SUBLANES, c), jnp.float32))

    run_direction(0, False)
    run_direction(1, True)


def rglru_branch(proj3, cw, cb, wg, gb, lam):
    b, s, _ = proj3.shape
    c = RNN_CHUNK
    n_chunks = D_RNN // c
    kern = functools.partial(_rglru_kernel, seq=s, tile=RNN_TILE)
    return pl.pallas_call(
        kern,
        grid=(b, n_chunks),
        in_specs=[pl.BlockSpec((1, s, c), lambda bi, ci: (bi, 0, COL_XRNN // c + ci)),
                  pl.BlockSpec((CONV_WIDTH, c), lambda bi, ci: (0, ci)),
                  pl.BlockSpec((1, c), lambda bi, ci: (0, ci)),
                  pl.BlockSpec((1, 2, c, 2 * c), lambda bi, ci: (ci, 0, 0, 0)),
                  pl.BlockSpec((1, 2, 1, 2 * c), lambda bi, ci: (ci, 0, 0, 0)),
                  pl.BlockSpec((1, 2, 1, c), lambda bi, ci: (ci, 0, 0, 0))],
        out_specs=pl.BlockSpec((1, s, c), lambda bi, ci: (bi, 0, ci)),
        out_shape=jax.ShapeDtypeStruct((b, s, D_RNN), jnp.bfloat16),
        scratch_shapes=[pltpu.VMEM((s, c), jnp.float32),
                        pltpu.VMEM((s, c), jnp.float32),
                        pltpu.VMEM((RNN_TILE, c), jnp.float32),
                        pltpu.VMEM((RNN_TILE, c), jnp.float32),
                        pltpu.VMEM((RNN_TILE, c), jnp.float32)],
        compiler_params=_cparams(2),
        name="rglru_branch",
    )(proj3, cw, cb, wg, gb, lam)


def _attn_group(refs, bias, o_dst, l_dst, *, dil, is_first, is_last):
    q_ref, kp_ref, kc_ref, kn_ref, vp_ref, vc_ref, vn_ref = refs
    rows = ATTN_ROWS
    per_res = rows // dil
    n_q = per_res // RADIUS
    win = 3 * RADIUS
    nb = dil * n_q

    def fold(ref, r, n):
        if dil == 1:
            return ref[0, pl.ds(r, n), :]
        return ref[0, pl.ds(r, n, stride=dil), :]

    qs, ks, vs = [], [], []
    for r in range(dil):
        qf = fold(q_ref, r, per_res).astype(jnp.bfloat16)
        kf = jnp.concatenate([fold(kp_ref, r, RADIUS), fold(kc_ref, r, per_res),
                              fold(kn_ref, r, RADIUS)], axis=0).astype(jnp.bfloat16)
        vf = jnp.concatenate([fold(vp_ref, r, RADIUS), fold(vc_ref, r, per_res),
                              fold(vn_ref, r, RADIUS)], axis=0).astype(jnp.bfloat16)
        for jq in range(n_q):
            qs.append(qf[jq * RADIUS:(jq + 1) * RADIUS])
            ks.append(kf[jq * RADIUS:jq * RADIUS + win])
            vs.append(vf[jq * RADIUS:jq * RADIUS + win])
    qb, kb, vb = jnp.stack(qs), jnp.stack(ks), jnp.stack(vs)

    shape = (nb, RADIUS, win)
    jq_of = lax.broadcasted_iota(jnp.int32, shape, 0) & (n_q - 1)
    qi = lax.broadcasted_iota(jnp.int32, shape, 1)
    kj = lax.broadcasted_iota(jnp.int32, shape, 2)
    lo = jnp.where(jq_of == 0, jnp.where(is_first, RADIUS, 0), 0)
    hi = jnp.where(jq_of == n_q - 1, jnp.where(is_last, 2 * RADIUS, win), win)
    mask = (jnp.abs(kj - RADIUS - qi) <= RADIUS) & (kj >= lo) & (kj < hi)

    s = jnp.einsum('bqd,bkd->bqk', qb, kb, preferred_element_type=jnp.float32) * (HEAD_DIM ** -0.5)
    s = jnp.where(mask, s + bias[None], NEG_INF)
    m = jnp.max(s, axis=-1, keepdims=True)
    p = jnp.exp(s - m)
    l = jnp.sum(p, axis=-1, keepdims=True)
    o = jnp.einsum('bqk,bkd->bqd', p.astype(jnp.bfloat16), vb, preferred_element_type=jnp.float32) / l
    lse = m + jnp.log(l)
    for bi in range(nb):
        r, jq = divmod(bi, n_q)
        start = jq * RADIUS * dil + r
        idx = pl.ds(start, RADIUS) if dil == 1 else pl.ds(start, RADIUS, stride=dil)
        o_dst[idx, :] = o[bi]
        l_dst[idx, :] = jnp.broadcast_to(lse[bi], (RADIUS, HEAD_DIM))


def _attn_kernel(*refs, n_row_blocks):
    n_g = N_ATTN_GROUPS
    in_refs = refs[:7 * n_g]
    bias_refs = refs[7 * n_g:8 * n_g]
    y_ref = refs[8 * n_g]
    o_s, l_s = refs[8 * n_g + 1:]
    i = pl.program_id(1)
    for g, (_, dil) in enumerate(DILATED_CONFIGS):
        _attn_group(in_refs[7 * g:7 * g + 7], bias_refs[g][0], o_s.at[g], l_s.at[g],
                    dil=dil, is_first=i == 0, is_last=i == n_row_blocks - 1)
    l0, l1, l2 = l_s[0], l_s[1], l_s[2]
    m = jnp.maximum(jnp.maximum(l0, l1), l2)
    e0, e1, e2 = jnp.exp(l0 - m), jnp.exp(l1 - m), jnp.exp(l2 - m)
    y = (e0 * o_s[0] + e1 * o_s[1] + e2 * o_s[2]) / (e0 + e1 + e2)
    y_ref[0] = y.astype(y_ref.dtype)


def dilated_attention(proj3, biases):
    b, s, _ = proj3.shape
    rows = ATTN_ROWS
    nrb = s // rows
    w = HEAD_DIM
    nh = HEADS_PER_GROUP

    def cur_map(col):
        return lambda bi, i, h: (bi, i, col + h)

    in_specs, operands = [], []
    for g, (_, dil) in enumerate(DILATED_CONFIGS):
        halo = RADIUS * dil
        ratio = rows // halo
        n_halo_blocks = s // halo
        qc, kc, vc = COL_Q // w + g * nh, COL_K // w + g * nh, COL_V // w + g * nh

        def prev_map(col, ratio=ratio):
            return lambda bi, i, h: (bi, jnp.maximum(i * ratio - 1, 0), col + h)

        def next_map(col, ratio=ratio, last=n_halo_blocks - 1):
            return lambda bi, i, h: (bi, jnp.minimum((i + 1) * ratio, last), col + h)

        in_specs += [pl.BlockSpec((1, rows, w), cur_map(qc)),
                     pl.BlockSpec((1, halo, w), prev_map(kc)),
                     pl.BlockSpec((1, rows, w), cur_map(kc)),
                     pl.BlockSpec((1, halo, w), next_map(kc)),
                     pl.BlockSpec((1, halo, w), prev_map(vc)),
                     pl.BlockSpec((1, rows, w), cur_map(vc)),
                     pl.BlockSpec((1, halo, w), next_map(vc))]
        operands += [proj3] * 7
    in_specs += [pl.BlockSpec((1, RADIUS, 3 * RADIUS), lambda bi, i, h: (h, 0, 0))] * N_ATTN_GROUPS
    operands += list(biases)
    return pl.pallas_call(
        functools.partial(_attn_kernel, n_row_blocks=nrb),
        grid=(b, nrb, nh),
        in_specs=in_specs,
        out_specs=pl.BlockSpec((1, rows, w), cur_map(0)),
        out_shape=jax.ShapeDtypeStruct((b, s, D_ATTN_OUT), jnp.bfloat16),
        scratch_shapes=[pltpu.VMEM((N_ATTN_GROUPS, rows, w), jnp.float32),
                        pltpu.VMEM((N_ATTN_GROUPS, rows, w), jnp.float32)],
        compiler_params=_cparams(3),
        name="dilated_attention",
    )(*operands)


def _store_token_tiles(ref, val, first_chunk=0):
    rows = val.shape[0]
    for c in range(val.shape[1] // LANES):
        ref[pl.ds(first_chunk + c, rows, stride=TOKEN_TILE), :] = val[:, c * LANES:(c + 1) * LANES]


def _load_token_tiles(ref, start, rows):
    return [ref[pl.ds(start * TOKEN_TILE + c, rows, stride=TOKEN_TILE), :] for c in range(TOKEN_TILE)]


def _layer_norm(z, g, b):
    mu = jnp.mean(z, axis=-1, keepdims=True)
    zc = z - mu
    var = jnp.mean(zc * zc, axis=-1, keepdims=True)
    return zc * lax.rsqrt(var + LN_EPS) * g + b


def _mix_kernel(x_ref, grnn_ref, gattn_ref, h_ref, y_ref,
                wr_ref, wa_ref, wo_ref, lng_ref, lnb_ref, rw_ref, rb_ref,
                xo_ref, eidx_ref, gate_ref):
    rnn = jnp.dot(h_ref[...], wr_ref[...], preferred_element_type=jnp.float32)
    att = jnp.dot(y_ref[...], wa_ref[...], preferred_element_type=jnp.float32)
    mixed = jax.nn.sigmoid(grnn_ref[...]) * rnn + jax.nn.sigmoid(gattn_ref[...]) * att
    z = ALPHA * x_ref[...] + jnp.dot(mixed.astype(jnp.bfloat16), wo_ref[...],
                                     preferred_element_type=jnp.float32)
    x1 = _layer_norm(z, lng_ref[...], lnb_ref[...])
    _store_token_tiles(xo_ref, x1)

    x_hi = x1.astype(jnp.bfloat16)
    x_lo = (x1 - x_hi.astype(jnp.float32)).astype(jnp.bfloat16)
    t = jnp.dot(x_hi, rw_ref[...], preferred_element_type=jnp.float32)
    logits = (t[:, :ROUTE_W] + t[:, ROUTE_W:]
              + jnp.dot(x_lo, rw_ref[:, :ROUTE_W], preferred_element_type=jnp.float32)) + rb_ref[...]
    tm = logits.shape[0]
    lane = lax.broadcasted_iota(jnp.int32, (tm, ROUTE_W), 1)
    lane_f = lane.astype(jnp.float32)
    far = float(ROUTE_W)
    is_group = lane < N_EXPERT_GROUPS
    gl = jnp.where(is_group, logits, -jnp.inf)
    gmax = jnp.max(gl, axis=-1, keepdims=True)
    gsel = jnp.min(jnp.where(gl == gmax, lane_f, far), axis=-1, keepdims=True)
    p_group = 1.0 / jnp.sum(jnp.where(is_group, jnp.exp(logits - gmax), 0.0), axis=-1, keepdims=True)
    lane_group = ((lane - N_EXPERT_GROUPS) >> 3).astype(jnp.float32)
    in_group = (lane >= N_EXPERT_GROUPS) & (lane < N_EXPERT_GROUPS + N_EXPERTS) & (lane_group == gsel)
    el = jnp.where(in_group, logits, -jnp.inf)
    m1 = jnp.max(el, axis=-1, keepdims=True)
    i1 = jnp.min(jnp.where(el == m1, lane_f, far), axis=-1, keepdims=True)
    el2 = jnp.where(lane_f == i1, -jnp.inf, el)
    m2 = jnp.max(el2, axis=-1, keepdims=True)
    i2 = jnp.min(jnp.where(el2 == m2, lane_f, far), axis=-1, keepdims=True)
    e21 = jnp.exp(m2 - m1)
    den = 1.0 + e21
    g1 = p_group * (1.0 / den)
    g2 = p_group * (e21 / den)
    eidx = jnp.where(lane == 0, i1, jnp.where(lane == 1, i2, float(N_EXPERT_GROUPS))) - float(N_EXPERT_GROUPS)
    eidx_ref[...] = eidx.astype(jnp.int32)
    gate_ref[...] = jnp.where(lane == 0, g1, jnp.where(lane == 1, g2, 0.0))


def mix_and_route(x, proj, h_rnn, y_attn, wr, wa, wo, lng, lnb, rw, rb, tm=512):
    n = x.shape[0]
    d = D_MODEL
    w = D_ATTN_OUT
    row = lambda width, col=0: pl.BlockSpec((tm, width), lambda i, c=col: (i, c))
    full = lambda a: pl.BlockSpec(a.shape, lambda i: (0,) * a.ndim)
    return pl.pallas_call(
        _mix_kernel,
        grid=(n // tm,),
        in_specs=[row(d), row(d, COL_GRNN // d), row(d, COL_GATTN // d), row(d), row(w),
                  full(wr), full(wa), full(wo), full(lng), full(lnb), full(rw), full(rb)],
        out_specs=[pl.BlockSpec((tm * TOKEN_TILE, LANES), lambda i: (i, 0)), row(ROUTE_W), row(ROUTE_W)],
        out_shape=[jax.ShapeDtypeStruct((n * TOKEN_TILE, LANES), jnp.float32),
                   jax.ShapeDtypeStruct((n, ROUTE_W), jnp.int32),
                   jax.ShapeDtypeStruct((n, ROUTE_W), jnp.float32)],
        compiler_params=_cparams(1),
        name="mix_and_route",
    )(x, proj, proj, h_rnn, y_attn, wr, wa, wo, lng, lnb, rw, rb)


def _rank_kernel(e_ref, dest_ref, cnt_ref, carry, *, tm):
    ph = pl.program_id(0)
    i = pl.program_id(1)
    lanes = ROUTE_W

    @pl.when((ph == 0) & (i == 0))
    def _():
        carry[...] = jnp.zeros_like(carry)

    @pl.when((ph == 1) & (i == 0))
    def _():
        cnt = carry[...]
        cnt_ref[...] = jnp.broadcast_to(cnt, (lanes, lanes)).astype(jnp.int32)
        padded = jnp.floor((cnt + (MOE_BLK - 1)) * (1.0 / MOE_BLK)) * MOE_BLK
        r = lax.broadcasted_iota(jnp.int32, (lanes, lanes), 0)
        c = lax.broadcasted_iota(jnp.int32, (lanes, lanes), 1)
        lower = (c < r).astype(jnp.float32)
        pstart = jnp.dot(lower, jnp.broadcast_to(padded, (lanes, lanes)),
                         preferred_element_type=jnp.float32, precision=lax.Precision.HIGHEST)
        carry[...] = pstart[:, 0:1]

    r8 = lax.broadcasted_iota(jnp.int32, (SUBLANES, lanes), 0)
    c8 = lax.broadcasted_iota(jnp.int32, (SUBLANES, lanes), 1)
    pick = ((r8 == c8) & (r8 < TOP_K)).astype(jnp.bfloat16)
    ef = e_ref[...].astype(jnp.float32).astype(jnp.bfloat16)
    et = lax.dot_general(pick, ef, (((1,), (1,)), ((), ())), preferred_element_type=jnp.float32)
    e0, e1 = et[0:1, :], et[1:2, :]
    sub = lax.broadcasted_iota(jnp.int32, (lanes, tm), 0).astype(jnp.float32)
    is0, is1 = sub == e0, sub == e1
    member = jnp.where(is0 | is1, 1.0, 0.0)

    @pl.when(ph == 1)
    def _():
        s_i = lax.broadcasted_iota(jnp.int32, (tm, tm), 0)
        t_i = lax.broadcasted_iota(jnp.int32, (tm, tm), 1)
        earlier = (s_i < t_i).astype(jnp.bfloat16)
        before = jnp.dot(member.astype(jnp.bfloat16), earlier, preferred_element_type=jnp.float32)
        pos = before + carry[...]
        d0 = jnp.sum(jnp.where(is0, pos, 0.0), axis=0, keepdims=True)
        d1 = jnp.sum(jnp.where(is1, pos, 0.0), axis=0, keepdims=True)
        row = lax.broadcasted_iota(jnp.int32, (SUBLANES, tm), 0)
        dest_ref[...] = jnp.where(row == 0, d0, jnp.where(row == 1, d1, 0.0)).astype(jnp.int32)

    carry[...] = carry[...] + jnp.sum(member, axis=1, keepdims=True)


def route_slots(eidx, tm=512):
    n = eidx.shape[0]
    nt = n // tm
    return pl.pallas_call(
        functools.partial(_rank_kernel, tm=tm),
        grid=(2, nt),
        in_specs=[pl.BlockSpec((tm, ROUTE_W), lambda ph, i: (i, 0))],
        out_specs=[pl.BlockSpec((SUBLANES, tm), lambda ph, i: (0, i * ph)),
                   pl.BlockSpec((ROUTE_W, ROUTE_W), lambda ph, i: (0, 0))],
        out_shape=[jax.ShapeDtypeStruct((SUBLANES, n), jnp.int32),
                   jax.ShapeDtypeStruct((ROUTE_W, ROUTE_W), jnp.int32)],
        scratch_shapes=[pltpu.VMEM((ROUTE_W, 1), jnp.float32)],
        compiler_params=_cparams(2),
        name="route_slots",
    )(eidx)


def _slot_map_kernel(dest_ref, slot_ref, *, n, slots):
    def init(j, c):
        slot_ref[j] = 0
        return c

    lax.fori_loop(0, slots, init, 0, unroll=32)

    def body(t, c):
        slot_ref[dest_ref[t]] = t
        slot_ref[dest_ref[n + t]] = t
        return c

    lax.fori_loop(0, n, body, 0, unroll=16)


def slot_map(dest_flat, n, slots):
    return pl.pallas_call(
        functools.partial(_slot_map_kernel, n=n, slots=slots),
        in_specs=[pl.BlockSpec(memory_space=pltpu.SMEM)],
        out_specs=pl.BlockSpec(memory_space=pltpu.SMEM),
        out_shape=jax.ShapeDtypeStruct((slots,), jnp.int32),
        name="slot_map",
    )(dest_flat)


def _expert_kernel(be_ref, nused_ref, st_ref, x_hbm, wg_ref, wu_ref, wd_ref, y_ref, xs, land, sem):
    i = pl.program_id(0)
    n_used = nused_ref[0]

    def gather_block(blk):
        base = blk * MOE_BLK
        for j in range(MOE_BLK):
            src = pl.multiple_of(st_ref[base + j] * TOKEN_TILE, TOKEN_TILE)
            pltpu.make_async_copy(x_hbm.at[pl.ds(src, TOKEN_TILE)], land.at[pl.ds(j * TOKEN_TILE, TOKEN_TILE)],
                                  sem.at[0]).start(priority=j % DMA_QUEUES)

    def wait_block():
        pltpu.make_async_copy(x_hbm.at[pl.ds(0, MOE_BLK * TOKEN_TILE)], land, sem.at[0]).wait()

    @pl.when(i == 0)
    def _():
        gather_block(0)

    @pl.when(i < n_used)
    def _():
        wait_block()
        for c, chunk in enumerate(_load_token_tiles(land, 0, MOE_BLK)):
            xs[:, c * LANES:(c + 1) * LANES] = chunk.astype(jnp.bfloat16)

    @pl.when(i + 1 <= n_used)
    def _():
        gather_block(i + 1)

    @pl.when(n_used > i)
    def _():
        xb = xs[...]
        gate = jnp.dot(xb, wg_ref[0], preferred_element_type=jnp.float32)
        up = jnp.dot(xb, wu_ref[0], preferred_element_type=jnp.float32)
        hid = (jax.nn.silu(gate) * up).astype(jnp.bfloat16)
        _store_token_tiles(y_ref, jnp.dot(hid, wd_ref[0], preferred_element_type=jnp.float32))

    @pl.when(i == n_used - 1)
    def _():
        wait_block()

    @pl.when(i >= n_used)
    def _():
        y_ref[...] = jnp.zeros_like(y_ref)


def expert_blocks(block_e, n_used, slot_tok, x1t, wg, wu, wd):
    d = D_MODEL
    n_blocks = slot_tok.shape[0] // MOE_BLK - 1
    slots = n_blocks * MOE_BLK
    tile_rows = MOE_BLK * TOKEN_TILE
    grid_spec = pltpu.PrefetchScalarGridSpec(
        num_scalar_prefetch=3,
        grid=(n_blocks,),
        in_specs=[pl.BlockSpec(memory_space=pl.ANY),
                  pl.BlockSpec((1, d, D_EXPERT), lambda i, be, nu, st: (be[i], 0, 0)),
                  pl.BlockSpec((1, d, D_EXPERT), lambda i, be, nu, st: (be[i], 0, 0)),
                  pl.BlockSpec((1, D_EXPERT, d), lambda i, be, nu, st: (be[i], 0, 0))],
        out_specs=pl.BlockSpec((tile_rows, LANES), lambda i, be, nu, st: (i, 0)),
        scratch_shapes=[pltpu.VMEM((MOE_BLK, d), jnp.bfloat16),
                        pltpu.VMEM((tile_rows, LANES), jnp.float32),
                        pltpu.SemaphoreType.DMA((1,))],
    )
    return pl.pallas_call(
        _expert_kernel,
        grid_spec=grid_spec,
        out_shape=jax.ShapeDtypeStruct((slots * TOKEN_TILE, LANES), jnp.float32),
        compiler_params=_cparams(1),
        name="expert_blocks",
    )(block_e, n_used, slot_tok, x1t, wg, wu, wd)


def _combine_kernel(dest_ref, x_ref, gate_ref, y_hbm, g_ref, b_ref, xo_ref, xob_ref, fbuf, sem, *, n, tm):
    i = pl.program_id(0)
    nt = pl.num_programs(0)
    half = tm * TOKEN_TILE

    def gather_tile(tile, slot):
        base = tile * tm
        for k in range(TOP_K):
            for t in range(tm):
                src = pl.multiple_of(dest_ref[k * n + base + t] * TOKEN_TILE, TOKEN_TILE)
                pltpu.make_async_copy(y_hbm.at[pl.ds(src, TOKEN_TILE)],
                                      fbuf.at[slot, pl.ds((k * tm + t) * TOKEN_TILE, TOKEN_TILE)],
                                      sem.at[slot]).start(priority=t % DMA_QUEUES)

    def wait_tile(slot):
        pltpu.make_async_copy(y_hbm.at[pl.ds(0, TOP_K * half)], fbuf.at[slot], sem.at[slot]).wait()

    @pl.when(i == 0)
    def _():
        gather_tile(0, 0)

    slot = i % 2
    wait_tile(slot)
    g0, g1 = gate_ref[:, 0:1], gate_ref[:, 1:2]
    f0 = _load_token_tiles(fbuf.at[slot], 0, tm)
    f1 = _load_token_tiles(fbuf.at[slot], tm, tm)
    xs = _load_token_tiles(x_ref, 0, tm)
    gather_tile(jnp.minimum(i + 1, nt - 1), 1 - slot)
    z = [ALPHA * xs[c] + (g0 * f0[c] + g1 * f1[c]) for c in range(TOKEN_TILE)]
    mu = sum(jnp.sum(zc, axis=-1, keepdims=True) for zc in z) * (1.0 / D_MODEL)
    zc = [zz - mu for zz in z]
    var = sum(jnp.sum(v * v, axis=-1, keepdims=True) for v in zc) * (1.0 / D_MODEL)
    inv = lax.rsqrt(var + LN_EPS)
    for c in range(TOKEN_TILE):
        cols = slice(c * LANES, (c + 1) * LANES)
        x2 = zc[c] * inv * g_ref[:, cols] + b_ref[:, cols]
        xo_ref[:, cols] = x2
        xob_ref[:, cols] = x2.astype(jnp.bfloat16)

    @pl.when(i == nt - 1)
    def _():
        wait_tile(1 - slot)


def combine_ln2(dest_flat, x1t, gate, ybt, g, b, tm=256):
    n = gate.shape[0]
    d = D_MODEL
    tile_rows = tm * TOKEN_TILE
    grid_spec = pltpu.PrefetchScalarGridSpec(
        num_scalar_prefetch=1,
        grid=(n // tm,),
        in_specs=[pl.BlockSpec((tile_rows, LANES), lambda i, ds: (i, 0)),
                  pl.BlockSpec((tm, ROUTE_W), lambda i, ds: (i, 0)),
                  pl.BlockSpec(memory_space=pl.ANY),
                  pl.BlockSpec((1, d), lambda i, ds: (0, 0)),
                  pl.BlockSpec((1, d), lambda i, ds: (0, 0))],
        out_specs=[pl.BlockSpec((tm, d), lambda i, ds: (i, 0)),
                   pl.BlockSpec((tm, d), lambda i, ds: (i, 0))],
        scratch_shapes=[pltpu.VMEM((2, TOP_K * tile_rows, LANES), jnp.float32),
                        pltpu.SemaphoreType.DMA((2,))],
    )
    return pl.pallas_call(
        functools.partial(_combine_kernel, n=n, tm=tm),
        grid_spec=grid_spec,
        out_shape=[jax.ShapeDtypeStruct((n, d), jnp.float32),
                   jax.ShapeDtypeStruct((n, d), jnp.bfloat16)],
        compiler_params=_cparams(1),
        name="combine_ln2",
    )(dest_flat, x1t, gate, ybt, g, b)


def moe_layer(x1, eidx, gate, wg, wu, wd, ln_g, ln_b):
    n = eidx.shape[0]
    m = n * TOP_K
    n_blocks = m // MOE_BLK + N_EXPERTS
    slots = n_blocks * MOE_BLK
    dest8, cnt = route_slots(eidx)
    dest_flat = dest8[:TOP_K].reshape(m)
    counts = cnt[:N_EXPERTS, 0]
    pend = jnp.cumsum((counts + MOE_BLK - 1) // MOE_BLK * MOE_BLK)
    blk_start = jnp.arange(n_blocks, dtype=jnp.int32) * MOE_BLK
    block_e = jnp.minimum(jnp.sum((pend[None, :] <= blk_start[:, None]).astype(jnp.int32), axis=1),
                          N_EXPERTS - 1)
    n_used = (pend[-1:] // MOE_BLK).astype(jnp.int32)
    slot_tok = slot_map(dest_flat, n, slots + MOE_BLK)
    yb = expert_blocks(block_e, n_used, slot_tok, x1, wg, wu, wd)
    return combine_ln2(dest_flat, x1, gate, yb, ln_g, ln_b)


def _t5_bucket(rel):
    half = NUM_BUCKETS // 2
    max_exact = half // 2
    n = np.abs(rel)
    large = max_exact + (np.log(np.maximum(n, 1) / max_exact) / np.log(MAX_DISTANCE / max_exact)
                         * (half - max_exact)).astype(np.int32)
    large = np.minimum(large, half - 1)
    return np.where(rel > 0, half, 0) + np.where(n < max_exact, n, large)


def _bias_table(rel_bias, g, dil):
    rel = (np.arange(3 * RADIUS)[None, :] - RADIUS - np.arange(RADIUS)[:, None]) * dil
    hs = slice(g * HEADS_PER_GROUP, (g + 1) * HEADS_PER_GROUP)
    onehot = jnp.asarray(_t5_bucket(rel)[..., None] == np.arange(NUM_BUCKETS), jnp.float32)
    return jnp.einsum('qkb,bh->hqk', onehot, rel_bias[:, hs].astype(jnp.float32),
                      precision=lax.Precision.HIGHEST)


def _chunk_block_diag(w):
    per = RNN_CHUNK // RNN_BLOCK_W
    n_chunks = RNN_BLOCKS // per
    w4 = w.reshape(n_chunks, per, RNN_BLOCK_W, RNN_BLOCK_W)
    dense = jnp.einsum('chij,hk->chikj', w4, jnp.eye(per, dtype=w.dtype))
    return dense.reshape(n_chunks, RNN_CHUNK, RNN_CHUNK)


def _gate_weights(wa, wi):
    per_dir = [jnp.concatenate([_chunk_block_diag(wa[d]), _chunk_block_diag(wi[d])], axis=-1)
               for d in range(2)]
    return jnp.stack(per_dir, axis=1).astype(jnp.bfloat16)


def _per_chunk(v):
    return v.reshape(2, D_RNN // RNN_CHUNK, 1, RNN_CHUNK).transpose(1, 0, 2, 3)


def _split_hi_lo(w):
    hi = w.astype(jnp.bfloat16)
    lo = (w - hi.astype(jnp.float32)).astype(jnp.bfloat16)
    return jnp.concatenate([hi, lo], axis=1)


def _permute_in_cols(a):
    x_rnn, q, k, v, g_rnn, g_attn = jnp.split(
        a, [D_RNN, D_RNN + D_ATTN, D_RNN + 2 * D_ATTN, D_RNN + 3 * D_ATTN,
            D_RNN + 3 * D_ATTN + D_MODEL], axis=-1)
    return jnp.concatenate([x_rnn, g_rnn, g_attn, q, k, v], axis=-1)


def _trunk(x, biases, lp):
    b, s, d = x.shape
    n = b * s
    x2 = x.reshape(n, d)
    x2_mm = x2
    for l in range(DEPTH):
        p = lp[l]
        proj = in_projection(x2_mm, p['w_in'], p['b_in'])
        proj3 = proj.reshape(b, s, D_IN)
        h_rnn = rglru_branch(proj3, p['conv_w'], p['conv_b'], p['wg'], p['gb'], p['lam'])
        y_attn = dilated_attention(proj3, biases)
        x1, eidx, gate = mix_and_route(
            x2, proj, h_rnn.reshape(n, D_RNN), y_attn.reshape(n, D_ATTN_OUT),
            p['w_rnn_out'], p['w_attn_out'], p['w_o'],
            p['ln1_g'], p['ln1_b'], p['router_w'], p['router_b'])
        x2, x2_mm = moe_layer(x1, eidx, gate, p['w_gate'], p['w_up'], p['w_down'], p['ln2_g'], p['ln2_b'])
    return x2.reshape(b, s, d)


def kernel(x_prompt, x_sample, rel_bias, w_in, b_in, conv_w, conv_b, rg_wa, rg_ba, rg_wi, rg_bi, rg_lam, w_rnn_out, w_attn_out, w_o, ln1_g, ln1_b, router_w, router_b, expert_router_w, expert_router_b, w_gate, w_up, w_down, ln2_g, ln2_b):
    bf = jnp.bfloat16
    biases = [_bias_table(rel_bias, g, dil) for g, (_, dil) in enumerate(DILATED_CONFIGS)]
    lp = []
    for l in range(DEPTH):
        rw = jnp.concatenate(
            [router_w[l], jnp.transpose(expert_router_w[l], (1, 0, 2)).reshape(D_MODEL, N_EXPERTS)], axis=1)
        rb = jnp.concatenate([router_b[l], expert_router_b[l].reshape(N_EXPERTS)])
        pad = ROUTE_W - rw.shape[1]
        lp.append(dict(
            w_in=_permute_in_cols(w_in[l]).astype(bf),
            b_in=_permute_in_cols(b_in[l])[None, :],
            conv_w=conv_w[l], conv_b=conv_b[l][None, :],
            wg=_gate_weights(rg_wa[l], rg_wi[l]),
            gb=jnp.concatenate([_per_chunk(rg_ba[l]), _per_chunk(rg_bi[l])], axis=-1),
            lam=_per_chunk(rg_lam[l]),
            w_rnn_out=w_rnn_out[l].astype(bf), w_attn_out=w_attn_out[l].astype(bf), w_o=w_o[l].astype(bf),
            ln1_g=ln1_g[l][None, :], ln1_b=ln1_b[l][None, :],
            router_w=_split_hi_lo(jnp.pad(rw, ((0, 0), (0, pad)))), router_b=jnp.pad(rb, (0, pad))[None, :],
            w_gate=w_gate[l].astype(bf), w_up=w_up[l].astype(bf), w_down=w_down[l].astype(bf),
            ln2_g=ln2_g[l][None, :], ln2_b=ln2_b[l][None, :]))
    return (_trunk(x_prompt, biases, lp), _trunk(x_sample, biases, lp))
```

```python
import functools

import numpy as np
import jax
import jax.numpy as jnp
from jax import lax
from jax.experimental import pallas as pl
from jax.experimental.pallas import tpu as pltpu

D_MODEL = 1024
DEPTH = 2
D_RNN = D_MODEL
RNN_BLOCKS = 16
RNN_BLOCK_W = D_RNN // RNN_BLOCKS
CONV_WIDTH = 4
RG_C = 8.0
DILATED_CONFIGS = ((128, 1), (512, 4), (2048, 16))
N_ATTN_GROUPS = len(DILATED_CONFIGS)
HEADS_PER_GROUP = 4
N_ATTN_HEADS = N_ATTN_GROUPS * HEADS_PER_GROUP
HEAD_DIM = 128
D_ATTN = N_ATTN_HEADS * HEAD_DIM
D_ATTN_OUT = HEADS_PER_GROUP * HEAD_DIM
NUM_BUCKETS = 32
MAX_DISTANCE = max(w for w, _ in DILATED_CONFIGS) // 2
NEG_INF = -1e30
D_IN = D_RNN + 3 * D_ATTN + 2 * D_MODEL
N_EXPERT_GROUPS = 4
EXPERTS_PER_GROUP = 8
N_EXPERTS = N_EXPERT_GROUPS * EXPERTS_PER_GROUP
TOP_K = 2
D_EXPERT = D_MODEL // 2
ALPHA = (2 * DEPTH) ** 0.25
LN_EPS = 1e-5

LANES = 128
SUBLANES = 8
VMEM_LIMIT = 56 * 1024 * 1024

COL_XRNN = 0
COL_GRNN = D_RNN
COL_GATTN = D_RNN + D_MODEL
COL_Q = D_RNN + 2 * D_MODEL
COL_K = COL_Q + D_ATTN
COL_V = COL_K + D_ATTN

RADIUS = 64
assert all(w // (2 * d) == RADIUS for w, d in DILATED_CONFIGS)
ATTN_ROWS = 2048
RNN_CHUNK = 512
RNN_TILE = 256
MOE_BLK = 256
DMA_QUEUES = 2
TOKEN_TILE = D_MODEL // LANES
assert TOKEN_TILE == SUBLANES
ROUTE_W = LANES


def _cparams(n_axes):
    return pltpu.CompilerParams(dimension_semantics=("arbitrary",) * n_axes,
                                vmem_limit_bytes=VMEM_LIMIT)


def _proj_kernel(x_ref, w_ref, b_ref, o_ref):
    acc = jnp.dot(x_ref[...].astype(jnp.bfloat16), w_ref[...], preferred_element_type=jnp.float32)
    o_ref[...] = acc + b_ref[...]


def in_projection(x_bf, w_bf, b, tm=512, tn=3840):
    n, k = x_bf.shape
    nout = w_bf.shape[1]
    return pl.pallas_call(
        _proj_kernel,
        grid=(nout // tn, n // tm),
        in_specs=[pl.BlockSpec((tm, k), lambda j, i: (i, 0)),
                  pl.BlockSpec((k, tn), lambda j, i: (0, j)),
                  pl.BlockSpec((1, tn), lambda j, i: (0, j))],
        out_specs=pl.BlockSpec((tm, tn), lambda j, i: (i, j)),
        out_shape=jax.ShapeDtypeStruct((n, nout), jnp.float32),
        compiler_params=_cparams(2),
        name="in_projection",
    )(x_bf, w_bf, b)


def _rglru_kernel(x_ref, cw_ref, cb_ref, wg_ref, gb_ref, lam_ref, o_ref,
                  hsum, xc_s, a_s, u_s, h_s, *, seq, tile):
    c = RNN_CHUNK
    n_tiles = seq // tile
    n_grp = tile // SUBLANES
    big_rows = tile + 2 * SUBLANES
    row_in_grp = lax.broadcasted_iota(jnp.int32, (n_grp, SUBLANES, c), 1)

    def conv_tile(t0):
        centre = x_ref[0, pl.ds(t0, tile), :]
        p0 = pl.multiple_of(jnp.maximum(t0 - SUBLANES, 0), SUBLANES)
        n0 = pl.multiple_of(jnp.minimum(t0 + tile, seq - SUBLANES), SUBLANES)
        prev = jnp.where(t0 > 0, x_ref[0, pl.ds(p0, SUBLANES), :], 0.0)
        nxt = jnp.where(t0 + tile < seq, x_ref[0, pl.ds(n0, SUBLANES), :], 0.0)
        big = jnp.concatenate([prev, centre, nxt], axis=0)
        xc = cb_ref[...] + cw_ref[1:2, :] * centre
        for k in (0, 2, 3):
            off = k - 1
            tap = pltpu.roll(big, (-off) % big_rows, 0)[SUBLANES:SUBLANES + tile]
            xc = xc + cw_ref[k:k + 1, :] * tap
        return xc

    def gates(xc, dr):
        g = jnp.dot(xc.astype(jnp.bfloat16), wg_ref[0, dr],
                    preferred_element_type=jnp.float32) + gb_ref[0, dr]
        gate_r = jax.nn.sigmoid(g[:, :c])
        gate_i = jax.nn.sigmoid(g[:, c:])
        neg_lam = -lam_ref[0, dr]
        softplus = jnp.maximum(neg_lam, 0.0) + jnp.log1p(jnp.exp(-jnp.abs(neg_lam)))
        log_a = -RG_C * gate_r * softplus
        a = jnp.exp(log_a)
        th = jnp.tanh(log_a)
        one_minus_a2 = -2.0 * th / (1.0 - th)
        u = jnp.sqrt(one_minus_a2) * (gate_i * xc)
        return a, u

    def tile_prefix(a, u, reverse):
        a = a.reshape(n_grp, SUBLANES, c)
        u = u.reshape(n_grp, SUBLANES, c)
        for s in (1, 2, 4):
            if reverse:
                a_sh = pltpu.roll(a, SUBLANES - s, 1)
                u_sh = pltpu.roll(u, SUBLANES - s, 1)
                m = row_in_grp < SUBLANES - s
            else:
                a_sh = pltpu.roll(a, s, 1)
                u_sh = pltpu.roll(u, s, 1)
                m = row_in_grp >= s
            u = jnp.where(m, u + a * u_sh, u)
            a = jnp.where(m, a * a_sh, a)
        a_s[...] = a.reshape(tile, c)
        u_s[...] = u.reshape(tile, c)

    def run_direction(dr, reverse):
        def tile_body(ti, h):
            t_idx = (n_tiles - 1 - ti) if reverse else ti
            t0 = pl.multiple_of(t_idx * tile, tile)
            if reverse:
                xc = xc_s[pl.ds(t0, tile), :]
            else:
                xc = conv_tile(t0)
                xc_s[pl.ds(t0, tile), :] = xc
            a, u = gates(xc, dr)
            tile_prefix(a, u, reverse)

            def grp_body(gi, hc):
                g_idx = (n_grp - 1 - gi) if reverse else gi
                r0 = pl.multiple_of(g_idx * SUBLANES, SUBLANES)
                hg = u_s[pl.ds(r0, SUBLANES), :] + a_s[pl.ds(r0, SUBLANES), :] * hc
                h_s[pl.ds(r0, SUBLANES), :] = hg
                edge = hg[0:1, :] if reverse else hg[SUBLANES - 1:SUBLANES, :]
                return jnp.broadcast_to(edge, (SUBLANES, c))

            h = lax.fori_loop(0, n_grp, grp_body, h, unroll=4)
            if reverse:
                o_ref[0, pl.ds(t0, tile), :] = (hsum[pl.ds(t0, tile), :] + h_s[...]).astype(o_ref.dtype)
            else:
                hsum[pl.ds(t0, tile), :] = h_s[...]
            return h

        lax.fori_loop(0, n_tiles, tile_body, jnp.zeros((SUBLANES, c), jnp.float32))

    run_direction(0, False)
    run_direction(1, True)


def rglru_branch(proj3, cw, cb, wg, gb, lam):
    b, s, _ = proj3.shape
    c = RNN_CHUNK
    n_chunks = D_RNN // c
    kern = functools.partial(_rglru_kernel, seq=s, tile=RNN_TILE)
    return pl.pallas_call(
        kern,
        grid=(b, n_chunks),
        in_specs=[pl.BlockSpec((1, s, c), lambda bi, ci: (bi, 0, COL_XRNN // c + ci)),
                  pl.BlockSpec((CONV_WIDTH, c), lambda bi, ci: (0, ci)),
                  pl.BlockSpec((1, c), lambda bi, ci: (0, ci)),
                  pl.BlockSpec((1, 2, c, 2 * c), lambda bi, ci: (ci, 0, 0, 0)),
                  pl.BlockSpec((1, 2, 1, 2 * c), lambda bi, ci: (ci, 0, 0, 0)),
                  pl.BlockSpec((1, 2, 1, c), lambda bi, ci: (ci, 0, 0, 0))],
        out_specs=pl.BlockSpec((1, s, c), lambda bi, ci: (bi, 0, ci)),
        out_shape=jax.ShapeDtypeStruct((b, s, D_RNN), jnp.bfloat16),
        scratch_shapes=[pltpu.VMEM((s, c), jnp.float32),
                        pltpu.VMEM((s, c), jnp.float32),
                        pltpu.VMEM((RNN_TILE, c), jnp.float32),
                        pltpu.VMEM((RNN_TILE, c), jnp.float32),
                        pltpu.VMEM((RNN_TILE, c), jnp.float32)],
        compiler_params=_cparams(2),
        name="rglru_branch",
    )(proj3, cw, cb, wg, gb, lam)


def _attn_group(refs, bias, o_dst, l_dst, *, dil, is_first, is_last):
    q_ref, kp_ref, kc_ref, kn_ref, vp_ref, vc_ref, vn_ref = refs
    rows = ATTN_ROWS
    per_res = rows // dil
    n_q = per_res // RADIUS
    win = 3 * RADIUS
    nb = dil * n_q

    def fold(ref, r, n):
        if dil == 1:
            return ref[0, pl.ds(r, n), :]
        return ref[0, pl.ds(r, n, stride=dil), :]

    qs, ks, vs = [], [], []
    for r in range(dil):
        qf = fold(q_ref, r, per_res).astype(jnp.bfloat16)
        kf = jnp.concatenate([fold(kp_ref, r, RADIUS), fold(kc_ref, r, per_res),
                              fold(kn_ref, r, RADIUS)], axis=0).astype(jnp.bfloat16)
        vf = jnp.concatenate([fold(vp_ref, r, RADIUS), fold(vc_ref, r, per_res),
                              fold(vn_ref, r, RADIUS)], axis=0).astype(jnp.bfloat16)
        for jq in range(n_q):
            qs.append(qf[jq * RADIUS:(jq + 1) * RADIUS])
            ks.append(kf[jq * RADIUS:jq * RADIUS + win])
            vs.append(vf[jq * RADIUS:jq * RADIUS + win])
    qb, kb, vb = jnp.stack(qs), jnp.stack(ks), jnp.stack(vs)

    shape = (nb, RADIUS, win)
    jq_of = lax.broadcasted_iota(jnp.int32, shape, 0) & (n_q - 1)
    qi = lax.broadcasted_iota(jnp.int32, shape, 1)
    kj = lax.broadcasted_iota(jnp.int32, shape, 2)
    lo = jnp.where(jq_of == 0, jnp.where(is_first, RADIUS, 0), 0)
    hi = jnp.where(jq_of == n_q - 1, jnp.where(is_last, 2 * RADIUS, win), win)
    mask = (jnp.abs(kj - RADIUS - qi) <= RADIUS) & (kj >= lo) & (kj < hi)

    s = jnp.einsum('bqd,bkd->bqk', qb, kb, preferred_element_type=jnp.float32) * (HEAD_DIM ** -0.5)
    s = jnp.where(mask, s + bias[None], NEG_INF)
    m = jnp.max(s, axis=-1, keepdims=True)
    p = jnp.exp(s - m)
    l = jnp.sum(p, axis=-1, keepdims=True)
    o = jnp.einsum('bqk,bkd->bqd', p.astype(jnp.bfloat16), vb, preferred_element_type=jnp.float32) / l
    lse = m + jnp.log(l)
    for bi in range(nb):
        r, jq = divmod(bi, n_q)
        start = jq * RADIUS * dil + r
        idx = pl.ds(start, RADIUS) if dil == 1 else pl.ds(start, RADIUS, stride=dil)
        o_dst[idx, :] = o[bi]
        l_dst[idx, :] = jnp.broadcast_to(lse[bi], (RADIUS, HEAD_DIM))


def _attn_kernel(*refs, n_row_blocks):
    n_g = N_ATTN_GROUPS
    in_refs = refs[:7 * n_g]
    bias_refs = refs[7 * n_g:8 * n_g]
    y_ref = refs[8 * n_g]
    o_s, l_s = refs[8 * n_g + 1:]
    i = pl.program_id(1)
    for g, (_, dil) in enumerate(DILATED_CONFIGS):
        _attn_group(in_refs[7 * g:7 * g + 7], bias_refs[g][0], o_s.at[g], l_s.at[g],
                    dil=dil, is_first=i == 0, is_last=i == n_row_blocks - 1)
    l0, l1, l2 = l_s[0], l_s[1], l_s[2]
    m = jnp.maximum(jnp.maximum(l0, l1), l2)
    e0, e1, e2 = jnp.exp(l0 - m), jnp.exp(l1 - m), jnp.exp(l2 - m)
    y = (e0 * o_s[0] + e1 * o_s[1] + e2 * o_s[2]) / (e0 + e1 + e2)
    y_ref[0] = y.astype(y_ref.dtype)


def dilated_attention(proj3, biases):
    b, s, _ = proj3.shape
    rows = ATTN_ROWS
    nrb = s // rows
    w = HEAD_DIM
    nh = HEADS_PER_GROUP

    def cur_map(col):
        return lambda bi, i, h: (bi, i, col + h)

    in_specs, operands = [], []
    for g, (_, dil) in enumerate(DILATED_CONFIGS):
        halo = RADIUS * dil
        ratio = rows // halo
        n_halo_blocks = s // halo
        qc, kc, vc = COL_Q // w + g * nh, COL_K // w + g * nh, COL_V // w + g * nh

        def prev_map(col, ratio=ratio):
            return lambda bi, i, h: (bi, jnp.maximum(i * ratio - 1, 0), col + h)

        def next_map(col, ratio=ratio, last=n_halo_blocks - 1):
            return lambda bi, i, h: (bi, jnp.minimum((i + 1) * ratio, last), col + h)

        in_specs += [pl.BlockSpec((1, rows, w), cur_map(qc)),
                     pl.BlockSpec((1, halo, w), prev_map(kc)),
                     pl.BlockSpec((1, rows, w), cur_map(kc)),
                     pl.BlockSpec((1, halo, w), next_map(kc)),
                     pl.BlockSpec((1, halo, w), prev_map(vc)),
                     pl.BlockSpec((1, rows, w), cur_map(vc)),
                     pl.BlockSpec((1, halo, w), next_map(vc))]
        operands += [proj3] * 7
    in_specs += [pl.BlockSpec((1, RADIUS, 3 * RADIUS), lambda bi, i, h: (h, 0, 0))] * N_ATTN_GROUPS
    operands += list(biases)
    return pl.pallas_call(
        functools.partial(_attn_kernel, n_row_blocks=nrb),
        grid=(b, nrb, nh),
        in_specs=in_specs,
        out_specs=pl.BlockSpec((1, rows, w), cur_map(0)),
        out_shape=jax.ShapeDtypeStruct((b, s, D_ATTN_OUT), jnp.bfloat16),
        scratch_shapes=[pltpu.VMEM((N_ATTN_GROUPS, rows, w), jnp.float32),
                        pltpu.VMEM((N_ATTN_GROUPS, rows, w), jnp.float32)],
        compiler_params=_cparams(3),
        name="dilated_attention",
    )(*operands)


def _store_token_tiles(ref, val, first_chunk=0):
    rows = val.shape[0]
    for c in range(val.shape[1] // LANES):
        ref[pl.ds(first_chunk + c, rows, stride=TOKEN_TILE), :] = val[:, c * LANES:(c + 1) * LANES]


def _load_token_tiles(ref, start, rows):
    return [ref[pl.ds(start * TOKEN_TILE + c, rows, stride=TOKEN_TILE), :] for c in range(TOKEN_TILE)]


def _layer_norm(z, g, b):
    mu = jnp.mean(z, axis=-1, keepdims=True)
    zc = z - mu
    var = jnp.mean(zc * zc, axis=-1, keepdims=True)
    return zc * lax.rsqrt(var + LN_EPS) * g + b


def _mix_kernel(x_ref, grnn_ref, gattn_ref, h_ref, y_ref,
                wr_ref, wa_ref, wo_ref, lng_ref, lnb_ref, rw_ref, rb_ref,
                xo_ref, eidx_ref, gate_ref):
    rnn = jnp.dot(h_ref[...], wr_ref[...], preferred_element_type=jnp.float32)
    att = jnp.dot(y_ref[...], wa_ref[...], preferred_element_type=jnp.float32)
    mixed = jax.nn.sigmoid(grnn_ref[...]) * rnn + jax.nn.sigmoid(gattn_ref[...]) * att
    z = ALPHA * x_ref[...] + jnp.dot(mixed.astype(jnp.bfloat16), wo_ref[...],
                                     preferred_element_type=jnp.float32)
    x1 = _layer_norm(z, lng_ref[...], lnb_ref[...])
    _store_token_tiles(xo_ref, x1)

    x_hi = x1.astype(jnp.bfloat16)
    x_lo = (x1 - x_hi.astype(jnp.float32)).astype(jnp.bfloat16)
    t = jnp.dot(x_hi, rw_ref[...], preferred_element_type=jnp.float32)
    logits = (t[:, :ROUTE_W] + t[:, ROUTE_W:]
              + jnp.dot(x_lo, rw_ref[:, :ROUTE_W], preferred_element_type=jnp.float32)) + rb_ref[...]
    tm = logits.shape[0]
    lane = lax.broadcasted_iota(jnp.int32, (tm, ROUTE_W), 1)
    lane_f = lane.astype(jnp.float32)
    far = float(ROUTE_W)
    is_group = lane < N_EXPERT_GROUPS
    gl = jnp.where(is_group, logits, -jnp.inf)
    gmax = jnp.max(gl, axis=-1, keepdims=True)
    gsel = jnp.min(jnp.where(gl == gmax, lane_f, far), axis=-1, keepdims=True)
    p_group = 1.0 / jnp.sum(jnp.where(is_group, jnp.exp(logits - gmax), 0.0), axis=-1, keepdims=True)
    lane_group = ((lane - N_EXPERT_GROUPS) >> 3).astype(jnp.float32)
    in_group = (lane >= N_EXPERT_GROUPS) & (lane < N_EXPERT_GROUPS + N_EXPERTS) & (lane_group == gsel)
    el = jnp.where(in_group, logits, -jnp.inf)
    m1 = jnp.max(el, axis=-1, keepdims=True)
    i1 = jnp.min(jnp.where(el == m1, lane_f, far), axis=-1, keepdims=True)
    el2 = jnp.where(lane_f == i1, -jnp.inf, el)
    m2 = jnp.max(el2, axis=-1, keepdims=True)
    i2 = jnp.min(jnp.where(el2 == m2, lane_f, far), axis=-1, keepdims=True)
    e21 = jnp.exp(m2 - m1)
    den = 1.0 + e21
    g1 = p_group * (1.0 / den)
    g2 = p_group * (e21 / den)
    eidx = jnp.where(lane == 0, i1, jnp.where(lane == 1, i2, float(N_EXPERT_GROUPS))) - float(N_EXPERT_GROUPS)
    eidx_ref[...] = eidx.astype(jnp.int32)
    gate_ref[...] = jnp.where(lane == 0, g1, jnp.where(lane == 1, g2, 0.0))


def mix_and_route(x, proj, h_rnn, y_attn, wr, wa, wo, lng, lnb, rw, rb, tm=512):
    n = x.shape[0]
    d = D_MODEL
    w = D_ATTN_OUT
    row = lambda width, col=0: pl.BlockSpec((tm, width), lambda i, c=col: (i, c))
    full = lambda a: pl.BlockSpec(a.shape, lambda i: (0,) * a.ndim)
    return pl.pallas_call(
        _mix_kernel,
        grid=(n // tm,),
        in_specs=[row(d), row(d, COL_GRNN // d), row(d, COL_GATTN // d), row(d), row(w),
                  full(wr), full(wa), full(wo), full(lng), full(lnb), full(rw), full(rb)],
        out_specs=[pl.BlockSpec((tm * TOKEN_TILE, LANES), lambda i: (i, 0)), row(ROUTE_W), row(ROUTE_W)],
        out_shape=[jax.ShapeDtypeStruct((n * TOKEN_TILE, LANES), jnp.float32),
                   jax.ShapeDtypeStruct((n, ROUTE_W), jnp.int32),
                   jax.ShapeDtypeStruct((n, ROUTE_W), jnp.float32)],
        compiler_params=_cparams(1),
        name="mix_and_route",
    )(x, proj, proj, h_rnn, y_attn, wr, wa, wo, lng, lnb, rw, rb)


def _rank_kernel(e_ref, dest_ref, cnt_ref, carry, *, tm):
    ph = pl.program_id(0)
    i = pl.program_id(1)
    lanes = ROUTE_W

    @pl.when((ph == 0) & (i == 0))
    def _():
        carry[...] = jnp.zeros_like(carry)

    @pl.when((ph == 1) & (i == 0))
    def _():
        cnt = carry[...]
        cnt_ref[...] = jnp.broadcast_to(cnt, (lanes, lanes)).astype(jnp.int32)
        padded = jnp.floor((cnt + (MOE_BLK - 1)) * (1.0 / MOE_BLK)) * MOE_BLK
        r = lax.broadcasted_iota(jnp.int32, (lanes, lanes), 0)
        c = lax.broadcasted_iota(jnp.int32, (lanes, lanes), 1)
        lower = (c < r).astype(jnp.float32)
        pstart = jnp.dot(lower, jnp.broadcast_to(padded, (lanes, lanes)),
                         preferred_element_type=jnp.float32, precision=lax.Precision.HIGHEST)
        carry[...] = pstart[:, 0:1]

    r8 = lax.broadcasted_iota(jnp.int32, (SUBLANES, lanes), 0)
    c8 = lax.broadcasted_iota(jnp.int32, (SUBLANES, lanes), 1)
    pick = ((r8 == c8) & (r8 < TOP_K)).astype(jnp.bfloat16)
    ef = e_ref[...].astype(jnp.float32).astype(jnp.bfloat16)
    et = lax.dot_general(pick, ef, (((1,), (1,)), ((), ())), preferred_element_type=jnp.float32)
    e0, e1 = et[0:1, :], et[1:2, :]
    sub = lax.broadcasted_iota(jnp.int32, (lanes, tm), 0).astype(jnp.float32)
    is0, is1 = sub == e0, sub == e1
    member = jnp.where(is0 | is1, 1.0, 0.0)

    @pl.when(ph == 1)
    def _():
        s_i = lax.broadcasted_iota(jnp.int32, (tm, tm), 0)
        t_i = lax.broadcasted_iota(jnp.int32, (tm, tm), 1)
        earlier = (s_i < t_i).astype(jnp.bfloat16)
        before = jnp.dot(member.astype(jnp.bfloat16), earlier, preferred_element_type=jnp.float32)
        pos = before + carry[...]
        d0 = jnp.sum(jnp.where(is0, pos, 0.0), axis=0, keepdims=True)
        d1 = jnp.sum(jnp.where(is1, pos, 0.0), axis=0, keepdims=True)
        row = lax.broadcasted_iota(jnp.int32, (SUBLANES, tm), 0)
        dest_ref[...] = jnp.where(row == 0, d0, jnp.where(row == 1, d1, 0.0)).astype(jnp.int32)

    carry[...] = carry[...] + jnp.sum(member, axis=1, keepdims=True)


def route_slots(eidx, tm=512):
    n = eidx.shape[0]
    nt = n // tm
    return pl.pallas_call(
        functools.partial(_rank_kernel, tm=tm),
        grid=(2, nt),
        in_specs=[pl.BlockSpec((tm, ROUTE_W), lambda ph, i: (i, 0))],
        out_specs=[pl.BlockSpec((SUBLANES, tm), lambda ph, i: (0, i * ph)),
                   pl.BlockSpec((ROUTE_W, ROUTE_W), lambda ph, i: (0, 0))],
        out_shape=[jax.ShapeDtypeStruct((SUBLANES, n), jnp.int32),
                   jax.ShapeDtypeStruct((ROUTE_W, ROUTE_W), jnp.int32)],
        scratch_shapes=[pltpu.VMEM((ROUTE_W, 1), jnp.float32)],
        compiler_params=_cparams(2),
        name="route_slots",
    )(eidx)


def _slot_map_kernel(dest_ref, slot_ref, *, n, slots):
    def init(j, c):
        slot_ref[j] = 0
        return c

    lax.fori_loop(0, slots, init, 0, unroll=32)

    def body(t, c):
        slot_ref[dest_ref[t]] = t
        slot_ref[dest_ref[n + t]] = t
        return c

    lax.fori_loop(0, n, body, 0, unroll=16)


def slot_map(dest_flat, n, slots):
    return pl.pallas_call(
        functools.partial(_slot_map_kernel, n=n, slots=slots),
        in_specs=[pl.BlockSpec(memory_space=pltpu.SMEM)],
        out_specs=pl.BlockSpec(memory_space=pltpu.SMEM),
        out_shape=jax.ShapeDtypeStruct((slots,), jnp.int32),
        name="slot_map",
    )(dest_flat)


def _expert_kernel(be_ref, nused_ref, st_ref, x_hbm, wg_ref, wu_ref, wd_ref, y_ref, xs, land, sem):
    i = pl.program_id(0)
    n_used = nused_ref[0]

    def gather_block(blk):
        base = blk * MOE_BLK
        for j in range(MOE_BLK):
            src = pl.multiple_of(st_ref[base + j] * TOKEN_TILE, TOKEN_TILE)
            pltpu.make_async_copy(x_hbm.at[pl.ds(src, TOKEN_TILE)], land.at[pl.ds(j * TOKEN_TILE, TOKEN_TILE)],
                                  sem.at[0]).start(priority=j % DMA_QUEUES)

    def wait_block():
        pltpu.make_async_copy(x_hbm.at[pl.ds(0, MOE_BLK * TOKEN_TILE)], land, sem.at[0]).wait()

    @pl.when(i == 0)
    def _():
        gather_block(0)

    @pl.when(i < n_used)
    def _():
        wait_block()
        for c, chunk in enumerate(_load_token_tiles(land, 0, MOE_BLK)):
            xs[:, c * LANES:(c + 1) * LANES] = chunk.astype(jnp.bfloat16)

    @pl.when(i + 1 <= n_used)
    def _():
        gather_block(i + 1)

    @pl.when(n_used > i)
    def _():
        xb = xs[...]
        gate = jnp.dot(xb, wg_ref[0], preferred_element_type=jnp.float32)
        up = jnp.dot(xb, wu_ref[0], preferred_element_type=jnp.float32)
        hid = (jax.nn.silu(gate) * up).astype(jnp.bfloat16)
        _store_token_tiles(y_ref, jnp.dot(hid, wd_ref[0], preferred_element_type=jnp.float32))

    @pl.when(i == n_used - 1)
    def _():
        wait_block()

    @pl.when(i >= n_used)
    def _():
        y_ref[...] = jnp.zeros_like(y_ref)


def expert_blocks(block_e, n_used, slot_tok, x1t, wg, wu, wd):
    d = D_MODEL
    n_blocks = slot_tok.shape[0] // MOE_BLK - 1
    slots = n_blocks * MOE_BLK
    tile_rows = MOE_BLK * TOKEN_TILE
    grid_spec = pltpu.PrefetchScalarGridSpec(
        num_scalar_prefetch=3,
        grid=(n_blocks,),
        in_specs=[pl.BlockSpec(memory_space=pl.ANY),
                  pl.BlockSpec((1, d, D_EXPERT), lambda i, be, nu, st: (be[i], 0, 0)),
                  pl.BlockSpec((1, d, D_EXPERT), lambda i, be, nu, st: (be[i], 0, 0)),
                  pl.BlockSpec((1, D_EXPERT, d), lambda i, be, nu, st: (be[i], 0, 0))],
        out_specs=pl.BlockSpec((tile_rows, LANES), lambda i, be, nu, st: (i, 0)),
        scratch_shapes=[pltpu.VMEM((MOE_BLK, d), jnp.bfloat16),
                        pltpu.VMEM((tile_rows, LANES), jnp.float32),
                        pltpu.SemaphoreType.DMA((1,))],
    )
    return pl.pallas_call(
        _expert_kernel,
        grid_spec=grid_spec,
        out_shape=jax.ShapeDtypeStruct((slots * TOKEN_TILE, LANES), jnp.float32),
        compiler_params=_cparams(1),
        name="expert_blocks",
    )(block_e, n_used, slot_tok, x1t, wg, wu, wd)


def _combine_kernel(dest_ref, x_ref, gate_ref, y_hbm, g_ref, b_ref, xo_ref, xob_ref, fbuf, sem, *, n, tm):
    i = pl.program_id(0)
    nt = pl.num_programs(0)
    half = tm * TOKEN_TILE

    def gather_tile(tile, slot):
        base = tile * tm
        for k in range(TOP_K):
            for t in range(tm):
                src = pl.multiple_of(dest_ref[k * n + base + t] * TOKEN_TILE, TOKEN_TILE)
                pltpu.make_async_copy(y_hbm.at[pl.ds(src, TOKEN_TILE)],
                                      fbuf.at[slot, pl.ds((k * tm + t) * TOKEN_TILE, TOKEN_TILE)],
                                      sem.at[slot]).start(priority=t % DMA_QUEUES)

    def wait_tile(slot):
        pltpu.make_async_copy(y_hbm.at[pl.ds(0, TOP_K * half)], fbuf.at[slot], sem.at[slot]).wait()

    @pl.when(i == 0)
    def _():
        gather_tile(0, 0)

    slot = i % 2
    wait_tile(slot)
    g0, g1 = gate_ref[:, 0:1], gate_ref[:, 1:2]
    f0 = _load_token_tiles(fbuf.at[slot], 0, tm)
    f1 = _load_token_tiles(fbuf.at[slot], tm, tm)
    xs = _load_token_tiles(x_ref, 0, tm)
    gather_tile(jnp.minimum(i + 1, nt - 1), 1 - slot)
    z = [ALPHA * xs[c] + (g0 * f0[c] + g1 * f1[c]) for c in range(TOKEN_TILE)]
    mu = sum(jnp.sum(zc, axis=-1, keepdims=True) for zc in z) * (1.0 / D_MODEL)
    zc = [zz - mu for zz in z]
    var = sum(jnp.sum(v * v, axis=-1, keepdims=True) for v in zc) * (1.0 / D_MODEL)
    inv = lax.rsqrt(var + LN_EPS)
    for c in range(TOKEN_TILE):
        cols = slice(c * LANES, (c + 1) * LANES)
        x2 = zc[c] * inv * g_ref[:, cols] + b_ref[:, cols]
        xo_ref[:, cols] = x2
        xob_ref[:, cols] = x2.astype(jnp.bfloat16)

    @pl.when(i == nt - 1)
    def _():
        wait_tile(1 - slot)


def combine_ln2(dest_flat, x1t, gate, ybt, g, b, tm=256):
    n = gate.shape[0]
    d = D_MODEL
    tile_rows = tm * TOKEN_TILE
    grid_spec = pltpu.PrefetchScalarGridSpec(
        num_scalar_prefetch=1,
        grid=(n // tm,),
        in_specs=[pl.BlockSpec((tile_rows, LANES), lambda i, ds: (i, 0)),
                  pl.BlockSpec((tm, ROUTE_W), lambda i, ds: (i, 0)),
                  pl.BlockSpec(memory_space=pl.ANY),
                  pl.BlockSpec((1, d), lambda i, ds: (0, 0)),
                  pl.BlockSpec((1, d), lambda i, ds: (0, 0))],
        out_specs=[pl.BlockSpec((tm, d), lambda i, ds: (i, 0)),
                   pl.BlockSpec((tm, d), lambda i, ds: (i, 0))],
        scratch_shapes=[pltpu.VMEM((2, TOP_K * tile_rows, LANES), jnp.float32),
                        pltpu.SemaphoreType.DMA((2,))],
    )
    return pl.pallas_call(
        functools.partial(_combine_kernel, n=n, tm=tm),
        grid_spec=grid_spec,
        out_shape=[jax.ShapeDtypeStruct((n, d), jnp.float32),
                   jax.ShapeDtypeStruct((n, d), jnp.bfloat16)],
        compiler_params=_cparams(1),
        name="combine_ln2",
    )(dest_flat, x1t, gate, ybt, g, b)


def moe_layer(x1, eidx, gate, wg, wu, wd, ln_g, ln_b):
    n = eidx.shape[0]
    m = n * TOP_K
    n_blocks = m // MOE_BLK + N_EXPERTS
    slots = n_blocks * MOE_BLK
    dest8, cnt = route_slots(eidx)
    dest_flat = dest8[:TOP_K].reshape(m)
    counts = cnt[:N_EXPERTS, 0]
    pend = jnp.cumsum((counts + MOE_BLK - 1) // MOE_BLK * MOE_BLK)
    blk_start = jnp.arange(n_blocks, dtype=jnp.int32) * MOE_BLK
    block_e = jnp.minimum(jnp.sum((pend[None, :] <= blk_start[:, None]).astype(jnp.int32), axis=1),
                          N_EXPERTS - 1)
    n_used = (pend[-1:] // MOE_BLK).astype(jnp.int32)
    slot_tok = slot_map(dest_flat, n, slots + MOE_BLK)
    yb = expert_blocks(block_e, n_used, slot_tok, x1, wg, wu, wd)
    return combine_ln2(dest_flat, x1, gate, yb, ln_g, ln_b)


def _t5_bucket(rel):
    half = NUM_BUCKETS // 2
    max_exact = half // 2
    n = np.abs(rel)
    large = max_exact + (np.log(np.maximum(n, 1) / max_exact) / np.log(MAX_DISTANCE / max_exact)
                         * (half - max_exact)).astype(np.int32)
    large = np.minimum(large, half - 1)
    return np.where(rel > 0, half, 0) + np.where(n < max_exact, n, large)


def _bias_table(rel_bias, g, dil):
    rel = (np.arange(3 * RADIUS)[None, :] - RADIUS - np.arange(RADIUS)[:, None]) * dil
    hs = slice(g * HEADS_PER_GROUP, (g + 1) * HEADS_PER_GROUP)
    onehot = jnp.asarray(_t5_bucket(rel)[..., None] == np.arange(NUM_BUCKETS), jnp.float32)
    return jnp.einsum('qkb,bh->hqk', onehot, rel_bias[:, hs].astype(jnp.float32),
                      precision=lax.Precision.HIGHEST)


def _chunk_block_diag(w):
    per = RNN_CHUNK // RNN_BLOCK_W
    n_chunks = RNN_BLOCKS // per
    w4 = w.reshape(n_chunks, per, RNN_BLOCK_W, RNN_BLOCK_W)
    dense = jnp.einsum('chij,hk->chikj', w4, jnp.eye(per, dtype=w.dtype))
    return dense.reshape(n_chunks, RNN_CHUNK, RNN_CHUNK)


def _gate_weights(wa, wi):
    per_dir = [jnp.concatenate([_chunk_block_diag(wa[d]), _chunk_block_diag(wi[d])], axis=-1)
               for d in range(2)]
    return jnp.stack(per_dir, axis=1).astype(jnp.bfloat16)


def _per_chunk(v):
    return v.reshape(2, D_RNN // RNN_CHUNK, 1, RNN_CHUNK).transpose(1, 0, 2, 3)


def _split_hi_lo(w):
    hi = w.astype(jnp.bfloat16)
    lo = (w - hi.astype(jnp.float32)).astype(jnp.bfloat16)
    return jnp.concatenate([hi, lo], axis=1)


def _permute_in_cols(a):
    x_rnn, q, k, v, g_rnn, g_attn = jnp.split(
        a, [D_RNN, D_RNN + D_ATTN, D_RNN + 2 * D_ATTN, D_RNN + 3 * D_ATTN,
            D_RNN + 3 * D_ATTN + D_MODEL], axis=-1)
    return jnp.concatenate([x_rnn, g_rnn, g_attn, q, k, v], axis=-1)


def _trunk(x, biases, lp):
    b, s, d = x.shape
    n = b * s
    x2 = x.reshape(n, d)
    x2_mm = x2
    for l in range(DEPTH):
        p = lp[l]
        proj = in_projection(x2_mm, p['w_in'], p['b_in'])
        proj3 = proj.reshape(b, s, D_IN)
        h_rnn = rglru_branch(proj3, p['conv_w'], p['conv_b'], p['wg'], p['gb'], p['lam'])
        y_attn = dilated_attention(proj3, biases)
        x1, eidx, gate = mix_and_route(
            x2, proj, h_rnn.reshape(n, D_RNN), y_attn.reshape(n, D_ATTN_OUT),
            p['w_rnn_out'], p['w_attn_out'], p['w_o'],
            p['ln1_g'], p['ln1_b'], p['router_w'], p['router_b'])
        x2, x2_mm = moe_layer(x1, eidx, gate, p['w_gate'], p['w_up'], p['w_down'], p['ln2_g'], p['ln2_b'])
    return x2.reshape(b, s, d)


def kernel(x_prompt, x_sample, rel_bias, w_in, b_in, conv_w, conv_b, rg_wa, rg_ba, rg_wi, rg_bi, rg_lam, w_rnn_out, w_attn_out, w_o, ln1_g, ln1_b, router_w, router_b, expert_router_w, expert_router_b, w_gate, w_up, w_down, ln2_g, ln2_b):
    bf = jnp.bfloat16
    biases = [_bias_table(rel_bias, g, dil) for g, (_, dil) in enumerate(DILATED_CONFIGS)]
    lp = []
    for l in range(DEPTH):
        rw = jnp.concatenate(
            [router_w[l], jnp.transpose(expert_router_w[l], (1, 0, 2)).reshape(D_MODEL, N_EXPERTS)], axis=1)
        rb = jnp.concatenate([router_b[l], expert_router_b[l].reshape(N_EXPERTS)])
        pad = ROUTE_W - rw.shape[1]
        lp.append(dict(
            w_in=_permute_in_cols(w_in[l]).astype(bf),
            b_in=_permute_in_cols(b_in[l])[None, :],
            conv_w=conv_w[l], conv_b=conv_b[l][None, :],
            wg=_gate_weights(rg_wa[l], rg_wi[l]),
            gb=jnp.concatenate([_per_chunk(rg_ba[l]), _per_chunk(rg_bi[l])], axis=-1),
            lam=_per_chunk(rg_lam[l]),
            w_rnn_out=w_rnn_out[l].astype(bf), w_attn_out=w_attn_out[l].astype(bf), w_o=w_o[l].astype(bf),
            ln1_g=ln1_g[l][None, :], ln1_b=ln1_b[l][None, :],
            router_w=_split_hi_lo(jnp.pad(rw, ((0, 0), (0, pad)))), router_b=jnp.pad(rb, (0, pad))[None, :],
            w_gate=w_gate[l].astype(bf), w_up=w_up[l].astype(bf), w_down=w_down[l].astype(bf),
            ln2_g=ln2_g[l][None, :], ln2_b=ln2_b[l][None, :]))
    return (_trunk(x_prompt, biases, lp), _trunk(x_sample, biases, lp))
```

```python
import functools

import numpy as np
import jax
import jax.numpy as jnp
from jax import lax
from jax.experimental import pallas as pl
from jax.experimental.pallas import tpu as pltpu

D_MODEL = 1024
DEPTH = 2
D_RNN = D_MODEL
RNN_BLOCKS = 16
RNN_BLOCK_W = D_RNN // RNN_BLOCKS
CONV_WIDTH = 4
RG_C = 8.0
DILATED_CONFIGS = ((128, 1), (512, 4), (2048, 16))
N_ATTN_GROUPS = len(DILATED_CONFIGS)
HEADS_PER_GROUP = 4
N_ATTN_HEADS = N_ATTN_GROUPS * HEADS_PER_GROUP
HEAD_DIM = 128
D_ATTN = N_ATTN_HEADS * HEAD_DIM
D_ATTN_OUT = HEADS_PER_GROUP * HEAD_DIM
NUM_BUCKETS = 32
MAX_DISTANCE = max(w for w, _ in DILATED_CONFIGS) // 2
NEG_INF = -1e30
D_IN = D_RNN + 3 * D_ATTN + 2 * D_MODEL
N_EXPERT_GROUPS = 4
EXPERTS_PER_GROUP = 8
N_EXPERTS = N_EXPERT_GROUPS * EXPERTS_PER_GROUP
TOP_K = 2
D_EXPERT = D_MODEL // 2
ALPHA = (2 * DEPTH) ** 0.25
LN_EPS = 1e-5

LANES = 128
SUBLANES = 8
VMEM_LIMIT = 56 * 1024 * 1024

COL_XRNN = 0
COL_GRNN = D_RNN
COL_GATTN = D_RNN + D_MODEL
COL_Q = D_RNN + 2 * D_MODEL
COL_K = COL_Q + D_ATTN
COL_V = COL_K + D_ATTN

RADIUS = 64
assert all(w // (2 * d) == RADIUS for w, d in DILATED_CONFIGS)
ATTN_ROWS = 2048
RNN_CHUNK = 512
RNN_TILE = 256
MOE_BLK = 256
DMA_QUEUES = 2
TOKEN_TILE = D_MODEL // LANES
assert TOKEN_TILE == SUBLANES
ROUTE_W = LANES


def _cparams(n_axes):
    return pltpu.CompilerParams(dimension_semantics=("arbitrary",) * n_axes,
                                vmem_limit_bytes=VMEM_LIMIT)


def _proj_kernel(x_ref, w_ref, b_ref, o_ref):
    acc = jnp.dot(x_ref[...].astype(jnp.bfloat16), w_ref[...], preferred_element_type=jnp.float32)
    o_ref[...] = acc + b_ref[...]


def in_projection(x_bf, w_bf, b, tm=512, tn=3840):
    n, k = x_bf.shape
    nout = w_bf.shape[1]
    return pl.pallas_call(
        _proj_kernel,
        grid=(nout // tn, n // tm),
        in_specs=[pl.BlockSpec((tm, k), lambda j, i: (i, 0)),
                  pl.BlockSpec((k, tn), lambda j, i: (0, j)),
                  pl.BlockSpec((1, tn), lambda j, i: (0, j))],
        out_specs=pl.BlockSpec((tm, tn), lambda j, i: (i, j)),
        out_shape=jax.ShapeDtypeStruct((n, nout), jnp.float32),
        compiler_params=_cparams(2),
        name="in_projection",
    )(x_bf, w_bf, b)


def _rglru_kernel(x_ref, cw_ref, cb_ref, wg_ref, gb_ref, lam_ref, o_ref,
                  hsum, xc_s, a_s, u_s, h_s, *, seq, tile):
    c = RNN_CHUNK
    n_tiles = seq // tile
    n_grp = tile // SUBLANES
    big_rows = tile + 2 * SUBLANES
    row_in_grp = lax.broadcasted_iota(jnp.int32, (n_grp, SUBLANES, c), 1)

    def conv_tile(t0):
        centre = x_ref[0, pl.ds(t0, tile), :]
        p0 = pl.multiple_of(jnp.maximum(t0 - SUBLANES, 0), SUBLANES)
        n0 = pl.multiple_of(jnp.minimum(t0 + tile, seq - SUBLANES), SUBLANES)
        prev = jnp.where(t0 > 0, x_ref[0, pl.ds(p0, SUBLANES), :], 0.0)
        nxt = jnp.where(t0 + tile < seq, x_ref[0, pl.ds(n0, SUBLANES), :], 0.0)
        big = jnp.concatenate([prev, centre, nxt], axis=0)
        xc = cb_ref[...] + cw_ref[1:2, :] * centre
        for k in (0, 2, 3):
            off = k - 1
            tap = pltpu.roll(big, (-off) % big_rows, 0)[SUBLANES:SUBLANES + tile]
            xc = xc + cw_ref[k:k + 1, :] * tap
        return xc

    def gates(xc, dr):
        g = jnp.dot(xc.astype(jnp.bfloat16), wg_ref[0, dr],
                    preferred_element_type=jnp.float32) + gb_ref[0, dr]
        gate_r = jax.nn.sigmoid(g[:, :c])
        gate_i = jax.nn.sigmoid(g[:, c:])
        neg_lam = -lam_ref[0, dr]
        softplus = jnp.maximum(neg_lam, 0.0) + jnp.log1p(jnp.exp(-jnp.abs(neg_lam)))
        log_a = -RG_C * gate_r * softplus
        a = jnp.exp(log_a)
        th = jnp.tanh(log_a)
        one_minus_a2 = -2.0 * th / (1.0 - th)
        u = jnp.sqrt(one_minus_a2) * (gate_i * xc)
        return a, u

    def tile_prefix(a, u, reverse):
        a = a.reshape(n_grp, SUBLANES, c)
        u = u.reshape(n_grp, SUBLANES, c)
        for s in (1, 2, 4):
            if reverse:
                a_sh = pltpu.roll(a, SUBLANES - s, 1)
                u_sh = pltpu.roll(u, SUBLANES - s, 1)
                m = row_in_grp < SUBLANES - s
            else:
                a_sh = pltpu.roll(a, s, 1)
                u_sh = pltpu.roll(u, s, 1)
                m = row_in_grp >= s
            u = jnp.where(m, u + a * u_sh, u)
            a = jnp.where(m, a * a_sh, a)
        a_s[...] = a.reshape(tile, c)
        u_s[...] = u.reshape(tile, c)

    def run_direction(dr, reverse):
        def tile_body(ti, h):
            t_idx = (n_tiles - 1 - ti) if reverse else ti
            t0 = pl.multiple_of(t_idx * tile, tile)
            if reverse:
                xc = xc_s[pl.ds(t0, tile), :]
            else:
                xc = conv_tile(t0)
                xc_s[pl.ds(t0, tile), :] = xc
            a, u = gates(xc, dr)
            tile_prefix(a, u, reverse)

            def grp_body(gi, hc):
                g_idx = (n_grp - 1 - gi) if reverse else gi
                r0 = pl.multiple_of(g_idx * SUBLANES, SUBLANES)
                hg = u_s[pl.ds(r0, SUBLANES), :] + a_s[pl.ds(r0, SUBLANES), :] * hc
                h_s[pl.ds(r0, SUBLANES), :] = hg
                edge = hg[0:1, :] if reverse else hg[SUBLANES - 1:SUBLANES, :]
                return jnp.broadcast_to(edge, (SUBLANES, c))

            h = lax.fori_loop(0, n_grp, grp_body, h, unroll=4)
            if reverse:
                o_ref[0, pl.ds(t0, tile), :] = (hsum[pl.ds(t0, tile), :] + h_s[...]).astype(o_ref.dtype)
            else:
                hsum[pl.ds(t0, tile), :] = h_s[...]
            return h

        lax.fori_loop(0, n_tiles, tile_body, jnp.zeros((SUBLANES, c), jnp.float32))

    run_direction(0, False)
    run_direction(1, True)


def rglru_branch(proj3, cw, cb, wg, gb, lam):
    b, s, _ = proj3.shape
    c = RNN_CHUNK
    n_chunks = D_RNN // c
    kern = functools.partial(_rglru_kernel, seq=s, tile=RNN_TILE)
    return pl.pallas_call(
        kern,
        grid=(b, n_chunks),
        in_specs=[pl.BlockSpec((1, s, c), lambda bi, ci: (bi, 0, COL_XRNN // c + ci)),
                  pl.BlockSpec((CONV_WIDTH, c), lambda bi, ci: (0, ci)),
                  pl.BlockSpec((1, c), lambda bi, ci: (0, ci)),
                  pl.BlockSpec((1, 2, c, 2 * c), lambda bi, ci: (ci, 0, 0, 0)),
                  pl.BlockSpec((1, 2, 1, 2 * c), lambda bi, ci: (ci, 0, 0, 0)),
                  pl.BlockSpec((1, 2, 1, c), lambda bi, ci: (ci, 0, 0, 0))],
        out_specs=pl.BlockSpec((1, s, c), lambda bi, ci: (bi, 0, ci)),
        out_shape=jax.ShapeDtypeStruct((b, s, D_RNN), jnp.bfloat16),
        scratch_shapes=[pltpu.VMEM((s, c), jnp.float32),
                        pltpu.VMEM((s, c), jnp.float32),
                        pltpu.VMEM((RNN_TILE, c), jnp.float32),
                        pltpu.VMEM((RNN_TILE, c), jnp.float32),
                        pltpu.VMEM((RNN_TILE, c), jnp.float32)],
        compiler_params=_cparams(2),
        name="rglru_branch",
    )(proj3, cw, cb, wg, gb, lam)


def _attn_group(refs, bias, o_dst, l_dst, *, dil, is_first, is_last):
    q_ref, kp_ref, kc_ref, kn_ref, vp_ref, vc_ref, vn_ref = refs
    rows = ATTN_ROWS
    per_res = rows // dil
    n_q = per_res // RADIUS
    win = 3 * RADIUS
    nb = dil * n_q

    def fold(ref, r, n):
        if dil == 1:
            return ref[0, pl.ds(r, n), :]
        return ref[0, pl.ds(r, n, stride=dil), :]

    qs, ks, vs = [], [], []
    for r in range(dil):
        qf = fold(q_ref, r, per_res).astype(jnp.bfloat16)
        kf = jnp.concatenate([fold(kp_ref, r, RADIUS), fold(kc_ref, r, per_res),
                              fold(kn_ref, r, RADIUS)], axis=0).astype(jnp.bfloat16)
        vf = jnp.concatenate([fold(vp_ref, r, RADIUS), fold(vc_ref, r, per_res),
                              fold(vn_ref, r, RADIUS)], axis=0).astype(jnp.bfloat16)
        for jq in range(n_q):
            qs.append(qf[jq * RADIUS:(jq + 1) * RADIUS])
            ks.append(kf[jq * RADIUS:jq * RADIUS + win])
            vs.append(vf[jq * RADIUS:jq * RADIUS + win])
    qb, kb, vb = jnp.stack(qs), jnp.stack(ks), jnp.stack(vs)

    shape = (nb, RADIUS, win)
    jq_of = lax.broadcasted_iota(jnp.int32, shape, 0) & (n_q - 1)
    qi = lax.broadcasted_iota(jnp.int32, shape, 1)
    kj = lax.broadcasted_iota(jnp.int32, shape, 2)
    lo = jnp.where(jq_of == 0, jnp.where(is_first, RADIUS, 0), 0)
    hi = jnp.where(jq_of == n_q - 1, jnp.where(is_last, 2 * RADIUS, win), win)
    mask = (jnp.abs(kj - RADIUS - qi) <= RADIUS) & (kj >= lo) & (kj < hi)

    s = jnp.einsum('bqd,bkd->bqk', qb, kb, preferred_element_type=jnp.float32) * (HEAD_DIM ** -0.5)
    s = jnp.where(mask, s + bias[None], NEG_INF)
    m = jnp.max(s, axis=-1, keepdims=True)
    p = jnp.exp(s - m)
    l = jnp.sum(p, axis=-1, keepdims=True)
    o = jnp.einsum('bqk,bkd->bqd', p.astype(jnp.bfloat16), vb, preferred_element_type=jnp.float32) / l
    lse = m + jnp.log(l)
    for bi in range(nb):
        r, jq = divmod(bi, n_q)
        start = jq * RADIUS * dil + r
        idx = pl.ds(start, RADIUS) if dil == 1 else pl.ds(start, RADIUS, stride=dil)
        o_dst[idx, :] = o[bi]
        l_dst[idx, :] = jnp.broadcast_to(lse[bi], (RADIUS, HEAD_DIM))


def _attn_kernel(*refs, n_row_blocks):
    n_g = N_ATTN_GROUPS
    in_refs = refs[:7 * n_g]
    bias_refs = refs[7 * n_g:8 * n_g]
    y_ref = refs[8 * n_g]
    o_s, l_s = refs[8 * n_g + 1:]
    i = pl.program_id(1)
    for g, (_, dil) in enumerate(DILATED_CONFIGS):
        _attn_group(in_refs[7 * g:7 * g + 7], bias_refs[g][0], o_s.at[g], l_s.at[g],
                    dil=dil, is_first=i == 0, is_last=i == n_row_blocks - 1)
    l0, l1, l2 = l_s[0], l_s[1], l_s[2]
    m = jnp.maximum(jnp.maximum(l0, l1), l2)
    e0, e1, e2 = jnp.exp(l0 - m), jnp.exp(l1 - m), jnp.exp(l2 - m)
    y = (e0 * o_s[0] + e1 * o_s[1] + e2 * o_s[2]) / (e0 + e1 + e2)
    y_ref[0] = y.astype(y_ref.dtype)


def dilated_attention(proj3, biases):
    b, s, _ = proj3.shape
    rows = ATTN_ROWS
    nrb = s // rows
    w = HEAD_DIM
    nh = HEADS_PER_GROUP

    def cur_map(col):
        return lambda bi, i, h: (bi, i, col + h)

    in_specs, operands = [], []
    for g, (_, dil) in enumerate(DILATED_CONFIGS):
        halo = RADIUS * dil
        ratio = rows // halo
        n_halo_blocks = s // halo
        qc, kc, vc = COL_Q // w + g * nh, COL_K // w + g * nh, COL_V // w + g * nh

        def prev_map(col, ratio=ratio):
            return lambda bi, i, h: (bi, jnp.maximum(i * ratio - 1, 0), col + h)

        def next_map(col, ratio=ratio, last=n_halo_blocks - 1):
            return lambda bi, i, h: (bi, jnp.minimum((i + 1) * ratio, last), col + h)

        in_specs += [pl.BlockSpec((1, rows, w), cur_map(qc)),
                     pl.BlockSpec((1, halo, w), prev_map(kc)),
                     pl.BlockSpec((1, rows, w), cur_map(kc)),
                     pl.BlockSpec((1, halo, w), next_map(kc)),
                     pl.BlockSpec((1, halo, w), prev_map(vc)),
                     pl.BlockSpec((1, rows, w), cur_map(vc)),
                     pl.BlockSpec((1, halo, w), next_map(vc))]
        operands += [proj3] * 7
    in_specs += [pl.BlockSpec((1, RADIUS, 3 * RADIUS), lambda bi, i, h: (h, 0, 0))] * N_ATTN_GROUPS
    operands += list(biases)
    return pl.pallas_call(
        functools.partial(_attn_kernel, n_row_blocks=nrb),
        grid=(b, nrb, nh),
        in_specs=in_specs,
        out_specs=pl.BlockSpec((1, rows, w), cur_map(0)),
        out_shape=jax.ShapeDtypeStruct((b, s, D_ATTN_OUT), jnp.bfloat16),
        scratch_shapes=[pltpu.VMEM((N_ATTN_GROUPS, rows, w), jnp.float32),
                        pltpu.VMEM((N_ATTN_GROUPS, rows, w), jnp.float32)],
        compiler_params=_cparams(3),
        name="dilated_attention",
    )(*operands)


def _store_token_tiles(ref, val, first_chunk=0):
    rows = val.shape[0]
    for c in range(val.shape[1] // LANES):
        ref[pl.ds(first_chunk + c, rows, stride=TOKEN_TILE), :] = val[:, c * LANES:(c + 1) * LANES]


def _load_token_tiles(ref, start, rows):
    return [ref[pl.ds(start * TOKEN_TILE + c, rows, stride=TOKEN_TILE), :] for c in range(TOKEN_TILE)]


def _layer_norm(z, g, b):
    mu = jnp.mean(z, axis=-1, keepdims=True)
    zc = z - mu
    var = jnp.mean(zc * zc, axis=-1, keepdims=True)
    return zc * lax.rsqrt(var + LN_EPS) * g + b


def _mix_kernel(x_ref, grnn_ref, gattn_ref, h_ref, y_ref,
                wr_ref, wa_ref, wo_ref, lng_ref, lnb_ref, rw_ref, rb_ref,
                xo_ref, eidx_ref, gate_ref):
    rnn = jnp.dot(h_ref[...], wr_ref[...], preferred_element_type=jnp.float32)
    att = jnp.dot(y_ref[...], wa_ref[...], preferred_element_type=jnp.float32)
    mixed = jax.nn.sigmoid(grnn_ref[...]) * rnn + jax.nn.sigmoid(gattn_ref[...]) * att
    z = ALPHA * x_ref[...] + jnp.dot(mixed.astype(jnp.bfloat16), wo_ref[...],
                                     preferred_element_type=jnp.float32)
    x1 = _layer_norm(z, lng_ref[...], lnb_ref[...])
    _store_token_tiles(xo_ref, x1)

    x_hi = x1.astype(jnp.bfloat16)
    x_lo = (x1 - x_hi.astype(jnp.float32)).astype(jnp.bfloat16)
    t = jnp.dot(x_hi, rw_ref[...], preferred_element_type=jnp.float32)
    logits = (t[:, :ROUTE_W] + t[:, ROUTE_W:]
              + jnp.dot(x_lo, rw_ref[:, :ROUTE_W], preferred_element_type=jnp.float32)) + rb_ref[...]
    tm = logits.shape[0]
    lane = lax.broadcasted_iota(jnp.int32, (tm, ROUTE_W), 1)
    lane_f = lane.astype(jnp.float32)
    far = float(ROUTE_W)
    is_group = lane < N_EXPERT_GROUPS
    gl = jnp.where(is_group, logits, -jnp.inf)
    gmax = jnp.max(gl, axis=-1, keepdims=True)
    gsel = jnp.min(jnp.where(gl == gmax, lane_f, far), axis=-1, keepdims=True)
    p_group = 1.0 / jnp.sum(jnp.where(is_group, jnp.exp(logits - gmax), 0.0), axis=-1, keepdims=True)
    lane_group = ((lane - N_EXPERT_GROUPS) >> 3).astype(jnp.float32)
    in_group = (lane >= N_EXPERT_GROUPS) & (lane < N_EXPERT_GROUPS + N_EXPERTS) & (lane_group == gsel)
    el = jnp.where(in_group, logits, -jnp.inf)
    m1 = jnp.max(el, axis=-1, keepdims=True)
    i1 = jnp.min(jnp.where(el == m1, lane_f, far), axis=-1, keepdims=True)
    el2 = jnp.where(lane_f == i1, -jnp.inf, el)
    m2 = jnp.max(el2, axis=-1, keepdims=True)
    i2 = jnp.min(jnp.where(el2 == m2, lane_f, far), axis=-1, keepdims=True)
    e21 = jnp.exp(m2 - m1)
    den = 1.0 + e21
    g1 = p_group * (1.0 / den)
    g2 = p_group * (e21 / den)
    eidx = jnp.where(lane == 0, i1, jnp.where(lane == 1, i2, float(N_EXPERT_GROUPS))) - float(N_EXPERT_GROUPS)
    eidx_ref[...] = eidx.astype(jnp.int32)
    gate_ref[...] = jnp.where(lane == 0, g1, jnp.where(lane == 1, g2, 0.0))


def mix_and_route(x, proj, h_rnn, y_attn, wr, wa, wo, lng, lnb, rw, rb, tm=512):
    n = x.shape[0]
    d = D_MODEL
    w = D_ATTN_OUT
    row = lambda width, col=0: pl.BlockSpec((tm, width), lambda i, c=col: (i, c))
    full = lambda a: pl.BlockSpec(a.shape, lambda i: (0,) * a.ndim)
    return pl.pallas_call(
        _mix_kernel,
        grid=(n // tm,),
        in_specs=[row(d), row(d, COL_GRNN // d), row(d, COL_GATTN // d), row(d), row(w),
                  full(wr), full(wa), full(wo), full(lng), full(lnb), full(rw), full(rb)],
        out_specs=[pl.BlockSpec((tm * TOKEN_TILE, LANES), lambda i: (i, 0)), row(ROUTE_W), row(ROUTE_W)],
        out_shape=[jax.ShapeDtypeStruct((n * TOKEN_TILE, LANES), jnp.float32),
                   jax.ShapeDtypeStruct((n, ROUTE_W), jnp.int32),
                   jax.ShapeDtypeStruct((n, ROUTE_W), jnp.float32)],
        compiler_params=_cparams(1),
        name="mix_and_route",
    )(x, proj, proj, h_rnn, y_attn, wr, wa, wo, lng, lnb, rw, rb)


def _rank_kernel(e_ref, dest_ref, cnt_ref, carry, *, tm):
    ph = pl.program_id(0)
    i = pl.program_id(1)
    lanes = ROUTE_W

    @pl.when((ph == 0) & (i == 0))
    def _():
        carry[...] = jnp.zeros_like(carry)

    @pl.when((ph == 1) & (i == 0))
    def _():
        cnt = carry[...]
        cnt_ref[...] = jnp.broadcast_to(cnt, (lanes, lanes)).astype(jnp.int32)
        padded = jnp.floor((cnt + (MOE_BLK - 1)) * (1.0 / MOE_BLK)) * MOE_BLK
        r = lax.broadcasted_iota(jnp.int32, (lanes, lanes), 0)
        c = lax.broadcasted_iota(jnp.int32, (lanes, lanes), 1)
        lower = (c < r).astype(jnp.float32)
        pstart = jnp.dot(lower, jnp.broadcast_to(padded, (lanes, lanes)),
                         preferred_element_type=jnp.float32, precision=lax.Precision.HIGHEST)
        carry[...] = pstart[:, 0:1]

    r8 = lax.broadcasted_iota(jnp.int32, (SUBLANES, lanes), 0)
    c8 = lax.broadcasted_iota(jnp.int32, (SUBLANES, lanes), 1)
    pick = ((r8 == c8) & (r8 < TOP_K)).astype(jnp.bfloat16)
    ef = e_ref[...].astype(jnp.float32).astype(jnp.bfloat16)
    et = lax.dot_general(pick, ef, (((1,), (1,)), ((), ())), preferred_element_type=jnp.float32)
    e0, e1 = et[0:1, :], et[1:2, :]
    sub = lax.broadcasted_iota(jnp.int32, (lanes, tm), 0).astype(jnp.float32)
    is0, is1 = sub == e0, sub == e1
    member = jnp.where(is0 | is1, 1.0, 0.0)

    @pl.when(ph == 1)
    def _():
        s_i = lax.broadcasted_iota(jnp.int32, (tm, tm), 0)
        t_i = lax.broadcasted_iota(jnp.int32, (tm, tm), 1)
        earlier = (s_i < t_i).astype(jnp.bfloat16)
        before = jnp.dot(member.astype(jnp.bfloat16), earlier, preferred_element_type=jnp.float32)
        pos = before + carry[...]
        d0 = jnp.sum(jnp.where(is0, pos, 0.0), axis=0, keepdims=True)
        d1 = jnp.sum(jnp.where(is1, pos, 0.0), axis=0, keepdims=True)
        row = lax.broadcasted_iota(jnp.int32, (SUBLANES, tm), 0)
        dest_ref[...] = jnp.where(row == 0, d0, jnp.where(row == 1, d1, 0.0)).astype(jnp.int32)

    carry[...] = carry[...] + jnp.sum(member, axis=1, keepdims=True)


def route_slots(eidx, tm=512):
    n = eidx.shape[0]
    nt = n // tm
    return pl.pallas_call(
        functools.partial(_rank_kernel, tm=tm),
        grid=(2, nt),
        in_specs=[pl.BlockSpec((tm, ROUTE_W), lambda ph, i: (i, 0))],
        out_specs=[pl.BlockSpec((SUBLANES, tm), lambda ph, i: (0, i * ph)),
                   pl.BlockSpec((ROUTE_W, ROUTE_W), lambda ph, i: (0, 0))],
        out_shape=[jax.ShapeDtypeStruct((SUBLANES, n), jnp.int32),
                   jax.ShapeDtypeStruct((ROUTE_W, ROUTE_W), jnp.int32)],
        scratch_shapes=[pltpu.VMEM((ROUTE_W, 1), jnp.float32)],
        compiler_params=_cparams(2),
        name="route_slots",
    )(eidx)


def _slot_map_kernel(dest_ref, slot_ref, *, n, slots):
    def init(j, c):
        slot_ref[j] = 0
        return c

    lax.fori_loop(0, slots, init, 0, unroll=32)

    def body(t, c):
        slot_ref[dest_ref[t]] = t
        slot_ref[dest_ref[n + t]] = t
        return c

    lax.fori_loop(0, n, body, 0, unroll=16)


def slot_map(dest_flat, n, slots):
    return pl.pallas_call(
        functools.partial(_slot_map_kernel, n=n, slots=slots),
        in_specs=[pl.BlockSpec(memory_space=pltpu.SMEM)],
        out_specs=pl.BlockSpec(memory_space=pltpu.SMEM),
        out_shape=jax.ShapeDtypeStruct((slots,), jnp.int32),
        name="slot_map",
    )(dest_flat)


def _expert_kernel(be_ref, nused_ref, st_ref, x_hbm, wg_ref, wu_ref, wd_ref, y_ref, xs, land, sem):
    i = pl.program_id(0)
    n_used = nused_ref[0]

    def gather_block(blk, slot):
        base = blk * MOE_BLK
        for j in range(MOE_BLK):
            src = pl.multiple_of(st_ref[base + j] * TOKEN_TILE, TOKEN_TILE)
            pltpu.make_async_copy(x_hbm.at[pl.ds(src, TOKEN_TILE)],
                                  land.at[slot, pl.ds(j * TOKEN_TILE, TOKEN_TILE)],
                                  sem.at[slot]).start(priority=j % DMA_QUEUES)

    def wait_block(slot):
        pltpu.make_async_copy(x_hbm.at[pl.ds(0, MOE_BLK * TOKEN_TILE)], land.at[slot], sem.at[slot]).wait()

    @pl.when(i == 0)
    def _():
        gather_block(0, 0)

    @pl.when(i + 1 <= n_used)
    def _():
        gather_block(i + 1, (i + 1) % 2)

    @pl.when(i < n_used)
    def _():
        slot = i % 2
        wait_block(slot)
        for c, chunk in enumerate(_load_token_tiles(land.at[slot], 0, MOE_BLK)):
            xs[:, c * LANES:(c + 1) * LANES] = chunk.astype(jnp.bfloat16)

    @pl.when(n_used > i)
    def _():
        xb = xs[...]
        gate = jnp.dot(xb, wg_ref[0], preferred_element_type=jnp.float32)
        up = jnp.dot(xb, wu_ref[0], preferred_element_type=jnp.float32)
        hid = (jax.nn.silu(gate) * up).astype(jnp.bfloat16)
        _store_token_tiles(y_ref, jnp.dot(hid, wd_ref[0], preferred_element_type=jnp.float32))

    @pl.when(i == n_used - 1)
    def _():
        wait_block((i + 1) % 2)

    @pl.when(i >= n_used)
    def _():
        y_ref[...] = jnp.zeros_like(y_ref)


def expert_blocks(block_e, n_used, slot_tok, x1t, wg, wu, wd):
    d = D_MODEL
    n_blocks = slot_tok.shape[0] // MOE_BLK - 1
    slots = n_blocks * MOE_BLK
    tile_rows = MOE_BLK * TOKEN_TILE
    grid_spec = pltpu.PrefetchScalarGridSpec(
        num_scalar_prefetch=3,
        grid=(n_blocks,),
        in_specs=[pl.BlockSpec(memory_space=pl.ANY),
                  pl.BlockSpec((1, d, D_EXPERT), lambda i, be, nu, st: (be[i], 0, 0)),
                  pl.BlockSpec((1, d, D_EXPERT), lambda i, be, nu, st: (be[i], 0, 0)),
                  pl.BlockSpec((1, D_EXPERT, d), lambda i, be, nu, st: (be[i], 0, 0))],
        out_specs=pl.BlockSpec((tile_rows, LANES), lambda i, be, nu, st: (i, 0)),
        scratch_shapes=[pltpu.VMEM((MOE_BLK, d), jnp.bfloat16),
                        pltpu.VMEM((2, tile_rows, LANES), jnp.float32),
                        pltpu.SemaphoreType.DMA((2,))],
    )
    return pl.pallas_call(
        _expert_kernel,
        grid_spec=grid_spec,
        out_shape=jax.ShapeDtypeStruct((slots * TOKEN_TILE, LANES), jnp.float32),
        compiler_params=_cparams(1),
        name="expert_blocks",
    )(block_e, n_used, slot_tok, x1t, wg, wu, wd)


def _combine_kernel(dest_ref, x_ref, gate_ref, y_hbm, g_ref, b_ref, xo_ref, xob_ref, fbuf, sem, *, n, tm):
    i = pl.program_id(0)
    nt = pl.num_programs(0)
    half = tm * TOKEN_TILE

    def gather_tile(tile, slot):
        base = tile * tm
        for k in range(TOP_K):
            for t in range(tm):
                src = pl.multiple_of(dest_ref[k * n + base + t] * TOKEN_TILE, TOKEN_TILE)
                pltpu.make_async_copy(y_hbm.at[pl.ds(src, TOKEN_TILE)],
                                      fbuf.at[slot, pl.ds((k * tm + t) * TOKEN_TILE, TOKEN_TILE)],
                                      sem.at[slot]).start(priority=t % DMA_QUEUES)

    def wait_tile(slot):
        pltpu.make_async_copy(y_hbm.at[pl.ds(0, TOP_K * half)], fbuf.at[slot], sem.at[slot]).wait()

    @pl.when(i == 0)
    def _():
        gather_tile(0, 0)

    slot = i % 2
    wait_tile(slot)
    g0, g1 = gate_ref[:, 0:1], gate_ref[:, 1:2]
    f0 = _load_token_tiles(fbuf.at[slot], 0, tm)
    f1 = _load_token_tiles(fbuf.at[slot], tm, tm)
    xs = _load_token_tiles(x_ref, 0, tm)
    gather_tile(jnp.minimum(i + 1, nt - 1), 1 - slot)
    z = [ALPHA * xs[c] + (g0 * f0[c] + g1 * f1[c]) for c in range(TOKEN_TILE)]
    mu = sum(jnp.sum(zc, axis=-1, keepdims=True) for zc in z) * (1.0 / D_MODEL)
    zc = [zz - mu for zz in z]
    var = sum(jnp.sum(v * v, axis=-1, keepdims=True) for v in zc) * (1.0 / D_MODEL)
    inv = lax.rsqrt(var + LN_EPS)
    for c in range(TOKEN_TILE):
        cols = slice(c * LANES, (c + 1) * LANES)
        x2 = zc[c] * inv * g_ref[:, cols] + b_ref[:, cols]
        xo_ref[:, cols] = x2
        xob_ref[:, cols] = x2.astype(jnp.bfloat16)

    @pl.when(i == nt - 1)
    def _():
        wait_tile(1 - slot)


def combine_ln2(dest_flat, x1t, gate, ybt, g, b, tm=256):
    n = gate.shape[0]
    d = D_MODEL
    tile_rows = tm * TOKEN_TILE
    grid_spec = pltpu.PrefetchScalarGridSpec(
        num_scalar_prefetch=1,
        grid=(n // tm,),
        in_specs=[pl.BlockSpec((tile_rows, LANES), lambda i, ds: (i, 0)),
                  pl.BlockSpec((tm, ROUTE_W), lambda i, ds: (i, 0)),
                  pl.BlockSpec(memory_space=pl.ANY),
                  pl.BlockSpec((1, d), lambda i, ds: (0, 0)),
                  pl.BlockSpec((1, d), lambda i, ds: (0, 0))],
        out_specs=[pl.BlockSpec((tm, d), lambda i, ds: (i, 0)),
                   pl.BlockSpec((tm, d), lambda i, ds: (i, 0))],
        scratch_shapes=[pltpu.VMEM((2, TOP_K * tile_rows, LANES), jnp.float32),
                        pltpu.SemaphoreType.DMA((2,))],
    )
    return pl.pallas_call(
        functools.partial(_combine_kernel, n=n, tm=tm),
        grid_spec=grid_spec,
        out_shape=[jax.ShapeDtypeStruct((n, d), jnp.float32),
                   jax.ShapeDtypeStruct((n, d), jnp.bfloat16)],
        compiler_params=_cparams(1),
        name="combine_ln2",
    )(dest_flat, x1t, gate, ybt, g, b)


def moe_layer(x1, eidx, gate, wg, wu, wd, ln_g, ln_b):
    n = eidx.shape[0]
    m = n * TOP_K
    n_blocks = m // MOE_BLK + N_EXPERTS
    slots = n_blocks * MOE_BLK
    dest8, cnt = route_slots(eidx)
    dest_flat = dest8[:TOP_K].reshape(m)
    counts = cnt[:N_EXPERTS, 0]
    pend = jnp.cumsum((counts + MOE_BLK - 1) // MOE_BLK * MOE_BLK)
    blk_start = jnp.arange(n_blocks, dtype=jnp.int32) * MOE_BLK
    block_e = jnp.minimum(jnp.sum((pend[None, :] <= blk_start[:, None]).astype(jnp.int32), axis=1),
                          N_EXPERTS - 1)
    n_used = (pend[-1:] // MOE_BLK).astype(jnp.int32)
    slot_tok = slot_map(dest_flat, n, slots + MOE_BLK)
    yb = expert_blocks(block_e, n_used, slot_tok, x1, wg, wu, wd)
    return combine_ln2(dest_flat, x1, gate, yb, ln_g, ln_b)


def _t5_bucket(rel):
    half = NUM_BUCKETS // 2
    max_exact = half // 2
    n = np.abs(rel)
    large = max_exact + (np.log(np.maximum(n, 1) / max_exact) / np.log(MAX_DISTANCE / max_exact)
                         * (half - max_exact)).astype(np.int32)
    large = np.minimum(large, half - 1)
    return np.where(rel > 0, half, 0) + np.where(n < max_exact, n, large)


def _bias_table(rel_bias, g, dil):
    rel = (np.arange(3 * RADIUS)[None, :] - RADIUS - np.arange(RADIUS)[:, None]) * dil
    hs = slice(g * HEADS_PER_GROUP, (g + 1) * HEADS_PER_GROUP)
    onehot = jnp.asarray(_t5_bucket(rel)[..., None] == np.arange(NUM_BUCKETS), jnp.float32)
    return jnp.einsum('qkb,bh->hqk', onehot, rel_bias[:, hs].astype(jnp.float32),
                      precision=lax.Precision.HIGHEST)


def _chunk_block_diag(w):
    per = RNN_CHUNK // RNN_BLOCK_W
    n_chunks = RNN_BLOCKS // per
    w4 = w.reshape(n_chunks, per, RNN_BLOCK_W, RNN_BLOCK_W)
    dense = jnp.einsum('chij,hk->chikj', w4, jnp.eye(per, dtype=w.dtype))
    return dense.reshape(n_chunks, RNN_CHUNK, RNN_CHUNK)


def _gate_weights(wa, wi):
    per_dir = [jnp.concatenate([_chunk_block_diag(wa[d]), _chunk_block_diag(wi[d])], axis=-1)
               for d in range(2)]
    return jnp.stack(per_dir, axis=1).astype(jnp.bfloat16)


def _per_chunk(v):
    return v.reshape(2, D_RNN // RNN_CHUNK, 1, RNN_CHUNK).transpose(1, 0, 2, 3)


def _split_hi_lo(w):
    hi = w.astype(jnp.bfloat16)
    lo = (w - hi.astype(jnp.float32)).astype(jnp.bfloat16)
    return jnp.concatenate([hi, lo], axis=1)


def _permute_in_cols(a):
    x_rnn, q, k, v, g_rnn, g_attn = jnp.split(
        a, [D_RNN, D_RNN + D_ATTN, D_RNN + 2 * D_ATTN, D_RNN + 3 * D_ATTN,
            D_RNN + 3 * D_ATTN + D_MODEL], axis=-1)
    return jnp.concatenate([x_rnn, g_rnn, g_attn, q, k, v], axis=-1)


def _trunk(x, biases, lp):
    b, s, d = x.shape
    n = b * s
    x2 = x.reshape(n, d)
    x2_mm = x2
    for l in range(DEPTH):
        p = lp[l]
        proj = in_projection(x2_mm, p['w_in'], p['b_in'])
        proj3 = proj.reshape(b, s, D_IN)
        h_rnn = rglru_branch(proj3, p['conv_w'], p['conv_b'], p['wg'], p['gb'], p['lam'])
        y_attn = dilated_attention(proj3, biases)
        x1, eidx, gate = mix_and_route(
            x2, proj, h_rnn.reshape(n, D_RNN), y_attn.reshape(n, D_ATTN_OUT),
            p['w_rnn_out'], p['w_attn_out'], p['w_o'],
            p['ln1_g'], p['ln1_b'], p['router_w'], p['router_b'])
        x2, x2_mm = moe_layer(x1, eidx, gate, p['w_gate'], p['w_up'], p['w_down'], p['ln2_g'], p['ln2_b'])
    return x2.reshape(b, s, d)


def kernel(x_prompt, x_sample, rel_bias, w_in, b_in, conv_w, conv_b, rg_wa, rg_ba, rg_wi, rg_bi, rg_lam, w_rnn_out, w_attn_out, w_o, ln1_g, ln1_b, router_w, router_b, expert_router_w, expert_router_b, w_gate, w_up, w_down, ln2_g, ln2_b):
    bf = jnp.bfloat16
    biases = [_bias_table(rel_bias, g, dil) for g, (_, dil) in enumerate(DILATED_CONFIGS)]
    lp = []
    for l in range(DEPTH):
        rw = jnp.concatenate(
            [router_w[l], jnp.transpose(expert_router_w[l], (1, 0, 2)).reshape(D_MODEL, N_EXPERTS)], axis=1)
        rb = jnp.concatenate([router_b[l], expert_router_b[l].reshape(N_EXPERTS)])
        pad = ROUTE_W - rw.shape[1]
        lp.append(dict(
            w_in=_permute_in_cols(w_in[l]).astype(bf),
            b_in=_permute_in_cols(b_in[l])[None, :],
            conv_w=conv_w[l], conv_b=conv_b[l][None, :],
            wg=_gate_weights(rg_wa[l], rg_wi[l]),
            gb=jnp.concatenate([_per_chunk(rg_ba[l]), _per_chunk(rg_bi[l])], axis=-1),
            lam=_per_chunk(rg_lam[l]),
            w_rnn_out=w_rnn_out[l].astype(bf), w_attn_out=w_attn_out[l].astype(bf), w_o=w_o[l].astype(bf),
            ln1_g=ln1_g[l][None, :], ln1_b=ln1_b[l][None, :],
            router_w=_split_hi_lo(jnp.pad(rw, ((0, 0), (0, pad)))), router_b=jnp.pad(rb, (0, pad))[None, :],
            w_gate=w_gate[l].astype(bf), w_up=w_up[l].astype(bf), w_down=w_down[l].astype(bf),
            ln2_g=ln2_g[l][None, :], ln2_b=ln2_b[l][None, :]))
    return (_trunk(x_prompt, biases, lp), _trunk(x_sample, biases, lp))
```
